```python
import math
import jax, jax.numpy as jnp
from jax import lax
import numpy as np

D_MODEL = 1024
BATCH = 2
SEQ = 8192
DEPTH = 2

MEM_LEN = 256
HEAD_DIM = 64
D_MIX = D_MODEL
W_CONV = D_MIX // 4
W_RWKV = D_MIX // 4
W_GMLP = D_MIX // 4
W_NSA = D_MIX - W_CONV - W_RWKV - W_GMLP
N_CONV_GROUPS = W_CONV // HEAD_DIM
N_RWKV_HEADS = W_RWKV // HEAD_DIM
N_GMLP_HEADS = W_GMLP // HEAD_DIM
N_NSA_HEADS = W_NSA // HEAD_DIM
CONV_WIDTH = 31
LORA_W = 64
LORA_A = 64
LORA_V = 32
LORA_G = 128
RWKV_COLS = 3 * W_RWKV + LORA_W + LORA_A + LORA_G
RWKV_SIZES = (W_RWKV, W_RWKV, W_RWKV, LORA_W, LORA_A, LORA_G)
GMLP_CHUNK = 128
CMP_LEN = 32
CMP_STRIDE = 16
CMP_HIDDEN = 256
SEL_LEN = 64
SEL_TOPN = 16
WINDOW = 512
Q_BLOCK = 128
ROPE_THETA = 10000.0
N_MEM_HEADS = 4
MEM_HEAD_DIM = D_MODEL // N_MEM_HEADS
N_EXPERTS = 16
N_EXPERT_GROUPS = 4
TOP_K = 2
D_EXPERT = 512
ALPHA = (2 * DEPTH) ** 0.25
BETA = (8 * DEPTH) ** -0.25
LN_EPS = 1e-5
RWKV_GN_EPS = 64e-5
NEG = -1e30
IN_SIZES = (W_CONV, W_CONV, RWKV_COLS, W_GMLP, W_GMLP, W_NSA,
            HEAD_DIM, HEAD_DIM, HEAD_DIM, HEAD_DIM, HEAD_DIM, HEAD_DIM, 3 * N_NSA_HEADS)
D_IN = sum(IN_SIZES)

kernel_name = 'hybrid_conv_rwkv7_gmlp_nsa_moe_deepnorm'


def _split_points(sizes):
    pts, acc = [], 0
    for s in sizes[:-1]:
        acc += s
        pts.append(acc)
    return pts


def layer_norm(x, g, b, eps=LN_EPS):
    xf = x.astype(jnp.float32)
    mu = xf.mean(-1, keepdims=True)
    var = jnp.square(xf - mu).mean(-1, keepdims=True)
    return ((xf - mu) * lax.rsqrt(var + eps)).astype(x.dtype) * g + b


def group_norm(x, n_groups, g, b, eps):
    shp = x.shape
    xf = x.astype(jnp.float32).reshape(shp[:-1] + (n_groups, shp[-1] // n_groups))
    mu = xf.mean(-1, keepdims=True)
    var = jnp.square(xf - mu).mean(-1, keepdims=True)
    y = ((xf - mu) * lax.rsqrt(var + eps)).reshape(shp).astype(x.dtype)
    return y * g + b


def token_shift(y, mu):
    y_prev = jnp.pad(y, ((0, 0), (1, 0), (0, 0)))[:, :-1]
    return y + mu * (y_prev - y)


def rope_tables(n_pos, dim):
    inv = ROPE_THETA ** (-jnp.arange(0, dim, 2, dtype=jnp.float32) / dim)
    ang = jnp.arange(n_pos, dtype=jnp.float32)[:, None] * inv[None, :]
    return jnp.cos(ang), jnp.sin(ang)


def apply_rope(x, cos, sin):
    x1, x2 = jnp.split(x, 2, axis=-1)
    return jnp.concatenate([x1 * cos - x2 * sin, x2 * cos + x1 * sin], axis=-1).astype(x.dtype)


def masked_softmax(s, valid):
    s = jnp.where(valid, s, NEG)
    m = jnp.max(s, axis=-1, keepdims=True)
    p = jnp.exp(s - m) * valid
    return p / jnp.maximum(p.sum(-1, keepdims=True), 1e-20)


def conv_mixer(val, gate, w_dw, b_dw, gn_g, gn_b):
    h = val * jax.nn.sigmoid(gate)
    h = lax.conv_general_dilated(h, w_dw[:, None, :], window_strides=(1,),
                                 padding=[(CONV_WIDTH - 1, 0)],
                                 dimension_numbers=('NWC', 'WIO', 'NWC'),
                                 feature_group_count=W_CONV) + b_dw
    h = group_norm(h, N_CONV_GROUPS, gn_g, gn_b, LN_EPS)
    return jax.nn.silu(h)


def wkv7_scan(r, decay, k, v, a_vec, b_vec):
    B_, S_, H_, D_ = r.shape
    xs = tuple(jnp.moveaxis(t.astype(jnp.float32), 1, 0) for t in (r, decay, k, v, a_vec, b_vec))

    def step(state, inp):
        r_t, w_t, k_t, v_t, a_t, b_t = inp
        sa = jnp.einsum('bhvk,bhk->bhv', state, a_t)
        state = (state * w_t[:, :, None, :] + sa[..., None] * b_t[:, :, None, :]
                 + v_t[..., None] * k_t[:, :, None, :])
        return state, jnp.einsum('bhvk,bhk->bhv', state, r_t)

    s0 = jnp.zeros((B_, H_, D_, D_), jnp.float32)
    _, out = lax.scan(step, s0, xs)
    return jnp.moveaxis(out, 0, 1).astype(r.dtype)


def rwkv7_mixer(cols, mu, w0, w_up, a0, a_up, g_up, k_k, k_a, r_k, gn_g, gn_b, v_first, v_mix):
    B_, S_, _ = cols.shape
    y = token_shift(cols, mu)
    r, k, v, wd, ad, gd = jnp.split(y, _split_points(RWKV_SIZES), axis=-1)
    w = -jax.nn.softplus(-(w0 + jnp.tanh(wd) @ w_up)) - 0.5
    a = jax.nn.sigmoid(a0 + ad @ a_up)
    g = jax.nn.sigmoid(gd) @ g_up
    if v_first is None:
        v_first = v
    else:
        v = v + (v_first - v) * v_mix
    hs = lambda t: t.reshape(B_, S_, N_RWKV_HEADS, HEAD_DIM)
    hp = lambda p: p.reshape(N_RWKV_HEADS, HEAD_DIM)
    r, k, v, w, a = hs(r), hs(k), hs(v), hs(w), hs(a)
    kk = (k * hp(k_k)).astype(jnp.float32)
    kk = (kk / jnp.maximum(jnp.sqrt(jnp.sum(kk * kk, -1, keepdims=True)), 1e-12)).astype(k.dtype)
    k = k * (1 + (a - 1) * hp(k_a))
    decay = jnp.exp(-jnp.exp(w.astype(jnp.float32)))
    o = wkv7_scan(r, decay, k, v, -kk, kk * a)
    o = group_norm(o.reshape(B_, S_, W_RWKV), N_RWKV_HEADS, gn_g, gn_b, RWKV_GN_EPS)
    bonus = (jnp.sum(r * k * hp(r_k), -1, keepdims=True) * v).reshape(B_, S_, W_RWKV)
    return (o + bonus) * g, v_first


def gmlp_mixer(u, v, ln_g, ln_b, w_s, b_s):
    u = jax.nn.gelu(u)
    v = layer_norm(jax.nn.gelu(v), ln_g, ln_b)
    B_, S_, C = v.shape
    v = v.reshape(B_, S_ // GMLP_CHUNK, GMLP_CHUNK, N_GMLP_HEADS, C // N_GMLP_HEADS)
    mask = jnp.tril(jnp.ones((GMLP_CHUNK, GMLP_CHUNK), dtype=bool))
    w = jnp.where(mask[None], w_s, 0.0)
    mixed = jnp.einsum('hts,bcshd->bcthd', w, v) + b_s.T[None, None, :, :, None]
    return u * mixed.reshape(B_, S_, C)


def compress_blocks(kv, pe, w1, w2):
    B_, S_, D_ = kv.shape
    r = CMP_LEN // CMP_STRIDE
    n_cmp = S_ // CMP_STRIDE - (r - 1)
    c = kv.reshape(B_, S_ // CMP_STRIDE, CMP_STRIDE, D_)
    blocks = jnp.concatenate([c[:, j:j + n_cmp] for j in range(r)], axis=2) + pe
    h = jax.nn.gelu(blocks.reshape(B_, n_cmp, CMP_LEN * D_) @ w1)
    return h @ w2


def nsa_mixer(q, kc, vc, ks, vs, kw, vw, gate_logits, pe_k, w1_k, w2_k, pe_v, w1_v, w2_v, cos, sin):
    B_, S_, _ = kc.shape
    q = q.reshape(B_, S_, N_NSA_HEADS, HEAD_DIM)
    q_rot = apply_rope(q, cos[:, None, :], sin[:, None, :])
    ks = apply_rope(ks, cos, sin)
    kw = apply_rope(kw, cos, sin)
    k_cmp = compress_blocks(kc, pe_k, w1_k, w2_k)
    v_cmp = compress_blocks(vc, pe_v, w1_v, w2_v)
    n_cmp = k_cmp.shape[1]
    n_sel = S_ // SEL_LEN
    n_top = min(SEL_TOPN, n_sel)
    cmp_start = jnp.arange(n_cmp) * CMP_STRIDE
    cmp_end = cmp_start + CMP_LEN - 1
    sel_start = jnp.arange(n_sel) * SEL_LEN
    overlap = jnp.clip(jnp.minimum(cmp_start[:, None] + CMP_LEN, sel_start[None, :] + SEL_LEN)
                       - jnp.maximum(cmp_start[:, None], sel_start[None, :]), 0, None).astype(jnp.float32) / CMP_LEN
    ks_blk = ks.reshape(B_, n_sel, SEL_LEN, HEAD_DIM)
    vs_blk = vs.reshape(B_, n_sel, SEL_LEN, HEAD_DIM)
    kw_pad = jnp.pad(kw, ((0, 0), (WINDOW, 0), (0, 0)))
    vw_pad = jnp.pad(vw, ((0, 0), (WINDOW, 0), (0, 0)))
    gates = jax.nn.sigmoid(gate_logits.reshape(B_, S_, N_NSA_HEADS, 3))
    scale = HEAD_DIM ** -0.5
    b_idx = jnp.arange(B_)[:, None, None]

    def block(i):
        t0 = i * Q_BLOCK
        pos = t0 + jnp.arange(Q_BLOCK)
        qn = lax.dynamic_slice_in_dim(q, t0, Q_BLOCK, axis=1)
        qr = lax.dynamic_slice_in_dim(q_rot, t0, Q_BLOCK, axis=1)
        gb = lax.dynamic_slice_in_dim(gates, t0, Q_BLOCK, axis=1)
        s_c = jnp.einsum('bthd,bnd->bhtn', qn, k_cmp).astype(jnp.float32) * scale
        p_c = masked_softmax(s_c, cmp_end[None, :] <= pos[:, None])
        o_c = jnp.einsum('bhtn,bnd->bthd', p_c.astype(v_cmp.dtype), v_cmp)
        imp = jnp.einsum('bhtn,nj->btj', p_c, overlap)
        forced = (sel_start[None, :] == (pos[:, None] // SEL_LEN) * SEL_LEN) | (sel_start[None, :] == 0)
        allowed = sel_start[None, :] <= pos[:, None]
        imp = jnp.where(forced, jnp.inf, jnp.where(allowed, imp, -jnp.inf))
        _, idx = lax.top_k(imp, n_top)
        kg = ks_blk[b_idx, idx]
        vg = vs_blk[b_idx, idx]
        key_pos = idx[..., None] * SEL_LEN + jnp.arange(SEL_LEN)
        valid_s = (key_pos <= pos[None, :, None, None]).reshape(B_, 1, Q_BLOCK, n_top * SEL_LEN)
        s_s = jnp.einsum('bthd,btnld->bhtnl', qr, kg).astype(jnp.float32).reshape(
            B_, N_NSA_HEADS, Q_BLOCK, n_top * SEL_LEN) * scale
        p_s = masked_softmax(s_s, valid_s).reshape(B_, N_NSA_HEADS, Q_BLOCK, n_top, SEL_LEN)
        o_s = jnp.einsum('bhtnl,btnld->bthd', p_s.astype(vg.dtype), vg)
        kwb = lax.dynamic_slice_in_dim(kw_pad, t0, WINDOW + Q_BLOCK, axis=1)
        vwb = lax.dynamic_slice_in_dim(vw_pad, t0, WINDOW + Q_BLOCK, axis=1)
        kpos = t0 - WINDOW + jnp.arange(WINDOW + Q_BLOCK)
        valid_w = ((kpos[None, :] <= pos[:, None]) & (kpos[None, :] > pos[:, None] - WINDOW)
                   & (kpos[None, :] >= 0))
        s_w = jnp.einsum('bthd,bsd->bhts', qr, kwb).astype(jnp.float32) * scale
        p_w = masked_softmax(s_w, valid_w)
        o_w = jnp.einsum('bhts,bsd->bthd', p_w.astype(vwb.dtype), vwb)
        return gb[..., 0:1] * o_c + gb[..., 1:2] * o_s + gb[..., 2:3] * o_w

    out = lax.map(block, jnp.arange(S_ // Q_BLOCK))
    return jnp.moveaxis(out, 0, 1).reshape(B_, S_, W_NSA)


def mem_cross_attention(x, mem, wq, wk, wv, wo):
    B_, S_, _ = x.shape
    M_ = mem.shape[1]
    q = (x @ wq).reshape(B_, S_, N_MEM_HEADS, MEM_HEAD_DIM)
    k = (mem @ wk).reshape(B_, M_, N_MEM_HEADS, MEM_HEAD_DIM)
    v = (mem @ wv).reshape(B_, M_, N_MEM_HEADS, MEM_HEAD_DIM)
    s = jnp.einsum('bshd,bmhd->bhsm', q, k).astype(jnp.float32) * (MEM_HEAD_DIM ** -0.5)
    p = jax.nn.softmax(s, axis=-1).astype(v.dtype)
    o = jnp.einsum('bhsm,bmhd->bshd', p, v).reshape(B_, S_, D_MODEL)
    return o @ wo


def moe_ffn(x, router_w, router_bias, w_gate, w_up, w_down):
    B_, S_, D_ = x.shape
    t = x.reshape(-1, D_)
    probs = jax.nn.softmax((t @ router_w).astype(jnp.float32), axis=-1)
    sel = probs + router_bias
    grp = sel.reshape(-1, N_EXPERT_GROUPS, N_EXPERTS // N_EXPERT_GROUPS)
    grp_score = lax.top_k(grp, TOP_K)[0].sum(-1)
    g_idx = jnp.argmax(grp_score, axis=-1)
    in_group = (jnp.arange(N_EXPERTS) // (N_EXPERTS // N_EXPERT_GROUPS))[None, :] == g_idx[:, None]
    _, e_idx = lax.top_k(jnp.where(in_group, sel, -jnp.inf), TOP_K)
    w_sel = jnp.take_along_axis(probs, e_idx, axis=-1)
    w_sel = w_sel / w_sel.sum(-1, keepdims=True)
    gates = jnp.sum(jax.nn.one_hot(e_idx, N_EXPERTS, dtype=jnp.float32) * w_sel[..., None], axis=1)
    y = jnp.zeros(t.shape, jnp.float32)
    for e in range(N_EXPERTS):
        he = jax.nn.silu(t @ w_gate[e]) * (t @ w_up[e])
        y = y + gates[:, e:e + 1] * (he @ w_down[e])
    return y.astype(x.dtype).reshape(B_, S_, D_)


def setup_inputs(seed: int = 0) -> dict:
    key = jax.random.key(seed)
    ks = iter(jax.random.split(key, 80))
    nrm = lambda shape, scale: scale * jax.random.normal(next(ks), shape, jnp.float32)
    L = DEPTH
    LV = DEPTH - 1
    D = D_MODEL
    return {
        'x': nrm((BATCH, SEQ, D), 1.0),
        'mem': nrm((BATCH, MEM_LEN, D), 1.0),
        'ln_in_g': 1.0 + nrm((D,), 0.05),
        'ln_in_b': nrm((D,), 0.02),
        'w_in': nrm((L, D, D_IN), D ** -0.5),
        'w_out': nrm((L, D_MIX, D), BETA * D_MIX ** -0.5),
        'conv_w': nrm((L, CONV_WIDTH, W_CONV), CONV_WIDTH ** -0.5),
        'conv_b': nrm((L, W_CONV), 0.02),
        'conv_gn_g': 1.0 + nrm((L, W_CONV), 0.05),
        'conv_gn_b': nrm((L, W_CONV), 0.02),
        'rwkv_mu': jax.random.uniform(next(ks), (L, RWKV_COLS), jnp.float32),
        'rwkv_w0': nrm((L, W_RWKV), 0.5),
        'rwkv_w_up': nrm((L, LORA_W, W_RWKV), 0.3 * LORA_W ** -0.5),
        'rwkv_a0': nrm((L, W_RWKV), 0.5),
        'rwkv_a_up': nrm((L, LORA_A, W_RWKV), 0.3 * LORA_A ** -0.5),
        'rwkv_g_up': nrm((L, LORA_G, W_RWKV), LORA_G ** -0.5),
        'rwkv_k_k': 0.85 + nrm((L, W_RWKV), 0.05),
        'rwkv_k_a': 1.0 + nrm((L, W_RWKV), 0.05),
        'rwkv_r_k': nrm((L, W_RWKV), 0.1),
        'rwkv_gn_g': 1.0 + nrm((L, W_RWKV), 0.05),
        'rwkv_gn_b': nrm((L, W_RWKV), 0.02),
        'rwkv_v_down': nrm((LV, D, LORA_V), D ** -0.5),
        'rwkv_v_mu': jax.random.uniform(next(ks), (LV, LORA_V), jnp.float32),
        'rwkv_v0': nrm((LV, W_RWKV), 0.5),
        'rwkv_v_up': nrm((LV, LORA_V, W_RWKV), LORA_V ** -0.5),
        'gmlp_ln_g': 1.0 + nrm((L, W_GMLP), 0.05),
        'gmlp_ln_b': nrm((L, W_GMLP), 0.02),
        'gmlp_w_s': nrm((L, N_GMLP_HEADS, GMLP_CHUNK, GMLP_CHUNK), 0.5 * GMLP_CHUNK ** -0.5),
        'gmlp_b_s': 1.0 + nrm((L, N_GMLP_HEADS, GMLP_CHUNK), 0.1),
        'nsa_pe_k': nrm((L, CMP_LEN, HEAD_DIM), 0.1),
        'nsa_w1_k': nrm((L, CMP_LEN * HEAD_DIM, CMP_HIDDEN), (CMP_LEN * HEAD_DIM) ** -0.5),
        'nsa_w2_k': nrm((L, CMP_HIDDEN, HEAD_DIM), CMP_HIDDEN ** -0.5),
        'nsa_pe_v': nrm((L, CMP_LEN, HEAD_DIM), 0.1),
        'nsa_w1_v': nrm((L, CMP_LEN * HEAD_DIM, CMP_HIDDEN), (CMP_LEN * HEAD_DIM) ** -0.5),
        'nsa_w2_v': nrm((L, CMP_HIDDEN, HEAD_DIM), CMP_HIDDEN ** -0.5),
        'ln1_g': 1.0 + nrm((L, D), 0.05),
        'ln1_b': nrm((L, D), 0.02),
        'xa_wq': nrm((L, D, D), D ** -0.5),
        'xa_wk': nrm((L, D, D), D ** -0.5),
        'xa_wv': nrm((L, D, D), BETA * D ** -0.5),
        'xa_wo': nrm((L, D, D), BETA * D ** -0.5),
        'ln2_g': 1.0 + nrm((L, D), 0.05),
        'ln2_b': nrm((L, D), 0.02),
        'router_w': nrm((D, N_EXPERTS), D ** -0.5),
        'router_bias': nrm((N_EXPERTS,), 0.01),
        'moe_w_gate': nrm((L, N_EXPERTS, D, D_EXPERT), D ** -0.5),
        'moe_w_up': nrm((L, N_EXPERTS, D, D_EXPERT), D ** -0.5),
        'moe_w_down': nrm((L, N_EXPERTS, D_EXPERT, D), BETA * D_EXPERT ** -0.5),
        'ln3_g': 1.0 + nrm((L, D), 0.05),
        'ln3_b': nrm((L, D), 0.02),
    }


def reference(x, mem, ln_in_g, ln_in_b, w_in, w_out, conv_w, conv_b, conv_gn_g, conv_gn_b,
              rwkv_mu, rwkv_w0, rwkv_w_up, rwkv_a0, rwkv_a_up, rwkv_g_up, rwkv_k_k, rwkv_k_a,
              rwkv_r_k, rwkv_gn_g, rwkv_gn_b, rwkv_v_down, rwkv_v_mu, rwkv_v0, rwkv_v_up,
              gmlp_ln_g, gmlp_ln_b, gmlp_w_s, gmlp_b_s,
              nsa_pe_k, nsa_w1_k, nsa_w2_k, nsa_pe_v, nsa_w1_v, nsa_w2_v,
              ln1_g, ln1_b, xa_wq, xa_wk, xa_wv, xa_wo, ln2_g, ln2_b,
              router_w, router_bias, moe_w_gate, moe_w_up, moe_w_down, ln3_g, ln3_b):
    S_ = x.shape[1]
    cos, sin = rope_tables(S_, HEAD_DIM)
    in_points = _split_points(IN_SIZES)
    x = layer_norm(x, ln_in_g, ln_in_b)
    v_first = None
    for l in range(DEPTH):
        cols = x @ w_in[l]
        (a_val, a_gate, b_cols, c_u, c_v, d_q, d_kc, d_vc, d_ks, d_vs, d_kw, d_vw,
         d_gate) = jnp.split(cols, in_points, axis=-1)
        out_a = conv_mixer(a_val, a_gate, conv_w[l], conv_b[l], conv_gn_g[l], conv_gn_b[l])
        v_mix = None
        if l > 0:
            vd = token_shift(x @ rwkv_v_down[l - 1], rwkv_v_mu[l - 1])
            v_mix = jax.nn.sigmoid(rwkv_v0[l - 1] + vd @ rwkv_v_up[l - 1])
        out_b, v_first = rwkv7_mixer(b_cols, rwkv_mu[l], rwkv_w0[l], rwkv_w_up[l], rwkv_a0[l],
                                     rwkv_a_up[l], rwkv_g_up[l], rwkv_k_k[l], rwkv_k_a[l],
                                     rwkv_r_k[l], rwkv_gn_g[l], rwkv_gn_b[l], v_first, v_mix)
        out_c = gmlp_mixer(c_u, c_v, gmlp_ln_g[l], gmlp_ln_b[l], gmlp_w_s[l], gmlp_b_s[l])
        out_d = nsa_mixer(d_q, d_kc, d_vc, d_ks, d_vs, d_kw, d_vw, d_gate,
                          nsa_pe_k[l], nsa_w1_k[l], nsa_w2_k[l], nsa_pe_v[l], nsa_w1_v[l], nsa_w2_v[l],
                          cos, sin)
        mix = jnp.concatenate([out_a, out_b, out_c, out_d], axis=-1) @ w_out[l]
        x = layer_norm(ALPHA * x + mix, ln1_g[l], ln1_b[l])
        x = layer_norm(ALPHA * x + mem_cross_attention(x, mem, xa_wq[l], xa_wk[l], xa_wv[l], xa_wo[l]),
                       ln2_g[l], ln2_b[l])
        x = layer_norm(ALPHA * x + moe_ffn(x, router_w, router_bias, moe_w_gate[l], moe_w_up[l], moe_w_down[l]),
                       ln3_g[l], ln3_b[l])
    return x
```

```python
import functools
import math

import jax
import jax.numpy as jnp
from jax import lax
from jax.experimental import pallas as pl
from jax.experimental.pallas import tpu as pltpu

F32 = jnp.float32
BF16 = jnp.bfloat16
HI = lax.Precision.HIGHEST

HEAD_DIM = 64
N_HEADS = 4
W_GRP = 256
CONV_WIDTH = 31
GMLP_CHUNK = 128
CMP_LEN = 32
CMP_STRIDE = 16
SEL_LEN = 64
SEL_TOPN = 16
WINDOW = 512
Q_BLOCK = 128
ROPE_THETA = 10000.0
N_MEM_HEADS = 4
N_EXPERTS = 16
N_EXPERT_GROUPS = 4
LN_EPS = 1e-5
RWKV_GN_EPS = 64e-5
NEG = -1e30
LANES = 128
WKV_CHUNK = 64

TM = 512
MOE_TM = 1024
SEL_KT = 512
VMEM_LIMIT = 56 * 1024 * 1024


def _cparams(*sem):
    return pltpu.CompilerParams(dimension_semantics=sem, vmem_limit_bytes=VMEM_LIMIT)


def _ln(x, g, b, eps=LN_EPS):
    mu = jnp.mean(x, axis=-1, keepdims=True)
    xc = x - mu
    var = jnp.mean(xc * xc, axis=-1, keepdims=True)
    return xc * lax.rsqrt(var + eps) * g + b


def _dot(a, b, precision=None):
    return jnp.dot(a, b, preferred_element_type=F32, precision=precision)


def _dot_nt(a, b, precision=None):
    return lax.dot_general(a, b, (((1,), (1,)), ((), ())),
                           preferred_element_type=F32, precision=precision)


def _group_avg_matrix(width, group):
    r = jnp.arange(width)[:, None] // group
    c = jnp.arange(width)[None, :] // group
    return jnp.where(r == c, 1.0 / group, 0.0).astype(F32)


IN_SPLITS = (("conv", 0, 512), ("rwkv", 512, 1536), ("gmlp", 1536, 2048),
             ("q", 2048, 2304), ("kv", 2304, 2688), ("misc", 2688, 2816))
IN_PAD = 2816
MISC_VD_OFF = 32


def _inproj_body(apply_ln, x_ref, g_ref, b_ref, w_ref, *outs):
    x = x_ref[...]
    if apply_ln:
        x = _ln(x, g_ref[...], b_ref[...])
        outs[0][...] = x
        outs = outs[1:]
    xb = x.astype(BF16)
    for o_ref, (_, lo, hi) in zip(outs, IN_SPLITS):
        o_ref[...] = _dot(xb, w_ref[:, lo:hi])


def in_proj(x2d, g, b, w_pad, apply_ln):
    n, d = x2d.shape
    row = lambda w: pl.BlockSpec((TM, w), lambda i: (i, 0))
    const = lambda s: pl.BlockSpec(s, lambda i: (0, 0))
    out_shapes = [jax.ShapeDtypeStruct((n, hi - lo), F32) for _, lo, hi in IN_SPLITS]
    out_specs = [row(hi - lo) for _, lo, hi in IN_SPLITS]
    if apply_ln:
        out_shapes = [jax.ShapeDtypeStruct((n, d), F32)] + out_shapes
        out_specs = [row(d)] + out_specs
    return pl.pallas_call(
        functools.partial(_inproj_body, apply_ln),
        grid=(n // TM,),
        in_specs=[row(d), const((1, d)), const((1, d)), const((d, IN_PAD))],
        out_specs=out_specs, out_shape=out_shapes,
        compiler_params=_cparams("parallel"), name="in_proj",
    )(x2d, g, b, w_pad)


CONV_HALO = 32


def _conv_body(ts, cur_ref, halo_ref, w_ref, b_ref, gg_ref, gb_ref, avg_ref, o_ref, hbuf):
    i = pl.program_id(1)
    cur = cur_ref[0]
    hal = halo_ref[0]
    h = cur[:, :W_GRP] * jax.nn.sigmoid(cur[:, W_GRP:])
    hh = hal[:, :W_GRP] * jax.nn.sigmoid(hal[:, W_GRP:])
    hbuf[0:CONV_HALO, :] = jnp.where(i > 0, hh, 0.0)
    hbuf[CONV_HALO:, :] = h
    acc = jnp.zeros((ts, W_GRP), F32)
    base = CONV_HALO - (CONV_WIDTH - 1)
    for j in range(CONV_WIDTH):
        acc = acc + w_ref[j:j + 1, :] * hbuf[base + j:base + j + ts, :]
    acc = acc + b_ref[...]
    avg = avg_ref[...]
    mu = _dot(acc, avg, HI)
    xc = acc - mu
    var = _dot(xc * xc, avg, HI)
    y = xc * lax.rsqrt(var + LN_EPS) * gg_ref[...] + gb_ref[...]
    o_ref[0] = y * jax.nn.sigmoid(y)


def conv_mixer(conv_in, w, b, gg, gb, avg):
    bsz, s, _ = conv_in.shape
    ts = TM
    r = ts // CONV_HALO
    const = lambda shp: pl.BlockSpec(shp, lambda bi, i: (0, 0))
    return pl.pallas_call(
        functools.partial(_conv_body, ts),
        grid=(bsz, s // ts),
        in_specs=[pl.BlockSpec((1, ts, 2 * W_GRP), lambda bi, i: (bi, i, 0)),
                  pl.BlockSpec((1, CONV_HALO, 2 * W_GRP),
                               lambda bi, i: (bi, jnp.maximum(i * r - 1, 0), 0)),
                  const((CONV_WIDTH, W_GRP)), const((1, W_GRP)), const((1, W_GRP)),
                  const((1, W_GRP)), const((W_GRP, W_GRP))],
        out_specs=pl.BlockSpec((1, ts, W_GRP), lambda bi, i: (bi, i, 0)),
        out_shape=jax.ShapeDtypeStruct((bsz, s, W_GRP), F32),
        scratch_shapes=[pltpu.VMEM((ts + CONV_HALO, W_GRP), F32)],
        compiler_params=_cparams("parallel", "parallel"), name="conv_mixer",
    )(conv_in, conv_in, w, b, gg, gb, avg)


SHIFT_HALO = 8


def _shift_prev(buf, cur, halo, first):
    ts = cur.shape[0]
    buf[0:SHIFT_HALO, :] = jnp.where(first, 0.0, halo)
    buf[SHIFT_HALO:, :] = cur
    return buf[SHIFT_HALO - 1:SHIFT_HALO - 1 + ts, :]


def _rwkv_prep_body(has_vfirst, *refs):
    if has_vfirst:
        (c_ref, ch_ref, m_ref, mh_ref, vf_ref, mu_ref, w0_ref, a0_ref, wup_ref, aup_ref, gup_ref,
         kk_ref, ka_ref, ones_ref, vmu_ref, v0_ref, vup_ref,
         r_o, lw_o, k_o, v_o, a_o, b_o, g_o, buf, mbuf) = refs
    else:
        (c_ref, ch_ref, mu_ref, w0_ref, a0_ref, wup_ref, aup_ref, gup_ref,
         kk_ref, ka_ref, ones_ref,
         r_o, lw_o, k_o, v_o, a_o, b_o, g_o, buf) = refs
    first = pl.program_id(1) == 0
    cur = c_ref[0]
    prev = _shift_prev(buf, cur, ch_ref[0], first)
    y = cur + mu_ref[...] * (prev - cur)
    r = y[:, 0:256]
    k = y[:, 256:512]
    v = y[:, 512:768]
    lora = y[:, 768:1024]
    w = w0_ref[...] + _dot(jnp.tanh(lora), wup_ref[...], HI)
    a = jax.nn.sigmoid(a0_ref[...] + _dot(lora, aup_ref[...], HI))
    g = _dot(jax.nn.sigmoid(lora), gup_ref[...], HI)
    z = -w
    sp = jnp.maximum(z, 0.0) + jnp.log(1.0 + jnp.exp(-jnp.abs(z)))
    lw = -jnp.exp(-sp - 0.5)
    if has_vfirst:
        mc = m_ref[0]
        mprev = _shift_prev(mbuf, mc, mh_ref[0], first)
        vd = mc + vmu_ref[...] * (mprev - mc)
        v_mix = jax.nn.sigmoid(v0_ref[...] + _dot(vd, vup_ref[...], HI))
        v = v + (vf_ref[0] - v) * v_mix
    kk = k * kk_ref[...]
    n2 = _dot(kk * kk, ones_ref[...], HI)
    kk = kk / jnp.maximum(jnp.sqrt(n2), 1e-12)
    k2 = k * (1.0 + (a - 1.0) * ka_ref[...])
    r_o[0] = r
    lw_o[0] = lw
    k_o[0] = k2
    v_o[0] = v
    a_o[0] = -kk
    b_o[0] = kk * a
    g_o[0] = g


def rwkv_prep(cols, misc, v_first, p):
    bsz, s, c = cols.shape
    ts = TM
    has_vfirst = v_first is not None
    r8 = ts // SHIFT_HALO
    tile = lambda w: pl.BlockSpec((1, ts, w), lambda bi, i: (bi, i, 0))
    halo = lambda w: pl.BlockSpec((1, SHIFT_HALO, w), lambda bi, i: (bi, jnp.maximum(i * r8 - 1, 0), 0))
    const = lambda a: pl.BlockSpec(a.shape, lambda bi, i: (0,) * a.ndim)
    params = [p["mu"], p["w0"], p["a0"], p["wup"], p["aup"], p["gup"], p["kk"], p["ka"], p["ones"]]
    if has_vfirst:
        inputs = [cols, cols, misc, misc, v_first] + params + [p["vmu"], p["v0"], p["vup"]]
        in_specs = ([tile(c), halo(c), tile(LANES), halo(LANES), tile(W_GRP)]
                    + [const(a) for a in params + [p["vmu"], p["v0"], p["vup"]]])
        scratch = [pltpu.VMEM((ts + SHIFT_HALO, c), F32), pltpu.VMEM((ts + SHIFT_HALO, LANES), F32)]
    else:
        inputs = [cols, cols] + params
        in_specs = [tile(c), halo(c)] + [const(a) for a in params]
        scratch = [pltpu.VMEM((ts + SHIFT_HALO, c), F32)]
    out = jax.ShapeDtypeStruct((bsz, s, W_GRP), F32)
    return pl.pallas_call(
        functools.partial(_rwkv_prep_body, has_vfirst),
        grid=(bsz, s // ts),
        in_specs=in_specs,
        out_specs=[tile(W_GRP)] * 7, out_shape=[out] * 7,
        scratch_shapes=scratch,
        compiler_params=_cparams("parallel", "parallel"), name="rwkv_prep",
    )(*inputs)


def _block_diag(x, headmask):
    return jnp.where(headmask, jnp.concatenate([x] * N_HEADS, axis=0), 0.0)


def _wkv_body(ts, r_ref, lw_ref, k_ref, v_ref, a_ref, b_ref, g_ref, rk_ref, gg_ref, gb_ref,
              avg_ref, ones_ref, o_ref, st_ref, ob_ref):
    C = WKV_CHUNK
    n = N_HEADS * C

    @pl.when(pl.program_id(1) == 0)
    def _():
        st_ref[...] = jnp.zeros_like(st_ref)

    ri = lax.broadcasted_iota(jnp.int32, (n, n), 0)
    ci = lax.broadcasted_iota(jnp.int32, (n, n), 1)
    headmask = (ri // C) == (ci // HEAD_DIM)
    lag = jnp.where((ri // C) == (ci // C), (ri % C) - (ci % C), -1)
    strict = lag > 0
    incl = lag >= 0
    eye = ri == ci
    tr = lax.broadcasted_iota(jnp.int32, (C, C), 0)
    tc = lax.broadcasted_iota(jnp.int32, (C, C), 1)
    tri = jnp.where(tc <= tr, 1.0, 0.0).astype(F32)

    def chunk(c, carry):
        rows = pl.ds(pl.multiple_of(c * C, C), C)
        lw = lw_ref[0, rows, :]
        cum = _dot(tri, lw, HI)
        cum_ex = cum - lw
        cum_c = cum[C - 1:C, :]
        e_in = jnp.exp(cum)
        e_neg = jnp.exp(-cum)
        e_tail = jnp.exp(cum_c - cum)
        a_t = _block_diag(a_ref[0, rows, :] * jnp.exp(cum_ex), headmask)
        r_t = _block_diag(r_ref[0, rows, :] * e_in, headmask)
        b_t = _block_diag(b_ref[0, rows, :] * e_neg, headmask)
        k_t = _block_diag(k_ref[0, rows, :] * e_neg, headmask)
        b_h = _block_diag(b_ref[0, rows, :] * e_tail, headmask)
        k_h = _block_diag(k_ref[0, rows, :] * e_tail, headmask)
        v_bd = _block_diag(v_ref[0, rows, :], headmask)
        m = _dot_nt(jnp.concatenate([a_t, r_t], axis=0), jnp.concatenate([b_t, k_t], axis=0), HI)
        a_ab = jnp.where(strict, m[:n, :n], 0.0)
        a_ak = jnp.where(strict, m[:n, n:], 0.0)
        a_rb = jnp.where(incl, m[n:, :n], 0.0)
        a_rk = jnp.where(incl, m[n:, n:], 0.0)
        t_inv = jnp.where(eye, 1.0, 0.0) + a_ab
        pw = a_ab
        for _ in range(int(math.log2(C)) - 1):
            pw = _dot(pw, pw, HI)
            t_inv = t_inv + _dot(t_inv, pw, HI)
        tt = _dot(t_inv, jnp.concatenate([a_t, a_ak], axis=1), HI)
        st = st_ref[...]
        u = _dot(tt, jnp.concatenate([st, v_bd], axis=0), HI)
        z = jnp.concatenate([st, u, v_bd], axis=0)
        o_bd = _dot(jnp.concatenate([r_t, a_rb, a_rk], axis=1), z, HI)
        d_w = jnp.where(eye, jnp.exp(cum_c), 0.0)
        st_ref[...] = _dot(jnp.concatenate([d_w, b_h.T, k_h.T], axis=1), z, HI)
        ob_ref[rows, :] = o_bd[0:C] + o_bd[C:2 * C] + o_bd[2 * C:3 * C] + o_bd[3 * C:4 * C]
        return carry

    lax.fori_loop(0, ts // C, chunk, 0)

    o = ob_ref[...]
    avg = avg_ref[...]
    mu = _dot(o, avg, HI)
    xc = o - mu
    var = _dot(xc * xc, avg, HI)
    on = xc * lax.rsqrt(var + RWKV_GN_EPS) * gg_ref[...] + gb_ref[...]
    r = r_ref[0]
    k = k_ref[0]
    v = v_ref[0]
    bonus = _dot(r * k * rk_ref[...], ones_ref[...], HI) * v
    o_ref[0] = (on + bonus) * g_ref[0]


def wkv_scan(r, lw, k, v, a, b, g, rk, gg, gb, avg, ones):
    bsz, s, _ = r.shape
    ts = TM
    tile = pl.BlockSpec((1, ts, W_GRP), lambda bi, i: (bi, i, 0))
    const = lambda arr: pl.BlockSpec(arr.shape, lambda bi, i: (0,) * arr.ndim)
    return pl.pallas_call(
        functools.partial(_wkv_body, ts),
        grid=(bsz, s // ts),
        in_specs=[tile] * 7 + [const(x) for x in (rk, gg, gb, avg, ones)],
        out_specs=tile, out_shape=jax.ShapeDtypeStruct((bsz, s, W_GRP), F32),
        scratch_shapes=[pltpu.VMEM((N_HEADS * WKV_CHUNK, W_GRP), F32), pltpu.VMEM((ts, W_GRP), F32)],
        compiler_params=_cparams("parallel", "arbitrary"), name="wkv_scan",
    )(r, lw, k, v, a, b, g, rk, gg, gb, avg, ones)


def _gmlp_body(ts, x_ref, g_ref, b_ref, ws_ref, bs_ref, o_ref):
    x = x_ref[...]
    u = jax.nn.gelu(x[:, :W_GRP])
    v = _ln(jax.nn.gelu(x[:, W_GRP:]), g_ref[...], b_ref[...]).astype(BF16)
    tr = lax.broadcasted_iota(jnp.int32, (GMLP_CHUNK, GMLP_CHUNK), 0)
    tc = lax.broadcasted_iota(jnp.int32, (GMLP_CHUNK, GMLP_CHUNK), 1)
    ws = [jnp.where(tc <= tr, ws_ref[h], 0.0).astype(BF16) for h in range(N_HEADS)]
    for c in range(ts // GMLP_CHUNK):
        rows = slice(c * GMLP_CHUNK, (c + 1) * GMLP_CHUNK)
        mixed = [_dot(ws[h], v[rows, h * HEAD_DIM:(h + 1) * HEAD_DIM]) + bs_ref[h]
                 for h in range(N_HEADS)]
        o_ref[rows, :] = u[rows, :] * jnp.concatenate(mixed, axis=-1)


def gmlp_mixer(x2d, g, b, ws, bs):
    n = x2d.shape[0]
    ts = TM
    const = lambda a: pl.BlockSpec(a.shape, lambda i: (0,) * a.ndim)
    return pl.pallas_call(
        functools.partial(_gmlp_body, ts),
        grid=(n // ts,),
        in_specs=[pl.BlockSpec((ts, 2 * W_GRP), lambda i: (i, 0)), const(g), const(b), const(ws), const(bs)],
        out_specs=pl.BlockSpec((ts, W_GRP), lambda i: (i, 0)),
        out_shape=jax.ShapeDtypeStruct((n, W_GRP), F32),
        compiler_params=_cparams("parallel"), name="gmlp_mixer",
    )(x2d, g, b, ws, bs)


def _swap_halves(x, lane):
    w = x.shape[-1]
    half = HEAD_DIM // 2
    fwd = pltpu.roll(x, w - half, 1)
    bwd = pltpu.roll(x, half, 1)
    return jnp.where((lane % HEAD_DIM) < half, fwd, bwd)


def _nsa_prep_body(q_ref, kv_ref, m_ref, cos_ref, sin_ref,
                   q_o, qr_o, kc_o, vc_o, ks_o, vs_o, kw_o, vw_o, g_o):
    scale = HEAD_DIM ** -0.5
    q = q_ref[0]
    kv = kv_ref[0]
    cos = cos_ref[...]
    sin = sin_ref[...]
    lane_q = lax.broadcasted_iota(jnp.int32, q.shape, 1)
    q_rot = q * cos[:, :W_GRP] + _swap_halves(q, lane_q) * sin[:, :W_GRP]
    lane_kv = lax.broadcasted_iota(jnp.int32, kv.shape, 1)
    kv_rot = kv * cos + _swap_halves(kv, lane_kv) * sin
    q_o[0] = (q * scale).astype(BF16)
    qr_o[0] = (q_rot * scale).astype(BF16)
    kc_o[0] = kv[:, 0:64]
    vc_o[0] = kv[:, 64:128]
    ks_o[0] = kv_rot[:, 128:192].astype(BF16)
    vs_o[0] = kv[:, 192:256].astype(BF16)
    kw_o[0] = kv_rot[:, 256:320].astype(BF16)
    vw_o[0] = kv[:, 320:384].astype(BF16)
    g_o[0] = jax.nn.sigmoid(m_ref[0])


def nsa_prep(q, kv, misc, cos_t, sin_t):
    bsz, s, _ = q.shape
    ts = TM
    tile = lambda w: pl.BlockSpec((1, ts, w), lambda bi, i: (bi, i, 0))
    tab = pl.BlockSpec((ts, 6 * HEAD_DIM), lambda bi, i: (i, 0))
    sd = lambda w, dt: jax.ShapeDtypeStruct((bsz, s, w), dt)
    return pl.pallas_call(
        _nsa_prep_body,
        grid=(bsz, s // ts),
        in_specs=[tile(W_GRP), tile(6 * HEAD_DIM), tile(LANES), tab, tab],
        out_specs=[tile(W_GRP), tile(W_GRP)] + [tile(HEAD_DIM)] * 6 + [tile(LANES)],
        out_shape=[sd(W_GRP, BF16), sd(W_GRP, BF16), sd(HEAD_DIM, F32), sd(HEAD_DIM, F32),
                   sd(HEAD_DIM, BF16), sd(HEAD_DIM, BF16), sd(HEAD_DIM, BF16), sd(HEAD_DIM, BF16),
                   sd(LANES, F32)],
        compiler_params=_cparams("parallel", "parallel"), name="nsa_prep",
    )(q, kv, misc, cos_t, sin_t)


def _compress_body(kc_ref, vc_ref, pe_ref, w1_ref, w2_ref, ko_ref, vo_ref):
    half = CMP_STRIDE * HEAD_DIM
    for j, (c_ref, o_ref) in enumerate(((kc_ref, ko_ref), (vc_ref, vo_ref))):
        c = c_ref[0].astype(BF16)
        w1 = w1_ref[j]
        lo = _dot(c, w1[:half])
        hi = _dot(c, w1[half:])
        nb = hi.shape[0]
        hi_next = pltpu.roll(hi, nb - 1, 0)
        pe = jnp.broadcast_to(pe_ref[j], (8, 2 * half))
        pe_term = _dot(pe, w1.astype(F32), HI)[0:1]
        h = jax.nn.gelu(lo + hi_next + pe_term)
        o_ref[0] = _dot(h.astype(BF16), w2_ref[j]).astype(BF16)


def nsa_compress(kc_r, vc_r, pe, w1, w2):
    bsz, nb, w = kc_r.shape
    blk = pl.BlockSpec((1, nb, w), lambda bi: (bi, 0, 0))
    const = lambda a: pl.BlockSpec(a.shape, lambda bi: (0,) * a.ndim)
    out = jax.ShapeDtypeStruct((bsz, nb, HEAD_DIM), BF16)
    ospec = pl.BlockSpec((1, nb, HEAD_DIM), lambda bi: (bi, 0, 0))
    return pl.pallas_call(
        _compress_body, grid=(bsz,),
        in_specs=[blk, blk, const(pe), const(w1), const(w2)],
        out_specs=[ospec, ospec], out_shape=[out, out],
        compiler_params=_cparams("parallel"), name="nsa_compress",
    )(kc_r, vc_r, pe, w1, w2)


def _stack_heads(x):
    return jnp.concatenate([x[:, h * HEAD_DIM:(h + 1) * HEAD_DIM] for h in range(N_HEADS)], axis=0)


def _nsa_attn_body(n_top, q_ref, qr_ref, kc_ref, vc_ref, ks_ref, vs_ref, kw_ref, vw_ref, g_ref,
                   ov_ref, o_ref):
    T = Q_BLOCK
    i = pl.program_id(1)
    t0 = i * T
    qs = _stack_heads(q_ref[0])
    qrs = _stack_heads(qr_ref[0])
    pos = t0 + lax.broadcasted_iota(jnp.int32, (T, 1), 0)
    tile4 = lambda x: jnp.concatenate([x] * N_HEADS, axis=0)

    kc = kc_ref[0]
    n_cmp = kc.shape[0]
    cmp_end = lax.broadcasted_iota(jnp.int32, (1, n_cmp), 1) * CMP_STRIDE + (CMP_LEN - 1)
    valid_c = tile4(jnp.where(cmp_end <= pos, 1.0, 0.0))
    s_c = jnp.where(valid_c > 0.5, _dot_nt(qs, kc), NEG)
    p_c = jnp.exp(s_c - jnp.max(s_c, axis=-1, keepdims=True)) * valid_c
    p_c = p_c / jnp.maximum(jnp.sum(p_c, axis=-1, keepdims=True), 1e-20)
    o_c = _dot(p_c.astype(BF16), vc_ref[0])
    p_sum = p_c[0:T] + p_c[T:2 * T] + p_c[2 * T:3 * T] + p_c[3 * T:4 * T]
    imp = _dot(p_sum, ov_ref[...], HI)

    n_sel = imp.shape[1]
    jj = lax.broadcasted_iota(jnp.int32, (T, n_sel), 1)
    blk = pos // SEL_LEN
    val = jnp.where(jj == blk, 3e38, jnp.where(jj == 0, 3e38, jnp.where(jj <= blk, imp, -1.0)))
    sel = jnp.zeros((T, n_sel), F32)
    for _ in range(n_top):
        mx = jnp.max(val, axis=-1, keepdims=True)
        idx = jnp.min(jnp.where(val == mx, jj, n_sel), axis=-1, keepdims=True)
        hit = jj == idx
        sel = jnp.where(hit, 1.0, sel)
        val = jnp.where(hit, -2.0, val)
    sel_b = sel.astype(BF16)

    KT = SEL_KT
    n_tiles = (t0 + T + KT - 1) // KT

    def sel_tile(jt, carry):
        m, l, acc = carry
        k0 = pl.multiple_of(jt * KT, KT)
        kt = ks_ref[0, pl.ds(k0, KT), :]
        vt = vs_ref[0, pl.ds(k0, KT), :]
        er = lax.broadcasted_iota(jnp.int32, (n_sel, KT), 0)
        ec = lax.broadcasted_iota(jnp.int32, (n_sel, KT), 1)
        expand = jnp.where(er == (k0 + ec) // SEL_LEN, 1.0, 0.0).astype(BF16)
        picked = _dot(sel_b, expand)
        kpos = k0 + lax.broadcasted_iota(jnp.int32, (1, KT), 1)
        valid = tile4(jnp.where(kpos <= pos, picked, 0.0))
        s = jnp.where(valid > 0.5, _dot_nt(qrs, kt), NEG)
        m_new = jnp.maximum(m, jnp.max(s, axis=-1, keepdims=True))
        alpha = jnp.exp(m - m_new)
        p = jnp.exp(s - m_new) * valid
        l = alpha * l + jnp.sum(p, axis=-1, keepdims=True)
        acc = alpha * acc + _dot(p.astype(BF16), vt)
        return m_new, l, acc

    init = (jnp.full((N_HEADS * T, 1), NEG, F32), jnp.zeros((N_HEADS * T, 1), F32),
            jnp.zeros((N_HEADS * T, HEAD_DIM), F32))
    _, l_s, acc_s = lax.fori_loop(0, n_tiles, sel_tile, init)
    o_s = acc_s / jnp.maximum(l_s, 1e-20)

    WK = WINDOW + T
    start = pl.multiple_of(jnp.maximum(t0 - WINDOW, 0), T)
    kw = kw_ref[0, pl.ds(start, WK), :]
    vw = vw_ref[0, pl.ds(start, WK), :]
    kpos_w = start + lax.broadcasted_iota(jnp.int32, (1, WK), 1)
    valid_w = tile4(jnp.where(kpos_w <= pos, jnp.where(kpos_w > pos - WINDOW, 1.0, 0.0), 0.0))
    s_w = jnp.where(valid_w > 0.5, _dot_nt(qrs, kw), NEG)
    p_w = jnp.exp(s_w - jnp.max(s_w, axis=-1, keepdims=True)) * valid_w
    p_w = p_w / jnp.maximum(jnp.sum(p_w, axis=-1, keepdims=True), 1e-20)
    o_w = _dot(p_w.astype(BF16), vw)

    g = g_ref[0]
    outs = []
    for h in range(N_HEADS):
        rows = slice(h * T, (h + 1) * T)
        outs.append(g[:, 3 * h:3 * h + 1] * o_c[rows] + g[:, 3 * h + 1:3 * h + 2] * o_s[rows]
                    + g[:, 3 * h + 2:3 * h + 3] * o_w[rows])
    o_ref[0] = jnp.concatenate(outs, axis=-1)


def nsa_attention(q, qr, k_cmp, v_cmp, ks, vs, kw, vw, gates, overlap):
    bsz, s, _ = q.shape
    n_top = min(SEL_TOPN, s // SEL_LEN)
    n_cmp = k_cmp.shape[1]
    qtile = lambda w: pl.BlockSpec((1, Q_BLOCK, w), lambda bi, i: (bi, i, 0))
    full = lambda rows, w: pl.BlockSpec((1, rows, w), lambda bi, i: (bi, 0, 0))
    return pl.pallas_call(
        functools.partial(_nsa_attn_body, n_top),
        grid=(bsz, s // Q_BLOCK),
        in_specs=[qtile(W_GRP), qtile(W_GRP), full(n_cmp, HEAD_DIM), full(n_cmp, HEAD_DIM),
                  full(s, HEAD_DIM), full(s, HEAD_DIM), full(s, HEAD_DIM), full(s, HEAD_DIM),
                  qtile(LANES), pl.BlockSpec(overlap.shape, lambda bi, i: (0, 0))],
        out_specs=qtile(W_GRP), out_shape=jax.ShapeDtypeStruct((bsz, s, W_GRP), F32),
        compiler_params=_cparams("parallel", "parallel"), name="nsa_attention",
    )(q, qr, k_cmp, v_cmp, ks, vs, kw, vw, gates, overlap)


def _outproj_body(alpha, a_ref, b_ref, c_ref, d_ref, x_ref, w_ref, g_ref, bb_ref, o_ref):
    mix = jnp.zeros(x_ref.shape, F32)
    for j, part in enumerate((a_ref, b_ref, c_ref, d_ref)):
        mix = mix + _dot(part[...].astype(BF16), w_ref[j * W_GRP:(j + 1) * W_GRP, :])
    o_ref[...] = _ln(alpha * x_ref[...] + mix, g_ref[...], bb_ref[...])


def out_proj(parts, x2d, w, g, b, alpha):
    n, d = x2d.shape
    row = lambda w_: pl.BlockSpec((TM, w_), lambda i: (i, 0))
    const = lambda a: pl.BlockSpec(a.shape, lambda i: (0,) * a.ndim)
    return pl.pallas_call(
        functools.partial(_outproj_body, alpha),
        grid=(n // TM,),
        in_specs=[row(W_GRP)] * 4 + [row(d), const(w), const(g), const(b)],
        out_specs=row(d), out_shape=jax.ShapeDtypeStruct((n, d), F32),
        compiler_params=_cparams("parallel"), name="out_proj",
    )(*parts, x2d, w, g, b)


def _memkv_body(m_ref, wk_ref, wv_ref, k_o, v_o):
    mb = m_ref[...].astype(BF16)
    k_o[...] = _dot(mb, wk_ref[...]).astype(BF16)
    v_o[...] = _dot(mb, wv_ref[...]).astype(BF16)


def mem_kv(mem2d, wk, wv):
    n, d = mem2d.shape
    full = lambda a: pl.BlockSpec(a.shape, lambda i: (0, 0))
    out = jax.ShapeDtypeStruct((n, d), BF16)
    return pl.pallas_call(
        _memkv_body, grid=(1,),
        in_specs=[full(mem2d), full(wk), full(wv)],
        out_specs=[pl.BlockSpec((n, d), lambda i: (0, 0))] * 2, out_shape=[out, out],
        compiler_params=_cparams("arbitrary"), name="mem_kv",
    )(mem2d, wk, wv)


def _xattn_body(alpha, x_ref, k_ref, v_ref, wq_ref, wo_ref, g_ref, b_ref, o_ref):
    x = x_ref[0]
    d = x.shape[-1]
    hd = d // N_MEM_HEADS
    q = (_dot(x.astype(BF16), wq_ref[...]) * (hd ** -0.5)).astype(BF16)
    k = k_ref[0]
    v = v_ref[0]
    heads = []
    for h in range(N_MEM_HEADS):
        cs = slice(h * hd, (h + 1) * hd)
        s = _dot_nt(q[:, cs], k[:, cs])
        p = jnp.exp(s - jnp.max(s, axis=-1, keepdims=True))
        p = p / jnp.sum(p, axis=-1, keepdims=True)
        heads.append(_dot(p.astype(BF16), v[:, cs]))
    o = jnp.concatenate(heads, axis=-1).astype(BF16)
    o_ref[0] = _ln(alpha * x + _dot(o, wo_ref[...]), g_ref[...], b_ref[...])


def cross_attention(x, k, v, wq, wo, g, b, alpha):
    bsz, s, d = x.shape
    m = k.shape[1]
    const = lambda a: pl.BlockSpec(a.shape, lambda bi, i: (0,) * a.ndim)
    return pl.pallas_call(
        functools.partial(_xattn_body, alpha),
        grid=(bsz, s // TM),
        in_specs=[pl.BlockSpec((1, TM, d), lambda bi, i: (bi, i, 0)),
                  pl.BlockSpec((1, m, d), lambda bi, i: (bi, 0, 0)),
                  pl.BlockSpec((1, m, d), lambda bi, i: (bi, 0, 0)),
                  const(wq), const(wo), const(g), const(b)],
        out_specs=pl.BlockSpec((1, TM, d), lambda bi, i: (bi, i, 0)),
        out_shape=jax.ShapeDtypeStruct((bsz, s, d), F32),
        compiler_params=_cparams("parallel", "parallel"), name="cross_attention",
    )(x, k, v, wq, wo, g, b)


def _router_body(x_ref, rw_ref, rb_ref, o_ref):
    per_grp = N_EXPERTS // N_EXPERT_GROUPS
    x = x_ref[...]
    logits = _dot(x, rw_ref[...], HI)
    lane = lax.broadcasted_iota(jnp.int32, logits.shape, 1)
    real = lane < N_EXPERTS
    lg = jnp.where(real, logits, NEG)
    ex = jnp.where(real, jnp.exp(lg - jnp.max(lg, axis=-1, keepdims=True)), 0.0)
    probs = ex / jnp.sum(ex, axis=-1, keepdims=True)
    sel = probs + rb_ref[...]
    sub = lane % per_grp

    def grot(v, k):
        ahead = pltpu.roll(v, LANES - k, 1)
        behind = pltpu.roll(v, per_grp - k, 1)
        return jnp.where(sub + k < per_grp, ahead, behind)

    others = jnp.maximum(jnp.maximum(grot(sel, 1), grot(sel, 2)), grot(sel, 3))
    pair = sel + others
    gscore = jnp.maximum(jnp.maximum(pair, grot(pair, 1)), jnp.maximum(grot(pair, 2), grot(pair, 3)))
    gs = jnp.where(real, gscore, -jnp.inf)
    grp = lane // per_grp
    g_idx = jnp.min(jnp.where(gs == jnp.max(gs, axis=-1, keepdims=True), grp, N_EXPERT_GROUPS),
                    axis=-1, keepdims=True)
    cand = jnp.where(grp == g_idx, jnp.where(real, sel, -jnp.inf), -jnp.inf)
    i1 = jnp.min(jnp.where(cand == jnp.max(cand, axis=-1, keepdims=True), lane, LANES),
                 axis=-1, keepdims=True)
    cand2 = jnp.where(lane == i1, -jnp.inf, cand)
    i2 = jnp.min(jnp.where(cand2 == jnp.max(cand2, axis=-1, keepdims=True), lane, LANES),
                 axis=-1, keepdims=True)
    w_sel = jnp.where(lane == i1, probs, jnp.where(lane == i2, probs, 0.0))
    o_ref[...] = w_sel / jnp.sum(w_sel, axis=-1, keepdims=True)


def moe_router(x2d, rw_pad, rb_pad):
    n, d = x2d.shape
    const = lambda a: pl.BlockSpec(a.shape, lambda i: (0, 0))
    return pl.pallas_call(
        _router_body, grid=(n // TM,),
        in_specs=[pl.BlockSpec((TM, d), lambda i: (i, 0)), const(rw_pad), const(rb_pad)],
        out_specs=pl.BlockSpec((TM, LANES), lambda i: (i, 0)),
        out_shape=jax.ShapeDtypeStruct((n, LANES), F32),
        compiler_params=_cparams("parallel"), name="moe_router",
    )(x2d, rw_pad, rb_pad)


def _moe_body(alpha, x_ref, gate_ref, wgu_ref, wd_ref, g_ref, b_ref, o_ref, xb_ref, acc_ref):
    e = pl.program_id(1)

    @pl.when(e == 0)
    def _():
        xb_ref[...] = x_ref[...].astype(BF16)
        acc_ref[...] = jnp.zeros_like(acc_ref)

    de = wd_ref.shape[1]
    gu = _dot(xb_ref[...], wgu_ref[0])
    gt = gu[:, :de]
    h = (gt * jax.nn.sigmoid(gt) * gu[:, de:]).astype(BF16)
    gates = gate_ref[...]
    lane = lax.broadcasted_iota(jnp.int32, gates.shape, 1)
    gcol = jnp.sum(jnp.where(lane == e, gates, 0.0), axis=-1, keepdims=True)
    acc_ref[...] += gcol * _dot(h, wd_ref[0])

    @pl.when(e == pl.num_programs(1) - 1)
    def _():
        o_ref[...] = _ln(alpha * x_ref[...] + acc_ref[...], g_ref[...], b_ref[...])


def moe_ffn(x2d, gates, wgu, wd, g, b, alpha):
    n, d = x2d.shape
    ne, de, _ = wd.shape
    tm = min(MOE_TM, n)
    const = lambda a: pl.BlockSpec(a.shape, lambda i, e: (0,) * a.ndim)
    return pl.pallas_call(
        functools.partial(_moe_body, alpha),
        grid=(n // tm, ne),
        in_specs=[pl.BlockSpec((tm, d), lambda i, e: (i, 0)),
                  pl.BlockSpec((tm, LANES), lambda i, e: (i, 0)),
                  pl.BlockSpec((1, d, 2 * de), lambda i, e: (e, 0, 0)),
                  pl.BlockSpec((1, de, d), lambda i, e: (e, 0, 0)),
                  const(g), const(b)],
        out_specs=pl.BlockSpec((tm, d), lambda i, e: (i, 0)),
        out_shape=jax.ShapeDtypeStruct((n, d), F32),
        scratch_shapes=[pltpu.VMEM((tm, d), BF16), pltpu.VMEM((tm, d), F32)],
        compiler_params=_cparams("parallel", "arbitrary"), name="moe_ffn",
    )(x2d, gates, wgu, wd, g, b)


def _rope_tables(s):
    inv = ROPE_THETA ** (-jnp.arange(0, HEAD_DIM, 2, dtype=F32) / HEAD_DIM)
    ang = jnp.arange(s, dtype=F32)[:, None] * inv[None, :]
    cos, sin = jnp.cos(ang), jnp.sin(ang)
    cos_h = jnp.concatenate([cos, cos], axis=-1)
    sin_h = jnp.concatenate([-sin, sin], axis=-1)
    return jnp.tile(cos_h, (1, 6)), jnp.tile(sin_h, (1, 6))


def _overlap_matrix(s, n_cmp_pad):
    n_sel = s // SEL_LEN
    cmp_start = jnp.arange(n_cmp_pad) * CMP_STRIDE
    sel_start = jnp.arange(n_sel) * SEL_LEN
    ov = jnp.clip(jnp.minimum(cmp_start[:, None] + CMP_LEN, sel_start[None, :] + SEL_LEN)
                  - jnp.maximum(cmp_start[:, None], sel_start[None, :]), 0, None).astype(F32) / CMP_LEN
    n_cmp = s // CMP_STRIDE - (CMP_LEN // CMP_STRIDE - 1)
    return jnp.where(jnp.arange(n_cmp_pad)[:, None] < n_cmp, ov, 0.0)


def _pad_rows(w, lo, total):
    out = jnp.zeros((total, w.shape[1]), w.dtype)
    return out.at[lo:lo + w.shape[0]].set(w)


def kernel(x, mem, ln_in_g, ln_in_b, w_in, w_out, conv_w, conv_b, conv_gn_g, conv_gn_b, rwkv_mu, rwkv_w0, rwkv_w_up, rwkv_a0, rwkv_a_up, rwkv_g_up, rwkv_k_k, rwkv_k_a, rwkv_r_k, rwkv_gn_g, rwkv_gn_b, rwkv_v_down, rwkv_v_mu, rwkv_v0, rwkv_v_up, gmlp_ln_g, gmlp_ln_b, gmlp_w_s, gmlp_b_s, nsa_pe_k, nsa_w1_k, nsa_w2_k, nsa_pe_v, nsa_w1_v, nsa_w2_v, ln1_g, ln1_b, xa_wq, xa_wk, xa_wv, xa_wo, ln2_g, ln2_b, router_w, router_bias, moe_w_gate, moe_w_up, moe_w_down, ln3_g, ln3_b):
    bsz, s, d = x.shape
    depth = w_in.shape[0]
    n = bsz * s
    alpha = (2 * depth) ** 0.25
    row = lambda a: a.reshape(1, -1)

    cos_t, sin_t = _rope_tables(s)
    n_blk = s // CMP_STRIDE
    overlap = _overlap_matrix(s, n_blk)
    avg64 = _group_avg_matrix(W_GRP, HEAD_DIM)
    ones64 = avg64 * HEAD_DIM
    rw_pad = jnp.zeros((d, LANES), F32).at[:, :N_EXPERTS].set(router_w)
    rb_pad = jnp.zeros((1, LANES), F32).at[0, :N_EXPERTS].set(router_bias)
    mem2d = mem.reshape(bsz * mem.shape[1], d)

    xs = x.reshape(n, d)
    v_first = None
    for l in range(depth):
        w_pad = jnp.zeros((d, IN_PAD), F32).at[:, :w_in.shape[2]].set(w_in[l])
        if l > 0:
            lo = IN_SPLITS[-1][1] + MISC_VD_OFF
            w_pad = w_pad.at[:, lo:lo + rwkv_v_down.shape[2]].set(rwkv_v_down[l - 1])
        outs = in_proj(xs, row(ln_in_g), row(ln_in_b), w_pad.astype(BF16), apply_ln=(l == 0))
        if l == 0:
            xs, outs = outs[0], outs[1:]
        conv_in, rwkv_in, gmlp_in, q_in, kv_in, misc = outs
        b3 = lambda a: a.reshape(bsz, s, a.shape[-1])

        out_a = conv_mixer(b3(conv_in), conv_w[l], row(conv_b[l]), row(conv_gn_g[l]), row(conv_gn_b[l]), avg64)

        rp = {"mu": row(rwkv_mu[l]), "w0": row(rwkv_w0[l]), "a0": row(rwkv_a0[l]),
              "wup": _pad_rows(rwkv_w_up[l], 0, W_GRP), "aup": _pad_rows(rwkv_a_up[l], 64, W_GRP),
              "gup": _pad_rows(rwkv_g_up[l], 128, W_GRP),
              "kk": row(rwkv_k_k[l]), "ka": row(rwkv_k_a[l]), "ones": ones64}
        if l > 0:
            rp["vmu"] = jnp.zeros((1, LANES), F32).at[0, MISC_VD_OFF:MISC_VD_OFF + rwkv_v_mu.shape[1]].set(rwkv_v_mu[l - 1])
            rp["v0"] = row(rwkv_v0[l - 1])
            rp["vup"] = _pad_rows(rwkv_v_up[l - 1], MISC_VD_OFF, LANES)
        r_, lw_, k_, v_, a_, b_, g_ = rwkv_prep(b3(rwkv_in), b3(misc), v_first, rp)
        if l == 0:
            v_first = v_
        out_b = wkv_scan(r_, lw_, k_, v_, a_, b_, g_, row(rwkv_r_k[l]), row(rwkv_gn_g[l]),
                         row(rwkv_gn_b[l]), avg64, ones64)

        out_c = gmlp_mixer(gmlp_in, row(gmlp_ln_g[l]), row(gmlp_ln_b[l]), gmlp_w_s[l],
                           gmlp_b_s[l].reshape(N_HEADS, GMLP_CHUNK, 1))

        q_b, qr_b, kc, vc, ks, vs, kw, vw, gates = nsa_prep(b3(q_in), b3(kv_in), b3(misc), cos_t, sin_t)
        pe = jnp.stack([nsa_pe_k[l].reshape(1, -1), nsa_pe_v[l].reshape(1, -1)])
        w1 = jnp.stack([nsa_w1_k[l], nsa_w1_v[l]]).astype(BF16)
        w2 = jnp.stack([nsa_w2_k[l], nsa_w2_v[l]]).astype(BF16)
        k_cmp, v_cmp = nsa_compress(kc.reshape(bsz, n_blk, CMP_STRIDE * HEAD_DIM),
                                    vc.reshape(bsz, n_blk, CMP_STRIDE * HEAD_DIM), pe, w1, w2)
        out_d = nsa_attention(q_b, qr_b, k_cmp, v_cmp, ks, vs, kw, vw, gates, overlap)

        flat = lambda a: a.reshape(n, W_GRP)
        xs = out_proj((flat(out_a), flat(out_b), out_c, flat(out_d)), xs, w_out[l].astype(BF16),
                      row(ln1_g[l]), row(ln1_b[l]), alpha)

        mk, mv = mem_kv(mem2d, xa_wk[l].astype(BF16), xa_wv[l].astype(BF16))
        m_len = mem.shape[1]
        xs = cross_attention(xs.reshape(bsz, s, d), mk.reshape(bsz, m_len, d), mv.reshape(bsz, m_len, d),
                             xa_wq[l].astype(BF16), xa_wo[l].astype(BF16),
                             row(ln2_g[l]), row(ln2_b[l]), alpha).reshape(n, d)

        gate_w = moe_router(xs, rw_pad, rb_pad)
        wgu = jnp.concatenate([moe_w_gate[l], moe_w_up[l]], axis=-1).astype(BF16)
        xs = moe_ffn(xs, gate_w, wgu, moe_w_down[l].astype(BF16), row(ln3_g[l]), row(ln3_b[l]), alpha)
    return xs.reshape(bsz, s, d)
```

```python
import functools
import math

import jax
import jax.numpy as jnp
from jax import lax
from jax.experimental import pallas as pl
from jax.experimental.pallas import tpu as pltpu

F32 = jnp.float32
BF16 = jnp.bfloat16
HI = lax.Precision.HIGHEST

HEAD_DIM = 64
N_HEADS = 4
W_GRP = 256
CONV_WIDTH = 31
GMLP_CHUNK = 128
CMP_LEN = 32
CMP_STRIDE = 16
SEL_LEN = 64
SEL_TOPN = 16
WINDOW = 512
Q_BLOCK = 128
ROPE_THETA = 10000.0
N_MEM_HEADS = 4
N_EXPERTS = 16
N_EXPERT_GROUPS = 4
LN_EPS = 1e-5
RWKV_GN_EPS = 64e-5
NEG = -1e30
LOG2E = math.log2(math.e)
LANES = 128
WKV_CHUNK = 64
WKV_GROUP = 8

TM = 512
MOE_TM = 1024
SEL_KT = 512
VMEM_LIMIT = 56 * 1024 * 1024


def _cparams(*sem):
    return pltpu.CompilerParams(dimension_semantics=sem, vmem_limit_bytes=VMEM_LIMIT)


def _ln(x, g, b, eps=LN_EPS):
    mu = jnp.mean(x, axis=-1, keepdims=True)
    xc = x - mu
    var = jnp.mean(xc * xc, axis=-1, keepdims=True)
    return xc * lax.rsqrt(var + eps) * g + b


def _dot(a, b, precision=None):
    return jnp.dot(a, b, preferred_element_type=F32, precision=precision)


def _dot_nt(a, b, precision=None):
    return lax.dot_general(a, b, (((1,), (1,)), ((), ())),
                           preferred_element_type=F32, precision=precision)


def _dot_hilo(x, w_bf16):
    hi = x.astype(BF16)
    lo = (x - hi.astype(F32)).astype(BF16)
    return _dot(hi, w_bf16) + _dot(lo, w_bf16)


def _group_avg_matrix(width, group):
    r = jnp.arange(width)[:, None] // group
    c = jnp.arange(width)[None, :] // group
    return jnp.where(r == c, 1.0 / group, 0.0).astype(F32)


IN_SPLITS = (("conv", 0, 512), ("rwkv", 512, 1536), ("gmlp", 1536, 2048),
             ("q", 2048, 2304), ("kv", 2304, 2688), ("misc", 2688, 2816))
IN_PAD = 2816
MISC_VD_OFF = 32


def _inproj_body(apply_ln, x_ref, g_ref, b_ref, w_ref, *outs):
    x = x_ref[...]
    if apply_ln:
        x = _ln(x, g_ref[...], b_ref[...])
        outs[0][...] = x
        outs = outs[1:]
    xb = x.astype(BF16)
    for o_ref, (_, lo, hi) in zip(outs, IN_SPLITS):
        o_ref[...] = _dot(xb, w_ref[:, lo:hi])


def in_proj(x2d, g, b, w_pad, apply_ln):
    n, d = x2d.shape
    row = lambda w: pl.BlockSpec((TM, w), lambda i: (i, 0))
    const = lambda s: pl.BlockSpec(s, lambda i: (0, 0))
    out_shapes = [jax.ShapeDtypeStruct((n, hi - lo), F32) for _, lo, hi in IN_SPLITS]
    out_specs = [row(hi - lo) for _, lo, hi in IN_SPLITS]
    if apply_ln:
        out_shapes = [jax.ShapeDtypeStruct((n, d), F32)] + out_shapes
        out_specs = [row(d)] + out_specs
    return pl.pallas_call(
        functools.partial(_inproj_body, apply_ln),
        grid=(n // TM,),
        in_specs=[row(d), const((1, d)), const((1, d)), const((d, IN_PAD))],
        out_specs=out_specs, out_shape=out_shapes,
        compiler_params=_cparams("parallel"), name="in_proj",
    )(x2d, g, b, w_pad)


CONV_HALO = 32


def _conv_body(ts, cur_ref, halo_ref, w_ref, b_ref, gg_ref, gb_ref, avg_ref, o_ref, hbuf):
    i = pl.program_id(1)
    cur = cur_ref[0]
    hal = halo_ref[0]
    h = cur[:, :W_GRP] * jax.nn.sigmoid(cur[:, W_GRP:])
    hh = hal[:, :W_GRP] * jax.nn.sigmoid(hal[:, W_GRP:])
    hbuf[0:CONV_HALO, :] = jnp.where(i > 0, hh, 0.0)
    hbuf[CONV_HALO:, :] = h
    acc = jnp.zeros((ts, W_GRP), F32)
    base = CONV_HALO - (CONV_WIDTH - 1)
    for j in range(CONV_WIDTH):
        acc = acc + w_ref[j:j + 1, :] * hbuf[base + j:base + j + ts, :]
    acc = acc + b_ref[...]
    avg = avg_ref[...].astype(BF16)
    mu = _dot_hilo(acc, avg)
    xc = acc - mu
    var = _dot_hilo(xc * xc, avg)
    y = xc * lax.rsqrt(var + LN_EPS) * gg_ref[...] + gb_ref[...]
    o_ref[0] = y * jax.nn.sigmoid(y)


def conv_mixer(conv_in, w, b, gg, gb, avg):
    bsz, s, _ = conv_in.shape
    ts = TM
    r = ts // CONV_HALO
    const = lambda shp: pl.BlockSpec(shp, lambda bi, i: (0, 0))
    return pl.pallas_call(
        functools.partial(_conv_body, ts),
        grid=(bsz, s // ts),
        in_specs=[pl.BlockSpec((1, ts, 2 * W_GRP), lambda bi, i: (bi, i, 0)),
                  pl.BlockSpec((1, CONV_HALO, 2 * W_GRP),
                               lambda bi, i: (bi, jnp.maximum(i * r - 1, 0), 0)),
                  const((CONV_WIDTH, W_GRP)), const((1, W_GRP)), const((1, W_GRP)),
                  const((1, W_GRP)), const((W_GRP, W_GRP))],
        out_specs=pl.BlockSpec((1, ts, W_GRP), lambda bi, i: (bi, i, 0)),
        out_shape=jax.ShapeDtypeStruct((bsz, s, W_GRP), F32),
        scratch_shapes=[pltpu.VMEM((ts + CONV_HALO, W_GRP), F32)],
        compiler_params=_cparams("parallel", "parallel"), name="conv_mixer",
    )(conv_in, conv_in, w, b, gg, gb, avg)


SHIFT_HALO = 8


def _shift_prev(buf, cur, halo, first):
    ts = cur.shape[0]
    buf[0:SHIFT_HALO, :] = jnp.where(first, 0.0, halo)
    buf[SHIFT_HALO:, :] = cur
    return buf[SHIFT_HALO - 1:SHIFT_HALO - 1 + ts, :]


def _rwkv_prep_body(has_vfirst, *refs):
    if has_vfirst:
        (c_ref, ch_ref, m_ref, mh_ref, vf_ref, mu_ref, w0_ref, a0_ref, wup_ref, aup_ref, gup_ref,
         kk_ref, ka_ref, ones_ref, vmu_ref, v0_ref, vup_ref,
         r_o, lw_o, k_o, v_o, a_o, b_o, g_o, buf, mbuf) = refs
    else:
        (c_ref, ch_ref, mu_ref, w0_ref, a0_ref, wup_ref, aup_ref, gup_ref,
         kk_ref, ka_ref, ones_ref,
         r_o, lw_o, k_o, v_o, a_o, b_o, g_o, buf) = refs
    first = pl.program_id(1) == 0
    cur = c_ref[0]
    prev = _shift_prev(buf, cur, ch_ref[0], first)
    y = cur + mu_ref[...] * (prev - cur)
    r = y[:, 0:256]
    k = y[:, 256:512]
    v = y[:, 512:768]
    lora = y[:, 768:1024]
    w = w0_ref[...] + _dot(jnp.tanh(lora), wup_ref[...], HI)
    a = jax.nn.sigmoid(a0_ref[...] + _dot(lora, aup_ref[...], HI))
    g = _dot(jax.nn.sigmoid(lora), gup_ref[...], HI)
    z = -w
    sp = jnp.maximum(z, 0.0) + jnp.log(1.0 + jnp.exp(-jnp.abs(z)))
    lw = -jnp.exp(-sp - 0.5)
    if has_vfirst:
        mc = m_ref[0]
        mprev = _shift_prev(mbuf, mc, mh_ref[0], first)
        vd = mc + vmu_ref[...] * (mprev - mc)
        v_mix = jax.nn.sigmoid(v0_ref[...] + _dot(vd, vup_ref[...], HI))
        v = v + (vf_ref[0] - v) * v_mix
    kk = k * kk_ref[...]
    n2 = _dot_hilo(kk * kk, ones_ref[...].astype(BF16))
    kk = kk / jnp.maximum(jnp.sqrt(n2), 1e-12)
    k2 = k * (1.0 + (a - 1.0) * ka_ref[...])
    r_o[0] = r
    lw_o[0] = lw
    k_o[0] = k2
    v_o[0] = v
    a_o[0] = -kk
    b_o[0] = kk * a
    g_o[0] = g


def rwkv_prep(cols, misc, v_first, p):
    bsz, s, c = cols.shape
    ts = TM
    has_vfirst = v_first is not None
    r8 = ts // SHIFT_HALO
    tile = lambda w: pl.BlockSpec((1, ts, w), lambda bi, i: (bi, i, 0))
    halo = lambda w: pl.BlockSpec((1, SHIFT_HALO, w), lambda bi, i: (bi, jnp.maximum(i * r8 - 1, 0), 0))
    const = lambda a: pl.BlockSpec(a.shape, lambda bi, i: (0,) * a.ndim)
    params = [p["mu"], p["w0"], p["a0"], p["wup"], p["aup"], p["gup"], p["kk"], p["ka"], p["ones"]]
    if has_vfirst:
        inputs = [cols, cols, misc, misc, v_first] + params + [p["vmu"], p["v0"], p["vup"]]
        in_specs = ([tile(c), halo(c), tile(LANES), halo(LANES), tile(W_GRP)]
                    + [const(a) for a in params + [p["vmu"], p["v0"], p["vup"]]])
        scratch = [pltpu.VMEM((ts + SHIFT_HALO, c), F32), pltpu.VMEM((ts + SHIFT_HALO, LANES), F32)]
    else:
        inputs = [cols, cols] + params
        in_specs = [tile(c), halo(c)] + [const(a) for a in params]
        scratch = [pltpu.VMEM((ts + SHIFT_HALO, c), F32)]
    out = jax.ShapeDtypeStruct((bsz, s, W_GRP), F32)
    return pl.pallas_call(
        functools.partial(_rwkv_prep_body, has_vfirst),
        grid=(bsz, s // ts),
        in_specs=in_specs,
        out_specs=[tile(W_GRP)] * 7, out_shape=[out] * 7,
        scratch_shapes=scratch,
        compiler_params=_cparams("parallel", "parallel"), name="rwkv_prep",
    )(*inputs)


def _block_diag(x, headmask):
    return jnp.concatenate([x] * N_HEADS, axis=0) * headmask


def _wkv_body(ts, r_ref, lw_ref, k_ref, v_ref, a_ref, b_ref, g_ref, rk_ref, gg_ref, gb_ref,
              avg_ref, ones_ref, o_ref, st_ref):
    C = WKV_CHUNK
    n = N_HEADS * C

    @pl.when(pl.program_id(1) == 0)
    def _():
        st_ref[...] = jnp.zeros_like(st_ref)

    ri = lax.broadcasted_iota(jnp.int32, (n, n), 0)
    ci = lax.broadcasted_iota(jnp.int32, (n, n), 1)
    head_f = jnp.where((ri // C) == (ci // HEAD_DIM), 1.0, 0.0)
    head_b = head_f.astype(BF16)
    lag = jnp.where((ri // C) == (ci // C), (ri % C) - (ci % C), -1)
    strict = lag > 0
    incl = lag >= 0
    eye = ri == ci
    eye_f = jnp.where(eye, 1.0, 0.0)
    tr = lax.broadcasted_iota(jnp.int32, (C, C), 0)
    tc = lax.broadcasted_iota(jnp.int32, (C, C), 1)
    tri = jnp.where(tc <= tr, 1.0, 0.0).astype(F32)
    cast = lambda x: x.astype(BF16)
    G = ts // C
    split = lambda ref: ref[0].reshape(G, C, W_GRP)
    tile_heads = lambda x: jnp.concatenate([x] * N_HEADS, axis=1)
    bd16 = lambda x: tile_heads(cast(x)) * head_b[None]
    bmm = lambda x, y: lax.dot_general(x, y, (((2,), (1,)), ((0,), (0,))), preferred_element_type=F32)
    bmm_nt = lambda x, y: lax.dot_general(x, y, (((2,), (2,)), ((0,), (0,))), preferred_element_type=F32)

    lw = split(lw_ref)
    cum = lax.dot_general(jnp.broadcast_to(tri[None], (G, C, C)), lw, (((2,), (1,)), ((0,), (0,))),
                          preferred_element_type=F32, precision=HI)
    cum_c = cum[:, C - 1:C, :]
    e_in = jnp.exp(cum)
    e_neg = jnp.exp(-cum)
    e_tail = jnp.exp(cum_c - cum)
    b_c = split(b_ref)
    k_c = split(k_ref)
    a_t = bd16(split(a_ref) * jnp.exp(cum - lw))
    r_t = bd16(split(r_ref) * e_in)
    b_t = bd16(b_c * e_neg)
    k_t = bd16(k_c * e_neg)
    v_bd = bd16(split(v_ref))
    bh_t = cast(jnp.swapaxes(tile_heads(b_c * e_tail) * head_f[None], 1, 2))
    kh_t = cast(jnp.swapaxes(tile_heads(k_c * e_tail) * head_f[None], 1, 2))
    a_ab = jnp.where(strict[None], bmm_nt(a_t, b_t), 0.0)
    a_ak = cast(jnp.where(strict[None], bmm_nt(a_t, k_t), 0.0))
    a_rb = cast(jnp.where(incl[None], bmm_nt(r_t, b_t), 0.0))
    a_rk = cast(jnp.where(incl[None], bmm_nt(r_t, k_t), 0.0))
    t_inv = eye_f[None] + a_ab
    pw = cast(a_ab)
    for _ in range(int(math.log2(C)) - 1):
        pw = cast(bmm(pw, pw))
        t_inv = t_inv + bmm(cast(t_inv), pw)
    t16 = cast(t_inv)
    ta = cast(bmm(t16, a_t))
    u0 = bmm(cast(bmm(t16, a_ak)), v_bd)
    o0 = bmm(a_rk, v_bd)
    s0 = bmm(kh_t, v_bd)
    o_lhs = jnp.concatenate([r_t, a_rb], axis=2)
    w_col = jnp.sum(jnp.where(eye[None], jnp.exp(cum_c), 0.0), axis=2, keepdims=True)

    st = st_ref[...]
    outs = []
    for g in range(G):
        st16 = cast(st)
        u = cast(_dot(ta[g], st16) + u0[g])
        o_bd = _dot(o_lhs[g], jnp.concatenate([st16, u], axis=0)) + o0[g]
        st = w_col[g] * st + _dot(bh_t[g], u) + s0[g]
        outs.append(o_bd[0:C] + o_bd[C:2 * C] + o_bd[2 * C:3 * C] + o_bd[3 * C:4 * C])
    st_ref[...] = st

    o = jnp.concatenate(outs, axis=0)
    avg = avg_ref[...].astype(BF16)
    mu = _dot_hilo(o, avg)
    xc = o - mu
    var = _dot_hilo(xc * xc, avg)
    on = xc * lax.rsqrt(var + RWKV_GN_EPS) * gg_ref[...] + gb_ref[...]
    r = r_ref[0]
    k = k_ref[0]
    v = v_ref[0]
    bonus = _dot_hilo(r * k * rk_ref[...], ones_ref[...].astype(BF16)) * v
    o_ref[0] = (on + bonus) * g_ref[0]


def wkv_scan(r, lw, k, v, a, b, g, rk, gg, gb, avg, ones):
    bsz, s, _ = r.shape
    ts = WKV_GROUP * WKV_CHUNK
    tile = pl.BlockSpec((1, ts, W_GRP), lambda bi, i: (bi, i, 0))
    const = lambda arr: pl.BlockSpec(arr.shape, lambda bi, i: (0,) * arr.ndim)
    return pl.pallas_call(
        functools.partial(_wkv_body, ts),
        grid=(bsz, s // ts),
        in_specs=[tile] * 7 + [const(x) for x in (rk, gg, gb, avg, ones)],
        out_specs=tile, out_shape=jax.ShapeDtypeStruct((bsz, s, W_GRP), F32),
        scratch_shapes=[pltpu.VMEM((N_HEADS * WKV_CHUNK, W_GRP), F32)],
        compiler_params=_cparams("parallel", "arbitrary"), name="wkv_scan",
    )(r, lw, k, v, a, b, g, rk, gg, gb, avg, ones)


def _gmlp_body(ts, x_ref, g_ref, b_ref, ws_ref, bs_ref, o_ref):
    x = x_ref[...]
    u = jax.nn.gelu(x[:, :W_GRP])
    v = _ln(jax.nn.gelu(x[:, W_GRP:]), g_ref[...], b_ref[...]).astype(BF16)
    tr = lax.broadcasted_iota(jnp.int32, (GMLP_CHUNK, GMLP_CHUNK), 0)
    tc = lax.broadcasted_iota(jnp.int32, (GMLP_CHUNK, GMLP_CHUNK), 1)
    ws = [jnp.where(tc <= tr, ws_ref[h], 0.0).astype(BF16) for h in range(N_HEADS)]
    for c in range(ts // GMLP_CHUNK):
        rows = slice(c * GMLP_CHUNK, (c + 1) * GMLP_CHUNK)
        mixed = [_dot(ws[h], v[rows, h * HEAD_DIM:(h + 1) * HEAD_DIM]) + bs_ref[h]
                 for h in range(N_HEADS)]
        o_ref[rows, :] = u[rows, :] * jnp.concatenate(mixed, axis=-1)


def gmlp_mixer(x2d, g, b, ws, bs):
    n = x2d.shape[0]
    ts = TM
    const = lambda a: pl.BlockSpec(a.shape, lambda i: (0,) * a.ndim)
    return pl.pallas_call(
        functools.partial(_gmlp_body, ts),
        grid=(n // ts,),
        in_specs=[pl.BlockSpec((ts, 2 * W_GRP), lambda i: (i, 0)), const(g), const(b), const(ws), const(bs)],
        out_specs=pl.BlockSpec((ts, W_GRP), lambda i: (i, 0)),
        out_shape=jax.ShapeDtypeStruct((n, W_GRP), F32),
        compiler_params=_cparams("parallel"), name="gmlp_mixer",
    )(x2d, g, b, ws, bs)


def _swap_halves(x, lane):
    w = x.shape[-1]
    half = HEAD_DIM // 2
    fwd = pltpu.roll(x, w - half, 1)
    bwd = pltpu.roll(x, half, 1)
    return jnp.where((lane % HEAD_DIM) < half, fwd, bwd)


def _nsa_prep_body(q_ref, kv_ref, m_ref, cos_ref, sin_ref,
                   q_o, qr_o, kc_o, vc_o, ks_o, vs_o, kw_o, vw_o, g_o):
    scale = HEAD_DIM ** -0.5
    q = q_ref[0]
    kv = kv_ref[0]
    cos = cos_ref[...]
    sin = sin_ref[...]
    lane_q = lax.broadcasted_iota(jnp.int32, q.shape, 1)
    q_rot = q * cos[:, :W_GRP] + _swap_halves(q, lane_q) * sin[:, :W_GRP]
    lane_kv = lax.broadcasted_iota(jnp.int32, kv.shape, 1)
    kv_rot = kv * cos + _swap_halves(kv, lane_kv) * sin
    q_o[0] = (q * scale).astype(BF16)
    qr_o[0] = (q_rot * (scale * LOG2E)).astype(BF16)
    kc_o[0] = kv[:, 0:64]
    vc_o[0] = kv[:, 64:128]
    ks_o[0] = kv_rot[:, 128:192].astype(BF16)
    lane = lax.broadcasted_iota(jnp.int32, (kv.shape[0], LANES), 1)
    vs_first = pltpu.roll(kv[:, 128:256], HEAD_DIM, 1)
    vs_o[0] = jnp.where(lane < HEAD_DIM, vs_first, jnp.where(lane == HEAD_DIM, 1.0, 0.0)).astype(BF16)
    kw_o[0] = kv_rot[:, 256:320].astype(BF16)
    vw_o[0] = kv[:, 320:384].astype(BF16)
    g_o[0] = jax.nn.sigmoid(m_ref[0])


def nsa_prep(q, kv, misc, cos_t, sin_t):
    bsz, s, _ = q.shape
    ts = TM
    tile = lambda w: pl.BlockSpec((1, ts, w), lambda bi, i: (bi, i, 0))
    tab = pl.BlockSpec((ts, 6 * HEAD_DIM), lambda bi, i: (i, 0))
    sd = lambda w, dt: jax.ShapeDtypeStruct((bsz, s, w), dt)
    return pl.pallas_call(
        _nsa_prep_body,
        grid=(bsz, s // ts),
        in_specs=[tile(W_GRP), tile(6 * HEAD_DIM), tile(LANES), tab, tab],
        out_specs=[tile(W_GRP), tile(W_GRP), tile(HEAD_DIM), tile(HEAD_DIM), tile(HEAD_DIM),
                   tile(LANES), tile(HEAD_DIM), tile(HEAD_DIM), tile(LANES)],
        out_shape=[sd(W_GRP, BF16), sd(W_GRP, BF16), sd(HEAD_DIM, F32), sd(HEAD_DIM, F32),
                   sd(HEAD_DIM, BF16), sd(LANES, BF16), sd(HEAD_DIM, BF16), sd(HEAD_DIM, BF16),
                   sd(LANES, F32)],
        compiler_params=_cparams("parallel", "parallel"), name="nsa_prep",
    )(q, kv, misc, cos_t, sin_t)


def _compress_body(kc_ref, vc_ref, pe_ref, w1_ref, w2_ref, ko_ref, vo_ref):
    half = CMP_STRIDE * HEAD_DIM
    for j, (c_ref, o_ref) in enumerate(((kc_ref, ko_ref), (vc_ref, vo_ref))):
        c = c_ref[0].astype(BF16)
        w1 = w1_ref[j]
        lo = _dot(c, w1[:half])
        hi = _dot(c, w1[half:])
        nb = hi.shape[0]
        hi_next = pltpu.roll(hi, nb - 1, 0)
        pe = jnp.broadcast_to(pe_ref[j], (8, 2 * half))
        pe_term = _dot(pe, w1.astype(F32), HI)[0:1]
        h = jax.nn.gelu(lo + hi_next + pe_term)
        o_ref[0] = _dot(h.astype(BF16), w2_ref[j]).astype(BF16)


def nsa_compress(kc_r, vc_r, pe, w1, w2):
    bsz, nb, w = kc_r.shape
    blk = pl.BlockSpec((1, nb, w), lambda bi: (bi, 0, 0))
    const = lambda a: pl.BlockSpec(a.shape, lambda bi: (0,) * a.ndim)
    out = jax.ShapeDtypeStruct((bsz, nb, HEAD_DIM), BF16)
    ospec = pl.BlockSpec((1, nb, HEAD_DIM), lambda bi: (bi, 0, 0))
    return pl.pallas_call(
        _compress_body, grid=(bsz,),
        in_specs=[blk, blk, const(pe), const(w1), const(w2)],
        out_specs=[ospec, ospec], out_shape=[out, out],
        compiler_params=_cparams("parallel"), name="nsa_compress",
    )(kc_r, vc_r, pe, w1, w2)


def _stack_heads(x):
    return jnp.concatenate([x[:, h * HEAD_DIM:(h + 1) * HEAD_DIM] for h in range(N_HEADS)], axis=0)


def _nsa_attn_body(n_top, q_ref, qr_ref, kc_ref, vc_ref, ks_ref, vs_ref, kw_ref, vw_ref, g_ref,
                   ov_ref, o_ref):
    T = Q_BLOCK
    i = pl.program_id(1)
    t0 = i * T
    qs = _stack_heads(q_ref[0])
    qrs = _stack_heads(qr_ref[0])
    pos = t0 + lax.broadcasted_iota(jnp.int32, (T, 1), 0)
    tile4 = lambda x: jnp.concatenate([x] * N_HEADS, axis=0)

    kc = kc_ref[0]
    n_cmp = kc.shape[0]
    cmp_end = lax.broadcasted_iota(jnp.int32, (1, n_cmp), 1) * CMP_STRIDE + (CMP_LEN - 1)
    valid_c = tile4(jnp.where(cmp_end <= pos, 1.0, 0.0))
    s_c = jnp.where(valid_c > 0.5, _dot_nt(qs, kc), NEG)
    p_c = jnp.exp(s_c - jnp.max(s_c, axis=-1, keepdims=True)) * valid_c
    p_c = p_c / jnp.maximum(jnp.sum(p_c, axis=-1, keepdims=True), 1e-20)
    o_c = _dot(p_c.astype(BF16), vc_ref[0])
    p_sum = p_c[0:T] + p_c[T:2 * T] + p_c[2 * T:3 * T] + p_c[3 * T:4 * T]
    imp_t = _dot_nt(ov_ref[...], p_sum, HI)

    n_sel = imp_t.shape[0]
    jj = lax.broadcasted_iota(jnp.int32, (n_sel, T), 0)
    blk = (t0 + lax.broadcasted_iota(jnp.int32, (1, T), 1)) // SEL_LEN
    val = jnp.where(jj == blk, 3e38, jnp.where(jj == 0, 3e38, jnp.where(jj <= blk, imp_t, -1.0)))
    sel_t = jnp.zeros((n_sel, T), F32)
    for _ in range(n_top):
        mx = jnp.max(val, axis=0, keepdims=True)
        idx = jnp.min(jnp.where(val == mx, jj, n_sel), axis=0, keepdims=True)
        hit = jj == idx
        sel_t = jnp.where(hit, 1.0, sel_t)
        val = jnp.where(hit, -2.0, val)
    sel = sel_t.T

    KT = SEL_KT
    n_tiles = (t0 + T + KT - 1) // KT
    er = lax.broadcasted_iota(jnp.int32, (n_sel, KT), 0)
    ec = lax.broadcasted_iota(jnp.int32, (n_sel, KT), 1)
    expand = jnp.where(er == ec // SEL_LEN, 1.0, 0.0).astype(BF16)
    kcol = lax.broadcasted_iota(jnp.int32, (1, KT), 1)

    def sel_tile(jt, carry):
        m, acc = carry
        k0 = pl.multiple_of(jt * KT, KT)
        kt = ks_ref[0, pl.ds(k0, KT), :]
        vt = vs_ref[0, pl.ds(k0, KT), :]
        shift = (n_sel - jt * (KT // SEL_LEN)) % n_sel
        picked = _dot(pltpu.roll(sel, shift, 1).astype(BF16), expand)
        bias = tile4((jnp.where(k0 + kcol <= pos, picked, 0.0) - 1.0) * (-NEG))
        s = _dot_nt(qrs, kt) + bias
        m_new = jnp.maximum(m, jnp.max(s, axis=-1, keepdims=True))
        p = jnp.exp2(s - m_new)
        acc = jnp.exp2(m - m_new) * acc + _dot(p.astype(BF16), vt)
        return m_new, acc

    init = (jnp.full((N_HEADS * T, 1), NEG, F32), jnp.zeros((N_HEADS * T, LANES), F32))
    _, acc_s = lax.fori_loop(0, n_tiles, sel_tile, init)
    o_s = acc_s[:, :HEAD_DIM] / acc_s[:, HEAD_DIM:HEAD_DIM + 1]

    WK = WINDOW + T
    start = pl.multiple_of(jnp.maximum(t0 - WINDOW, 0), T)
    kw = kw_ref[0, pl.ds(start, WK), :]
    vw = vw_ref[0, pl.ds(start, WK), :]
    kpos_w = start + lax.broadcasted_iota(jnp.int32, (1, WK), 1)
    bias_w = tile4(jnp.where(kpos_w <= pos, jnp.where(kpos_w > pos - WINDOW, 0.0, NEG), NEG))
    s_w = _dot_nt(qrs, kw) + bias_w
    p_w = jnp.exp2(s_w - jnp.max(s_w, axis=-1, keepdims=True))
    p_w = p_w / jnp.sum(p_w, axis=-1, keepdims=True)
    o_w = _dot(p_w.astype(BF16), vw)

    g = g_ref[0]
    outs = []
    for h in range(N_HEADS):
        rows = slice(h * T, (h + 1) * T)
        outs.append(g[:, 3 * h:3 * h + 1] * o_c[rows] + g[:, 3 * h + 1:3 * h + 2] * o_s[rows]
                    + g[:, 3 * h + 2:3 * h + 3] * o_w[rows])
    o_ref[0] = jnp.concatenate(outs, axis=-1)


def nsa_attention(q, qr, k_cmp, v_cmp, ks, vs, kw, vw, gates, overlap):
    bsz, s, _ = q.shape
    n_top = min(SEL_TOPN, s // SEL_LEN)
    n_cmp = k_cmp.shape[1]
    qtile = lambda w: pl.BlockSpec((1, Q_BLOCK, w), lambda bi, i: (bi, i, 0))
    full = lambda rows, w: pl.BlockSpec((1, rows, w), lambda bi, i: (bi, 0, 0))
    return pl.pallas_call(
        functools.partial(_nsa_attn_body, n_top),
        grid=(bsz, s // Q_BLOCK),
        in_specs=[qtile(W_GRP), qtile(W_GRP), full(n_cmp, HEAD_DIM), full(n_cmp, HEAD_DIM),
                  full(s, HEAD_DIM), full(s, LANES), full(s, HEAD_DIM), full(s, HEAD_DIM),
                  qtile(LANES), pl.BlockSpec(overlap.shape, lambda bi, i: (0, 0))],
        out_specs=qtile(W_GRP), out_shape=jax.ShapeDtypeStruct((bsz, s, W_GRP), F32),
        compiler_params=_cparams("parallel", "parallel"), name="nsa_attention",
    )(q, qr, k_cmp, v_cmp, ks, vs, kw, vw, gates, overlap)


def _outproj_body(alpha, a_ref, b_ref, c_ref, d_ref, x_ref, w_ref, g_ref, bb_ref, o_ref):
    mix = jnp.zeros(x_ref.shape, F32)
    for j, part in enumerate((a_ref, b_ref, c_ref, d_ref)):
        mix = mix + _dot(part[...].astype(BF16), w_ref[j * W_GRP:(j + 1) * W_GRP, :])
    o_ref[...] = _ln(alpha * x_ref[...] + mix, g_ref[...], bb_ref[...])


def out_proj(parts, x2d, w, g, b, alpha):
    n, d = x2d.shape
    row = lambda w_: pl.BlockSpec((TM, w_), lambda i: (i, 0))
    const = lambda a: pl.BlockSpec(a.shape, lambda i: (0,) * a.ndim)
    return pl.pallas_call(
        functools.partial(_outproj_body, alpha),
        grid=(n // TM,),
        in_specs=[row(W_GRP)] * 4 + [row(d), const(w), const(g), const(b)],
        out_specs=row(d), out_shape=jax.ShapeDtypeStruct((n, d), F32),
        compiler_params=_cparams("parallel"), name="out_proj",
    )(*parts, x2d, w, g, b)


def _memkv_body(m_ref, wk_ref, wv_ref, k_o, v_o):
    mb = m_ref[...].astype(BF16)
    k_o[...] = _dot(mb, wk_ref[...]).astype(BF16)
    v_o[...] = _dot(mb, wv_ref[...]).astype(BF16)


def mem_kv(mem2d, wk, wv):
    n, d = mem2d.shape
    full = lambda a: pl.BlockSpec(a.shape, lambda i: (0, 0))
    out = jax.ShapeDtypeStruct((n, d), BF16)
    return pl.pallas_call(
        _memkv_body, grid=(1,),
        in_specs=[full(mem2d), full(wk), full(wv)],
        out_specs=[pl.BlockSpec((n, d), lambda i: (0, 0))] * 2, out_shape=[out, out],
        compiler_params=_cparams("arbitrary"), name="mem_kv",
    )(mem2d, wk, wv)


def _xattn_body(alpha, x_ref, k_ref, v_ref, wq_ref, wo_ref, g_ref, b_ref, o_ref):
    x = x_ref[0]
    d = x.shape[-1]
    hd = d // N_MEM_HEADS
    q = (_dot(x.astype(BF16), wq_ref[...]) * (hd ** -0.5)).astype(BF16)
    k = k_ref[0]
    v = v_ref[0]
    heads = []
    for h in range(N_MEM_HEADS):
        cs = slice(h * hd, (h + 1) * hd)
        s = _dot_nt(q[:, cs], k[:, cs])
        p = jnp.exp(s - jnp.max(s, axis=-1, keepdims=True))
        p = p / jnp.sum(p, axis=-1, keepdims=True)
        heads.append(_dot(p.astype(BF16), v[:, cs]))
    o = jnp.concatenate(heads, axis=-1).astype(BF16)
    o_ref[0] = _ln(alpha * x + _dot(o, wo_ref[...]), g_ref[...], b_ref[...])


def cross_attention(x, k, v, wq, wo, g, b, alpha):
    bsz, s, d = x.shape
    m = k.shape[1]
    const = lambda a: pl.BlockSpec(a.shape, lambda bi, i: (0,) * a.ndim)
    return pl.pallas_call(
        functools.partial(_xattn_body, alpha),
        grid=(bsz, s // TM),
        in_specs=[pl.BlockSpec((1, TM, d), lambda bi, i: (bi, i, 0)),
                  pl.BlockSpec((1, m, d), lambda bi, i: (bi, 0, 0)),
                  pl.BlockSpec((1, m, d), lambda bi, i: (bi, 0, 0)),
                  const(wq), const(wo), const(g), const(b)],
        out_specs=pl.BlockSpec((1, TM, d), lambda bi, i: (bi, i, 0)),
        out_shape=jax.ShapeDtypeStruct((bsz, s, d), F32),
        compiler_params=_cparams("parallel", "parallel"), name="cross_attention",
    )(x, k, v, wq, wo, g, b)


def _router_body(x_ref, rw_ref, rb_ref, o_ref):
    per_grp = N_EXPERTS // N_EXPERT_GROUPS
    x = x_ref[...]
    logits = _dot(x, rw_ref[...], HI)
    lane = lax.broadcasted_iota(jnp.int32, logits.shape, 1)
    real = lane < N_EXPERTS
    lg = jnp.where(real, logits, NEG)
    ex = jnp.where(real, jnp.exp(lg - jnp.max(lg, axis=-1, keepdims=True)), 0.0)
    probs = ex / jnp.sum(ex, axis=-1, keepdims=True)
    sel = probs + rb_ref[...]
    sub = lane % per_grp

    def grot(v, k):
        ahead = pltpu.roll(v, LANES - k, 1)
        behind = pltpu.roll(v, per_grp - k, 1)
        return jnp.where(sub + k < per_grp, ahead, behind)

    others = jnp.maximum(jnp.maximum(grot(sel, 1), grot(sel, 2)), grot(sel, 3))
    pair = sel + others
    gscore = jnp.maximum(jnp.maximum(pair, grot(pair, 1)), jnp.maximum(grot(pair, 2), grot(pair, 3)))
    gs = jnp.where(real, gscore, -jnp.inf)
    grp = lane // per_grp
    g_idx = jnp.min(jnp.where(gs == jnp.max(gs, axis=-1, keepdims=True), grp, N_EXPERT_GROUPS),
                    axis=-1, keepdims=True)
    cand = jnp.where(grp == g_idx, jnp.where(real, sel, -jnp.inf), -jnp.inf)
    i1 = jnp.min(jnp.where(cand == jnp.max(cand, axis=-1, keepdims=True), lane, LANES),
                 axis=-1, keepdims=True)
    cand2 = jnp.where(lane == i1, -jnp.inf, cand)
    i2 = jnp.min(jnp.where(cand2 == jnp.max(cand2, axis=-1, keepdims=True), lane, LANES),
                 axis=-1, keepdims=True)
    w_sel = jnp.where(lane == i1, probs, jnp.where(lane == i2, probs, 0.0))
    o_ref[...] = w_sel / jnp.sum(w_sel, axis=-1, keepdims=True)


def moe_router(x2d, rw_pad, rb_pad):
    n, d = x2d.shape
    const = lambda a: pl.BlockSpec(a.shape, lambda i: (0, 0))
    return pl.pallas_call(
        _router_body, grid=(n // TM,),
        in_specs=[pl.BlockSpec((TM, d), lambda i: (i, 0)), const(rw_pad), const(rb_pad)],
        out_specs=pl.BlockSpec((TM, LANES), lambda i: (i, 0)),
        out_shape=jax.ShapeDtypeStruct((n, LANES), F32),
        compiler_params=_cparams("parallel"), name="moe_router",
    )(x2d, rw_pad, rb_pad)


def _moe_body(alpha, x_ref, gate_ref, wgu_ref, wd_ref, g_ref, b_ref, o_ref, xb_ref, acc_ref):
    e = pl.program_id(1)

    @pl.when(e == 0)
    def _():
        xb_ref[...] = x_ref[...].astype(BF16)
        acc_ref[...] = jnp.zeros_like(acc_ref)

    de = wd_ref.shape[1]
    gu = _dot(xb_ref[...], wgu_ref[0])
    gt = gu[:, :de]
    h = (gt * jax.nn.sigmoid(gt) * gu[:, de:]).astype(BF16)
    gates = gate_ref[...]
    lane = lax.broadcasted_iota(jnp.int32, gates.shape, 1)
    gcol = jnp.sum(jnp.where(lane == e, gates, 0.0), axis=-1, keepdims=True)
    acc_ref[...] += gcol * _dot(h, wd_ref[0])

    @pl.when(e == pl.num_programs(1) - 1)
    def _():
        o_ref[...] = _ln(alpha * x_ref[...] + acc_ref[...], g_ref[...], b_ref[...])


def moe_ffn(x2d, gates, wgu, wd, g, b, alpha):
    n, d = x2d.shape
    ne, de, _ = wd.shape
    tm = min(MOE_TM, n)
    const = lambda a: pl.BlockSpec(a.shape, lambda i, e: (0,) * a.ndim)
    return pl.pallas_call(
        functools.partial(_moe_body, alpha),
        grid=(n // tm, ne),
        in_specs=[pl.BlockSpec((tm, d), lambda i, e: (i, 0)),
                  pl.BlockSpec((tm, LANES), lambda i, e: (i, 0)),
                  pl.BlockSpec((1, d, 2 * de), lambda i, e: (e, 0, 0)),
                  pl.BlockSpec((1, de, d), lambda i, e: (e, 0, 0)),
                  const(g), const(b)],
        out_specs=pl.BlockSpec((tm, d), lambda i, e: (i, 0)),
        out_shape=jax.ShapeDtypeStruct((n, d), F32),
        scratch_shapes=[pltpu.VMEM((tm, d), BF16), pltpu.VMEM((tm, d), F32)],
        compiler_params=_cparams("parallel", "arbitrary"), name="moe_ffn",
    )(x2d, gates, wgu, wd, g, b)


def _rope_tables(s):
    inv = ROPE_THETA ** (-jnp.arange(0, HEAD_DIM, 2, dtype=F32) / HEAD_DIM)
    ang = jnp.arange(s, dtype=F32)[:, None] * inv[None, :]
    cos, sin = jnp.cos(ang), jnp.sin(ang)
    cos_h = jnp.concatenate([cos, cos], axis=-1)
    sin_h = jnp.concatenate([-sin, sin], axis=-1)
    return jnp.tile(cos_h, (1, 6)), jnp.tile(sin_h, (1, 6))


def _overlap_matrix(s, n_cmp_pad):
    n_sel = s // SEL_LEN
    assert n_sel <= LANES
    cmp_start = jnp.arange(n_cmp_pad) * CMP_STRIDE
    sel_start = jnp.arange(LANES) * SEL_LEN
    ov = jnp.clip(jnp.minimum(cmp_start[None, :] + CMP_LEN, sel_start[:, None] + SEL_LEN)
                  - jnp.maximum(cmp_start[None, :], sel_start[:, None]), 0, None).astype(F32) / CMP_LEN
    n_cmp = s // CMP_STRIDE - (CMP_LEN // CMP_STRIDE - 1)
    real = (jnp.arange(n_cmp_pad)[None, :] < n_cmp) & (jnp.arange(LANES)[:, None] < n_sel)
    return jnp.where(real, ov, 0.0)


def _pad_rows(w, lo, total):
    out = jnp.zeros((total, w.shape[1]), w.dtype)
    return out.at[lo:lo + w.shape[0]].set(w)


def kernel(x, mem, ln_in_g, ln_in_b, w_in, w_out, conv_w, conv_b, conv_gn_g, conv_gn_b, rwkv_mu, rwkv_w0, rwkv_w_up, rwkv_a0, rwkv_a_up, rwkv_g_up, rwkv_k_k, rwkv_k_a, rwkv_r_k, rwkv_gn_g, rwkv_gn_b, rwkv_v_down, rwkv_v_mu, rwkv_v0, rwkv_v_up, gmlp_ln_g, gmlp_ln_b, gmlp_w_s, gmlp_b_s, nsa_pe_k, nsa_w1_k, nsa_w2_k, nsa_pe_v, nsa_w1_v, nsa_w2_v, ln1_g, ln1_b, xa_wq, xa_wk, xa_wv, xa_wo, ln2_g, ln2_b, router_w, router_bias, moe_w_gate, moe_w_up, moe_w_down, ln3_g, ln3_b):
    bsz, s, d = x.shape
    depth = w_in.shape[0]
    n = bsz * s
    alpha = (2 * depth) ** 0.25
    row = lambda a: a.reshape(1, -1)

    cos_t, sin_t = _rope_tables(s)
    n_blk = s // CMP_STRIDE
    overlap = _overlap_matrix(s, n_blk)
    avg64 = _group_avg_matrix(W_GRP, HEAD_DIM)
    ones64 = avg64 * HEAD_DIM
    rw_pad = jnp.zeros((d, LANES), F32).at[:, :N_EXPERTS].set(router_w)
    rb_pad = jnp.zeros((1, LANES), F32).at[0, :N_EXPERTS].set(router_bias)
    mem2d = mem.reshape(bsz * mem.shape[1], d)

    xs = x.reshape(n, d)
    v_first = None
    for l in range(depth):
        w_pad = jnp.zeros((d, IN_PAD), F32).at[:, :w_in.shape[2]].set(w_in[l])
        if l > 0:
            lo = IN_SPLITS[-1][1] + MISC_VD_OFF
            w_pad = w_pad.at[:, lo:lo + rwkv_v_down.shape[2]].set(rwkv_v_down[l - 1])
        outs = in_proj(xs, row(ln_in_g), row(ln_in_b), w_pad.astype(BF16), apply_ln=(l == 0))
        if l == 0:
            xs, outs = outs[0], outs[1:]
        conv_in, rwkv_in, gmlp_in, q_in, kv_in, misc = outs
        b3 = lambda a: a.reshape(bsz, s, a.shape[-1])

        out_a = conv_mixer(b3(conv_in), conv_w[l], row(conv_b[l]), row(conv_gn_g[l]), row(conv_gn_b[l]), avg64)

        rp = {"mu": row(rwkv_mu[l]), "w0": row(rwkv_w0[l]), "a0": row(rwkv_a0[l]),
              "wup": _pad_rows(rwkv_w_up[l], 0, W_GRP), "aup": _pad_rows(rwkv_a_up[l], 64, W_GRP),
              "gup": _pad_rows(rwkv_g_up[l], 128, W_GRP),
              "kk": row(rwkv_k_k[l]), "ka": row(rwkv_k_a[l]), "ones": ones64}
        if l > 0:
            rp["vmu"] = jnp.zeros((1, LANES), F32).at[0, MISC_VD_OFF:MISC_VD_OFF + rwkv_v_mu.shape[1]].set(rwkv_v_mu[l - 1])
            rp["v0"] = row(rwkv_v0[l - 1])
            rp["vup"] = _pad_rows(rwkv_v_up[l - 1], MISC_VD_OFF, LANES)
        r_, lw_, k_, v_, a_, b_, g_ = rwkv_prep(b3(rwkv_in), b3(misc), v_first, rp)
        if l == 0:
            v_first = v_
        out_b = wkv_scan(r_, lw_, k_, v_, a_, b_, g_, row(rwkv_r_k[l]), row(rwkv_gn_g[l]),
                         row(rwkv_gn_b[l]), avg64, ones64)

        out_c = gmlp_mixer(gmlp_in, row(gmlp_ln_g[l]), row(gmlp_ln_b[l]), gmlp_w_s[l],
                           gmlp_b_s[l].reshape(N_HEADS, GMLP_CHUNK, 1))

        q_b, qr_b, kc, vc, ks, vs, kw, vw, gates = nsa_prep(b3(q_in), b3(kv_in), b3(misc), cos_t, sin_t)
        pe = jnp.stack([nsa_pe_k[l].reshape(1, -1), nsa_pe_v[l].reshape(1, -1)])
        w1 = jnp.stack([nsa_w1_k[l], nsa_w1_v[l]]).astype(BF16)
        w2 = jnp.stack([nsa_w2_k[l], nsa_w2_v[l]]).astype(BF16)
        k_cmp, v_cmp = nsa_compress(kc.reshape(bsz, n_blk, CMP_STRIDE * HEAD_DIM),
                                    vc.reshape(bsz, n_blk, CMP_STRIDE * HEAD_DIM), pe, w1, w2)
        out_d = nsa_attention(q_b, qr_b, k_cmp, v_cmp, ks, vs, kw, vw, gates, overlap)

        flat = lambda a: a.reshape(n, W_GRP)
        xs = out_proj((flat(out_a), flat(out_b), out_c, flat(out_d)), xs, w_out[l].astype(BF16),
                      row(ln1_g[l]), row(ln1_b[l]), alpha)

        mk, mv = mem_kv(mem2d, xa_wk[l].astype(BF16), xa_wv[l].astype(BF16))
        m_len = mem.shape[1]
        xs = cross_attention(xs.reshape(bsz, s, d), mk.reshape(bsz, m_len, d), mv.reshape(bsz, m_len, d),
                             xa_wq[l].astype(BF16), xa_wo[l].astype(BF16),
                             row(ln2_g[l]), row(ln2_b[l]), alpha).reshape(n, d)

        gate_w = moe_router(xs, rw_pad, rb_pad)
        wgu = jnp.concatenate([moe_w_gate[l], moe_w_up[l]], axis=-1).astype(BF16)
        xs = moe_ffn(xs, gate_w, wgu, moe_w_down[l].astype(BF16), row(ln3_g[l]), row(ln3_b[l]), alpha)
    return xs.reshape(bsz, s, d)
```

```python
import functools
import math

import jax
import jax.numpy as jnp
from jax import lax
from jax.experimental import pallas as pl
from jax.experimental.pallas import tpu as pltpu

F32 = jnp.float32
BF16 = jnp.bfloat16
HI = lax.Precision.HIGHEST

HEAD_DIM = 64
N_HEADS = 4
W_GRP = 256
CONV_WIDTH = 31
GMLP_CHUNK = 128
CMP_LEN = 32
CMP_STRIDE = 16
SEL_LEN = 64
SEL_TOPN = 16
WINDOW = 512
Q_BLOCK = 128
ROPE_THETA = 10000.0
N_MEM_HEADS = 4
N_EXPERTS = 16
N_EXPERT_GROUPS = 4
LN_EPS = 1e-5
RWKV_GN_EPS = 64e-5
NEG = -1e30
LOG2E = math.log2(math.e)
LANES = 128
WKV_CHUNK = 64
WKV_GROUP = 8

TM = 512
MOE_TM = 1024
MOE_BLK = 256
SEL_KT = 512
VMEM_LIMIT = 56 * 1024 * 1024


def _cparams(*sem):
    return pltpu.CompilerParams(dimension_semantics=sem, vmem_limit_bytes=VMEM_LIMIT)


def _ln(x, g, b, eps=LN_EPS):
    mu = jnp.mean(x, axis=-1, keepdims=True)
    xc = x - mu
    var = jnp.mean(xc * xc, axis=-1, keepdims=True)
    return xc * lax.rsqrt(var + eps) * g + b


def _dot(a, b, precision=None):
    return jnp.dot(a, b, preferred_element_type=F32, precision=precision)


def _dot_nt(a, b, precision=None):
    return lax.dot_general(a, b, (((1,), (1,)), ((), ())),
                           preferred_element_type=F32, precision=precision)


def _dot_hilo(x, w_bf16):
    hi = x.astype(BF16)
    lo = (x - hi.astype(F32)).astype(BF16)
    return _dot(hi, w_bf16) + _dot(lo, w_bf16)


def _group_avg_matrix(width, group):
    r = jnp.arange(width)[:, None] // group
    c = jnp.arange(width)[None, :] // group
    return jnp.where(r == c, 1.0 / group, 0.0).astype(F32)


IN_SPLITS = (("conv", 0, 512), ("rwkv", 512, 1536), ("gmlp", 1536, 2048),
             ("q", 2048, 2304), ("kv", 2304, 2688), ("misc", 2688, 2816))
IN_PAD = 2816
MISC_VD_OFF = 32


def _inproj_body(apply_ln, x_ref, g_ref, b_ref, w_ref, *outs):
    x = x_ref[...]
    if apply_ln:
        x = _ln(x, g_ref[...], b_ref[...])
        outs[0][...] = x
        outs = outs[1:]
    xb = x.astype(BF16)
    for o_ref, (_, lo, hi) in zip(outs, IN_SPLITS):
        o_ref[...] = _dot(xb, w_ref[:, lo:hi])


def in_proj(x2d, g, b, w_pad, apply_ln):
    n, d = x2d.shape
    row = lambda w: pl.BlockSpec((TM, w), lambda i: (i, 0))
    const = lambda s: pl.BlockSpec(s, lambda i: (0, 0))
    out_shapes = [jax.ShapeDtypeStruct((n, hi - lo), F32) for _, lo, hi in IN_SPLITS]
    out_specs = [row(hi - lo) for _, lo, hi in IN_SPLITS]
    if apply_ln:
        out_shapes = [jax.ShapeDtypeStruct((n, d), F32)] + out_shapes
        out_specs = [row(d)] + out_specs
    return pl.pallas_call(
        functools.partial(_inproj_body, apply_ln),
        grid=(n // TM,),
        in_specs=[row(d), const((1, d)), const((1, d)), const((d, IN_PAD))],
        out_specs=out_specs, out_shape=out_shapes,
        compiler_params=_cparams("parallel"), name="in_proj",
    )(x2d, g, b, w_pad)


CONV_HALO = 32


def _conv_body(ts, cur_ref, halo_ref, w_ref, b_ref, gg_ref, gb_ref, avg_ref, o_ref, hbuf):
    i = pl.program_id(1)
    cur = cur_ref[0]
    hal = halo_ref[0]
    h = cur[:, :W_GRP] * jax.nn.sigmoid(cur[:, W_GRP:])
    hh = hal[:, :W_GRP] * jax.nn.sigmoid(hal[:, W_GRP:])
    hbuf[0:CONV_HALO, :] = jnp.where(i > 0, hh, 0.0)
    hbuf[CONV_HALO:, :] = h
    acc = jnp.zeros((ts, W_GRP), F32)
    base = CONV_HALO - (CONV_WIDTH - 1)
    for j in range(CONV_WIDTH):
        acc = acc + w_ref[j:j + 1, :] * hbuf[base + j:base + j + ts, :]
    acc = acc + b_ref[...]
    avg = avg_ref[...].astype(BF16)
    mu = _dot_hilo(acc, avg)
    xc = acc - mu
    var = _dot_hilo(xc * xc, avg)
    y = xc * lax.rsqrt(var + LN_EPS) * gg_ref[...] + gb_ref[...]
    o_ref[0] = y * jax.nn.sigmoid(y)


def conv_mixer(conv_in, w, b, gg, gb, avg):
    bsz, s, _ = conv_in.shape
    ts = TM
    r = ts // CONV_HALO
    const = lambda shp: pl.BlockSpec(shp, lambda bi, i: (0, 0))
    return pl.pallas_call(
        functools.partial(_conv_body, ts),
        grid=(bsz, s // ts),
        in_specs=[pl.BlockSpec((1, ts, 2 * W_GRP), lambda bi, i: (bi, i, 0)),
                  pl.BlockSpec((1, CONV_HALO, 2 * W_GRP),
                               lambda bi, i: (bi, jnp.maximum(i * r - 1, 0), 0)),
                  const((CONV_WIDTH, W_GRP)), const((1, W_GRP)), const((1, W_GRP)),
                  const((1, W_GRP)), const((W_GRP, W_GRP))],
        out_specs=pl.BlockSpec((1, ts, W_GRP), lambda bi, i: (bi, i, 0)),
        out_shape=jax.ShapeDtypeStruct((bsz, s, W_GRP), F32),
        scratch_shapes=[pltpu.VMEM((ts + CONV_HALO, W_GRP), F32)],
        compiler_params=_cparams("parallel", "parallel"), name="conv_mixer",
    )(conv_in, conv_in, w, b, gg, gb, avg)


SHIFT_HALO = 8


def _shift_prev(buf, cur, halo, first):
    ts = cur.shape[0]
    buf[0:SHIFT_HALO, :] = jnp.where(first, 0.0, halo)
    buf[SHIFT_HALO:, :] = cur
    return buf[SHIFT_HALO - 1:SHIFT_HALO - 1 + ts, :]


def _rwkv_prep_body(has_vfirst, *refs):
    if has_vfirst:
        (c_ref, ch_ref, m_ref, mh_ref, vf_ref, mu_ref, w0_ref, a0_ref, wup_ref, aup_ref, gup_ref,
         kk_ref, ka_ref, ones_ref, vmu_ref, v0_ref, vup_ref,
         r_o, lw_o, k_o, v_o, a_o, b_o, g_o, buf, mbuf) = refs
    else:
        (c_ref, ch_ref, mu_ref, w0_ref, a0_ref, wup_ref, aup_ref, gup_ref,
         kk_ref, ka_ref, ones_ref,
         r_o, lw_o, k_o, v_o, a_o, b_o, g_o, buf) = refs
    first = pl.program_id(1) == 0
    cur = c_ref[0]
    prev = _shift_prev(buf, cur, ch_ref[0], first)
    y = cur + mu_ref[...] * (prev - cur)
    r = y[:, 0:256]
    k = y[:, 256:512]
    v = y[:, 512:768]
    lora = y[:, 768:1024]
    w = w0_ref[...] + _dot(jnp.tanh(lora), wup_ref[...], HI)
    a = jax.nn.sigmoid(a0_ref[...] + _dot(lora, aup_ref[...], HI))
    g = _dot(jax.nn.sigmoid(lora), gup_ref[...], HI)
    z = -w
    sp = jnp.maximum(z, 0.0) + jnp.log(1.0 + jnp.exp(-jnp.abs(z)))
    lw = -jnp.exp(-sp - 0.5)
    if has_vfirst:
        mc = m_ref[0]
        mprev = _shift_prev(mbuf, mc, mh_ref[0], first)
        vd = mc + vmu_ref[...] * (mprev - mc)
        v_mix = jax.nn.sigmoid(v0_ref[...] + _dot(vd, vup_ref[...], HI))
        v = v + (vf_ref[0] - v) * v_mix
    kk = k * kk_ref[...]
    n2 = _dot_hilo(kk * kk, ones_ref[...].astype(BF16))
    kk = kk / jnp.maximum(jnp.sqrt(n2), 1e-12)
    k2 = k * (1.0 + (a - 1.0) * ka_ref[...])
    r_o[0] = r
    lw_o[0] = lw
    k_o[0] = k2
    v_o[0] = v
    a_o[0] = -kk
    b_o[0] = kk * a
    g_o[0] = g


def rwkv_prep(cols, misc, v_first, p):
    bsz, s, c = cols.shape
    ts = TM
    has_vfirst = v_first is not None
    r8 = ts // SHIFT_HALO
    tile = lambda w: pl.BlockSpec((1, ts, w), lambda bi, i: (bi, i, 0))
    halo = lambda w: pl.BlockSpec((1, SHIFT_HALO, w), lambda bi, i: (bi, jnp.maximum(i * r8 - 1, 0), 0))
    const = lambda a: pl.BlockSpec(a.shape, lambda bi, i: (0,) * a.ndim)
    params = [p["mu"], p["w0"], p["a0"], p["wup"], p["aup"], p["gup"], p["kk"], p["ka"], p["ones"]]
    if has_vfirst:
        inputs = [cols, cols, misc, misc, v_first] + params + [p["vmu"], p["v0"], p["vup"]]
        in_specs = ([tile(c), halo(c), tile(LANES), halo(LANES), tile(W_GRP)]
                    + [const(a) for a in params + [p["vmu"], p["v0"], p["vup"]]])
        scratch = [pltpu.VMEM((ts + SHIFT_HALO, c), F32), pltpu.VMEM((ts + SHIFT_HALO, LANES), F32)]
    else:
        inputs = [cols, cols] + params
        in_specs = [tile(c), halo(c)] + [const(a) for a in params]
        scratch = [pltpu.VMEM((ts + SHIFT_HALO, c), F32)]
    out = jax.ShapeDtypeStruct((bsz, s, W_GRP), F32)
    return pl.pallas_call(
        functools.partial(_rwkv_prep_body, has_vfirst),
        grid=(bsz, s // ts),
        in_specs=in_specs,
        out_specs=[tile(W_GRP)] * 7, out_shape=[out] * 7,
        scratch_shapes=scratch,
        compiler_params=_cparams("parallel", "parallel"), name="rwkv_prep",
    )(*inputs)


def _block_diag(x, headmask):
    return jnp.concatenate([x] * N_HEADS, axis=0) * headmask


def _wkv_body(ts, r_ref, lw_ref, k_ref, v_ref, a_ref, b_ref, g_ref, rk_ref, gg_ref, gb_ref,
              avg_ref, ones_ref, o_ref, st_ref):
    C = WKV_CHUNK
    n = N_HEADS * C

    @pl.when(pl.program_id(1) == 0)
    def _():
        st_ref[...] = jnp.zeros_like(st_ref)

    ri = lax.broadcasted_iota(jnp.int32, (n, n), 0)
    ci = lax.broadcasted_iota(jnp.int32, (n, n), 1)
    head_f = jnp.where((ri // C) == (ci // HEAD_DIM), 1.0, 0.0)
    head_b = head_f.astype(BF16)
    lag = jnp.where((ri // C) == (ci // C), (ri % C) - (ci % C), -1)
    strict = lag > 0
    incl = lag >= 0
    eye = ri == ci
    eye_f = jnp.where(eye, 1.0, 0.0)
    tr = lax.broadcasted_iota(jnp.int32, (C, C), 0)
    tc = lax.broadcasted_iota(jnp.int32, (C, C), 1)
    tri = jnp.where(tc <= tr, 1.0, 0.0).astype(F32)
    cast = lambda x: x.astype(BF16)
    G = ts // C
    split = lambda ref: ref[0].reshape(G, C, W_GRP)
    tile_heads = lambda x: jnp.concatenate([x] * N_HEADS, axis=1)
    bd16 = lambda x: tile_heads(cast(x)) * head_b[None]
    bmm = lambda x, y: lax.dot_general(x, y, (((2,), (1,)), ((0,), (0,))), preferred_element_type=F32)
    bmm_nt = lambda x, y: lax.dot_general(x, y, (((2,), (2,)), ((0,), (0,))), preferred_element_type=F32)

    lw = split(lw_ref)
    cum = lax.dot_general(jnp.broadcast_to(tri[None], (G, C, C)), lw, (((2,), (1,)), ((0,), (0,))),
                          preferred_element_type=F32, precision=HI)
    cum_c = cum[:, C - 1:C, :]
    e_in = jnp.exp(cum)
    e_neg = jnp.exp(-cum)
    e_tail = jnp.exp(cum_c - cum)
    b_c = split(b_ref)
    k_c = split(k_ref)
    a_t = bd16(split(a_ref) * jnp.exp(cum - lw))
    r_t = bd16(split(r_ref) * e_in)
    b_t = bd16(b_c * e_neg)
    k_t = bd16(k_c * e_neg)
    v_bd = bd16(split(v_ref))
    bh_t = cast(jnp.swapaxes(tile_heads(b_c * e_tail) * head_f[None], 1, 2))
    kh_t = cast(jnp.swapaxes(tile_heads(k_c * e_tail) * head_f[None], 1, 2))
    a_ab = jnp.where(strict[None], bmm_nt(a_t, b_t), 0.0)
    a_ak = cast(jnp.where(strict[None], bmm_nt(a_t, k_t), 0.0))
    a_rb = cast(jnp.where(incl[None], bmm_nt(r_t, b_t), 0.0))
    a_rk = cast(jnp.where(incl[None], bmm_nt(r_t, k_t), 0.0))
    t_inv = eye_f[None] + a_ab
    pw = cast(a_ab)
    for _ in range(int(math.log2(C)) - 1):
        pw = cast(bmm(pw, pw))
        t_inv = t_inv + bmm(cast(t_inv), pw)
    t16 = cast(t_inv)
    ta = cast(bmm(t16, a_t))
    u0 = bmm(cast(bmm(t16, a_ak)), v_bd)
    o0 = bmm(a_rk, v_bd)
    s0 = bmm(kh_t, v_bd)
    o_lhs = jnp.concatenate([r_t, a_rb], axis=2)
    w_col = jnp.sum(jnp.where(eye[None], jnp.exp(cum_c), 0.0), axis=2, keepdims=True)

    st = st_ref[...]
    outs = []
    for g in range(G):
        st16 = cast(st)
        u = cast(_dot(ta[g], st16) + u0[g])
        o_bd = _dot(o_lhs[g], jnp.concatenate([st16, u], axis=0)) + o0[g]
        st = w_col[g] * st + _dot(bh_t[g], u) + s0[g]
        outs.append(o_bd[0:C] + o_bd[C:2 * C] + o_bd[2 * C:3 * C] + o_bd[3 * C:4 * C])
    st_ref[...] = st

    o = jnp.concatenate(outs, axis=0)
    avg = avg_ref[...].astype(BF16)
    mu = _dot_hilo(o, avg)
    xc = o - mu
    var = _dot_hilo(xc * xc, avg)
    on = xc * lax.rsqrt(var + RWKV_GN_EPS) * gg_ref[...] + gb_ref[...]
    r = r_ref[0]
    k = k_ref[0]
    v = v_ref[0]
    bonus = _dot_hilo(r * k * rk_ref[...], ones_ref[...].astype(BF16)) * v
    o_ref[0] = (on + bonus) * g_ref[0]


def wkv_scan(r, lw, k, v, a, b, g, rk, gg, gb, avg, ones):
    bsz, s, _ = r.shape
    ts = WKV_GROUP * WKV_CHUNK
    tile = pl.BlockSpec((1, ts, W_GRP), lambda bi, i: (bi, i, 0))
    const = lambda arr: pl.BlockSpec(arr.shape, lambda bi, i: (0,) * arr.ndim)
    return pl.pallas_call(
        functools.partial(_wkv_body, ts),
        grid=(bsz, s // ts),
        in_specs=[tile] * 7 + [const(x) for x in (rk, gg, gb, avg, ones)],
        out_specs=tile, out_shape=jax.ShapeDtypeStruct((bsz, s, W_GRP), F32),
        scratch_shapes=[pltpu.VMEM((N_HEADS * WKV_CHUNK, W_GRP), F32)],
        compiler_params=_cparams("parallel", "arbitrary"), name="wkv_scan",
    )(r, lw, k, v, a, b, g, rk, gg, gb, avg, ones)


def _gmlp_body(ts, x_ref, g_ref, b_ref, ws_ref, bs_ref, o_ref):
    x = x_ref[...]
    u = jax.nn.gelu(x[:, :W_GRP])
    v = _ln(jax.nn.gelu(x[:, W_GRP:]), g_ref[...], b_ref[...]).astype(BF16)
    tr = lax.broadcasted_iota(jnp.int32, (GMLP_CHUNK, GMLP_CHUNK), 0)
    tc = lax.broadcasted_iota(jnp.int32, (GMLP_CHUNK, GMLP_CHUNK), 1)
    ws = [jnp.where(tc <= tr, ws_ref[h], 0.0).astype(BF16) for h in range(N_HEADS)]
    for c in range(ts // GMLP_CHUNK):
        rows = slice(c * GMLP_CHUNK, (c + 1) * GMLP_CHUNK)
        mixed = [_dot(ws[h], v[rows, h * HEAD_DIM:(h + 1) * HEAD_DIM]) + bs_ref[h]
                 for h in range(N_HEADS)]
        o_ref[rows, :] = u[rows, :] * jnp.concatenate(mixed, axis=-1)


def gmlp_mixer(x2d, g, b, ws, bs):
    n = x2d.shape[0]
    ts = TM
    const = lambda a: pl.BlockSpec(a.shape, lambda i: (0,) * a.ndim)
    return pl.pallas_call(
        functools.partial(_gmlp_body, ts),
        grid=(n // ts,),
        in_specs=[pl.BlockSpec((ts, 2 * W_GRP), lambda i: (i, 0)), const(g), const(b), const(ws), const(bs)],
        out_specs=pl.BlockSpec((ts, W_GRP), lambda i: (i, 0)),
        out_shape=jax.ShapeDtypeStruct((n, W_GRP), F32),
        compiler_params=_cparams("parallel"), name="gmlp_mixer",
    )(x2d, g, b, ws, bs)


def _swap_halves(x, lane):
    w = x.shape[-1]
    half = HEAD_DIM // 2
    fwd = pltpu.roll(x, w - half, 1)
    bwd = pltpu.roll(x, half, 1)
    return jnp.where((lane % HEAD_DIM) < half, fwd, bwd)


def _nsa_prep_body(q_ref, kv_ref, m_ref, cos_ref, sin_ref,
                   q_o, qr_o, kc_o, vc_o, ks_o, vs_o, kw_o, vw_o, g_o):
    scale = HEAD_DIM ** -0.5
    q = q_ref[0]
    kv = kv_ref[0]
    cos = cos_ref[...]
    sin = sin_ref[...]
    lane_q = lax.broadcasted_iota(jnp.int32, q.shape, 1)
    q_rot = q * cos[:, :W_GRP] + _swap_halves(q, lane_q) * sin[:, :W_GRP]
    lane_kv = lax.broadcasted_iota(jnp.int32, kv.shape, 1)
    kv_rot = kv * cos + _swap_halves(kv, lane_kv) * sin
    q_o[0] = (q * scale).astype(BF16)
    qr_o[0] = (q_rot * (scale * LOG2E)).astype(BF16)
    kc_o[0] = kv[:, 0:64]
    vc_o[0] = kv[:, 64:128]
    ks_o[0] = kv_rot[:, 128:192].astype(BF16)
    lane = lax.broadcasted_iota(jnp.int32, (kv.shape[0], LANES), 1)
    vs_first = pltpu.roll(kv[:, 128:256], HEAD_DIM, 1)
    vs_o[0] = jnp.where(lane < HEAD_DIM, vs_first, jnp.where(lane == HEAD_DIM, 1.0, 0.0)).astype(BF16)
    kw_o[0] = kv_rot[:, 256:320].astype(BF16)
    vw_o[0] = kv[:, 320:384].astype(BF16)
    g_o[0] = jax.nn.sigmoid(m_ref[0])


def nsa_prep(q, kv, misc, cos_t, sin_t):
    bsz, s, _ = q.shape
    ts = TM
    tile = lambda w: pl.BlockSpec((1, ts, w), lambda bi, i: (bi, i, 0))
    tab = pl.BlockSpec((ts, 6 * HEAD_DIM), lambda bi, i: (i, 0))
    sd = lambda w, dt: jax.ShapeDtypeStruct((bsz, s, w), dt)
    return pl.pallas_call(
        _nsa_prep_body,
        grid=(bsz, s // ts),
        in_specs=[tile(W_GRP), tile(6 * HEAD_DIM), tile(LANES), tab, tab],
        out_specs=[tile(W_GRP), tile(W_GRP), tile(HEAD_DIM), tile(HEAD_DIM), tile(HEAD_DIM),
                   tile(LANES), tile(HEAD_DIM), tile(HEAD_DIM), tile(LANES)],
        out_shape=[sd(W_GRP, BF16), sd(W_GRP, BF16), sd(HEAD_DIM, F32), sd(HEAD_DIM, F32),
                   sd(HEAD_DIM, BF16), sd(LANES, BF16), sd(HEAD_DIM, BF16), sd(HEAD_DIM, BF16),
                   sd(LANES, F32)],
        compiler_params=_cparams("parallel", "parallel"), name="nsa_prep",
    )(q, kv, misc, cos_t, sin_t)


def _compress_body(kc_ref, vc_ref, pe_ref, w1_ref, w2_ref, ko_ref, vo_ref):
    half = CMP_STRIDE * HEAD_DIM
    for j, (c_ref, o_ref) in enumerate(((kc_ref, ko_ref), (vc_ref, vo_ref))):
        c = c_ref[0].astype(BF16)
        w1 = w1_ref[j]
        lo = _dot(c, w1[:half])
        hi = _dot(c, w1[half:])
        nb = hi.shape[0]
        hi_next = pltpu.roll(hi, nb - 1, 0)
        pe = jnp.broadcast_to(pe_ref[j], (8, 2 * half))
        pe_term = _dot(pe, w1.astype(F32), HI)[0:1]
        h = jax.nn.gelu(lo + hi_next + pe_term)
        o_ref[0] = _dot(h.astype(BF16), w2_ref[j]).astype(BF16)


def nsa_compress(kc_r, vc_r, pe, w1, w2):
    bsz, nb, w = kc_r.shape
    blk = pl.BlockSpec((1, nb, w), lambda bi: (bi, 0, 0))
    const = lambda a: pl.BlockSpec(a.shape, lambda bi: (0,) * a.ndim)
    out = jax.ShapeDtypeStruct((bsz, nb, HEAD_DIM), BF16)
    ospec = pl.BlockSpec((1, nb, HEAD_DIM), lambda bi: (bi, 0, 0))
    return pl.pallas_call(
        _compress_body, grid=(bsz,),
        in_specs=[blk, blk, const(pe), const(w1), const(w2)],
        out_specs=[ospec, ospec], out_shape=[out, out],
        compiler_params=_cparams("parallel"), name="nsa_compress",
    )(kc_r, vc_r, pe, w1, w2)


def _stack_heads(x):
    return jnp.concatenate([x[:, h * HEAD_DIM:(h + 1) * HEAD_DIM] for h in range(N_HEADS)], axis=0)


def _nsa_attn_body(n_top, q_ref, qr_ref, kc_ref, vc_ref, ks_ref, vs_ref, kw_ref, vw_ref, g_ref,
                   ov_ref, o_ref):
    T = Q_BLOCK
    i = pl.program_id(1)
    t0 = i * T
    qs = _stack_heads(q_ref[0])
    qrs = _stack_heads(qr_ref[0])
    pos = t0 + lax.broadcasted_iota(jnp.int32, (T, 1), 0)
    tile4 = lambda x: jnp.concatenate([x] * N_HEADS, axis=0)

    kc = kc_ref[0]
    n_cmp = kc.shape[0]
    cmp_end = lax.broadcasted_iota(jnp.int32, (1, n_cmp), 1) * CMP_STRIDE + (CMP_LEN - 1)
    valid_c = tile4(jnp.where(cmp_end <= pos, 1.0, 0.0))
    s_c = jnp.where(valid_c > 0.5, _dot_nt(qs, kc), NEG)
    p_c = jnp.exp(s_c - jnp.max(s_c, axis=-1, keepdims=True)) * valid_c
    p_c = p_c / jnp.maximum(jnp.sum(p_c, axis=-1, keepdims=True), 1e-20)
    o_c = _dot(p_c.astype(BF16), vc_ref[0])
    p_sum = p_c[0:T] + p_c[T:2 * T] + p_c[2 * T:3 * T] + p_c[3 * T:4 * T]
    imp_t = _dot_nt(ov_ref[...], p_sum, HI)

    n_sel = imp_t.shape[0]
    jj = lax.broadcasted_iota(jnp.int32, (n_sel, T), 0)
    blk = (t0 + lax.broadcasted_iota(jnp.int32, (1, T), 1)) // SEL_LEN
    val = jnp.where(jj == blk, 3e38, jnp.where(jj == 0, 3e38, jnp.where(jj <= blk, imp_t, -1.0)))
    sel_t = jnp.zeros((n_sel, T), F32)
    for _ in range(n_top):
        mx = jnp.max(val, axis=0, keepdims=True)
        idx = jnp.min(jnp.where(val == mx, jj, n_sel), axis=0, keepdims=True)
        hit = jj == idx
        sel_t = jnp.where(hit, 1.0, sel_t)
        val = jnp.where(hit, -2.0, val)
    sel = sel_t.T

    KT = SEL_KT
    n_tiles = (t0 + T + KT - 1) // KT
    er = lax.broadcasted_iota(jnp.int32, (n_sel, KT), 0)
    ec = lax.broadcasted_iota(jnp.int32, (n_sel, KT), 1)
    expand = jnp.where(er == ec // SEL_LEN, 1.0, 0.0).astype(BF16)
    kcol = lax.broadcasted_iota(jnp.int32, (1, KT), 1)

    def sel_tile(jt, carry):
        m, acc = carry
        k0 = pl.multiple_of(jt * KT, KT)
        kt = ks_ref[0, pl.ds(k0, KT), :]
        vt = vs_ref[0, pl.ds(k0, KT), :]
        shift = (n_sel - jt * (KT // SEL_LEN)) % n_sel
        picked = _dot(pltpu.roll(sel, shift, 1).astype(BF16), expand)
        bias = tile4((jnp.where(k0 + kcol <= pos, picked, 0.0) - 1.0) * (-NEG))
        s = _dot_nt(qrs, kt) + bias
        m_new = jnp.maximum(m, jnp.max(s, axis=-1, keepdims=True))
        p = jnp.exp2(s - m_new)
        acc = jnp.exp2(m - m_new) * acc + _dot(p.astype(BF16), vt)
        return m_new, acc

    init = (jnp.full((N_HEADS * T, 1), NEG, F32), jnp.zeros((N_HEADS * T, LANES), F32))
    _, acc_s = lax.fori_loop(0, n_tiles, sel_tile, init)
    o_s = acc_s[:, :HEAD_DIM] / acc_s[:, HEAD_DIM:HEAD_DIM + 1]

    WK = WINDOW + T
    start = pl.multiple_of(jnp.maximum(t0 - WINDOW, 0), T)
    kw = kw_ref[0, pl.ds(start, WK), :]
    vw = vw_ref[0, pl.ds(start, WK), :]
    kpos_w = start + lax.broadcasted_iota(jnp.int32, (1, WK), 1)
    bias_w = tile4(jnp.where(kpos_w <= pos, jnp.where(kpos_w > pos - WINDOW, 0.0, NEG), NEG))
    s_w = _dot_nt(qrs, kw) + bias_w
    p_w = jnp.exp2(s_w - jnp.max(s_w, axis=-1, keepdims=True))
    p_w = p_w / jnp.sum(p_w, axis=-1, keepdims=True)
    o_w = _dot(p_w.astype(BF16), vw)

    g = g_ref[0]
    outs = []
    for h in range(N_HEADS):
        rows = slice(h * T, (h + 1) * T)
        outs.append(g[:, 3 * h:3 * h + 1] * o_c[rows] + g[:, 3 * h + 1:3 * h + 2] * o_s[rows]
                    + g[:, 3 * h + 2:3 * h + 3] * o_w[rows])
    o_ref[0] = jnp.concatenate(outs, axis=-1)


def nsa_attention(q, qr, k_cmp, v_cmp, ks, vs, kw, vw, gates, overlap):
    bsz, s, _ = q.shape
    n_top = min(SEL_TOPN, s // SEL_LEN)
    n_cmp = k_cmp.shape[1]
    qtile = lambda w: pl.BlockSpec((1, Q_BLOCK, w), lambda bi, i: (bi, i, 0))
    full = lambda rows, w: pl.BlockSpec((1, rows, w), lambda bi, i: (bi, 0, 0))
    return pl.pallas_call(
        functools.partial(_nsa_attn_body, n_top),
        grid=(bsz, s // Q_BLOCK),
        in_specs=[qtile(W_GRP), qtile(W_GRP), full(n_cmp, HEAD_DIM), full(n_cmp, HEAD_DIM),
                  full(s, HEAD_DIM), full(s, LANES), full(s, HEAD_DIM), full(s, HEAD_DIM),
                  qtile(LANES), pl.BlockSpec(overlap.shape, lambda bi, i: (0, 0))],
        out_specs=qtile(W_GRP), out_shape=jax.ShapeDtypeStruct((bsz, s, W_GRP), F32),
        compiler_params=_cparams("parallel", "parallel"), name="nsa_attention",
    )(q, qr, k_cmp, v_cmp, ks, vs, kw, vw, gates, overlap)


def _outproj_body(alpha, a_ref, b_ref, c_ref, d_ref, x_ref, w_ref, g_ref, bb_ref, o_ref):
    mix = jnp.zeros(x_ref.shape, F32)
    for j, part in enumerate((a_ref, b_ref, c_ref, d_ref)):
        mix = mix + _dot(part[...].astype(BF16), w_ref[j * W_GRP:(j + 1) * W_GRP, :])
    o_ref[...] = _ln(alpha * x_ref[...] + mix, g_ref[...], bb_ref[...])


def out_proj(parts, x2d, w, g, b, alpha):
    n, d = x2d.shape
    row = lambda w_: pl.BlockSpec((TM, w_), lambda i: (i, 0))
    const = lambda a: pl.BlockSpec(a.shape, lambda i: (0,) * a.ndim)
    return pl.pallas_call(
        functools.partial(_outproj_body, alpha),
        grid=(n // TM,),
        in_specs=[row(W_GRP)] * 4 + [row(d), const(w), const(g), const(b)],
        out_specs=row(d), out_shape=jax.ShapeDtypeStruct((n, d), F32),
        compiler_params=_cparams("parallel"), name="out_proj",
    )(*parts, x2d, w, g, b)


def _memkv_body(m_ref, wk_ref, wv_ref, k_o, v_o):
    mb = m_ref[...].astype(BF16)
    k_o[...] = _dot(mb, wk_ref[...]).astype(BF16)
    v_o[...] = _dot(mb, wv_ref[...]).astype(BF16)


def mem_kv(mem2d, wk, wv):
    n, d = mem2d.shape
    full = lambda a: pl.BlockSpec(a.shape, lambda i: (0, 0))
    out = jax.ShapeDtypeStruct((n, d), BF16)
    return pl.pallas_call(
        _memkv_body, grid=(1,),
        in_specs=[full(mem2d), full(wk), full(wv)],
        out_specs=[pl.BlockSpec((n, d), lambda i: (0, 0))] * 2, out_shape=[out, out],
        compiler_params=_cparams("arbitrary"), name="mem_kv",
    )(mem2d, wk, wv)


def _xattn_body(alpha, x_ref, k_ref, v_ref, wq_ref, wo_ref, g_ref, b_ref, o_ref):
    x = x_ref[0]
    d = x.shape[-1]
    hd = d // N_MEM_HEADS
    q = (_dot(x.astype(BF16), wq_ref[...]) * (hd ** -0.5)).astype(BF16)
    k = k_ref[0]
    v = v_ref[0]
    heads = []
    for h in range(N_MEM_HEADS):
        cs = slice(h * hd, (h + 1) * hd)
        s = _dot_nt(q[:, cs], k[:, cs])
        p = jnp.exp(s - jnp.max(s, axis=-1, keepdims=True))
        p = p / jnp.sum(p, axis=-1, keepdims=True)
        heads.append(_dot(p.astype(BF16), v[:, cs]))
    o = jnp.concatenate(heads, axis=-1).astype(BF16)
    o_ref[0] = _ln(alpha * x + _dot(o, wo_ref[...]), g_ref[...], b_ref[...])


def cross_attention(x, k, v, wq, wo, g, b, alpha):
    bsz, s, d = x.shape
    m = k.shape[1]
    const = lambda a: pl.BlockSpec(a.shape, lambda bi, i: (0,) * a.ndim)
    return pl.pallas_call(
        functools.partial(_xattn_body, alpha),
        grid=(bsz, s // TM),
        in_specs=[pl.BlockSpec((1, TM, d), lambda bi, i: (bi, i, 0)),
                  pl.BlockSpec((1, m, d), lambda bi, i: (bi, 0, 0)),
                  pl.BlockSpec((1, m, d), lambda bi, i: (bi, 0, 0)),
                  const(wq), const(wo), const(g), const(b)],
        out_specs=pl.BlockSpec((1, TM, d), lambda bi, i: (bi, i, 0)),
        out_shape=jax.ShapeDtypeStruct((bsz, s, d), F32),
        compiler_params=_cparams("parallel", "parallel"), name="cross_attention",
    )(x, k, v, wq, wo, g, b)


PER_GRP = N_EXPERTS // N_EXPERT_GROUPS
LPOS_LANE = PER_GRP
SUB = 8


def _router_body(x_ref, rwt_ref, rb_ref, upper_ref, tok_o, lrow_o, cnt_o):
    tm = x_ref.shape[0]
    logits = _dot_nt(rwt_ref[...], x_ref[...], HI)
    ex = jnp.exp(logits - jnp.max(logits, axis=0, keepdims=True))
    probs = ex / jnp.sum(ex, axis=0, keepdims=True)
    sel = probs + rb_ref[...]
    srow = [sel[e:e + 1] for e in range(N_EXPERTS)]
    prow = [probs[e:e + 1] for e in range(N_EXPERTS)]
    gscore = []
    for g in range(N_EXPERT_GROUPS):
        r = srow[g * PER_GRP:(g + 1) * PER_GRP]
        best = None
        for a in range(PER_GRP):
            for b in range(a + 1, PER_GRP):
                best = r[a] + r[b] if best is None else jnp.maximum(best, r[a] + r[b])
        gscore.append(best)
    g_idx = jnp.zeros((1, tm), jnp.int32)
    top = gscore[0]
    for g in range(1, N_EXPERT_GROUPS):
        better = gscore[g] > top
        g_idx = jnp.where(better, g, g_idx)
        top = jnp.where(better, gscore[g], top)

    def of_group(rows, e):
        out = rows[e]
        for g in range(1, N_EXPERT_GROUPS):
            out = jnp.where(g_idx == g, rows[g * PER_GRP + e], out)
        return out

    sg = [of_group(srow, e) for e in range(PER_GRP)]
    pg = [of_group(prow, e) for e in range(PER_GRP)]
    w = []
    for e in range(PER_GRP):
        rank = jnp.zeros((1, tm), F32)
        for o in range(PER_GRP):
            if o != e:
                ahead = (sg[o] >= sg[e]) if o < e else (sg[o] > sg[e])
                rank = rank + jnp.where(ahead, 1.0, 0.0)
        w.append(jnp.where(rank < 1.5, pg[e], 0.0))
    w_sum = w[0] + w[1] + w[2] + w[3]
    row8 = lax.broadcasted_iota(jnp.int32, (SUB, tm), 0)
    onehot = jnp.where(row8 == g_idx, 1.0, 0.0)
    before = _dot(onehot.astype(BF16), upper_ref[...])
    cnt = jnp.sum(onehot, axis=1, keepdims=True)
    offs = [jnp.zeros((1, 1), F32)]
    for g in range(1, N_EXPERT_GROUPS):
        offs.append(offs[-1] + cnt[g - 1:g])
    lpos = jnp.zeros((1, tm), F32)
    for g in range(N_EXPERT_GROUPS):
        lpos = lpos + onehot[g:g + 1] * (offs[g] + before[g:g + 1])
    tok = jnp.zeros((SUB, tm), F32)
    for e in range(PER_GRP):
        tok = jnp.where(row8 == e, w[e] / w_sum, tok)
    tok = jnp.where(row8 == LPOS_LANE, lpos, tok)
    tok_o[...] = jnp.concatenate([tok, jnp.zeros((LANES - SUB, tm), F32)], axis=0).T
    lrow_o[0] = lpos.astype(jnp.int32)
    rowc = lax.broadcasted_iota(jnp.int32, (SUB, LANES), 0)
    stats = jnp.zeros((SUB, LANES), F32)
    for g in range(N_EXPERT_GROUPS):
        stats = jnp.where(rowc == g, cnt[g:g + 1], stats)
        stats = jnp.where(rowc == N_EXPERT_GROUPS + g, offs[g], stats)
    cnt_o[0] = stats.astype(jnp.int32)


def moe_router(x2d, rwt, rb_col, upper):
    n, d = x2d.shape
    tm = upper.shape[0]
    nt = n // tm
    const = lambda a: pl.BlockSpec(a.shape, lambda i: (0, 0))
    return pl.pallas_call(
        _router_body, grid=(nt,),
        in_specs=[pl.BlockSpec((tm, d), lambda i: (i, 0)), const(rwt), const(rb_col), const(upper)],
        out_specs=[pl.BlockSpec((tm, LANES), lambda i: (i, 0)),
                   pl.BlockSpec((1, 1, tm), lambda i: (i, 0, 0)),
                   pl.BlockSpec((1, SUB, LANES), lambda i: (i, 0, 0))],
        out_shape=[jax.ShapeDtypeStruct((n, LANES), F32), jax.ShapeDtypeStruct((nt, 1, tm), jnp.int32),
                   jax.ShapeDtypeStruct((nt, SUB, LANES), jnp.int32)],
        compiler_params=_cparams("parallel"), name="moe_router",
    )(x2d, rwt, rb_col, upper)


def _split_bf16(x):
    hi = x.astype(BF16)
    return hi, (x - hi.astype(F32)).astype(BF16)


def _moe_body(alpha, offs_ref, cnts_ref, x_ref, tok_ref, lrow_ref, wg_ref, wu_ref, wd_ref, g_ref, b_ref, o_ref,
              xs_ref, gs_ref, acc_ref):
    i = pl.program_id(0)
    e = pl.program_id(1)
    tm = x_ref.shape[0]

    @pl.when(e == 0)
    def _():
        slot = lax.broadcasted_iota(jnp.int32, (tm, tm), 0)
        perm = jnp.where(slot == lrow_ref[0], 1.0, 0.0).astype(BF16)
        xs_ref[...] = _dot(perm, x_ref[...].astype(BF16)).astype(BF16)
        t_hi, t_lo = _split_bf16(tok_ref[...])
        gs_ref[...] = _dot(perm, t_hi) + _dot(perm, t_lo)
        acc_ref[...] = jnp.zeros_like(acc_ref)

    grp = e // PER_GRP
    off = offs_ref[i * N_EXPERT_GROUPS + grp]
    end = off + cnts_ref[i * N_EXPERT_GROUPS + grp]
    lane = lax.broadcasted_iota(jnp.int32, (MOE_BLK, LANES), 1)
    for s in range(tm // MOE_BLK):
        lo = s * MOE_BLK

        @pl.when(jnp.logical_and(off < lo + MOE_BLK, end > lo))
        def _(lo=lo):
            rows = slice(lo, lo + MOE_BLK)
            xs = xs_ref[rows, :]
            gt = _dot(xs, wg_ref[0, 0])
            h = (gt * jax.nn.sigmoid(gt) * _dot(xs, wu_ref[0, 0])).astype(BF16)
            ridx = lo + lax.broadcasted_iota(jnp.int32, (MOE_BLK, 1), 0)
            in_run = jnp.where(ridx >= off, jnp.where(ridx < end, 1.0, 0.0), 0.0)
            gcol = jnp.sum(jnp.where(lane == e % PER_GRP, gs_ref[rows, :], 0.0), axis=-1, keepdims=True)
            acc_ref[rows, :] += (gcol * in_run) * _dot(h, wd_ref[0, 0])

    @pl.when(e == pl.num_programs(1) - 1)
    def _():
        lcol = tok_ref[:, LPOS_LANE:LPOS_LANE + 1].astype(jnp.int32)
        slot = lax.broadcasted_iota(jnp.int32, (tm, tm), 1)
        unperm = jnp.where(slot == lcol, 1.0, 0.0).astype(BF16)
        a_hi, a_lo = _split_bf16(acc_ref[...])
        y = _dot(unperm, a_hi) + _dot(unperm, a_lo)
        o_ref[...] = _ln(alpha * x_ref[...] + y, g_ref[...], b_ref[...])


def moe_ffn(x2d, tok, lrow, offs, cnts, layer, wg, wu, wd, g, b, alpha):
    n, d = x2d.shape
    _, ne, de, _ = wd.shape
    tm = lrow.shape[2]
    const = lambda a: pl.BlockSpec(a.shape, lambda i, e, o, c: (0,) * a.ndim)
    grid_spec = pltpu.PrefetchScalarGridSpec(
        num_scalar_prefetch=2, grid=(n // tm, ne),
        in_specs=[pl.BlockSpec((tm, d), lambda i, e, o, c: (i, 0)),
                  pl.BlockSpec((tm, LANES), lambda i, e, o, c: (i, 0)),
                  pl.BlockSpec((1, 1, tm), lambda i, e, o, c: (i, 0, 0)),
                  pl.BlockSpec((1, 1, d, de), lambda i, e, o, c: (layer, e, 0, 0)),
                  pl.BlockSpec((1, 1, d, de), lambda i, e, o, c: (layer, e, 0, 0)),
                  pl.BlockSpec((1, 1, de, d), lambda i, e, o, c: (layer, e, 0, 0)),
                  const(g), const(b)],
        out_specs=pl.BlockSpec((tm, d), lambda i, e, o, c: (i, 0)),
        scratch_shapes=[pltpu.VMEM((tm, d), BF16), pltpu.VMEM((tm, LANES), F32), pltpu.VMEM((tm, d), F32)])
    return pl.pallas_call(
        functools.partial(_moe_body, alpha), grid_spec=grid_spec,
        out_shape=jax.ShapeDtypeStruct((n, d), F32),
        compiler_params=_cparams("parallel", "arbitrary"), name="moe_ffn",
    )(offs, cnts, x2d, tok, lrow, wg, wu, wd, g, b)


def _rope_tables(s):
    inv = ROPE_THETA ** (-jnp.arange(0, HEAD_DIM, 2, dtype=F32) / HEAD_DIM)
    ang = jnp.arange(s, dtype=F32)[:, None] * inv[None, :]
    cos, sin = jnp.cos(ang), jnp.sin(ang)
    cos_h = jnp.concatenate([cos, cos], axis=-1)
    sin_h = jnp.concatenate([-sin, sin], axis=-1)
    return jnp.tile(cos_h, (1, 6)), jnp.tile(sin_h, (1, 6))


def _overlap_matrix(s, n_cmp_pad):
    n_sel = s // SEL_LEN
    assert n_sel <= LANES
    cmp_start = jnp.arange(n_cmp_pad) * CMP_STRIDE
    sel_start = jnp.arange(LANES) * SEL_LEN
    ov = jnp.clip(jnp.minimum(cmp_start[None, :] + CMP_LEN, sel_start[:, None] + SEL_LEN)
                  - jnp.maximum(cmp_start[None, :], sel_start[:, None]), 0, None).astype(F32) / CMP_LEN
    n_cmp = s // CMP_STRIDE - (CMP_LEN // CMP_STRIDE - 1)
    real = (jnp.arange(n_cmp_pad)[None, :] < n_cmp) & (jnp.arange(LANES)[:, None] < n_sel)
    return jnp.where(real, ov, 0.0)


def _pad_rows(w, lo, total):
    out = jnp.zeros((total, w.shape[1]), w.dtype)
    return out.at[lo:lo + w.shape[0]].set(w)


def kernel(x, mem, ln_in_g, ln_in_b, w_in, w_out, conv_w, conv_b, conv_gn_g, conv_gn_b, rwkv_mu, rwkv_w0, rwkv_w_up, rwkv_a0, rwkv_a_up, rwkv_g_up, rwkv_k_k, rwkv_k_a, rwkv_r_k, rwkv_gn_g, rwkv_gn_b, rwkv_v_down, rwkv_v_mu, rwkv_v0, rwkv_v_up, gmlp_ln_g, gmlp_ln_b, gmlp_w_s, gmlp_b_s, nsa_pe_k, nsa_w1_k, nsa_w2_k, nsa_pe_v, nsa_w1_v, nsa_w2_v, ln1_g, ln1_b, xa_wq, xa_wk, xa_wv, xa_wo, ln2_g, ln2_b, router_w, router_bias, moe_w_gate, moe_w_up, moe_w_down, ln3_g, ln3_b):
    bsz, s, d = x.shape
    depth = w_in.shape[0]
    n = bsz * s
    alpha = (2 * depth) ** 0.25
    row = lambda a: a.reshape(1, -1)

    cos_t, sin_t = _rope_tables(s)
    n_blk = s // CMP_STRIDE
    overlap = _overlap_matrix(s, n_blk)
    avg64 = _group_avg_matrix(W_GRP, HEAD_DIM)
    ones64 = avg64 * HEAD_DIM
    moe_tm = min(MOE_TM, n)
    t_idx = jnp.arange(moe_tm)
    upper = (t_idx[:, None] < t_idx[None, :]).astype(BF16)
    mem2d = mem.reshape(bsz * mem.shape[1], d)

    wg_b, wu_b, wd_b = moe_w_gate.astype(BF16), moe_w_up.astype(BF16), moe_w_down.astype(BF16)

    xs = x.reshape(n, d)
    v_first = None
    for l in range(depth):
        w_pad = jnp.zeros((d, IN_PAD), F32).at[:, :w_in.shape[2]].set(w_in[l])
        if l > 0:
            lo = IN_SPLITS[-1][1] + MISC_VD_OFF
            w_pad = w_pad.at[:, lo:lo + rwkv_v_down.shape[2]].set(rwkv_v_down[l - 1])
        outs = in_proj(xs, row(ln_in_g), row(ln_in_b), w_pad.astype(BF16), apply_ln=(l == 0))
        if l == 0:
            xs, outs = outs[0], outs[1:]
        conv_in, rwkv_in, gmlp_in, q_in, kv_in, misc = outs
        b3 = lambda a: a.reshape(bsz, s, a.shape[-1])

        out_a = conv_mixer(b3(conv_in), conv_w[l], row(conv_b[l]), row(conv_gn_g[l]), row(conv_gn_b[l]), avg64)

        rp = {"mu": row(rwkv_mu[l]), "w0": row(rwkv_w0[l]), "a0": row(rwkv_a0[l]),
              "wup": _pad_rows(rwkv_w_up[l], 0, W_GRP), "aup": _pad_rows(rwkv_a_up[l], 64, W_GRP),
              "gup": _pad_rows(rwkv_g_up[l], 128, W_GRP),
              "kk": row(rwkv_k_k[l]), "ka": row(rwkv_k_a[l]), "ones": ones64}
        if l > 0:
            rp["vmu"] = jnp.zeros((1, LANES), F32).at[0, MISC_VD_OFF:MISC_VD_OFF + rwkv_v_mu.shape[1]].set(rwkv_v_mu[l - 1])
            rp["v0"] = row(rwkv_v0[l - 1])
            rp["vup"] = _pad_rows(rwkv_v_up[l - 1], MISC_VD_OFF, LANES)
        r_, lw_, k_, v_, a_, b_, g_ = rwkv_prep(b3(rwkv_in), b3(misc), v_first, rp)
        if l == 0:
            v_first = v_
        out_b = wkv_scan(r_, lw_, k_, v_, a_, b_, g_, row(rwkv_r_k[l]), row(rwkv_gn_g[l]),
                         row(rwkv_gn_b[l]), avg64, ones64)

        out_c = gmlp_mixer(gmlp_in, row(gmlp_ln_g[l]), row(gmlp_ln_b[l]), gmlp_w_s[l],
                           gmlp_b_s[l].reshape(N_HEADS, GMLP_CHUNK, 1))

        q_b, qr_b, kc, vc, ks, vs, kw, vw, gates = nsa_prep(b3(q_in), b3(kv_in), b3(misc), cos_t, sin_t)
        pe = jnp.stack([nsa_pe_k[l].reshape(1, -1), nsa_pe_v[l].reshape(1, -1)])
        w1 = jnp.stack([nsa_w1_k[l], nsa_w1_v[l]]).astype(BF16)
        w2 = jnp.stack([nsa_w2_k[l], nsa_w2_v[l]]).astype(BF16)
        k_cmp, v_cmp = nsa_compress(kc.reshape(bsz, n_blk, CMP_STRIDE * HEAD_DIM),
                                    vc.reshape(bsz, n_blk, CMP_STRIDE * HEAD_DIM), pe, w1, w2)
        out_d = nsa_attention(q_b, qr_b, k_cmp, v_cmp, ks, vs, kw, vw, gates, overlap)

        flat = lambda a: a.reshape(n, W_GRP)
        xs = out_proj((flat(out_a), flat(out_b), out_c, flat(out_d)), xs, w_out[l].astype(BF16),
                      row(ln1_g[l]), row(ln1_b[l]), alpha)

        mk, mv = mem_kv(mem2d, xa_wk[l].astype(BF16), xa_wv[l].astype(BF16))
        m_len = mem.shape[1]
        xs = cross_attention(xs.reshape(bsz, s, d), mk.reshape(bsz, m_len, d), mv.reshape(bsz, m_len, d),
                             xa_wq[l].astype(BF16), xa_wo[l].astype(BF16),
                             row(ln2_g[l]), row(ln2_b[l]), alpha).reshape(n, d)

        tok, lrow, stats = moe_router(xs, router_w.T, router_bias.reshape(-1, 1), upper)
        cnts = stats[:, :N_EXPERT_GROUPS, 0].reshape(-1)
        offs = stats[:, N_EXPERT_GROUPS:2 * N_EXPERT_GROUPS, 0].reshape(-1)
        xs = moe_ffn(xs, tok, lrow, offs, cnts, l, wg_b, wu_b, wd_b,
                     row(ln3_g[l]), row(ln3_b[l]), alpha)
    return xs.reshape(bsz, s, d)
```

```python
import functools
import math

import jax
import jax.numpy as jnp
from jax import lax
from jax.experimental import pallas as pl
from jax.experimental.pallas import tpu as pltpu

F32 = jnp.float32
BF16 = jnp.bfloat16
HI = lax.Precision.HIGHEST

HEAD_DIM = 64
N_HEADS = 4
W_GRP = 256
CONV_WIDTH = 31
GMLP_CHUNK = 128
CMP_LEN = 32
CMP_STRIDE = 16
SEL_LEN = 64
SEL_TOPN = 16
WINDOW = 512
Q_BLOCK = 256
ROPE_THETA = 10000.0
N_MEM_HEADS = 4
N_EXPERTS = 16
N_EXPERT_GROUPS = 4
LN_EPS = 1e-5
RWKV_GN_EPS = 64e-5
NEG = -1e30
LOG2E = math.log2(math.e)
LANES = 128
WKV_CHUNK = 64
WKV_GROUP = 8

TM = 512
MOE_TM = 1024
MOE_BLK = 256
SEL_KT = 512
VMEM_LIMIT = 56 * 1024 * 1024


def _cparams(*sem):
    return pltpu.CompilerParams(dimension_semantics=sem, vmem_limit_bytes=VMEM_LIMIT)


def _ln(x, g, b, eps=LN_EPS):
    mu = jnp.mean(x, axis=-1, keepdims=True)
    xc = x - mu
    var = jnp.mean(xc * xc, axis=-1, keepdims=True)
    return xc * lax.rsqrt(var + eps) * g + b


def _dot(a, b, precision=None):
    return jnp.dot(a, b, preferred_element_type=F32, precision=precision)


def _dot_nt(a, b, precision=None):
    return lax.dot_general(a, b, (((1,), (1,)), ((), ())),
                           preferred_element_type=F32, precision=precision)


def _dot_hilo(x, w_bf16):
    hi = x.astype(BF16)
    lo = (x - hi.astype(F32)).astype(BF16)
    return _dot(hi, w_bf16) + _dot(lo, w_bf16)


def _group_avg_matrix(width, group):
    r = jnp.arange(width)[:, None] // group
    c = jnp.arange(width)[None, :] // group
    return jnp.where(r == c, 1.0 / group, 0.0).astype(F32)


IN_SPLITS = (("conv", 0, 512), ("rwkv", 512, 1536), ("gmlp", 1536, 2048),
             ("q", 2048, 2304), ("kv", 2304, 2688), ("misc", 2688, 2816))
IN_PAD = 2816
MISC_VD_OFF = 32


def _inproj_body(apply_ln, x_ref, g_ref, b_ref, w_ref, *outs):
    x = x_ref[...]
    if apply_ln:
        x = _ln(x, g_ref[...], b_ref[...])
        outs[0][...] = x
        outs = outs[1:]
    xb = x.astype(BF16)
    for o_ref, (_, lo, hi) in zip(outs, IN_SPLITS):
        o_ref[...] = _dot(xb, w_ref[:, lo:hi])


def in_proj(x2d, g, b, w_pad, apply_ln):
    n, d = x2d.shape
    row = lambda w: pl.BlockSpec((TM, w), lambda i: (i, 0))
    const = lambda s: pl.BlockSpec(s, lambda i: (0, 0))
    out_shapes = [jax.ShapeDtypeStruct((n, hi - lo), F32) for _, lo, hi in IN_SPLITS]
    out_specs = [row(hi - lo) for _, lo, hi in IN_SPLITS]
    if apply_ln:
        out_shapes = [jax.ShapeDtypeStruct((n, d), F32)] + out_shapes
        out_specs = [row(d)] + out_specs
    return pl.pallas_call(
        functools.partial(_inproj_body, apply_ln),
        grid=(n // TM,),
        in_specs=[row(d), const((1, d)), const((1, d)), const((d, IN_PAD))],
        out_specs=out_specs, out_shape=out_shapes,
        compiler_params=_cparams("parallel"), name="in_proj",
    )(x2d, g, b, w_pad)


CONV_HALO = 32


def _conv_body(ts, cur_ref, halo_ref, w_ref, b_ref, gg_ref, gb_ref, avg_ref, o_ref, hbuf):
    i = pl.program_id(1)
    cur = cur_ref[0]
    hal = halo_ref[0]
    h = cur[:, :W_GRP] * jax.nn.sigmoid(cur[:, W_GRP:])
    hh = hal[:, :W_GRP] * jax.nn.sigmoid(hal[:, W_GRP:])
    hbuf[0:CONV_HALO, :] = jnp.where(i > 0, hh, 0.0)
    hbuf[CONV_HALO:, :] = h
    acc = jnp.zeros((ts, W_GRP), F32)
    base = CONV_HALO - (CONV_WIDTH - 1)
    for j in range(CONV_WIDTH):
        acc = acc + w_ref[j:j + 1, :] * hbuf[base + j:base + j + ts, :]
    acc = acc + b_ref[...]
    avg = avg_ref[...].astype(BF16)
    mu = _dot_hilo(acc, avg)
    xc = acc - mu
    var = _dot_hilo(xc * xc, avg)
    y = xc * lax.rsqrt(var + LN_EPS) * gg_ref[...] + gb_ref[...]
    o_ref[0] = y * jax.nn.sigmoid(y)


def conv_mixer(conv_in, w, b, gg, gb, avg):
    bsz, s, _ = conv_in.shape
    ts = TM
    r = ts // CONV_HALO
    const = lambda shp: pl.BlockSpec(shp, lambda bi, i: (0, 0))
    return pl.pallas_call(
        functools.partial(_conv_body, ts),
        grid=(bsz, s // ts),
        in_specs=[pl.BlockSpec((1, ts, 2 * W_GRP), lambda bi, i: (bi, i, 0)),
                  pl.BlockSpec((1, CONV_HALO, 2 * W_GRP),
                               lambda bi, i: (bi, jnp.maximum(i * r - 1, 0), 0)),
                  const((CONV_WIDTH, W_GRP)), const((1, W_GRP)), const((1, W_GRP)),
                  const((1, W_GRP)), const((W_GRP, W_GRP))],
        out_specs=pl.BlockSpec((1, ts, W_GRP), lambda bi, i: (bi, i, 0)),
        out_shape=jax.ShapeDtypeStruct((bsz, s, W_GRP), F32),
        scratch_shapes=[pltpu.VMEM((ts + CONV_HALO, W_GRP), F32)],
        compiler_params=_cparams("parallel", "parallel"), name="conv_mixer",
    )(conv_in, conv_in, w, b, gg, gb, avg)


SHIFT_HALO = 8


def _shift_prev(buf, cur, halo, first):
    ts = cur.shape[0]
    buf[0:SHIFT_HALO, :] = jnp.where(first, 0.0, halo)
    buf[SHIFT_HALO:, :] = cur
    return buf[SHIFT_HALO - 1:SHIFT_HALO - 1 + ts, :]


def _rwkv_prep_body(has_vfirst, *refs):
    if has_vfirst:
        (c_ref, ch_ref, m_ref, mh_ref, vf_ref, mu_ref, w0_ref, a0_ref, wup_ref, aup_ref, gup_ref,
         kk_ref, ka_ref, ones_ref, vmu_ref, v0_ref, vup_ref,
         r_o, lw_o, k_o, v_o, a_o, b_o, g_o, buf, mbuf) = refs
    else:
        (c_ref, ch_ref, mu_ref, w0_ref, a0_ref, wup_ref, aup_ref, gup_ref,
         kk_ref, ka_ref, ones_ref,
         r_o, lw_o, k_o, v_o, a_o, b_o, g_o, buf) = refs
    first = pl.program_id(1) == 0
    cur = c_ref[0]
    prev = _shift_prev(buf, cur, ch_ref[0], first)
    y = cur + mu_ref[...] * (prev - cur)
    r = y[:, 0:256]
    k = y[:, 256:512]
    v = y[:, 512:768]
    lora = y[:, 768:1024]
    w = w0_ref[...] + _dot(jnp.tanh(lora), wup_ref[...], HI)
    a = jax.nn.sigmoid(a0_ref[...] + _dot(lora, aup_ref[...], HI))
    g = _dot(jax.nn.sigmoid(lora), gup_ref[...], HI)
    z = -w
    sp = jnp.maximum(z, 0.0) + jnp.log(1.0 + jnp.exp(-jnp.abs(z)))
    lw = -jnp.exp(-sp - 0.5)
    if has_vfirst:
        mc = m_ref[0]
        mprev = _shift_prev(mbuf, mc, mh_ref[0], first)
        vd = mc + vmu_ref[...] * (mprev - mc)
        v_mix = jax.nn.sigmoid(v0_ref[...] + _dot(vd, vup_ref[...], HI))
        v = v + (vf_ref[0] - v) * v_mix
    kk = k * kk_ref[...]
    n2 = _dot_hilo(kk * kk, ones_ref[...].astype(BF16))
    kk = kk / jnp.maximum(jnp.sqrt(n2), 1e-12)
    k2 = k * (1.0 + (a - 1.0) * ka_ref[...])
    r_o[0] = r
    lw_o[0] = lw
    k_o[0] = k2
    v_o[0] = v
    a_o[0] = -kk
    b_o[0] = kk * a
    g_o[0] = g


def rwkv_prep(cols, misc, v_first, p):
    bsz, s, c = cols.shape
    ts = TM
    has_vfirst = v_first is not None
    r8 = ts // SHIFT_HALO
    tile = lambda w: pl.BlockSpec((1, ts, w), lambda bi, i: (bi, i, 0))
    halo = lambda w: pl.BlockSpec((1, SHIFT_HALO, w), lambda bi, i: (bi, jnp.maximum(i * r8 - 1, 0), 0))
    const = lambda a: pl.BlockSpec(a.shape, lambda bi, i: (0,) * a.ndim)
    params = [p["mu"], p["w0"], p["a0"], p["wup"], p["aup"], p["gup"], p["kk"], p["ka"], p["ones"]]
    if has_vfirst:
        inputs = [cols, cols, misc, misc, v_first] + params + [p["vmu"], p["v0"], p["vup"]]
        in_specs = ([tile(c), halo(c), tile(LANES), halo(LANES), tile(W_GRP)]
                    + [const(a) for a in params + [p["vmu"], p["v0"], p["vup"]]])
        scratch = [pltpu.VMEM((ts + SHIFT_HALO, c), F32), pltpu.VMEM((ts + SHIFT_HALO, LANES), F32)]
    else:
        inputs = [cols, cols] + params
        in_specs = [tile(c), halo(c)] + [const(a) for a in params]
        scratch = [pltpu.VMEM((ts + SHIFT_HALO, c), F32)]
    out = jax.ShapeDtypeStruct((bsz, s, W_GRP), F32)
    return pl.pallas_call(
        functools.partial(_rwkv_prep_body, has_vfirst),
        grid=(bsz, s // ts),
        in_specs=in_specs,
        out_specs=[tile(W_GRP)] * 7, out_shape=[out] * 7,
        scratch_shapes=scratch,
        compiler_params=_cparams("parallel", "parallel"), name="rwkv_prep",
    )(*inputs)


def _block_diag(x, headmask):
    return jnp.concatenate([x] * N_HEADS, axis=0) * headmask


def _wkv_body(ts, r_ref, lw_ref, k_ref, v_ref, a_ref, b_ref, g_ref, rk_ref, gg_ref, gb_ref,
              avg_ref, ones_ref, o_ref, st_ref):
    C = WKV_CHUNK
    n = N_HEADS * C

    @pl.when(pl.program_id(1) == 0)
    def _():
        st_ref[...] = jnp.zeros_like(st_ref)

    ri = lax.broadcasted_iota(jnp.int32, (n, n), 0)
    ci = lax.broadcasted_iota(jnp.int32, (n, n), 1)
    head_f = jnp.where((ri // C) == (ci // HEAD_DIM), 1.0, 0.0)
    head_b = head_f.astype(BF16)
    lag = jnp.where((ri // C) == (ci // C), (ri % C) - (ci % C), -1)
    strict = lag > 0
    incl = lag >= 0
    eye = ri == ci
    eye_f = jnp.where(eye, 1.0, 0.0)
    tr = lax.broadcasted_iota(jnp.int32, (C, C), 0)
    tc = lax.broadcasted_iota(jnp.int32, (C, C), 1)
    tri = jnp.where(tc <= tr, 1.0, 0.0).astype(F32)
    cast = lambda x: x.astype(BF16)
    G = ts // C
    split = lambda ref: ref[0].reshape(G, C, W_GRP)
    tile_heads = lambda x: jnp.concatenate([x] * N_HEADS, axis=1)
    bd16 = lambda x: tile_heads(cast(x)) * head_b[None]
    bmm = lambda x, y: lax.dot_general(x, y, (((2,), (1,)), ((0,), (0,))), preferred_element_type=F32)
    bmm_nt = lambda x, y: lax.dot_general(x, y, (((2,), (2,)), ((0,), (0,))), preferred_element_type=F32)

    lw = split(lw_ref)
    cum = lax.dot_general(jnp.broadcast_to(tri[None], (G, C, C)), lw, (((2,), (1,)), ((0,), (0,))),
                          preferred_element_type=F32, precision=HI)
    cum_c = cum[:, C - 1:C, :]
    e_in = jnp.exp(cum)
    e_neg = jnp.exp(-cum)
    e_tail = jnp.exp(cum_c - cum)
    b_c = split(b_ref)
    k_c = split(k_ref)
    a_t = bd16(split(a_ref) * jnp.exp(cum - lw))
    r_t = bd16(split(r_ref) * e_in)
    b_t = bd16(b_c * e_neg)
    k_t = bd16(k_c * e_neg)
    v_bd = bd16(split(v_ref))
    bh_t = cast(jnp.swapaxes(tile_heads(b_c * e_tail) * head_f[None], 1, 2))
    kh_t = cast(jnp.swapaxes(tile_heads(k_c * e_tail) * head_f[None], 1, 2))
    a_ab = jnp.where(strict[None], bmm_nt(a_t, b_t), 0.0)
    a_ak = cast(jnp.where(strict[None], bmm_nt(a_t, k_t), 0.0))
    a_rb = cast(jnp.where(incl[None], bmm_nt(r_t, b_t), 0.0))
    a_rk = cast(jnp.where(incl[None], bmm_nt(r_t, k_t), 0.0))
    t_inv = eye_f[None] + a_ab
    pw = cast(a_ab)
    for _ in range(int(math.log2(C)) - 1):
        pw = cast(bmm(pw, pw))
        t_inv = t_inv + bmm(cast(t_inv), pw)
    t16 = cast(t_inv)
    ta = cast(bmm(t16, a_t))
    u0 = bmm(cast(bmm(t16, a_ak)), v_bd)
    o0 = bmm(a_rk, v_bd)
    s0 = bmm(kh_t, v_bd)
    o_lhs = jnp.concatenate([r_t, a_rb], axis=2)
    w_col = jnp.sum(jnp.where(eye[None], jnp.exp(cum_c), 0.0), axis=2, keepdims=True)

    st = st_ref[...]
    outs = []
    for g in range(G):
        st16 = cast(st)
        u = cast(_dot(ta[g], st16) + u0[g])
        o_bd = _dot(o_lhs[g], jnp.concatenate([st16, u], axis=0)) + o0[g]
        st = w_col[g] * st + _dot(bh_t[g], u) + s0[g]
        outs.append(o_bd[0:C] + o_bd[C:2 * C] + o_bd[2 * C:3 * C] + o_bd[3 * C:4 * C])
    st_ref[...] = st

    o = jnp.concatenate(outs, axis=0)
    avg = avg_ref[...].astype(BF16)
    mu = _dot_hilo(o, avg)
    xc = o - mu
    var = _dot_hilo(xc * xc, avg)
    on = xc * lax.rsqrt(var + RWKV_GN_EPS) * gg_ref[...] + gb_ref[...]
    r = r_ref[0]
    k = k_ref[0]
    v = v_ref[0]
    bonus = _dot_hilo(r * k * rk_ref[...], ones_ref[...].astype(BF16)) * v
    o_ref[0] = (on + bonus) * g_ref[0]


def wkv_scan(r, lw, k, v, a, b, g, rk, gg, gb, avg, ones):
    bsz, s, _ = r.shape
    ts = WKV_GROUP * WKV_CHUNK
    tile = pl.BlockSpec((1, ts, W_GRP), lambda bi, i: (bi, i, 0))
    const = lambda arr: pl.BlockSpec(arr.shape, lambda bi, i: (0,) * arr.ndim)
    return pl.pallas_call(
        functools.partial(_wkv_body, ts),
        grid=(bsz, s // ts),
        in_specs=[tile] * 7 + [const(x) for x in (rk, gg, gb, avg, ones)],
        out_specs=tile, out_shape=jax.ShapeDtypeStruct((bsz, s, W_GRP), F32),
        scratch_shapes=[pltpu.VMEM((N_HEADS * WKV_CHUNK, W_GRP), F32)],
        compiler_params=_cparams("parallel", "arbitrary"), name="wkv_scan",
    )(r, lw, k, v, a, b, g, rk, gg, gb, avg, ones)


def _gmlp_body(ts, x_ref, g_ref, b_ref, ws_ref, bs_ref, o_ref):
    x = x_ref[...]
    u = jax.nn.gelu(x[:, :W_GRP])
    v = _ln(jax.nn.gelu(x[:, W_GRP:]), g_ref[...], b_ref[...]).astype(BF16)
    tr = lax.broadcasted_iota(jnp.int32, (GMLP_CHUNK, GMLP_CHUNK), 0)
    tc = lax.broadcasted_iota(jnp.int32, (GMLP_CHUNK, GMLP_CHUNK), 1)
    ws = [jnp.where(tc <= tr, ws_ref[h], 0.0).astype(BF16) for h in range(N_HEADS)]
    for c in range(ts // GMLP_CHUNK):
        rows = slice(c * GMLP_CHUNK, (c + 1) * GMLP_CHUNK)
        mixed = [_dot(ws[h], v[rows, h * HEAD_DIM:(h + 1) * HEAD_DIM]) + bs_ref[h]
                 for h in range(N_HEADS)]
        o_ref[rows, :] = u[rows, :] * jnp.concatenate(mixed, axis=-1)


def gmlp_mixer(x2d, g, b, ws, bs):
    n = x2d.shape[0]
    ts = TM
    const = lambda a: pl.BlockSpec(a.shape, lambda i: (0,) * a.ndim)
    return pl.pallas_call(
        functools.partial(_gmlp_body, ts),
        grid=(n // ts,),
        in_specs=[pl.BlockSpec((ts, 2 * W_GRP), lambda i: (i, 0)), const(g), const(b), const(ws), const(bs)],
        out_specs=pl.BlockSpec((ts, W_GRP), lambda i: (i, 0)),
        out_shape=jax.ShapeDtypeStruct((n, W_GRP), F32),
        compiler_params=_cparams("parallel"), name="gmlp_mixer",
    )(x2d, g, b, ws, bs)


def _swap_halves(x, lane):
    w = x.shape[-1]
    half = HEAD_DIM // 2
    fwd = pltpu.roll(x, w - half, 1)
    bwd = pltpu.roll(x, half, 1)
    return jnp.where((lane % HEAD_DIM) < half, fwd, bwd)


def _nsa_prep_body(q_ref, kv_ref, m_ref, cos_ref, sin_ref,
                   q_o, qr_o, kc_o, vc_o, ks_o, vs_o, kw_o, vw_o, g_o):
    scale = HEAD_DIM ** -0.5
    q = q_ref[0]
    kv = kv_ref[0]
    cos = cos_ref[...]
    sin = sin_ref[...]
    lane_q = lax.broadcasted_iota(jnp.int32, q.shape, 1)
    q_rot = q * cos[:, :W_GRP] + _swap_halves(q, lane_q) * sin[:, :W_GRP]
    lane_kv = lax.broadcasted_iota(jnp.int32, kv.shape, 1)
    kv_rot = kv * cos + _swap_halves(kv, lane_kv) * sin
    q_o[0] = (q * scale).astype(BF16)
    qr_o[0] = (q_rot * (scale * LOG2E)).astype(BF16)
    kc_o[0] = kv[:, 0:64]
    vc_o[0] = kv[:, 64:128]
    ts = kv.shape[0]
    lane = lax.broadcasted_iota(jnp.int32, (ts, LANES), 1)
    key_blk = (pl.program_id(1) * ts + lax.broadcasted_iota(jnp.int32, (ts, 1), 0)) // SEL_LEN
    ks_o[0] = jnp.concatenate([jnp.where(lane < HEAD_DIM, kv_rot[:, 128:256], 0.0),
                               jnp.where(lane == key_blk, NEG, 0.0)], axis=1).astype(BF16)
    ones_col = jnp.where(lane == HEAD_DIM, 1.0, 0.0)
    vs_first = pltpu.roll(kv[:, 128:256], HEAD_DIM, 1)
    vs_o[0] = jnp.where(lane < HEAD_DIM, vs_first, ones_col).astype(BF16)
    kw_o[0] = kv_rot[:, 256:320].astype(BF16)
    vw_first = pltpu.roll(kv[:, 256:384], HEAD_DIM, 1)
    vw_o[0] = jnp.where(lane < HEAD_DIM, vw_first, ones_col).astype(BF16)
    g_o[0] = jax.nn.sigmoid(m_ref[0])


def nsa_prep(q, kv, misc, cos_t, sin_t):
    bsz, s, _ = q.shape
    ts = TM
    tile = lambda w: pl.BlockSpec((1, ts, w), lambda bi, i: (bi, i, 0))
    tab = pl.BlockSpec((ts, 6 * HEAD_DIM), lambda bi, i: (i, 0))
    sd = lambda w, dt: jax.ShapeDtypeStruct((bsz, s, w), dt)
    return pl.pallas_call(
        _nsa_prep_body,
        grid=(bsz, s // ts),
        in_specs=[tile(W_GRP), tile(6 * HEAD_DIM), tile(LANES), tab, tab],
        out_specs=[tile(W_GRP), tile(W_GRP), tile(HEAD_DIM), tile(HEAD_DIM), tile(2 * LANES),
                   tile(LANES), tile(HEAD_DIM), tile(LANES), tile(LANES)],
        out_shape=[sd(W_GRP, BF16), sd(W_GRP, BF16), sd(HEAD_DIM, F32), sd(HEAD_DIM, F32),
                   sd(2 * LANES, BF16), sd(LANES, BF16), sd(HEAD_DIM, BF16), sd(LANES, BF16),
                   sd(LANES, F32)],
        compiler_params=_cparams("parallel", "parallel"), name="nsa_prep",
    )(q, kv, misc, cos_t, sin_t)


def _compress_body(kc_ref, vc_ref, pe_ref, w1_ref, w2_ref, ko_ref, vo_ref):
    half = CMP_STRIDE * HEAD_DIM
    for j, (c_ref, o_ref) in enumerate(((kc_ref, ko_ref), (vc_ref, vo_ref))):
        c = c_ref[0].astype(BF16)
        w1 = w1_ref[j]
        lo = _dot(c, w1[:half])
        hi = _dot(c, w1[half:])
        nb = hi.shape[0]
        hi_next = pltpu.roll(hi, nb - 1, 0)
        pe = jnp.broadcast_to(pe_ref[j], (8, 2 * half))
        pe_term = _dot(pe, w1.astype(F32), HI)[0:1]
        h = jax.nn.gelu(lo + hi_next + pe_term)
        o_ref[0] = _dot(h.astype(BF16), w2_ref[j]).astype(BF16)


def nsa_compress(kc_r, vc_r, pe, w1, w2):
    bsz, nb, w = kc_r.shape
    blk = pl.BlockSpec((1, nb, w), lambda bi: (bi, 0, 0))
    const = lambda a: pl.BlockSpec(a.shape, lambda bi: (0,) * a.ndim)
    out = jax.ShapeDtypeStruct((bsz, nb, HEAD_DIM), BF16)
    ospec = pl.BlockSpec((1, nb, HEAD_DIM), lambda bi: (bi, 0, 0))
    return pl.pallas_call(
        _compress_body, grid=(bsz,),
        in_specs=[blk, blk, const(pe), const(w1), const(w2)],
        out_specs=[ospec, ospec], out_shape=[out, out],
        compiler_params=_cparams("parallel"), name="nsa_compress",
    )(kc_r, vc_r, pe, w1, w2)


def _stack_heads(x):
    return jnp.concatenate([x[:, h * HEAD_DIM:(h + 1) * HEAD_DIM] for h in range(N_HEADS)], axis=0)


def _nsa_attn_body(n_top, q_ref, qr_ref, kc_ref, vc_ref, ks_ref, vs_ref, kw_ref, vw_ref, g_ref,
                   ov_ref, o_ref):
    T = Q_BLOCK
    i = pl.program_id(1)
    t0 = i * T
    qs = _stack_heads(q_ref[0])
    qrs = _stack_heads(qr_ref[0])
    pos = t0 + lax.broadcasted_iota(jnp.int32, (T, 1), 0)
    tile4 = lambda x: jnp.concatenate([x] * N_HEADS, axis=0)

    kc = kc_ref[0]
    n_cmp = kc.shape[0]
    cmp_end = lax.broadcasted_iota(jnp.int32, (1, n_cmp), 1) * CMP_STRIDE + (CMP_LEN - 1)
    valid_c = tile4(jnp.where(cmp_end <= pos, 1.0, 0.0))
    s_c = jnp.where(valid_c > 0.5, _dot_nt(qs, kc), NEG)
    p_c = jnp.exp(s_c - jnp.max(s_c, axis=-1, keepdims=True)) * valid_c
    p_c = p_c / jnp.maximum(jnp.sum(p_c, axis=-1, keepdims=True), 1e-20)
    o_c = _dot(p_c.astype(BF16), vc_ref[0])
    p_sum = p_c[0:T] + p_c[T:2 * T] + p_c[2 * T:3 * T] + p_c[3 * T:4 * T]
    ps_hi, ps_lo = _split_bf16(p_sum)
    ov = ov_ref[...].astype(BF16)
    imp_t = _dot_nt(ov, ps_hi) + _dot_nt(ov, ps_lo)

    WK = WINDOW + T
    start = pl.multiple_of(jnp.maximum(t0 - WINDOW, 0), T)
    kw = kw_ref[0, pl.ds(start, WK), :]
    vw = vw_ref[0, pl.ds(start, WK), :]
    kpos_w = start + lax.broadcasted_iota(jnp.int32, (1, WK), 1)
    bias_w = tile4(jnp.where(kpos_w <= pos, jnp.where(kpos_w > pos - WINDOW, 0.0, NEG), NEG))
    s_w = _dot_nt(qrs, kw) + bias_w
    p_w = jnp.exp2(s_w - jnp.max(s_w, axis=-1, keepdims=True))
    acc_w = _dot(p_w.astype(BF16), vw)
    o_w = acc_w[:, :HEAD_DIM] / acc_w[:, HEAD_DIM:HEAD_DIM + 1]

    n_sel = imp_t.shape[0]
    jj = lax.broadcasted_iota(jnp.int32, (n_sel, T), 0)
    blk = (t0 + lax.broadcasted_iota(jnp.int32, (1, T), 1)) // SEL_LEN
    val = jnp.where(jj == blk, 3e38, jnp.where(jj == 0, 3e38, jnp.where(jj <= blk, imp_t, -1.0)))
    sel_t = jnp.zeros((n_sel, T), F32)
    for _ in range(n_top):
        mx = jnp.max(val, axis=0, keepdims=True)
        idx = jnp.min(jnp.where(val == mx, jj, n_sel), axis=0, keepdims=True)
        hit = jj == idx
        sel_t = jnp.where(hit, 1.0, sel_t)
        val = jnp.where(hit, -2.0, val)
    sel = sel_t.T

    KT = SEL_KT
    n_tiles = (t0 + T + KT - 1) // KT
    unsel = tile4((1.0 - sel).astype(BF16))
    q_aug = jnp.concatenate([qrs, jnp.zeros((N_HEADS * T, HEAD_DIM), BF16), unsel], axis=1)
    kcol = lax.broadcasted_iota(jnp.int32, (1, KT), 1)

    def sel_tile(jt, m, acc):
        k0 = pl.multiple_of(jt * KT, KT)
        s = _dot_nt(q_aug, ks_ref[0, pl.ds(k0, KT), :]) + tile4(jnp.where(k0 + kcol <= pos, 0.0, NEG))
        m_new = jnp.maximum(m, jnp.max(s, axis=-1, keepdims=True))
        p = jnp.exp2(s - m_new)
        acc = jnp.exp2(m - m_new) * acc + _dot(p.astype(BF16), vs_ref[0, pl.ds(k0, KT), :])
        return m_new, acc

    init = (jnp.full((N_HEADS * T, 1), NEG, F32), jnp.zeros((N_HEADS * T, LANES), F32))
    _, acc_s = lax.fori_loop(0, n_tiles, lambda jt, c: sel_tile(jt, *c), init)
    o_s = acc_s[:, :HEAD_DIM] / acc_s[:, HEAD_DIM:HEAD_DIM + 1]

    g = g_ref[0]
    outs = []
    for h in range(N_HEADS):
        rows = slice(h * T, (h + 1) * T)
        outs.append(g[:, 3 * h:3 * h + 1] * o_c[rows] + g[:, 3 * h + 1:3 * h + 2] * o_s[rows]
                    + g[:, 3 * h + 2:3 * h + 3] * o_w[rows])
    o_ref[0] = jnp.concatenate(outs, axis=-1)


def nsa_attention(q, qr, k_cmp, v_cmp, ks, vs, kw, vw, gates, overlap):
    bsz, s, _ = q.shape
    n_top = min(SEL_TOPN, s // SEL_LEN)
    n_cmp = k_cmp.shape[1]
    qtile = lambda w: pl.BlockSpec((1, Q_BLOCK, w), lambda bi, i: (bi, i, 0))
    full = lambda rows, w: pl.BlockSpec((1, rows, w), lambda bi, i: (bi, 0, 0))
    return pl.pallas_call(
        functools.partial(_nsa_attn_body, n_top),
        grid=(bsz, s // Q_BLOCK),
        in_specs=[qtile(W_GRP), qtile(W_GRP), full(n_cmp, HEAD_DIM), full(n_cmp, HEAD_DIM),
                  full(s, 2 * LANES), full(s, LANES), full(s, HEAD_DIM), full(s, LANES),
                  qtile(LANES), pl.BlockSpec(overlap.shape, lambda bi, i: (0, 0))],
        out_specs=qtile(W_GRP), out_shape=jax.ShapeDtypeStruct((bsz, s, W_GRP), F32),
        compiler_params=_cparams("parallel", "parallel"), name="nsa_attention",
    )(q, qr, k_cmp, v_cmp, ks, vs, kw, vw, gates, overlap)


def _outproj_body(alpha, a_ref, b_ref, c_ref, d_ref, x_ref, w_ref, g_ref, bb_ref, o_ref):
    mix = jnp.zeros(x_ref.shape, F32)
    for j, part in enumerate((a_ref, b_ref, c_ref, d_ref)):
        mix = mix + _dot(part[...].astype(BF16), w_ref[j * W_GRP:(j + 1) * W_GRP, :])
    o_ref[...] = _ln(alpha * x_ref[...] + mix, g_ref[...], bb_ref[...])


def out_proj(parts, x2d, w, g, b, alpha):
    n, d = x2d.shape
    row = lambda w_: pl.BlockSpec((TM, w_), lambda i: (i, 0))
    const = lambda a: pl.BlockSpec(a.shape, lambda i: (0,) * a.ndim)
    return pl.pallas_call(
        functools.partial(_outproj_body, alpha),
        grid=(n // TM,),
        in_specs=[row(W_GRP)] * 4 + [row(d), const(w), const(g), const(b)],
        out_specs=row(d), out_shape=jax.ShapeDtypeStruct((n, d), F32),
        compiler_params=_cparams("parallel"), name="out_proj",
    )(*parts, x2d, w, g, b)


def _memkv_body(m_ref, wk_ref, wv_ref, k_o, v_o):
    mb = m_ref[...].astype(BF16)
    k_o[...] = _dot(mb, wk_ref[...]).astype(BF16)
    v_o[...] = _dot(mb, wv_ref[...]).astype(BF16)


def mem_kv(mem2d, wk, wv):
    n, d = mem2d.shape
    full = lambda a: pl.BlockSpec(a.shape, lambda i: (0, 0))
    out = jax.ShapeDtypeStruct((n, d), BF16)
    return pl.pallas_call(
        _memkv_body, grid=(1,),
        in_specs=[full(mem2d), full(wk), full(wv)],
        out_specs=[pl.BlockSpec((n, d), lambda i: (0, 0))] * 2, out_shape=[out, out],
        compiler_params=_cparams("arbitrary"), name="mem_kv",
    )(mem2d, wk, wv)


def _xattn_body(alpha, x_ref, k_ref, v_ref, wq_ref, wo_ref, g_ref, b_ref, o_ref):
    x = x_ref[0]
    d = x.shape[-1]
    hd = d // N_MEM_HEADS
    q = (_dot(x.astype(BF16), wq_ref[...]) * (hd ** -0.5)).astype(BF16)
    k = k_ref[0]
    v = v_ref[0]
    heads = []
    for h in range(N_MEM_HEADS):
        cs = slice(h * hd, (h + 1) * hd)
        s = _dot_nt(q[:, cs], k[:, cs])
        p = jnp.exp(s - jnp.max(s, axis=-1, keepdims=True))
        p = p / jnp.sum(p, axis=-1, keepdims=True)
        heads.append(_dot(p.astype(BF16), v[:, cs]))
    o = jnp.concatenate(heads, axis=-1).astype(BF16)
    o_ref[0] = _ln(alpha * x + _dot(o, wo_ref[...]), g_ref[...], b_ref[...])


def cross_attention(x, k, v, wq, wo, g, b, alpha):
    bsz, s, d = x.shape
    m = k.shape[1]
    const = lambda a: pl.BlockSpec(a.shape, lambda bi, i: (0,) * a.ndim)
    return pl.pallas_call(
        functools.partial(_xattn_body, alpha),
        grid=(bsz, s // TM),
        in_specs=[pl.BlockSpec((1, TM, d), lambda bi, i: (bi, i, 0)),
                  pl.BlockSpec((1, m, d), lambda bi, i: (bi, 0, 0)),
                  pl.BlockSpec((1, m, d), lambda bi, i: (bi, 0, 0)),
                  const(wq), const(wo), const(g), const(b)],
        out_specs=pl.BlockSpec((1, TM, d), lambda bi, i: (bi, i, 0)),
        out_shape=jax.ShapeDtypeStruct((bsz, s, d), F32),
        compiler_params=_cparams("parallel", "parallel"), name="cross_attention",
    )(x, k, v, wq, wo, g, b)


PER_GRP = N_EXPERTS // N_EXPERT_GROUPS
LPOS_LANE = PER_GRP
SUB = 8


def _router_body(x_ref, rwt_ref, rb_ref, upper_ref, tok_o, lrow_o, cnt_o):
    tm = x_ref.shape[0]
    logits = _dot_nt(rwt_ref[...], x_ref[...], HI)
    ex = jnp.exp(logits - jnp.max(logits, axis=0, keepdims=True))
    probs = ex / jnp.sum(ex, axis=0, keepdims=True)
    sel = probs + rb_ref[...]
    srow = [sel[e:e + 1] for e in range(N_EXPERTS)]
    prow = [probs[e:e + 1] for e in range(N_EXPERTS)]
    gscore = []
    for g in range(N_EXPERT_GROUPS):
        r = srow[g * PER_GRP:(g + 1) * PER_GRP]
        best = None
        for a in range(PER_GRP):
            for b in range(a + 1, PER_GRP):
                best = r[a] + r[b] if best is None else jnp.maximum(best, r[a] + r[b])
        gscore.append(best)
    g_idx = jnp.zeros((1, tm), jnp.int32)
    top = gscore[0]
    for g in range(1, N_EXPERT_GROUPS):
        better = gscore[g] > top
        g_idx = jnp.where(better, g, g_idx)
        top = jnp.where(better, gscore[g], top)

    def of_group(rows, e):
        out = rows[e]
        for g in range(1, N_EXPERT_GROUPS):
            out = jnp.where(g_idx == g, rows[g * PER_GRP + e], out)
        return out

    sg = [of_group(srow, e) for e in range(PER_GRP)]
    pg = [of_group(prow, e) for e in range(PER_GRP)]
    w = []
    for e in range(PER_GRP):
        rank = jnp.zeros((1, tm), F32)
        for o in range(PER_GRP):
            if o != e:
                ahead = (sg[o] >= sg[e]) if o < e else (sg[o] > sg[e])
                rank = rank + jnp.where(ahead, 1.0, 0.0)
        w.append(jnp.where(rank < 1.5, pg[e], 0.0))
    w_sum = w[0] + w[1] + w[2] + w[3]
    row8 = lax.broadcasted_iota(jnp.int32, (SUB, tm), 0)
    onehot = jnp.where(row8 == g_idx, 1.0, 0.0)
    before = _dot(onehot.astype(BF16), upper_ref[...])
    cnt = jnp.sum(onehot, axis=1, keepdims=True)
    offs = [jnp.zeros((1, 1), F32)]
    for g in range(1, N_EXPERT_GROUPS):
        offs.append(offs[-1] + cnt[g - 1:g])
    lpos = jnp.zeros((1, tm), F32)
    for g in range(N_EXPERT_GROUPS):
        lpos = lpos + onehot[g:g + 1] * (offs[g] + before[g:g + 1])
    tok = jnp.zeros((SUB, tm), F32)
    for e in range(PER_GRP):
        tok = jnp.where(row8 == e, w[e] / w_sum, tok)
    tok = jnp.where(row8 == LPOS_LANE, lpos, tok)
    tok_o[...] = jnp.concatenate([tok, jnp.zeros((LANES - SUB, tm), F32)], axis=0).T
    lrow_o[0] = lpos.astype(jnp.int32)
    rowc = lax.broadcasted_iota(jnp.int32, (SUB, LANES), 0)
    stats = jnp.zeros((SUB, LANES), F32)
    for g in range(N_EXPERT_GROUPS):
        stats = jnp.where(rowc == g, cnt[g:g + 1], stats)
        stats = jnp.where(rowc == N_EXPERT_GROUPS + g, offs[g], stats)
    cnt_o[0] = stats.astype(jnp.int32)


def moe_router(x2d, rwt, rb_col, upper):
    n, d = x2d.shape
    tm = upper.shape[0]
    nt = n // tm
    const = lambda a: pl.BlockSpec(a.shape, lambda i: (0, 0))
    return pl.pallas_call(
        _router_body, grid=(nt,),
        in_specs=[pl.BlockSpec((tm, d), lambda i: (i, 0)), const(rwt), const(rb_col), const(upper)],
        out_specs=[pl.BlockSpec((tm, LANES), lambda i: (i, 0)),
                   pl.BlockSpec((1, 1, tm), lambda i: (i, 0, 0)),
                   pl.BlockSpec((1, SUB, LANES), lambda i: (i, 0, 0))],
        out_shape=[jax.ShapeDtypeStruct((n, LANES), F32), jax.ShapeDtypeStruct((nt, 1, tm), jnp.int32),
                   jax.ShapeDtypeStruct((nt, SUB, LANES), jnp.int32)],
        compiler_params=_cparams("parallel"), name="moe_router",
    )(x2d, rwt, rb_col, upper)


def _split_bf16(x):
    hi = x.astype(BF16)
    return hi, (x - hi.astype(F32)).astype(BF16)


def _moe_body(alpha, offs_ref, cnts_ref, x_ref, tok_ref, lrow_ref, wg_ref, wu_ref, wd_ref, g_ref, b_ref, o_ref,
              xs_ref, gs_ref, acc_ref):
    i = pl.program_id(0)
    e = pl.program_id(1)
    tm = x_ref.shape[0]

    @pl.when(e == 0)
    def _():
        slot = lax.broadcasted_iota(jnp.int32, (tm, tm), 0)
        perm = jnp.where(slot == lrow_ref[0], 1.0, 0.0).astype(BF16)
        xs_ref[...] = _dot(perm, x_ref[...].astype(BF16)).astype(BF16)
        t_hi, t_lo = _split_bf16(tok_ref[...])
        gs_ref[...] = _dot(perm, t_hi) + _dot(perm, t_lo)
        acc_ref[...] = jnp.zeros_like(acc_ref)

    grp = e // PER_GRP
    off = offs_ref[i * N_EXPERT_GROUPS + grp]
    end = off + cnts_ref[i * N_EXPERT_GROUPS + grp]
    lane = lax.broadcasted_iota(jnp.int32, (MOE_BLK, LANES), 1)
    for s in range(tm // MOE_BLK):
        lo = s * MOE_BLK

        @pl.when(jnp.logical_and(off < lo + MOE_BLK, end > lo))
        def _(lo=lo):
            rows = slice(lo, lo + MOE_BLK)
            xs = xs_ref[rows, :]
            gt = _dot(xs, wg_ref[0, 0])
            h = (gt * jax.nn.sigmoid(gt) * _dot(xs, wu_ref[0, 0])).astype(BF16)
            ridx = lo + lax.broadcasted_iota(jnp.int32, (MOE_BLK, 1), 0)
            in_run = jnp.where(ridx >= off, jnp.where(ridx < end, 1.0, 0.0), 0.0)
            gcol = jnp.sum(jnp.where(lane == e % PER_GRP, gs_ref[rows, :], 0.0), axis=-1, keepdims=True)
            acc_ref[rows, :] += (gcol * in_run) * _dot(h, wd_ref[0, 0])

    @pl.when(e == pl.num_programs(1) - 1)
    def _():
        lcol = tok_ref[:, LPOS_LANE:LPOS_LANE + 1].astype(jnp.int32)
        slot = lax.broadcasted_iota(jnp.int32, (tm, tm), 1)
        unperm = jnp.where(slot == lcol, 1.0, 0.0).astype(BF16)
        a_hi, a_lo = _split_bf16(acc_ref[...])
        y = _dot(unperm, a_hi) + _dot(unperm, a_lo)
        o_ref[...] = _ln(alpha * x_ref[...] + y, g_ref[...], b_ref[...])


def moe_ffn(x2d, tok, lrow, offs, cnts, layer, wg, wu, wd, g, b, alpha):
    n, d = x2d.shape
    _, ne, de, _ = wd.shape
    tm = lrow.shape[2]
    const = lambda a: pl.BlockSpec(a.shape, lambda i, e, o, c: (0,) * a.ndim)
    grid_spec = pltpu.PrefetchScalarGridSpec(
        num_scalar_prefetch=2, grid=(n // tm, ne),
        in_specs=[pl.BlockSpec((tm, d), lambda i, e, o, c: (i, 0)),
                  pl.BlockSpec((tm, LANES), lambda i, e, o, c: (i, 0)),
                  pl.BlockSpec((1, 1, tm), lambda i, e, o, c: (i, 0, 0)),
                  pl.BlockSpec((1, 1, d, de), lambda i, e, o, c: (layer, e, 0, 0)),
                  pl.BlockSpec((1, 1, d, de), lambda i, e, o, c: (layer, e, 0, 0)),
                  pl.BlockSpec((1, 1, de, d), lambda i, e, o, c: (layer, e, 0, 0)),
                  const(g), const(b)],
        out_specs=pl.BlockSpec((tm, d), lambda i, e, o, c: (i, 0)),
        scratch_shapes=[pltpu.VMEM((tm, d), BF16), pltpu.VMEM((tm, LANES), F32), pltpu.VMEM((tm, d), F32)])
    return pl.pallas_call(
        functools.partial(_moe_body, alpha), grid_spec=grid_spec,
        out_shape=jax.ShapeDtypeStruct((n, d), F32),
        compiler_params=_cparams("parallel", "arbitrary"), name="moe_ffn",
    )(offs, cnts, x2d, tok, lrow, wg, wu, wd, g, b)


def _rope_tables(s):
    inv = ROPE_THETA ** (-jnp.arange(0, HEAD_DIM, 2, dtype=F32) / HEAD_DIM)
    ang = jnp.arange(s, dtype=F32)[:, None] * inv[None, :]
    cos, sin = jnp.cos(ang), jnp.sin(ang)
    cos_h = jnp.concatenate([cos, cos], axis=-1)
    sin_h = jnp.concatenate([-sin, sin], axis=-1)
    return jnp.tile(cos_h, (1, 6)), jnp.tile(sin_h, (1, 6))


def _overlap_matrix(s, n_cmp_pad):
    n_sel = s // SEL_LEN
    assert n_sel <= LANES
    cmp_start = jnp.arange(n_cmp_pad) * CMP_STRIDE
    sel_start = jnp.arange(LANES) * SEL_LEN
    ov = jnp.clip(jnp.minimum(cmp_start[None, :] + CMP_LEN, sel_start[:, None] + SEL_LEN)
                  - jnp.maximum(cmp_start[None, :], sel_start[:, None]), 0, None).astype(F32) / CMP_LEN
    n_cmp = s // CMP_STRIDE - (CMP_LEN // CMP_STRIDE - 1)
    real = (jnp.arange(n_cmp_pad)[None, :] < n_cmp) & (jnp.arange(LANES)[:, None] < n_sel)
    return jnp.where(real, ov, 0.0)


def _pad_rows(w, lo, total):
    out = jnp.zeros((total, w.shape[1]), w.dtype)
    return out.at[lo:lo + w.shape[0]].set(w)


def kernel(x, mem, ln_in_g, ln_in_b, w_in, w_out, conv_w, conv_b, conv_gn_g, conv_gn_b, rwkv_mu, rwkv_w0, rwkv_w_up, rwkv_a0, rwkv_a_up, rwkv_g_up, rwkv_k_k, rwkv_k_a, rwkv_r_k, rwkv_gn_g, rwkv_gn_b, rwkv_v_down, rwkv_v_mu, rwkv_v0, rwkv_v_up, gmlp_ln_g, gmlp_ln_b, gmlp_w_s, gmlp_b_s, nsa_pe_k, nsa_w1_k, nsa_w2_k, nsa_pe_v, nsa_w1_v, nsa_w2_v, ln1_g, ln1_b, xa_wq, xa_wk, xa_wv, xa_wo, ln2_g, ln2_b, router_w, router_bias, moe_w_gate, moe_w_up, moe_w_down, ln3_g, ln3_b):
    bsz, s, d = x.shape
    depth = w_in.shape[0]
    n = bsz * s
    alpha = (2 * depth) ** 0.25
    row = lambda a: a.reshape(1, -1)

    cos_t, sin_t = _rope_tables(s)
    n_blk = s // CMP_STRIDE
    overlap = _overlap_matrix(s, n_blk)
    avg64 = _group_avg_matrix(W_GRP, HEAD_DIM)
    ones64 = avg64 * HEAD_DIM
    moe_tm = min(MOE_TM, n)
    t_idx = jnp.arange(moe_tm)
    upper = (t_idx[:, None] < t_idx[None, :]).astype(BF16)
    mem2d = mem.reshape(bsz * mem.shape[1], d)

    wg_b, wu_b, wd_b = moe_w_gate.astype(BF16), moe_w_up.astype(BF16), moe_w_down.astype(BF16)

    xs = x.reshape(n, d)
    v_first = None
    for l in range(depth):
        w_pad = jnp.zeros((d, IN_PAD), F32).at[:, :w_in.shape[2]].set(w_in[l])
        if l > 0:
            lo = IN_SPLITS[-1][1] + MISC_VD_OFF
            w_pad = w_pad.at[:, lo:lo + rwkv_v_down.shape[2]].set(rwkv_v_down[l - 1])
        outs = in_proj(xs, row(ln_in_g), row(ln_in_b), w_pad.astype(BF16), apply_ln=(l == 0))
        if l == 0:
            xs, outs = outs[0], outs[1:]
        conv_in, rwkv_in, gmlp_in, q_in, kv_in, misc = outs
        b3 = lambda a: a.reshape(bsz, s, a.shape[-1])

        out_a = conv_mixer(b3(conv_in), conv_w[l], row(conv_b[l]), row(conv_gn_g[l]), row(conv_gn_b[l]), avg64)

        rp = {"mu": row(rwkv_mu[l]), "w0": row(rwkv_w0[l]), "a0": row(rwkv_a0[l]),
              "wup": _pad_rows(rwkv_w_up[l], 0, W_GRP), "aup": _pad_rows(rwkv_a_up[l], 64, W_GRP),
              "gup": _pad_rows(rwkv_g_up[l], 128, W_GRP),
              "kk": row(rwkv_k_k[l]), "ka": row(rwkv_k_a[l]), "ones": ones64}
        if l > 0:
            rp["vmu"] = jnp.zeros((1, LANES), F32).at[0, MISC_VD_OFF:MISC_VD_OFF + rwkv_v_mu.shape[1]].set(rwkv_v_mu[l - 1])
            rp["v0"] = row(rwkv_v0[l - 1])
            rp["vup"] = _pad_rows(rwkv_v_up[l - 1], MISC_VD_OFF, LANES)
        r_, lw_, k_, v_, a_, b_, g_ = rwkv_prep(b3(rwkv_in), b3(misc), v_first, rp)
        if l == 0:
            v_first = v_
        out_b = wkv_scan(r_, lw_, k_, v_, a_, b_, g_, row(rwkv_r_k[l]), row(rwkv_gn_g[l]),
                         row(rwkv_gn_b[l]), avg64, ones64)

        out_c = gmlp_mixer(gmlp_in, row(gmlp_ln_g[l]), row(gmlp_ln_b[l]), gmlp_w_s[l],
                           gmlp_b_s[l].reshape(N_HEADS, GMLP_CHUNK, 1))

        q_b, qr_b, kc, vc, ks, vs, kw, vw, gates = nsa_prep(b3(q_in), b3(kv_in), b3(misc), cos_t, sin_t)
        pe = jnp.stack([nsa_pe_k[l].reshape(1, -1), nsa_pe_v[l].reshape(1, -1)])
        w1 = jnp.stack([nsa_w1_k[l], nsa_w1_v[l]]).astype(BF16)
        w2 = jnp.stack([nsa_w2_k[l], nsa_w2_v[l]]).astype(BF16)
        k_cmp, v_cmp = nsa_compress(kc.reshape(bsz, n_blk, CMP_STRIDE * HEAD_DIM),
                                    vc.reshape(bsz, n_blk, CMP_STRIDE * HEAD_DIM), pe, w1, w2)
        out_d = nsa_attention(q_b, qr_b, k_cmp, v_cmp, ks, vs, kw, vw, gates, overlap)

        flat = lambda a: a.reshape(n, W_GRP)
        xs = out_proj((flat(out_a), flat(out_b), out_c, flat(out_d)), xs, w_out[l].astype(BF16),
                      row(ln1_g[l]), row(ln1_b[l]), alpha)

        mk, mv = mem_kv(mem2d, xa_wk[l].astype(BF16), xa_wv[l].astype(BF16))
        m_len = mem.shape[1]
        xs = cross_attention(xs.reshape(bsz, s, d), mk.reshape(bsz, m_len, d), mv.reshape(bsz, m_len, d),
                             xa_wq[l].astype(BF16), xa_wo[l].astype(BF16),
                             row(ln2_g[l]), row(ln2_b[l]), alpha).reshape(n, d)

        tok, lrow, stats = moe_router(xs, router_w.T, router_bias.reshape(-1, 1), upper)
        cnts = stats[:, :N_EXPERT_GROUPS, 0].reshape(-1)
        offs = stats[:, N_EXPERT_GROUPS:2 * N_EXPERT_GROUPS, 0].reshape(-1)
        xs = moe_ffn(xs, tok, lrow, offs, cnts, l, wg_b, wu_b, wd_b,
                     row(ln3_g[l]), row(ln3_b[l]), alpha)
    return xs.reshape(bsz, s, d)
```

```python
import functools
import math

import jax
import jax.numpy as jnp
from jax import lax
from jax.experimental import pallas as pl
from jax.experimental.pallas import tpu as pltpu

F32 = jnp.float32
BF16 = jnp.bfloat16
HI = lax.Precision.HIGHEST

HEAD_DIM = 64
N_HEADS = 4
W_GRP = 256
CONV_WIDTH = 31
GMLP_CHUNK = 128
CMP_LEN = 32
CMP_STRIDE = 16
SEL_LEN = 64
SEL_TOPN = 16
WINDOW = 512
Q_BLOCK = 256
ROPE_THETA = 10000.0
N_MEM_HEADS = 4
N_EXPERTS = 16
N_EXPERT_GROUPS = 4
LN_EPS = 1e-5
RWKV_GN_EPS = 64e-5
NEG = -1e30
LOG2E = math.log2(math.e)
LANES = 128
WKV_CHUNK = 64
WKV_GROUP = 8

TM = 512
MOE_TM = 1024
MOE_BLK = 256
MOE_EPS = 4
SEL_KT = 1024
VMEM_LIMIT = 56 * 1024 * 1024


def _cparams(*sem):
    return pltpu.CompilerParams(dimension_semantics=sem, vmem_limit_bytes=VMEM_LIMIT)


def _ln(x, g, b, eps=LN_EPS):
    mu = jnp.mean(x, axis=-1, keepdims=True)
    xc = x - mu
    var = jnp.mean(xc * xc, axis=-1, keepdims=True)
    return xc * lax.rsqrt(var + eps) * g + b


def _dot(a, b, precision=None):
    return jnp.dot(a, b, preferred_element_type=F32, precision=precision)


def _dot_nt(a, b, precision=None):
    return lax.dot_general(a, b, (((1,), (1,)), ((), ())),
                           preferred_element_type=F32, precision=precision)


def _dot_hilo(x, w_bf16):
    hi = x.astype(BF16)
    lo = (x - hi.astype(F32)).astype(BF16)
    return _dot(hi, w_bf16) + _dot(lo, w_bf16)


def _group_avg_matrix(width, group):
    r = jnp.arange(width)[:, None] // group
    c = jnp.arange(width)[None, :] // group
    return jnp.where(r == c, 1.0 / group, 0.0).astype(F32)


IN_SPLITS = (("conv", 0, 512), ("rwkv", 512, 1536), ("gmlp", 1536, 2048),
             ("q", 2048, 2304), ("kv", 2304, 2688), ("misc", 2688, 2816))
MISC_VD_OFF = 32


def _inproj_body(apply_ln, x_ref, g_ref, b_ref, w_ref, wm_ref, *rest):
    outs, wb_ref = rest[:-1], rest[-1]
    main = IN_SPLITS[-1][1]

    @pl.when(pl.program_id(0) == 0)
    def _():
        wb_ref[...] = w_ref[0, :, :main].astype(BF16)

    x = x_ref[...]
    if apply_ln:
        x = _ln(x, g_ref[...], b_ref[...])
        outs[0][...] = x
        outs = outs[1:]
    xb = x.astype(BF16)
    for o_ref, (_, lo, hi) in zip(outs[:-1], IN_SPLITS[:-1]):
        o_ref[...] = _dot(xb, wb_ref[:, lo:hi])
    outs[-1][...] = _dot(xb, wm_ref[...])


def in_proj(x2d, g, b, w_in, layer, w_misc, apply_ln):
    n, d = x2d.shape
    row = lambda w: pl.BlockSpec((TM, w), lambda i: (i, 0))
    const = lambda s: pl.BlockSpec(s, lambda i: (0, 0))
    out_shapes = [jax.ShapeDtypeStruct((n, hi - lo), F32) for _, lo, hi in IN_SPLITS]
    out_specs = [row(hi - lo) for _, lo, hi in IN_SPLITS]
    if apply_ln:
        out_shapes = [jax.ShapeDtypeStruct((n, d), F32)] + out_shapes
        out_specs = [row(d)] + out_specs
    return pl.pallas_call(
        functools.partial(_inproj_body, apply_ln),
        grid=(n // TM,),
        in_specs=[row(d), const((1, d)), const((1, d)),
                  pl.BlockSpec((1,) + w_in.shape[1:], lambda i: (layer, 0, 0), pipeline_mode=pl.Buffered(1)),
                  const(w_misc.shape)],
        out_specs=out_specs, out_shape=out_shapes,
        scratch_shapes=[pltpu.VMEM((d, IN_SPLITS[-1][1]), BF16)],
        compiler_params=_cparams("arbitrary"), name="in_proj",
    )(x2d, g, b, w_in, w_misc)


CONV_HALO = 32


def _conv_body(ts, cur_ref, halo_ref, w_ref, b_ref, gg_ref, gb_ref, avg_ref, o_ref, hbuf, shifted):
    i = pl.program_id(1)
    cur = cur_ref[0]
    hal = halo_ref[0]
    h = cur[:, :W_GRP] * jax.nn.sigmoid(cur[:, W_GRP:])
    hh = hal[:, :W_GRP] * jax.nn.sigmoid(hal[:, W_GRP:])
    hbuf[0:CONV_HALO, :] = jnp.where(i > 0, hh, 0.0)
    hbuf[CONV_HALO:, :] = h
    acc = jnp.zeros((ts, W_GRP), F32)
    base = CONV_HALO - (CONV_WIDTH - 1)
    for phase in range(SUB):
        taps = [j for j in range(CONV_WIDTH) if (base + j) % SUB == phase]
        span = (base + taps[-1]) // SUB * SUB + ts
        shifted[0:span, :] = hbuf[phase:phase + span, :]
        for j in taps:
            lo = (base + j) // SUB * SUB
            acc = acc + w_ref[j:j + 1, :] * shifted[lo:lo + ts, :]
    acc = acc + b_ref[...]
    avg = avg_ref[...].astype(BF16)
    mu = _dot_hilo(acc, avg)
    xc = acc - mu
    var = _dot_hilo(xc * xc, avg)
    y = xc * lax.rsqrt(var + LN_EPS) * gg_ref[...] + gb_ref[...]
    o_ref[0] = y * jax.nn.sigmoid(y)


def conv_mixer(conv_in, w, b, gg, gb, avg):
    bsz, s, _ = conv_in.shape
    ts = TM
    r = ts // CONV_HALO
    const = lambda shp: pl.BlockSpec(shp, lambda bi, i: (0, 0))
    return pl.pallas_call(
        functools.partial(_conv_body, ts),
        grid=(bsz, s // ts),
        in_specs=[pl.BlockSpec((1, ts, 2 * W_GRP), lambda bi, i: (bi, i, 0)),
                  pl.BlockSpec((1, CONV_HALO, 2 * W_GRP),
                               lambda bi, i: (bi, jnp.maximum(i * r - 1, 0), 0)),
                  const((CONV_WIDTH, W_GRP)), const((1, W_GRP)), const((1, W_GRP)),
                  const((1, W_GRP)), const((W_GRP, W_GRP))],
        out_specs=pl.BlockSpec((1, ts, W_GRP), lambda bi, i: (bi, i, 0)),
        out_shape=jax.ShapeDtypeStruct((bsz, s, W_GRP), F32),
        scratch_shapes=[pltpu.VMEM((ts + CONV_HALO, W_GRP), F32), pltpu.VMEM((ts + CONV_HALO, W_GRP), F32)],
        compiler_params=_cparams("parallel", "parallel"), name="conv_mixer",
    )(conv_in, conv_in, w, b, gg, gb, avg)


SHIFT_HALO = 8


def _shift_prev(buf, cur, halo, first):
    ts = cur.shape[0]
    buf[0:SHIFT_HALO, :] = jnp.where(first, 0.0, halo)
    buf[SHIFT_HALO:, :] = cur
    return buf[SHIFT_HALO - 1:SHIFT_HALO - 1 + ts, :]


def _rwkv_prep_body(has_vfirst, *refs):
    if has_vfirst:
        (c_ref, ch_ref, m_ref, mh_ref, vf_ref, mu_ref, w0_ref, a0_ref, wup_ref, aup_ref, gup_ref,
         kk_ref, ka_ref, ones_ref, vmu_ref, v0_ref, vup_ref,
         r_o, lw_o, k_o, v_o, a_o, b_o, g_o, buf, mbuf) = refs
    else:
        (c_ref, ch_ref, mu_ref, w0_ref, a0_ref, wup_ref, aup_ref, gup_ref,
         kk_ref, ka_ref, ones_ref,
         r_o, lw_o, k_o, v_o, a_o, b_o, g_o, buf) = refs
    first = pl.program_id(1) == 0
    cur = c_ref[0]
    prev = _shift_prev(buf, cur, ch_ref[0], first)
    y = cur + mu_ref[...] * (prev - cur)
    r = y[:, 0:256]
    k = y[:, 256:512]
    v = y[:, 512:768]
    lora = y[:, 768:1024]
    w = w0_ref[...] + _dot(jnp.tanh(lora), wup_ref[...], HI)
    a = jax.nn.sigmoid(a0_ref[...] + _dot(lora, aup_ref[...], HI))
    g = _dot(jax.nn.sigmoid(lora), gup_ref[...], HI)
    z = -w
    sp = jnp.maximum(z, 0.0) + jnp.log(1.0 + jnp.exp(-jnp.abs(z)))
    lw = -jnp.exp(-sp - 0.5)
    if has_vfirst:
        mc = m_ref[0]
        mprev = _shift_prev(mbuf, mc, mh_ref[0], first)
        vd = mc + vmu_ref[...] * (mprev - mc)
        v_mix = jax.nn.sigmoid(v0_ref[...] + _dot(vd, vup_ref[...], HI))
        v = v + (vf_ref[0] - v) * v_mix
    kk = k * kk_ref[...]
    n2 = _dot_hilo(kk * kk, ones_ref[...].astype(BF16))
    kk = kk / jnp.maximum(jnp.sqrt(n2), 1e-12)
    k2 = k * (1.0 + (a - 1.0) * ka_ref[...])
    r_o[0] = r
    lw_o[0] = lw
    k_o[0] = k2
    v_o[0] = v
    a_o[0] = -kk
    b_o[0] = kk * a
    g_o[0] = g


def rwkv_prep(cols, misc, v_first, p):
    bsz, s, c = cols.shape
    ts = TM
    has_vfirst = v_first is not None
    r8 = ts // SHIFT_HALO
    tile = lambda w: pl.BlockSpec((1, ts, w), lambda bi, i: (bi, i, 0))
    halo = lambda w: pl.BlockSpec((1, SHIFT_HALO, w), lambda bi, i: (bi, jnp.maximum(i * r8 - 1, 0), 0))
    const = lambda a: pl.BlockSpec(a.shape, lambda bi, i: (0,) * a.ndim)
    params = [p["mu"], p["w0"], p["a0"], p["wup"], p["aup"], p["gup"], p["kk"], p["ka"], p["ones"]]
    if has_vfirst:
        inputs = [cols, cols, misc, misc, v_first] + params + [p["vmu"], p["v0"], p["vup"]]
        in_specs = ([tile(c), halo(c), tile(LANES), halo(LANES), tile(W_GRP)]
                    + [const(a) for a in params + [p["vmu"], p["v0"], p["vup"]]])
        scratch = [pltpu.VMEM((ts + SHIFT_HALO, c), F32), pltpu.VMEM((ts + SHIFT_HALO, LANES), F32)]
    else:
        inputs = [cols, cols] + params
        in_specs = [tile(c), halo(c)] + [const(a) for a in params]
        scratch = [pltpu.VMEM((ts + SHIFT_HALO, c), F32)]
    out = jax.ShapeDtypeStruct((bsz, s, W_GRP), F32)
    return pl.pallas_call(
        functools.partial(_rwkv_prep_body, has_vfirst),
        grid=(bsz, s // ts),
        in_specs=in_specs,
        out_specs=[tile(W_GRP)] * 7, out_shape=[out] * 7,
        scratch_shapes=scratch,
        compiler_params=_cparams("parallel", "parallel"), name="rwkv_prep",
    )(*inputs)


def _block_diag(x, headmask):
    return jnp.concatenate([x] * N_HEADS, axis=0) * headmask


def _wkv_body(ts, r_ref, lw_ref, k_ref, v_ref, a_ref, b_ref, g_ref, rk_ref, gg_ref, gb_ref,
              avg_ref, ones_ref, o_ref, st_ref):
    C = WKV_CHUNK
    n = N_HEADS * C

    @pl.when(pl.program_id(1) == 0)
    def _():
        st_ref[...] = jnp.zeros_like(st_ref)

    ri = lax.broadcasted_iota(jnp.int32, (n, n), 0)
    ci = lax.broadcasted_iota(jnp.int32, (n, n), 1)
    head_f = jnp.where((ri // C) == (ci // HEAD_DIM), 1.0, 0.0)
    head_b = head_f.astype(BF16)
    lag = jnp.where((ri // C) == (ci // C), (ri % C) - (ci % C), -1)
    strict = lag > 0
    incl = lag >= 0
    eye = ri == ci
    eye_f = jnp.where(eye, 1.0, 0.0)
    tr = lax.broadcasted_iota(jnp.int32, (C, C), 0)
    tc = lax.broadcasted_iota(jnp.int32, (C, C), 1)
    tri = jnp.where(tc <= tr, 1.0, 0.0).astype(F32)
    cast = lambda x: x.astype(BF16)
    G = ts // C
    split = lambda ref: ref[0].reshape(G, C, W_GRP)
    tile_heads = lambda x: jnp.concatenate([x] * N_HEADS, axis=1)
    bd16 = lambda x: tile_heads(cast(x)) * head_b[None]
    bmm = lambda x, y: lax.dot_general(x, y, (((2,), (1,)), ((0,), (0,))), preferred_element_type=F32)
    bmm_nt = lambda x, y: lax.dot_general(x, y, (((2,), (2,)), ((0,), (0,))), preferred_element_type=F32)

    lw = split(lw_ref)
    cum = lax.dot_general(jnp.broadcast_to(tri[None], (G, C, C)), lw, (((2,), (1,)), ((0,), (0,))),
                          preferred_element_type=F32, precision=HI)
    cum_c = cum[:, C - 1:C, :]
    e_in = jnp.exp(cum)
    e_neg = jnp.exp(-cum)
    e_tail = jnp.exp(cum_c - cum)
    b_c = split(b_ref)
    k_c = split(k_ref)
    a_t = bd16(split(a_ref) * jnp.exp(cum - lw))
    r_t = bd16(split(r_ref) * e_in)
    b_t = bd16(b_c * e_neg)
    k_t = bd16(k_c * e_neg)
    v_bd = bd16(split(v_ref))
    bh_t = cast(jnp.swapaxes(tile_heads(b_c * e_tail) * head_f[None], 1, 2))
    kh_t = cast(jnp.swapaxes(tile_heads(k_c * e_tail) * head_f[None], 1, 2))
    a_ab = jnp.where(strict[None], bmm_nt(a_t, b_t), 0.0)
    a_ak = cast(jnp.where(strict[None], bmm_nt(a_t, k_t), 0.0))
    a_rb = cast(jnp.where(incl[None], bmm_nt(r_t, b_t), 0.0))
    a_rk = cast(jnp.where(incl[None], bmm_nt(r_t, k_t), 0.0))
    t_inv = eye_f[None] + a_ab
    pw = cast(a_ab)
    for _ in range(int(math.log2(C)) - 1):
        pw = cast(bmm(pw, pw))
        t_inv = t_inv + bmm(cast(t_inv), pw)
    t16 = cast(t_inv)
    ta = cast(bmm(t16, a_t))
    u0 = bmm(cast(bmm(t16, a_ak)), v_bd)
    o0 = bmm(a_rk, v_bd)
    s0 = bmm(kh_t, v_bd)
    o_lhs = jnp.concatenate([r_t, a_rb], axis=2)
    w_col = jnp.sum(jnp.where(eye[None], jnp.exp(cum_c), 0.0), axis=2, keepdims=True)

    st = st_ref[...]
    outs = []
    for g in range(G):
        st16 = cast(st)
        u = cast(_dot(ta[g], st16) + u0[g])
        o_bd = _dot(o_lhs[g], jnp.concatenate([st16, u], axis=0)) + o0[g]
        st = w_col[g] * st + _dot(bh_t[g], u) + s0[g]
        outs.append(o_bd[0:C] + o_bd[C:2 * C] + o_bd[2 * C:3 * C] + o_bd[3 * C:4 * C])
    st_ref[...] = st

    o = jnp.concatenate(outs, axis=0)
    avg = avg_ref[...].astype(BF16)
    mu = _dot_hilo(o, avg)
    xc = o - mu
    var = _dot_hilo(xc * xc, avg)
    on = xc * lax.rsqrt(var + RWKV_GN_EPS) * gg_ref[...] + gb_ref[...]
    r = r_ref[0]
    k = k_ref[0]
    v = v_ref[0]
    bonus = _dot_hilo(r * k * rk_ref[...], ones_ref[...].astype(BF16)) * v
    o_ref[0] = (on + bonus) * g_ref[0]


def wkv_scan(r, lw, k, v, a, b, g, rk, gg, gb, avg, ones):
    bsz, s, _ = r.shape
    ts = WKV_GROUP * WKV_CHUNK
    tile = pl.BlockSpec((1, ts, W_GRP), lambda bi, i: (bi, i, 0))
    const = lambda arr: pl.BlockSpec(arr.shape, lambda bi, i: (0,) * arr.ndim)
    return pl.pallas_call(
        functools.partial(_wkv_body, ts),
        grid=(bsz, s // ts),
        in_specs=[tile] * 7 + [const(x) for x in (rk, gg, gb, avg, ones)],
        out_specs=tile, out_shape=jax.ShapeDtypeStruct((bsz, s, W_GRP), F32),
        scratch_shapes=[pltpu.VMEM((N_HEADS * WKV_CHUNK, W_GRP), F32)],
        compiler_params=_cparams("parallel", "arbitrary"), name="wkv_scan",
    )(r, lw, k, v, a, b, g, rk, gg, gb, avg, ones)


def _gmlp_body(ts, x_ref, g_ref, b_ref, ws_ref, bs_ref, o_ref):
    x = x_ref[...]
    u = jax.nn.gelu(x[:, :W_GRP])
    v = _ln(jax.nn.gelu(x[:, W_GRP:]), g_ref[...], b_ref[...]).astype(BF16)
    tr = lax.broadcasted_iota(jnp.int32, (GMLP_CHUNK, GMLP_CHUNK), 0)
    tc = lax.broadcasted_iota(jnp.int32, (GMLP_CHUNK, GMLP_CHUNK), 1)
    ws = [jnp.where(tc <= tr, ws_ref[h], 0.0).astype(BF16) for h in range(N_HEADS)]
    for c in range(ts // GMLP_CHUNK):
        rows = slice(c * GMLP_CHUNK, (c + 1) * GMLP_CHUNK)
        mixed = [_dot(ws[h], v[rows, h * HEAD_DIM:(h + 1) * HEAD_DIM]) + bs_ref[h]
                 for h in range(N_HEADS)]
        o_ref[rows, :] = u[rows, :] * jnp.concatenate(mixed, axis=-1)


def gmlp_mixer(x2d, g, b, ws, bs):
    n = x2d.shape[0]
    ts = TM
    const = lambda a: pl.BlockSpec(a.shape, lambda i: (0,) * a.ndim)
    return pl.pallas_call(
        functools.partial(_gmlp_body, ts),
        grid=(n // ts,),
        in_specs=[pl.BlockSpec((ts, 2 * W_GRP), lambda i: (i, 0)), const(g), const(b), const(ws), const(bs)],
        out_specs=pl.BlockSpec((ts, W_GRP), lambda i: (i, 0)),
        out_shape=jax.ShapeDtypeStruct((n, W_GRP), F32),
        compiler_params=_cparams("parallel"), name="gmlp_mixer",
    )(x2d, g, b, ws, bs)


def _swap_halves(x, lane):
    w = x.shape[-1]
    half = HEAD_DIM // 2
    fwd = pltpu.roll(x, w - half, 1)
    bwd = pltpu.roll(x, half, 1)
    return jnp.where((lane % HEAD_DIM) < half, fwd, bwd)


def _nsa_prep_body(q_ref, kv_ref, m_ref, cos_ref, sin_ref,
                   q_o, qr_o, kc_o, vc_o, ks_o, vs_o, kw_o, vw_o, g_o):
    scale = HEAD_DIM ** -0.5
    q = q_ref[0]
    kv = kv_ref[0]
    cos = jnp.concatenate([cos_ref[...]] * 3, axis=1)
    sin = jnp.concatenate([sin_ref[...]] * 3, axis=1)
    lane_q = lax.broadcasted_iota(jnp.int32, q.shape, 1)
    q_rot = q * cos[:, :W_GRP] + _swap_halves(q, lane_q) * sin[:, :W_GRP]
    lane_kv = lax.broadcasted_iota(jnp.int32, kv.shape, 1)
    kv_rot = kv * cos + _swap_halves(kv, lane_kv) * sin
    q_o[0] = (q * scale).astype(BF16)
    qr_o[0] = (q_rot * (scale * LOG2E)).astype(BF16)
    kc_o[0] = kv[:, 0:64]
    vc_o[0] = kv[:, 64:128]
    ts = kv.shape[0]
    lane = lax.broadcasted_iota(jnp.int32, (ts, LANES), 1)
    key_blk = (pl.program_id(1) * ts + lax.broadcasted_iota(jnp.int32, (ts, 1), 0)) // SEL_LEN
    ks_o[0] = jnp.concatenate([jnp.where(lane < HEAD_DIM, kv_rot[:, 128:256], 0.0),
                               jnp.where(lane == key_blk, NEG, 0.0)], axis=1).astype(BF16)
    ones_col = jnp.where(lane == HEAD_DIM, 1.0, 0.0)
    vs_first = pltpu.roll(kv[:, 128:256], HEAD_DIM, 1)
    vs_o[0] = jnp.where(lane < HEAD_DIM, vs_first, ones_col).astype(BF16)
    kw_o[0] = kv_rot[:, 256:320].astype(BF16)
    vw_first = pltpu.roll(kv[:, 256:384], HEAD_DIM, 1)
    vw_o[0] = jnp.where(lane < HEAD_DIM, vw_first, ones_col).astype(BF16)
    g_o[0] = jax.nn.sigmoid(m_ref[0])


def nsa_prep(q, kv, misc, cos_t, sin_t):
    bsz, s, _ = q.shape
    ts = TM
    tile = lambda w: pl.BlockSpec((1, ts, w), lambda bi, i: (bi, i, 0))
    tab = pl.BlockSpec((ts, LANES), lambda bi, i: (i, 0))
    sd = lambda w, dt: jax.ShapeDtypeStruct((bsz, s, w), dt)
    return pl.pallas_call(
        _nsa_prep_body,
        grid=(bsz, s // ts),
        in_specs=[tile(W_GRP), tile(6 * HEAD_DIM), tile(LANES), tab, tab],
        out_specs=[tile(W_GRP), tile(W_GRP), tile(HEAD_DIM), tile(HEAD_DIM), tile(2 * LANES),
                   tile(LANES), tile(HEAD_DIM), tile(LANES), tile(LANES)],
        out_shape=[sd(W_GRP, BF16), sd(W_GRP, BF16), sd(HEAD_DIM, F32), sd(HEAD_DIM, F32),
                   sd(2 * LANES, BF16), sd(LANES, BF16), sd(HEAD_DIM, BF16), sd(LANES, BF16),
                   sd(LANES, F32)],
        compiler_params=_cparams("parallel", "parallel"), name="nsa_prep",
    )(q, kv, misc, cos_t, sin_t)


def _compress_body(kc_ref, vc_ref, pe_ref, w1_ref, w2_ref, ko_ref, vo_ref):
    half = CMP_STRIDE * HEAD_DIM
    for j, (c_ref, o_ref) in enumerate(((kc_ref, ko_ref), (vc_ref, vo_ref))):
        c = c_ref[0].astype(BF16)
        w1 = w1_ref[j]
        lo = _dot(c, w1[:half])
        hi = _dot(c, w1[half:])
        nb = hi.shape[0]
        hi_next = pltpu.roll(hi, nb - 1, 0)
        pe = jnp.broadcast_to(pe_ref[j], (8, 2 * half))
        pe_term = _dot(pe, w1.astype(F32), HI)[0:1]
        h = jax.nn.gelu(lo + hi_next + pe_term)
        o_ref[0] = _dot(h.astype(BF16), w2_ref[j]).astype(BF16)


def nsa_compress(kc_r, vc_r, pe, w1, w2):
    bsz, nb, w = kc_r.shape
    blk = pl.BlockSpec((1, nb, w), lambda bi: (bi, 0, 0))
    const = lambda a: pl.BlockSpec(a.shape, lambda bi: (0,) * a.ndim)
    out = jax.ShapeDtypeStruct((bsz, nb, HEAD_DIM), BF16)
    ospec = pl.BlockSpec((1, nb, HEAD_DIM), lambda bi: (bi, 0, 0))
    return pl.pallas_call(
        _compress_body, grid=(bsz,),
        in_specs=[blk, blk, const(pe), const(w1), const(w2)],
        out_specs=[ospec, ospec], out_shape=[out, out],
        compiler_params=_cparams("parallel"), name="nsa_compress",
    )(kc_r, vc_r, pe, w1, w2)


def _stack_heads(x):
    return jnp.concatenate([x[:, h * HEAD_DIM:(h + 1) * HEAD_DIM] for h in range(N_HEADS)], axis=0)


def _nsa_attn_body(n_top, q_ref, qr_ref, kc_ref, vc_ref, ks_ref, vs_ref, kw_ref, vw_ref, g_ref,
                   ov_ref, o_ref):
    T = Q_BLOCK
    i = pl.program_id(1)
    t0 = i * T
    qs = _stack_heads(q_ref[0])
    qrs = _stack_heads(qr_ref[0])
    pos = t0 + lax.broadcasted_iota(jnp.int32, (T, 1), 0)
    tile4 = lambda x: jnp.concatenate([x] * N_HEADS, axis=0)

    kc = kc_ref[0]
    n_cmp = kc.shape[0]
    cmp_end = lax.broadcasted_iota(jnp.int32, (1, n_cmp), 1) * CMP_STRIDE + (CMP_LEN - 1)
    valid_c = tile4(jnp.where(cmp_end <= pos, 1.0, 0.0))
    s_c = jnp.where(valid_c > 0.5, _dot_nt(qs, kc), NEG)
    p_c = jnp.exp(s_c - jnp.max(s_c, axis=-1, keepdims=True)) * valid_c
    p_c = p_c / jnp.maximum(jnp.sum(p_c, axis=-1, keepdims=True), 1e-20)
    o_c = _dot(p_c.astype(BF16), vc_ref[0])
    p_sum = p_c[0:T] + p_c[T:2 * T] + p_c[2 * T:3 * T] + p_c[3 * T:4 * T]
    ps_hi, ps_lo = _split_bf16(p_sum)
    ov = ov_ref[...].astype(BF16)
    imp_t = _dot_nt(ov, ps_hi) + _dot_nt(ov, ps_lo)

    WK = WINDOW + T
    start = pl.multiple_of(jnp.maximum(t0 - WINDOW, 0), T)
    kw = kw_ref[0, pl.ds(start, WK), :]
    vw = vw_ref[0, pl.ds(start, WK), :]
    kpos_w = start + lax.broadcasted_iota(jnp.int32, (1, WK), 1)
    bias_w = tile4(jnp.where(kpos_w <= pos, jnp.where(kpos_w > pos - WINDOW, 0.0, NEG), NEG))
    s_w = _dot_nt(qrs, kw) + bias_w
    p_w = jnp.exp2(s_w - jnp.max(s_w, axis=-1, keepdims=True))
    acc_w = _dot(p_w.astype(BF16), vw)
    o_w = acc_w[:, :HEAD_DIM] / acc_w[:, HEAD_DIM:HEAD_DIM + 1]

    n_sel = imp_t.shape[0]
    jj = lax.broadcasted_iota(jnp.int32, (n_sel, T), 0)
    blk = (t0 + lax.broadcasted_iota(jnp.int32, (1, T), 1)) // SEL_LEN
    val = jnp.where(jj == blk, 3e38, jnp.where(jj == 0, 3e38, jnp.where(jj <= blk, imp_t, -1.0)))
    sel_t = jnp.zeros((n_sel, T), F32)
    for _ in range(n_top):
        mx = jnp.max(val, axis=0, keepdims=True)
        idx = jnp.min(jnp.where(val == mx, jj, n_sel), axis=0, keepdims=True)
        hit = jj == idx
        sel_t = jnp.where(hit, 1.0, sel_t)
        val = jnp.where(hit, -2.0, val)
    sel = sel_t.T

    KT = SEL_KT
    n_tiles = (t0 + T + KT - 1) // KT
    unsel = tile4((1.0 - sel).astype(BF16))
    q_aug = jnp.concatenate([qrs, jnp.zeros((N_HEADS * T, HEAD_DIM), BF16), unsel], axis=1)
    kcol = lax.broadcasted_iota(jnp.int32, (1, KT), 1)

    def sel_tile(jt, m, acc):
        k0 = pl.multiple_of(jt * KT, KT)
        s = _dot_nt(q_aug, ks_ref[0, pl.ds(k0, KT), :]) + tile4(jnp.where(k0 + kcol <= pos, 0.0, NEG))
        m_new = jnp.maximum(m, jnp.max(s, axis=-1, keepdims=True))
        p = jnp.exp2(s - m_new)
        acc = jnp.exp2(m - m_new) * acc + _dot(p.astype(BF16), vs_ref[0, pl.ds(k0, KT), :])
        return m_new, acc

    init = (jnp.full((N_HEADS * T, 1), NEG, F32), jnp.zeros((N_HEADS * T, LANES), F32))
    _, acc_s = lax.fori_loop(0, n_tiles, lambda jt, c: sel_tile(jt, *c), init)
    o_s = acc_s[:, :HEAD_DIM] / acc_s[:, HEAD_DIM:HEAD_DIM + 1]

    g = g_ref[0]
    outs = []
    for h in range(N_HEADS):
        rows = slice(h * T, (h + 1) * T)
        outs.append(g[:, 3 * h:3 * h + 1] * o_c[rows] + g[:, 3 * h + 1:3 * h + 2] * o_s[rows]
                    + g[:, 3 * h + 2:3 * h + 3] * o_w[rows])
    o_ref[0] = jnp.concatenate(outs, axis=-1)


def nsa_attention(q, qr, k_cmp, v_cmp, ks, vs, kw, vw, gates, overlap):
    bsz, s, _ = q.shape
    n_top = min(SEL_TOPN, s // SEL_LEN)
    n_cmp = k_cmp.shape[1]
    qtile = lambda w: pl.BlockSpec((1, Q_BLOCK, w), lambda bi, i: (bi, i, 0))
    full = lambda rows, w: pl.BlockSpec((1, rows, w), lambda bi, i: (bi, 0, 0))
    return pl.pallas_call(
        functools.partial(_nsa_attn_body, n_top),
        grid=(bsz, s // Q_BLOCK),
        in_specs=[qtile(W_GRP), qtile(W_GRP), full(n_cmp, HEAD_DIM), full(n_cmp, HEAD_DIM),
                  full(s, 2 * LANES), full(s, LANES), full(s, HEAD_DIM), full(s, LANES),
                  qtile(LANES), pl.BlockSpec(overlap.shape, lambda bi, i: (0, 0))],
        out_specs=qtile(W_GRP), out_shape=jax.ShapeDtypeStruct((bsz, s, W_GRP), F32),
        compiler_params=_cparams("parallel", "parallel"), name="nsa_attention",
    )(q, qr, k_cmp, v_cmp, ks, vs, kw, vw, gates, overlap)


def _outproj_body(alpha, a_ref, b_ref, c_ref, d_ref, x_ref, w_ref, g_ref, bb_ref, o_ref):
    mix = jnp.zeros(x_ref.shape, F32)
    for j, part in enumerate((a_ref, b_ref, c_ref, d_ref)):
        mix = mix + _dot(part[...].astype(BF16), w_ref[j * W_GRP:(j + 1) * W_GRP, :])
    o_ref[...] = _ln(alpha * x_ref[...] + mix, g_ref[...], bb_ref[...])


def out_proj(parts, x2d, w, g, b, alpha):
    n, d = x2d.shape
    row = lambda w_: pl.BlockSpec((TM, w_), lambda i: (i, 0))
    const = lambda a: pl.BlockSpec(a.shape, lambda i: (0,) * a.ndim)
    return pl.pallas_call(
        functools.partial(_outproj_body, alpha),
        grid=(n // TM,),
        in_specs=[row(W_GRP)] * 4 + [row(d), const(w), const(g), const(b)],
        out_specs=row(d), out_shape=jax.ShapeDtypeStruct((n, d), F32),
        compiler_params=_cparams("parallel"), name="out_proj",
    )(*parts, x2d, w, g, b)


def _memkv_body(m_ref, wk_ref, wv_ref, k_o, v_o):
    mb = m_ref[...].astype(BF16)
    k_o[...] = _dot(mb, wk_ref[...]).astype(BF16)
    v_o[...] = _dot(mb, wv_ref[...]).astype(BF16)


def mem_kv(mem2d, wk, wv):
    n, d = mem2d.shape
    full = lambda a: pl.BlockSpec(a.shape, lambda i: (0, 0))
    out = jax.ShapeDtypeStruct((n, d), BF16)
    return pl.pallas_call(
        _memkv_body, grid=(1,),
        in_specs=[full(mem2d), full(wk), full(wv)],
        out_specs=[pl.BlockSpec((n, d), lambda i: (0, 0))] * 2, out_shape=[out, out],
        compiler_params=_cparams("arbitrary"), name="mem_kv",
    )(mem2d, wk, wv)


def _xattn_body(alpha, x_ref, k_ref, v_ref, wq_ref, wo_ref, g_ref, b_ref, o_ref):
    x = x_ref[0]
    d = x.shape[-1]
    hd = d // N_MEM_HEADS
    q = (_dot(x.astype(BF16), wq_ref[...]) * (hd ** -0.5)).astype(BF16)
    k = k_ref[0]
    v = v_ref[0]
    heads = []
    for h in range(N_MEM_HEADS):
        cs = slice(h * hd, (h + 1) * hd)
        s = _dot_nt(q[:, cs], k[:, cs])
        p = jnp.exp(s - jnp.max(s, axis=-1, keepdims=True))
        p = p / jnp.sum(p, axis=-1, keepdims=True)
        heads.append(_dot(p.astype(BF16), v[:, cs]))
    o = jnp.concatenate(heads, axis=-1).astype(BF16)
    o_ref[0] = _ln(alpha * x + _dot(o, wo_ref[...]), g_ref[...], b_ref[...])


def cross_attention(x, k, v, wq, wo, g, b, alpha):
    bsz, s, d = x.shape
    m = k.shape[1]
    const = lambda a: pl.BlockSpec(a.shape, lambda bi, i: (0,) * a.ndim)
    return pl.pallas_call(
        functools.partial(_xattn_body, alpha),
        grid=(bsz, s // TM),
        in_specs=[pl.BlockSpec((1, TM, d), lambda bi, i: (bi, i, 0)),
                  pl.BlockSpec((1, m, d), lambda bi, i: (bi, 0, 0)),
                  pl.BlockSpec((1, m, d), lambda bi, i: (bi, 0, 0)),
                  const(wq), const(wo), const(g), const(b)],
        out_specs=pl.BlockSpec((1, TM, d), lambda bi, i: (bi, i, 0)),
        out_shape=jax.ShapeDtypeStruct((bsz, s, d), F32),
        compiler_params=_cparams("parallel", "parallel"), name="cross_attention",
    )(x, k, v, wq, wo, g, b)


PER_GRP = N_EXPERTS // N_EXPERT_GROUPS
LPOS_LANE = PER_GRP
SUB = 8


def _router_body(x_ref, rwt_ref, rb_ref, upper_ref, tok_o, lrow_o, cnt_o):
    tm = x_ref.shape[0]
    logits = _dot_nt(rwt_ref[...], x_ref[...], HI)
    ex = jnp.exp(logits - jnp.max(logits, axis=0, keepdims=True))
    probs = ex / jnp.sum(ex, axis=0, keepdims=True)
    sel = probs + rb_ref[...]
    srow = [sel[e:e + 1] for e in range(N_EXPERTS)]
    prow = [probs[e:e + 1] for e in range(N_EXPERTS)]
    gscore = []
    for g in range(N_EXPERT_GROUPS):
        r = srow[g * PER_GRP:(g + 1) * PER_GRP]
        best = None
        for a in range(PER_GRP):
            for b in range(a + 1, PER_GRP):
                best = r[a] + r[b] if best is None else jnp.maximum(best, r[a] + r[b])
        gscore.append(best)
    g_idx = jnp.zeros((1, tm), jnp.int32)
    top = gscore[0]
    for g in range(1, N_EXPERT_GROUPS):
        better = gscore[g] > top
        g_idx = jnp.where(better, g, g_idx)
        top = jnp.where(better, gscore[g], top)

    def of_group(rows, e):
        out = rows[e]
        for g in range(1, N_EXPERT_GROUPS):
            out = jnp.where(g_idx == g, rows[g * PER_GRP + e], out)
        return out

    sg = [of_group(srow, e) for e in range(PER_GRP)]
    pg = [of_group(prow, e) for e in range(PER_GRP)]
    w = []
    for e in range(PER_GRP):
        rank = jnp.zeros((1, tm), F32)
        for o in range(PER_GRP):
            if o != e:
                ahead = (sg[o] >= sg[e]) if o < e else (sg[o] > sg[e])
                rank = rank + jnp.where(ahead, 1.0, 0.0)
        w.append(jnp.where(rank < 1.5, pg[e], 0.0))
    w_sum = w[0] + w[1] + w[2] + w[3]
    row8 = lax.broadcasted_iota(jnp.int32, (SUB, tm), 0)
    onehot = jnp.where(row8 == g_idx, 1.0, 0.0)
    before = _dot(onehot.astype(BF16), upper_ref[...])
    cnt = jnp.sum(onehot, axis=1, keepdims=True)
    offs = [jnp.zeros((1, 1), F32)]
    for g in range(1, N_EXPERT_GROUPS):
        offs.append(offs[-1] + cnt[g - 1:g])
    lpos = jnp.zeros((1, tm), F32)
    for g in range(N_EXPERT_GROUPS):
        lpos = lpos + onehot[g:g + 1] * (offs[g] + before[g:g + 1])
    tok = jnp.zeros((SUB, tm), F32)
    for e in range(PER_GRP):
        tok = jnp.where(row8 == e, w[e] / w_sum, tok)
    tok = jnp.where(row8 == LPOS_LANE, lpos, tok)
    tok_o[...] = jnp.concatenate([tok, jnp.zeros((LANES - SUB, tm), F32)], axis=0).T
    lrow_o[0] = lpos.astype(jnp.int32)
    rowc = lax.broadcasted_iota(jnp.int32, (SUB, LANES), 0)
    stats = jnp.zeros((SUB, LANES), F32)
    for g in range(N_EXPERT_GROUPS):
        stats = jnp.where(rowc == g, cnt[g:g + 1], stats)
        stats = jnp.where(rowc == N_EXPERT_GROUPS + g, offs[g], stats)
    cnt_o[0] = stats.astype(jnp.int32)


def moe_router(x2d, rwt, rb_col, upper):
    n, d = x2d.shape
    tm = upper.shape[0]
    nt = n // tm
    const = lambda a: pl.BlockSpec(a.shape, lambda i: (0, 0))
    return pl.pallas_call(
        _router_body, grid=(nt,),
        in_specs=[pl.BlockSpec((tm, d), lambda i: (i, 0)), const(rwt), const(rb_col), const(upper)],
        out_specs=[pl.BlockSpec((tm, LANES), lambda i: (i, 0)),
                   pl.BlockSpec((1, 1, tm), lambda i: (i, 0, 0)),
                   pl.BlockSpec((1, SUB, LANES), lambda i: (i, 0, 0))],
        out_shape=[jax.ShapeDtypeStruct((n, LANES), F32), jax.ShapeDtypeStruct((nt, 1, tm), jnp.int32),
                   jax.ShapeDtypeStruct((nt, SUB, LANES), jnp.int32)],
        compiler_params=_cparams("parallel"), name="moe_router",
    )(x2d, rwt, rb_col, upper)


def _split_bf16(x):
    hi = x.astype(BF16)
    return hi, (x - hi.astype(F32)).astype(BF16)


def _moe_body(alpha, offs_ref, cnts_ref, x_ref, tok_ref, lrow_ref, wg_ref, wu_ref, wd_ref, g_ref, b_ref, o_ref,
              xs_ref, gs_ref, acc_ref):
    i = pl.program_id(0)
    step = pl.program_id(1)
    tm = x_ref.shape[0]
    eps = wd_ref.shape[1]

    @pl.when(step == 0)
    def _():
        slot = lax.broadcasted_iota(jnp.int32, (tm, tm), 0)
        perm = jnp.where(slot == lrow_ref[0], 1.0, 0.0).astype(BF16)
        xs_ref[...] = _dot(perm, x_ref[...].astype(BF16)).astype(BF16)
        t_hi, t_lo = _split_bf16(tok_ref[...])
        gs_ref[...] = _dot(perm, t_hi) + _dot(perm, t_lo)
        acc_ref[...] = jnp.zeros_like(acc_ref)

    grp = (step * eps) // PER_GRP
    off = offs_ref[i * N_EXPERT_GROUPS + grp]
    end = off + cnts_ref[i * N_EXPERT_GROUPS + grp]
    lane = lax.broadcasted_iota(jnp.int32, (MOE_BLK, LANES), 1)
    for s in range(tm // MOE_BLK):
        lo = s * MOE_BLK

        @pl.when(jnp.logical_and(off < lo + MOE_BLK, end > lo))
        def _(lo=lo):
            rows = slice(lo, lo + MOE_BLK)
            xs = xs_ref[rows, :]
            ridx = lo + lax.broadcasted_iota(jnp.int32, (MOE_BLK, 1), 0)
            in_run = jnp.where(ridx >= off, jnp.where(ridx < end, 1.0, 0.0), 0.0)
            gs = gs_ref[rows, :]
            y = jnp.zeros((MOE_BLK, x_ref.shape[1]), F32)
            for k in range(eps):
                gt = _dot(xs, wg_ref[0, k])
                h = (gt * jax.nn.sigmoid(gt) * _dot(xs, wu_ref[0, k])).astype(BF16)
                in_grp = (step * eps + k) % PER_GRP
                gcol = jnp.sum(jnp.where(lane == in_grp, gs, 0.0), axis=-1, keepdims=True)
                y = y + (gcol * in_run) * _dot(h, wd_ref[0, k])
            acc_ref[rows, :] += y

    @pl.when(step == pl.num_programs(1) - 1)
    def _():
        lcol = tok_ref[:, LPOS_LANE:LPOS_LANE + 1].astype(jnp.int32)
        slot = lax.broadcasted_iota(jnp.int32, (tm, tm), 1)
        unperm = jnp.where(slot == lcol, 1.0, 0.0).astype(BF16)
        a_hi, a_lo = _split_bf16(acc_ref[...])
        y = _dot(unperm, a_hi) + _dot(unperm, a_lo)
        o_ref[...] = _ln(alpha * x_ref[...] + y, g_ref[...], b_ref[...])


def moe_ffn(x2d, tok, lrow, offs, cnts, layer, wg, wu, wd, g, b, alpha):
    n, d = x2d.shape
    _, ne, de, _ = wd.shape
    tm = lrow.shape[2]
    const = lambda a: pl.BlockSpec(a.shape, lambda i, e, o, c: (0,) * a.ndim)
    grid_spec = pltpu.PrefetchScalarGridSpec(
        num_scalar_prefetch=2, grid=(n // tm, ne // MOE_EPS),
        in_specs=[pl.BlockSpec((tm, d), lambda i, e, o, c: (i, 0), pipeline_mode=pl.Buffered(1)),
                  pl.BlockSpec((tm, LANES), lambda i, e, o, c: (i, 0)),
                  pl.BlockSpec((1, 1, tm), lambda i, e, o, c: (i, 0, 0)),
                  pl.BlockSpec((1, MOE_EPS, d, de), lambda i, e, o, c: (layer, e, 0, 0)),
                  pl.BlockSpec((1, MOE_EPS, d, de), lambda i, e, o, c: (layer, e, 0, 0)),
                  pl.BlockSpec((1, MOE_EPS, de, d), lambda i, e, o, c: (layer, e, 0, 0)),
                  const(g), const(b)],
        out_specs=pl.BlockSpec((tm, d), lambda i, e, o, c: (i, 0)),
        scratch_shapes=[pltpu.VMEM((tm, d), BF16), pltpu.VMEM((tm, LANES), F32), pltpu.VMEM((tm, d), F32)])
    return pl.pallas_call(
        functools.partial(_moe_body, alpha), grid_spec=grid_spec,
        out_shape=jax.ShapeDtypeStruct((n, d), F32),
        compiler_params=_cparams("parallel", "arbitrary"), name="moe_ffn",
    )(offs, cnts, x2d, tok, lrow, wg, wu, wd, g, b)


def _rope_tables(s):
    inv = ROPE_THETA ** (-jnp.arange(0, HEAD_DIM, 2, dtype=F32) / HEAD_DIM)
    ang = jnp.arange(s, dtype=F32)[:, None] * inv[None, :]
    cos, sin = jnp.cos(ang), jnp.sin(ang)
    cos_h = jnp.concatenate([cos, cos], axis=-1)
    sin_h = jnp.concatenate([-sin, sin], axis=-1)
    return jnp.tile(cos_h, (1, 2)), jnp.tile(sin_h, (1, 2))


def _overlap_matrix(s, n_cmp_pad):
    n_sel = s // SEL_LEN
    assert n_sel <= LANES
    cmp_start = jnp.arange(n_cmp_pad) * CMP_STRIDE
    sel_start = jnp.arange(LANES) * SEL_LEN
    ov = jnp.clip(jnp.minimum(cmp_start[None, :] + CMP_LEN, sel_start[:, None] + SEL_LEN)
                  - jnp.maximum(cmp_start[None, :], sel_start[:, None]), 0, None).astype(F32) / CMP_LEN
    n_cmp = s // CMP_STRIDE - (CMP_LEN // CMP_STRIDE - 1)
    real = (jnp.arange(n_cmp_pad)[None, :] < n_cmp) & (jnp.arange(LANES)[:, None] < n_sel)
    return jnp.where(real, ov, 0.0)


def _pad_rows(w, lo, total):
    out = jnp.zeros((total, w.shape[1]), w.dtype)
    return out.at[lo:lo + w.shape[0]].set(w)


def kernel(x, mem, ln_in_g, ln_in_b, w_in, w_out, conv_w, conv_b, conv_gn_g, conv_gn_b, rwkv_mu, rwkv_w0, rwkv_w_up, rwkv_a0, rwkv_a_up, rwkv_g_up, rwkv_k_k, rwkv_k_a, rwkv_r_k, rwkv_gn_g, rwkv_gn_b, rwkv_v_down, rwkv_v_mu, rwkv_v0, rwkv_v_up, gmlp_ln_g, gmlp_ln_b, gmlp_w_s, gmlp_b_s, nsa_pe_k, nsa_w1_k, nsa_w2_k, nsa_pe_v, nsa_w1_v, nsa_w2_v, ln1_g, ln1_b, xa_wq, xa_wk, xa_wv, xa_wo, ln2_g, ln2_b, router_w, router_bias, moe_w_gate, moe_w_up, moe_w_down, ln3_g, ln3_b):
    bsz, s, d = x.shape
    depth = w_in.shape[0]
    n = bsz * s
    alpha = (2 * depth) ** 0.25
    row = lambda a: a.reshape(1, -1)

    cos_t, sin_t = _rope_tables(s)
    n_blk = s // CMP_STRIDE
    overlap = _overlap_matrix(s, n_blk)
    avg64 = _group_avg_matrix(W_GRP, HEAD_DIM)
    ones64 = avg64 * HEAD_DIM
    moe_tm = min(MOE_TM, n)
    t_idx = jnp.arange(moe_tm)
    upper = (t_idx[:, None] < t_idx[None, :]).astype(BF16)
    mem2d = mem.reshape(bsz * mem.shape[1], d)

    wg_b, wu_b, wd_b = moe_w_gate.astype(BF16), moe_w_up.astype(BF16), moe_w_down.astype(BF16)

    xs = x.reshape(n, d)
    v_first = None
    for l in range(depth):
        main = IN_SPLITS[-1][1]
        w_misc = jnp.zeros((d, LANES), F32).at[:, :w_in.shape[2] - main].set(w_in[l, :, main:])
        if l > 0:
            w_misc = w_misc.at[:, MISC_VD_OFF:MISC_VD_OFF + rwkv_v_down.shape[2]].set(rwkv_v_down[l - 1])
        outs = in_proj(xs, row(ln_in_g), row(ln_in_b), w_in, l, w_misc.astype(BF16), apply_ln=(l == 0))
        if l == 0:
            xs, outs = outs[0], outs[1:]
        conv_in, rwkv_in, gmlp_in, q_in, kv_in, misc = outs
        b3 = lambda a: a.reshape(bsz, s, a.shape[-1])

        out_a = conv_mixer(b3(conv_in), conv_w[l], row(conv_b[l]), row(conv_gn_g[l]), row(conv_gn_b[l]), avg64)

        rp = {"mu": row(rwkv_mu[l]), "w0": row(rwkv_w0[l]), "a0": row(rwkv_a0[l]),
              "wup": _pad_rows(rwkv_w_up[l], 0, W_GRP), "aup": _pad_rows(rwkv_a_up[l], 64, W_GRP),
              "gup": _pad_rows(rwkv_g_up[l], 128, W_GRP),
              "kk": row(rwkv_k_k[l]), "ka": row(rwkv_k_a[l]), "ones": ones64}
        if l > 0:
            rp["vmu"] = jnp.zeros((1, LANES), F32).at[0, MISC_VD_OFF:MISC_VD_OFF + rwkv_v_mu.shape[1]].set(rwkv_v_mu[l - 1])
            rp["v0"] = row(rwkv_v0[l - 1])
            rp["vup"] = _pad_rows(rwkv_v_up[l - 1], MISC_VD_OFF, LANES)
        r_, lw_, k_, v_, a_, b_, g_ = rwkv_prep(b3(rwkv_in), b3(misc), v_first, rp)
        if l == 0:
            v_first = v_
        out_b = wkv_scan(r_, lw_, k_, v_, a_, b_, g_, row(rwkv_r_k[l]), row(rwkv_gn_g[l]),
                         row(rwkv_gn_b[l]), avg64, ones64)

        out_c = gmlp_mixer(gmlp_in, row(gmlp_ln_g[l]), row(gmlp_ln_b[l]), gmlp_w_s[l],
                           gmlp_b_s[l].reshape(N_HEADS, GMLP_CHUNK, 1))

        q_b, qr_b, kc, vc, ks, vs, kw, vw, gates = nsa_prep(b3(q_in), b3(kv_in), b3(misc), cos_t, sin_t)
        pe = jnp.stack([nsa_pe_k[l].reshape(1, -1), nsa_pe_v[l].reshape(1, -1)])
        w1 = jnp.stack([nsa_w1_k[l], nsa_w1_v[l]]).astype(BF16)
        w2 = jnp.stack([nsa_w2_k[l], nsa_w2_v[l]]).astype(BF16)
        k_cmp, v_cmp = nsa_compress(kc.reshape(bsz, n_blk, CMP_STRIDE * HEAD_DIM),
                                    vc.reshape(bsz, n_blk, CMP_STRIDE * HEAD_DIM), pe, w1, w2)
        out_d = nsa_attention(q_b, qr_b, k_cmp, v_cmp, ks, vs, kw, vw, gates, overlap)

        flat = lambda a: a.reshape(n, W_GRP)
        xs = out_proj((flat(out_a), flat(out_b), out_c, flat(out_d)), xs, w_out[l].astype(BF16),
                      row(ln1_g[l]), row(ln1_b[l]), alpha)

        mk, mv = mem_kv(mem2d, xa_wk[l].astype(BF16), xa_wv[l].astype(BF16))
        m_len = mem.shape[1]
        xs = cross_attention(xs.reshape(bsz, s, d), mk.reshape(bsz, m_len, d), mv.reshape(bsz, m_len, d),
                             xa_wq[l].astype(BF16), xa_wo[l].astype(BF16),
                             row(ln2_g[l]), row(ln2_b[l]), alpha).reshape(n, d)

        tok, lrow, stats = moe_router(xs, router_w.T, router_bias.reshape(-1, 1), upper)
        cnts = stats[:, :N_EXPERT_GROUPS, 0].reshape(-1)
        offs = stats[:, N_EXPERT_GROUPS:2 * N_EXPERT_GROUPS, 0].reshape(-1)
        xs = moe_ffn(xs, tok, lrow, offs, cnts, l, wg_b, wu_b, wd_b,
                     row(ln3_g[l]), row(ln3_b[l]), alpha)
    return xs.reshape(bsz, s, d)
```

```python
import functools
import math

import jax
import jax.numpy as jnp
from jax import lax
from jax.experimental import pallas as pl
from jax.experimental.pallas import tpu as pltpu

F32 = jnp.float32
BF16 = jnp.bfloat16
HI = lax.Precision.HIGHEST

HEAD_DIM = 64
N_HEADS = 4
W_GRP = 256
CONV_WIDTH = 31
GMLP_CHUNK = 128
CMP_LEN = 32
CMP_STRIDE = 16
SEL_LEN = 64
SEL_TOPN = 16
WINDOW = 512
Q_BLOCK = 256
ROPE_THETA = 10000.0
N_MEM_HEADS = 4
N_EXPERTS = 16
N_EXPERT_GROUPS = 4
LN_EPS = 1e-5
RWKV_GN_EPS = 64e-5
NEG = -1e30
LOG2E = math.log2(math.e)
LANES = 128
WKV_CHUNK = 64
WKV_GROUP = 8

TM = 512
MOE_TM = 1024
MOE_BLK = 256
MOE_EPS = 4
SEL_KT = 1024
VMEM_LIMIT = 56 * 1024 * 1024


def _cparams(*sem):
    return pltpu.CompilerParams(dimension_semantics=sem, vmem_limit_bytes=VMEM_LIMIT)


def _ln(x, g, b, eps=LN_EPS):
    mu = jnp.mean(x, axis=-1, keepdims=True)
    xc = x - mu
    var = jnp.mean(xc * xc, axis=-1, keepdims=True)
    return xc * lax.rsqrt(var + eps) * g + b


def _dot(a, b, precision=None):
    return jnp.dot(a, b, preferred_element_type=F32, precision=precision)


def _dot_nt(a, b, precision=None):
    return lax.dot_general(a, b, (((1,), (1,)), ((), ())),
                           preferred_element_type=F32, precision=precision)


def _dot_hilo(x, w_bf16):
    hi = x.astype(BF16)
    lo = (x - hi.astype(F32)).astype(BF16)
    return _dot(hi, w_bf16) + _dot(lo, w_bf16)


def _group_avg_matrix(width, group):
    r = jnp.arange(width)[:, None] // group
    c = jnp.arange(width)[None, :] // group
    return jnp.where(r == c, 1.0 / group, 0.0).astype(F32)


IN_SPLITS = (("conv", 0, 512), ("rwkv", 512, 1536), ("gmlp", 1536, 2048),
             ("q", 2048, 2304), ("kv", 2304, 2688), ("misc", 2688, 2816))
MISC_VD_OFF = 32


def _inproj_body(apply_ln, tiles_per_seq, x_ref, g_ref, b_ref, w_ref, wm_ref,
                 gg_ref, gb_ref, gws_ref, gbs_ref, cos_ref, sin_ref, *rest):
    outs, wb_ref = rest[:-1], rest[-1]
    main = IN_SPLITS[-1][1]

    @pl.when(pl.program_id(0) == 0)
    def _():
        wb_ref[...] = w_ref[0, :, :main].astype(BF16)

    x = x_ref[...]
    if apply_ln:
        x = _ln(x, g_ref[...], b_ref[...])
        outs[0][...] = x
        outs = outs[1:]
    conv_o, rwkv_o, misc_o, gmlp_o = outs[:4]
    xb = x.astype(BF16)
    cols = {name: (lo, hi) for name, lo, hi in IN_SPLITS}
    proj = lambda name: _dot(xb, wb_ref[:, cols[name][0]:cols[name][1]])
    conv_o[...] = proj("conv")
    rwkv_o[...] = proj("rwkv")
    misc = _dot(xb, wm_ref[...])
    misc_o[...] = misc
    _gmlp_math(proj("gmlp"), gg_ref, gb_ref, gws_ref, gbs_ref, gmlp_o)
    pos0 = (pl.program_id(0) % tiles_per_seq) * x.shape[0]
    _nsa_prep_math(proj("q"), proj("kv"), misc, pos0, cos_ref, sin_ref, *outs[4:])


def in_proj(x2d, seq_len, g, b, w_in, layer, w_misc, gmlp_p, cos_t, sin_t, apply_ln):
    n, d = x2d.shape
    bsz = n // seq_len
    tps = seq_len // TM
    row = lambda w: pl.BlockSpec((TM, w), lambda i: (i, 0))
    const = lambda a: pl.BlockSpec(a.shape, lambda i: (0,) * a.ndim)
    seq = lambda w: pl.BlockSpec((1, TM, w), lambda i: (i // tps, i % tps, 0))
    sd = lambda w, dt: jax.ShapeDtypeStruct((bsz, seq_len, w), dt)
    flat = lambda w: jax.ShapeDtypeStruct((n, w), F32)
    out_shapes = [flat(2 * W_GRP), flat(4 * W_GRP), flat(LANES), flat(W_GRP),
                  sd(W_GRP, BF16), sd(W_GRP, BF16), sd(HEAD_DIM, F32), sd(HEAD_DIM, F32),
                  sd(2 * LANES, BF16), jax.ShapeDtypeStruct((bsz, LANES, seq_len), BF16),
                  sd(HEAD_DIM, BF16), sd(LANES, BF16), sd(LANES, F32)]
    out_specs = [row(2 * W_GRP), row(4 * W_GRP), row(LANES), row(W_GRP),
                 seq(W_GRP), seq(W_GRP), seq(HEAD_DIM), seq(HEAD_DIM), seq(2 * LANES),
                 pl.BlockSpec((1, LANES, TM), lambda i: (i // tps, 0, i % tps)),
                 seq(HEAD_DIM), seq(LANES), seq(LANES)]
    if apply_ln:
        out_shapes = [flat(d)] + out_shapes
        out_specs = [row(d)] + out_specs
    tab = pl.BlockSpec((TM, LANES), lambda i: (i % tps, 0))
    return pl.pallas_call(
        functools.partial(_inproj_body, apply_ln, tps),
        grid=(n // TM,),
        in_specs=[row(d), const(g), const(b),
                  pl.BlockSpec((1,) + w_in.shape[1:], lambda i: (layer, 0, 0), pipeline_mode=pl.Buffered(1)),
                  const(w_misc)] + [const(a) for a in gmlp_p] + [tab, tab],
        out_specs=out_specs, out_shape=out_shapes,
        scratch_shapes=[pltpu.VMEM((d, IN_SPLITS[-1][1]), BF16)],
        compiler_params=_cparams("arbitrary"), name="in_proj",
    )(x2d, g, b, w_in, w_misc, *gmlp_p, cos_t, sin_t)


CONV_HALO = 32


def _conv_body(ts, cur_ref, halo_ref, w_ref, b_ref, gg_ref, gb_ref, avg_ref, o_ref, hbuf, shifted):
    i = pl.program_id(1)
    cur = cur_ref[0]
    hal = halo_ref[0]
    h = cur[:, :W_GRP] * jax.nn.sigmoid(cur[:, W_GRP:])
    hh = hal[:, :W_GRP] * jax.nn.sigmoid(hal[:, W_GRP:])
    hbuf[0:CONV_HALO, :] = jnp.where(i > 0, hh, 0.0)
    hbuf[CONV_HALO:, :] = h
    acc = jnp.zeros((ts, W_GRP), F32)
    base = CONV_HALO - (CONV_WIDTH - 1)
    for phase in range(SUB):
        taps = [j for j in range(CONV_WIDTH) if (base + j) % SUB == phase]
        span = (base + taps[-1]) // SUB * SUB + ts
        shifted[0:span, :] = hbuf[phase:phase + span, :]
        for j in taps:
            lo = (base + j) // SUB * SUB
            acc = acc + w_ref[j:j + 1, :] * shifted[lo:lo + ts, :]
    acc = acc + b_ref[...]
    avg = avg_ref[...].astype(BF16)
    mu = _dot_hilo(acc, avg)
    xc = acc - mu
    var = _dot_hilo(xc * xc, avg)
    y = xc * lax.rsqrt(var + LN_EPS) * gg_ref[...] + gb_ref[...]
    o_ref[0] = y * jax.nn.sigmoid(y)


def conv_mixer(conv_in, w, b, gg, gb, avg):
    bsz, s, _ = conv_in.shape
    ts = TM
    r = ts // CONV_HALO
    const = lambda shp: pl.BlockSpec(shp, lambda bi, i: (0, 0))
    return pl.pallas_call(
        functools.partial(_conv_body, ts),
        grid=(bsz, s // ts),
        in_specs=[pl.BlockSpec((1, ts, 2 * W_GRP), lambda bi, i: (bi, i, 0)),
                  pl.BlockSpec((1, CONV_HALO, 2 * W_GRP),
                               lambda bi, i: (bi, jnp.maximum(i * r - 1, 0), 0)),
                  const((CONV_WIDTH, W_GRP)), const((1, W_GRP)), const((1, W_GRP)),
                  const((1, W_GRP)), const((W_GRP, W_GRP))],
        out_specs=pl.BlockSpec((1, ts, W_GRP), lambda bi, i: (bi, i, 0)),
        out_shape=jax.ShapeDtypeStruct((bsz, s, W_GRP), F32),
        scratch_shapes=[pltpu.VMEM((ts + CONV_HALO, W_GRP), F32), pltpu.VMEM((ts + CONV_HALO, W_GRP), F32)],
        compiler_params=_cparams("parallel", "parallel"), name="conv_mixer",
    )(conv_in, conv_in, w, b, gg, gb, avg)


SHIFT_HALO = 8


def _shift_prev(buf, cur, halo, first):
    ts = cur.shape[0]
    buf[0:SHIFT_HALO, :] = jnp.where(first, 0.0, halo)
    buf[SHIFT_HALO:, :] = cur
    return buf[SHIFT_HALO - 1:SHIFT_HALO - 1 + ts, :]


def _rwkv_prep_body(has_vfirst, *refs):
    if has_vfirst:
        (c_ref, ch_ref, m_ref, mh_ref, vf_ref, mu_ref, w0_ref, a0_ref, wup_ref, aup_ref, gup_ref,
         kk_ref, ka_ref, ones_ref, vmu_ref, v0_ref, vup_ref,
         r_o, lw_o, k_o, v_o, a_o, b_o, g_o, buf, mbuf) = refs
    else:
        (c_ref, ch_ref, mu_ref, w0_ref, a0_ref, wup_ref, aup_ref, gup_ref,
         kk_ref, ka_ref, ones_ref,
         r_o, lw_o, k_o, v_o, a_o, b_o, g_o, buf) = refs
    first = pl.program_id(1) == 0
    cur = c_ref[0]
    prev = _shift_prev(buf, cur, ch_ref[0], first)
    y = cur + mu_ref[...] * (prev - cur)
    r = y[:, 0:256]
    k = y[:, 256:512]
    v = y[:, 512:768]
    lora = y[:, 768:1024]
    w = w0_ref[...] + _dot(jnp.tanh(lora), wup_ref[...], HI)
    a = jax.nn.sigmoid(a0_ref[...] + _dot(lora, aup_ref[...], HI))
    g = _dot(jax.nn.sigmoid(lora), gup_ref[...], HI)
    z = -w
    sp = jnp.maximum(z, 0.0) + jnp.log(1.0 + jnp.exp(-jnp.abs(z)))
    lw = -jnp.exp(-sp - 0.5)
    if has_vfirst:
        mc = m_ref[0]
        mprev = _shift_prev(mbuf, mc, mh_ref[0], first)
        vd = mc + vmu_ref[...] * (mprev - mc)
        v_mix = jax.nn.sigmoid(v0_ref[...] + _dot(vd, vup_ref[...], HI))
        v = v + (vf_ref[0] - v) * v_mix
    kk = k * kk_ref[...]
    n2 = _dot_hilo(kk * kk, ones_ref[...].astype(BF16))
    kk = kk / jnp.maximum(jnp.sqrt(n2), 1e-12)
    k2 = k * (1.0 + (a - 1.0) * ka_ref[...])
    r_o[0] = r
    lw_o[0] = lw
    k_o[0] = k2
    v_o[0] = v
    a_o[0] = -kk
    b_o[0] = kk * a
    g_o[0] = g


def rwkv_prep(cols, misc, v_first, p):
    bsz, s, c = cols.shape
    ts = TM
    has_vfirst = v_first is not None
    r8 = ts // SHIFT_HALO
    tile = lambda w: pl.BlockSpec((1, ts, w), lambda bi, i: (bi, i, 0))
    halo = lambda w: pl.BlockSpec((1, SHIFT_HALO, w), lambda bi, i: (bi, jnp.maximum(i * r8 - 1, 0), 0))
    const = lambda a: pl.BlockSpec(a.shape, lambda bi, i: (0,) * a.ndim)
    params = [p["mu"], p["w0"], p["a0"], p["wup"], p["aup"], p["gup"], p["kk"], p["ka"], p["ones"]]
    if has_vfirst:
        inputs = [cols, cols, misc, misc, v_first] + params + [p["vmu"], p["v0"], p["vup"]]
        in_specs = ([tile(c), halo(c), tile(LANES), halo(LANES), tile(W_GRP)]
                    + [const(a) for a in params + [p["vmu"], p["v0"], p["vup"]]])
        scratch = [pltpu.VMEM((ts + SHIFT_HALO, c), F32), pltpu.VMEM((ts + SHIFT_HALO, LANES), F32)]
    else:
        inputs = [cols, cols] + params
        in_specs = [tile(c), halo(c)] + [const(a) for a in params]
        scratch = [pltpu.VMEM((ts + SHIFT_HALO, c), F32)]
    out = jax.ShapeDtypeStruct((bsz, s, W_GRP), F32)
    return pl.pallas_call(
        functools.partial(_rwkv_prep_body, has_vfirst),
        grid=(bsz, s // ts),
        in_specs=in_specs,
        out_specs=[tile(W_GRP)] * 7, out_shape=[out] * 7,
        scratch_shapes=scratch,
        compiler_params=_cparams("parallel", "parallel"), name="rwkv_prep",
    )(*inputs)


def _block_diag(x, headmask):
    return jnp.concatenate([x] * N_HEADS, axis=0) * headmask


def _wkv_body(ts, r_ref, lw_ref, k_ref, v_ref, a_ref, b_ref, g_ref, rk_ref, gg_ref, gb_ref,
              avg_ref, ones_ref, o_ref, st_ref):
    C = WKV_CHUNK
    n = N_HEADS * C

    @pl.when(pl.program_id(1) == 0)
    def _():
        st_ref[...] = jnp.zeros_like(st_ref)

    ri = lax.broadcasted_iota(jnp.int32, (n, n), 0)
    ci = lax.broadcasted_iota(jnp.int32, (n, n), 1)
    head_f = jnp.where((ri // C) == (ci // HEAD_DIM), 1.0, 0.0)
    head_b = head_f.astype(BF16)
    lag = jnp.where((ri // C) == (ci // C), (ri % C) - (ci % C), -1)
    strict = lag > 0
    incl = lag >= 0
    eye = ri == ci
    eye_f = jnp.where(eye, 1.0, 0.0)
    tr = lax.broadcasted_iota(jnp.int32, (C, C), 0)
    tc = lax.broadcasted_iota(jnp.int32, (C, C), 1)
    tri = jnp.where(tc <= tr, 1.0, 0.0).astype(F32)
    cast = lambda x: x.astype(BF16)
    G = ts // C
    split = lambda ref: ref[0].reshape(G, C, W_GRP)
    tile_heads = lambda x: jnp.concatenate([x] * N_HEADS, axis=1)
    bd16 = lambda x: tile_heads(cast(x)) * head_b[None]
    bmm = lambda x, y: lax.dot_general(x, y, (((2,), (1,)), ((0,), (0,))), preferred_element_type=F32)
    bmm_nt = lambda x, y: lax.dot_general(x, y, (((2,), (2,)), ((0,), (0,))), preferred_element_type=F32)

    lw = split(lw_ref)
    cum = lax.dot_general(jnp.broadcast_to(tri[None], (G, C, C)), lw, (((2,), (1,)), ((0,), (0,))),
                          preferred_element_type=F32, precision=HI)
    cum_c = cum[:, C - 1:C, :]
    e_in = jnp.exp(cum)
    e_neg = jnp.exp(-cum)
    e_tail = jnp.exp(cum_c - cum)
    b_c = split(b_ref)
    k_c = split(k_ref)
    a_t = bd16(split(a_ref) * jnp.exp(cum - lw))
    r_t = bd16(split(r_ref) * e_in)
    b_t = bd16(b_c * e_neg)
    k_t = bd16(k_c * e_neg)
    v_bd = bd16(split(v_ref))
    bh_t = cast(jnp.swapaxes(tile_heads(b_c * e_tail) * head_f[None], 1, 2))
    kh_t = cast(jnp.swapaxes(tile_heads(k_c * e_tail) * head_f[None], 1, 2))
    a_ab = jnp.where(strict[None], bmm_nt(a_t, b_t), 0.0)
    a_ak = cast(jnp.where(strict[None], bmm_nt(a_t, k_t), 0.0))
    a_rb = cast(jnp.where(incl[None], bmm_nt(r_t, b_t), 0.0))
    a_rk = cast(jnp.where(incl[None], bmm_nt(r_t, k_t), 0.0))
    t_inv = eye_f[None] + a_ab
    pw = cast(a_ab)
    for _ in range(int(math.log2(C)) - 1):
        pw = cast(bmm(pw, pw))
        t_inv = t_inv + bmm(cast(t_inv), pw)
    t16 = cast(t_inv)
    ta = cast(bmm(t16, a_t))
    u0 = bmm(cast(bmm(t16, a_ak)), v_bd)
    o0 = bmm(a_rk, v_bd)
    s0 = bmm(kh_t, v_bd)
    o_lhs = jnp.concatenate([r_t, a_rb], axis=2)
    w_col = jnp.sum(jnp.where(eye[None], jnp.exp(cum_c), 0.0), axis=2, keepdims=True)

    st = st_ref[...]
    outs = []
    for g in range(G):
        st16 = cast(st)
        u = cast(_dot(ta[g], st16) + u0[g])
        o_bd = _dot(o_lhs[g], jnp.concatenate([st16, u], axis=0)) + o0[g]
        st = w_col[g] * st + _dot(bh_t[g], u) + s0[g]
        outs.append(o_bd[0:C] + o_bd[C:2 * C] + o_bd[2 * C:3 * C] + o_bd[3 * C:4 * C])
    st_ref[...] = st

    o = jnp.concatenate(outs, axis=0)
    avg = avg_ref[...].astype(BF16)
    mu = _dot_hilo(o, avg)
    xc = o - mu
    var = _dot_hilo(xc * xc, avg)
    on = xc * lax.rsqrt(var + RWKV_GN_EPS) * gg_ref[...] + gb_ref[...]
    r = r_ref[0]
    k = k_ref[0]
    v = v_ref[0]
    bonus = _dot_hilo(r * k * rk_ref[...], ones_ref[...].astype(BF16)) * v
    o_ref[0] = (on + bonus) * g_ref[0]


def wkv_scan(r, lw, k, v, a, b, g, rk, gg, gb, avg, ones):
    bsz, s, _ = r.shape
    ts = WKV_GROUP * WKV_CHUNK
    tile = pl.BlockSpec((1, ts, W_GRP), lambda bi, i: (bi, i, 0))
    const = lambda arr: pl.BlockSpec(arr.shape, lambda bi, i: (0,) * arr.ndim)
    return pl.pallas_call(
        functools.partial(_wkv_body, ts),
        grid=(bsz, s // ts),
        in_specs=[tile] * 7 + [const(x) for x in (rk, gg, gb, avg, ones)],
        out_specs=tile, out_shape=jax.ShapeDtypeStruct((bsz, s, W_GRP), F32),
        scratch_shapes=[pltpu.VMEM((N_HEADS * WKV_CHUNK, W_GRP), F32)],
        compiler_params=_cparams("parallel", "arbitrary"), name="wkv_scan",
    )(r, lw, k, v, a, b, g, rk, gg, gb, avg, ones)


def _gmlp_math(x, g_ref, b_ref, ws_ref, bs_ref, o_ref):
    ts = x.shape[0]
    u = jax.nn.gelu(x[:, :W_GRP])
    v = _ln(jax.nn.gelu(x[:, W_GRP:]), g_ref[...], b_ref[...]).astype(BF16)
    tr = lax.broadcasted_iota(jnp.int32, (GMLP_CHUNK, GMLP_CHUNK), 0)
    tc = lax.broadcasted_iota(jnp.int32, (GMLP_CHUNK, GMLP_CHUNK), 1)
    ws = [jnp.where(tc <= tr, ws_ref[h], 0.0).astype(BF16) for h in range(N_HEADS)]
    for c in range(ts // GMLP_CHUNK):
        rows = slice(c * GMLP_CHUNK, (c + 1) * GMLP_CHUNK)
        mixed = [_dot(ws[h], v[rows, h * HEAD_DIM:(h + 1) * HEAD_DIM]) + bs_ref[h]
                 for h in range(N_HEADS)]
        o_ref[rows, :] = u[rows, :] * jnp.concatenate(mixed, axis=-1)


def _swap_halves(x, lane):
    w = x.shape[-1]
    half = HEAD_DIM // 2
    fwd = pltpu.roll(x, w - half, 1)
    bwd = pltpu.roll(x, half, 1)
    return jnp.where((lane % HEAD_DIM) < half, fwd, bwd)


def _nsa_prep_math(q, kv, misc, pos0, cos_ref, sin_ref,
                   q_o, qr_o, kc_o, vc_o, ks_o, vs_o, kw_o, vw_o, g_o):
    scale = HEAD_DIM ** -0.5
    cos = jnp.concatenate([cos_ref[...]] * 3, axis=1)
    sin = jnp.concatenate([sin_ref[...]] * 3, axis=1)
    lane_q = lax.broadcasted_iota(jnp.int32, q.shape, 1)
    q_rot = q * cos[:, :W_GRP] + _swap_halves(q, lane_q) * sin[:, :W_GRP]
    lane_kv = lax.broadcasted_iota(jnp.int32, kv.shape, 1)
    kv_rot = kv * cos + _swap_halves(kv, lane_kv) * sin
    q_o[0] = (q * scale).astype(BF16)
    qr_o[0] = (q_rot * (scale * LOG2E)).astype(BF16)
    kc_o[0] = kv[:, 0:64]
    vc_o[0] = kv[:, 64:128]
    ts = kv.shape[0]
    lane = lax.broadcasted_iota(jnp.int32, (ts, LANES), 1)
    key_blk = (pos0 + lax.broadcasted_iota(jnp.int32, (ts, 1), 0)) // SEL_LEN
    ks_o[0] = jnp.concatenate([jnp.where(lane < HEAD_DIM, kv_rot[:, 128:256], 0.0),
                               jnp.where(lane == key_blk, NEG, 0.0)], axis=1).astype(BF16)
    ones_col = jnp.where(lane == HEAD_DIM, 1.0, 0.0)
    vs_first = pltpu.roll(kv[:, 128:256], HEAD_DIM, 1)
    vs_o[0] = jnp.where(lane < HEAD_DIM, vs_first, ones_col).T.astype(BF16)
    kw_o[0] = kv_rot[:, 256:320].astype(BF16)
    vw_first = pltpu.roll(kv[:, 256:384], HEAD_DIM, 1)
    vw_o[0] = jnp.where(lane < HEAD_DIM, vw_first, ones_col).astype(BF16)
    g_o[0] = jax.nn.sigmoid(misc)


def _compress_body(kc_ref, vc_ref, pe_ref, w1_ref, w2_ref, ko_ref, vo_ref):
    half = CMP_STRIDE * HEAD_DIM
    for j, (c_ref, o_ref) in enumerate(((kc_ref, ko_ref), (vc_ref, vo_ref))):
        c = c_ref[0].astype(BF16)
        w1 = w1_ref[j]
        lo = _dot(c, w1[:half])
        hi = _dot(c, w1[half:])
        nb = hi.shape[0]
        hi_next = pltpu.roll(hi, nb - 1, 0)
        pe = jnp.broadcast_to(pe_ref[j], (8, 2 * half))
        pe_term = _dot(pe, w1.astype(F32), HI)[0:1]
        h = jax.nn.gelu(lo + hi_next + pe_term)
        o_ref[0] = _dot(h.astype(BF16), w2_ref[j]).astype(BF16)


def nsa_compress(kc_r, vc_r, pe, w1, w2):
    bsz, nb, w = kc_r.shape
    blk = pl.BlockSpec((1, nb, w), lambda bi: (bi, 0, 0))
    const = lambda a: pl.BlockSpec(a.shape, lambda bi: (0,) * a.ndim)
    out = jax.ShapeDtypeStruct((bsz, nb, HEAD_DIM), BF16)
    ospec = pl.BlockSpec((1, nb, HEAD_DIM), lambda bi: (bi, 0, 0))
    return pl.pallas_call(
        _compress_body, grid=(bsz,),
        in_specs=[blk, blk, const(pe), const(w1), const(w2)],
        out_specs=[ospec, ospec], out_shape=[out, out],
        compiler_params=_cparams("parallel"), name="nsa_compress",
    )(kc_r, vc_r, pe, w1, w2)


def _stack_heads(x):
    return jnp.concatenate([x[:, h * HEAD_DIM:(h + 1) * HEAD_DIM] for h in range(N_HEADS)], axis=0)


def _nsa_attn_body(n_top, q_ref, qr_ref, kc_ref, vc_ref, ks_ref, vst_ref, kw_ref, vw_ref, g_ref,
                   ov_ref, o_ref):
    T = Q_BLOCK
    i = pl.program_id(1)
    t0 = i * T
    qs = _stack_heads(q_ref[0])
    qrs = _stack_heads(qr_ref[0])
    pos = t0 + lax.broadcasted_iota(jnp.int32, (T, 1), 0)
    tile4 = lambda x: jnp.concatenate([x] * N_HEADS, axis=0)

    kc = kc_ref[0]
    n_cmp = kc.shape[0]
    cmp_end = lax.broadcasted_iota(jnp.int32, (1, n_cmp), 1) * CMP_STRIDE + (CMP_LEN - 1)
    valid_c = tile4(jnp.where(cmp_end <= pos, 1.0, 0.0))
    s_c = jnp.where(valid_c > 0.5, _dot_nt(qs, kc), NEG)
    p_c = jnp.exp(s_c - jnp.max(s_c, axis=-1, keepdims=True)) * valid_c
    p_c = p_c / jnp.maximum(jnp.sum(p_c, axis=-1, keepdims=True), 1e-20)
    o_c = _dot(p_c.astype(BF16), vc_ref[0])
    p_sum = p_c[0:T] + p_c[T:2 * T] + p_c[2 * T:3 * T] + p_c[3 * T:4 * T]
    ps_hi, ps_lo = _split_bf16(p_sum)
    ov = ov_ref[...].astype(BF16)
    imp_t = _dot_nt(ov, ps_hi) + _dot_nt(ov, ps_lo)

    WK = WINDOW + T
    start = pl.multiple_of(jnp.maximum(t0 - WINDOW, 0), T)
    kw = kw_ref[0, pl.ds(start, WK), :]
    vw = vw_ref[0, pl.ds(start, WK), :]
    kpos_w = start + lax.broadcasted_iota(jnp.int32, (1, WK), 1)
    bias_w = tile4(jnp.where(kpos_w <= pos, jnp.where(kpos_w > pos - WINDOW, 0.0, NEG), NEG))
    s_w = _dot_nt(qrs, kw) + bias_w
    p_w = jnp.exp2(s_w - jnp.max(s_w, axis=-1, keepdims=True))
    acc_w = _dot(p_w.astype(BF16), vw)
    o_w = acc_w[:, :HEAD_DIM] / acc_w[:, HEAD_DIM:HEAD_DIM + 1]

    n_sel = imp_t.shape[0]
    jj = lax.broadcasted_iota(jnp.int32, (n_sel, T), 0)
    blk = (t0 + lax.broadcasted_iota(jnp.int32, (1, T), 1)) // SEL_LEN
    val = jnp.where(jj == blk, 3e38, jnp.where(jj == 0, 3e38, jnp.where(jj <= blk, imp_t, -1.0)))
    sel_t = jnp.zeros((n_sel, T), F32)
    for _ in range(n_top):
        mx = jnp.max(val, axis=0, keepdims=True)
        idx = jnp.min(jnp.where(val == mx, jj, n_sel), axis=0, keepdims=True)
        hit = jj == idx
        sel_t = jnp.where(hit, 1.0, sel_t)
        val = jnp.where(hit, -2.0, val)
    sel = sel_t.T

    KT = SEL_KT
    unsel = tile4((1.0 - sel).astype(BF16))
    q_aug = jnp.concatenate([qrs, jnp.zeros((N_HEADS * T, HEAD_DIM), BF16), unsel], axis=1)
    krow = lax.broadcasted_iota(jnp.int32, (KT, 1), 0)
    pos_row = t0 + lax.broadcasted_iota(jnp.int32, (1, T), 1)

    def sel_tile(jt, carry, diagonal):
        m, acc = carry
        k0 = pl.multiple_of(jt * KT, KT)
        s_t = _dot_nt(ks_ref[0, pl.ds(k0, KT), :], q_aug)
        if diagonal:
            causal = jnp.where(k0 + krow <= pos_row, 0.0, NEG)
            s_t = s_t + jnp.concatenate([causal] * N_HEADS, axis=1)
        m_new = jnp.maximum(m, jnp.max(s_t, axis=0, keepdims=True))
        p_t = jnp.exp2(s_t - m_new)
        acc = jnp.exp2(m - m_new) * acc + _dot(vst_ref[0, :, pl.ds(k0, KT)], p_t.astype(BF16))
        return m_new, acc

    init = (jnp.full((1, N_HEADS * T), NEG, F32), jnp.zeros((LANES, N_HEADS * T), F32))
    n_full = t0 // KT
    carry = lax.fori_loop(0, n_full, functools.partial(sel_tile, diagonal=False), init)
    _, acc_t = sel_tile(n_full, carry, True)
    acc_s = acc_t.T
    o_s = acc_s[:, :HEAD_DIM] / acc_s[:, HEAD_DIM:HEAD_DIM + 1]

    g = g_ref[0]
    outs = []
    for h in range(N_HEADS):
        rows = slice(h * T, (h + 1) * T)
        outs.append(g[:, 3 * h:3 * h + 1] * o_c[rows] + g[:, 3 * h + 1:3 * h + 2] * o_s[rows]
                    + g[:, 3 * h + 2:3 * h + 3] * o_w[rows])
    o_ref[0] = jnp.concatenate(outs, axis=-1)


def nsa_attention(q, qr, k_cmp, v_cmp, ks, vs, kw, vw, gates, overlap):
    bsz, s, _ = q.shape
    n_top = min(SEL_TOPN, s // SEL_LEN)
    n_cmp = k_cmp.shape[1]
    qtile = lambda w: pl.BlockSpec((1, Q_BLOCK, w), lambda bi, i: (bi, i, 0))
    full = lambda rows, w: pl.BlockSpec((1, rows, w), lambda bi, i: (bi, 0, 0))
    return pl.pallas_call(
        functools.partial(_nsa_attn_body, n_top),
        grid=(bsz, s // Q_BLOCK),
        in_specs=[qtile(W_GRP), qtile(W_GRP), full(n_cmp, HEAD_DIM), full(n_cmp, HEAD_DIM),
                  full(s, 2 * LANES), full(LANES, s), full(s, HEAD_DIM), full(s, LANES),
                  qtile(LANES), pl.BlockSpec(overlap.shape, lambda bi, i: (0, 0))],
        out_specs=qtile(W_GRP), out_shape=jax.ShapeDtypeStruct((bsz, s, W_GRP), F32),
        compiler_params=_cparams("parallel", "parallel"), name="nsa_attention",
    )(q, qr, k_cmp, v_cmp, ks, vs, kw, vw, gates, overlap)


def _outproj_body(alpha, a_ref, b_ref, c_ref, d_ref, x_ref, w_ref, g_ref, bb_ref, o_ref):
    mix = jnp.zeros(x_ref.shape, F32)
    for j, part in enumerate((a_ref, b_ref, c_ref, d_ref)):
        mix = mix + _dot(part[...].astype(BF16), w_ref[j * W_GRP:(j + 1) * W_GRP, :])
    o_ref[...] = _ln(alpha * x_ref[...] + mix, g_ref[...], bb_ref[...])


def out_proj(parts, x2d, w, g, b, alpha):
    n, d = x2d.shape
    row = lambda w_: pl.BlockSpec((TM, w_), lambda i: (i, 0))
    const = lambda a: pl.BlockSpec(a.shape, lambda i: (0,) * a.ndim)
    return pl.pallas_call(
        functools.partial(_outproj_body, alpha),
        grid=(n // TM,),
        in_specs=[row(W_GRP)] * 4 + [row(d), const(w), const(g), const(b)],
        out_specs=row(d), out_shape=jax.ShapeDtypeStruct((n, d), F32),
        compiler_params=_cparams("parallel"), name="out_proj",
    )(*parts, x2d, w, g, b)


def _memkv_body(m_ref, wk_ref, wv_ref, k_o, v_o):
    mb = m_ref[...].astype(BF16)
    k_o[...] = _dot(mb, wk_ref[...]).astype(BF16)
    v_o[...] = _dot(mb, wv_ref[...]).astype(BF16)


def mem_kv(mem2d, wk, wv):
    n, d = mem2d.shape
    full = lambda a: pl.BlockSpec(a.shape, lambda i: (0, 0))
    out = jax.ShapeDtypeStruct((n, d), BF16)
    return pl.pallas_call(
        _memkv_body, grid=(1,),
        in_specs=[full(mem2d), full(wk), full(wv)],
        out_specs=[pl.BlockSpec((n, d), lambda i: (0, 0))] * 2, out_shape=[out, out],
        compiler_params=_cparams("arbitrary"), name="mem_kv",
    )(mem2d, wk, wv)


def _xattn_body(alpha, x_ref, k_ref, v_ref, wq_ref, wo_ref, g_ref, b_ref, o_ref):
    x = x_ref[0]
    d = x.shape[-1]
    hd = d // N_MEM_HEADS
    q = (_dot(x.astype(BF16), wq_ref[...]) * (hd ** -0.5)).astype(BF16)
    k = k_ref[0]
    v = v_ref[0]
    heads = []
    for h in range(N_MEM_HEADS):
        cs = slice(h * hd, (h + 1) * hd)
        s = _dot_nt(q[:, cs], k[:, cs])
        p = jnp.exp(s - jnp.max(s, axis=-1, keepdims=True))
        p = p / jnp.sum(p, axis=-1, keepdims=True)
        heads.append(_dot(p.astype(BF16), v[:, cs]))
    o = jnp.concatenate(heads, axis=-1).astype(BF16)
    o_ref[0] = _ln(alpha * x + _dot(o, wo_ref[...]), g_ref[...], b_ref[...])


def cross_attention(x, k, v, wq, wo, g, b, alpha):
    bsz, s, d = x.shape
    m = k.shape[1]
    const = lambda a: pl.BlockSpec(a.shape, lambda bi, i: (0,) * a.ndim)
    return pl.pallas_call(
        functools.partial(_xattn_body, alpha),
        grid=(bsz, s // TM),
        in_specs=[pl.BlockSpec((1, TM, d), lambda bi, i: (bi, i, 0)),
                  pl.BlockSpec((1, m, d), lambda bi, i: (bi, 0, 0)),
                  pl.BlockSpec((1, m, d), lambda bi, i: (bi, 0, 0)),
                  const(wq), const(wo), const(g), const(b)],
        out_specs=pl.BlockSpec((1, TM, d), lambda bi, i: (bi, i, 0)),
        out_shape=jax.ShapeDtypeStruct((bsz, s, d), F32),
        compiler_params=_cparams("parallel", "parallel"), name="cross_attention",
    )(x, k, v, wq, wo, g, b)


PER_GRP = N_EXPERTS // N_EXPERT_GROUPS
LPOS_LANE = PER_GRP
SUB = 8


def _router_body(x_ref, rwt_ref, rb_ref, upper_ref, tok_o, lrow_o, cnt_o):
    tm = x_ref.shape[0]
    logits = _dot_nt(rwt_ref[...], x_ref[...], HI)
    ex = jnp.exp(logits - jnp.max(logits, axis=0, keepdims=True))
    probs = ex / jnp.sum(ex, axis=0, keepdims=True)
    sel = probs + rb_ref[...]
    srow = [sel[e:e + 1] for e in range(N_EXPERTS)]
    prow = [probs[e:e + 1] for e in range(N_EXPERTS)]
    gscore = []
    for g in range(N_EXPERT_GROUPS):
        r = srow[g * PER_GRP:(g + 1) * PER_GRP]
        best = None
        for a in range(PER_GRP):
            for b in range(a + 1, PER_GRP):
                best = r[a] + r[b] if best is None else jnp.maximum(best, r[a] + r[b])
        gscore.append(best)
    g_idx = jnp.zeros((1, tm), jnp.int32)
    top = gscore[0]
    for g in range(1, N_EXPERT_GROUPS):
        better = gscore[g] > top
        g_idx = jnp.where(better, g, g_idx)
        top = jnp.where(better, gscore[g], top)

    def of_group(rows, e):
        out = rows[e]
        for g in range(1, N_EXPERT_GROUPS):
            out = jnp.where(g_idx == g, rows[g * PER_GRP + e], out)
        return out

    sg = [of_group(srow, e) for e in range(PER_GRP)]
    pg = [of_group(prow, e) for e in range(PER_GRP)]
    w = []
    for e in range(PER_GRP):
        rank = jnp.zeros((1, tm), F32)
        for o in range(PER_GRP):
            if o != e:
                ahead = (sg[o] >= sg[e]) if o < e else (sg[o] > sg[e])
                rank = rank + jnp.where(ahead, 1.0, 0.0)
        w.append(jnp.where(rank < 1.5, pg[e], 0.0))
    w_sum = w[0] + w[1] + w[2] + w[3]
    row8 = lax.broadcasted_iota(jnp.int32, (SUB, tm), 0)
    onehot = jnp.where(row8 == g_idx, 1.0, 0.0)
    before = _dot(onehot.astype(BF16), upper_ref[...])
    cnt = jnp.sum(onehot, axis=1, keepdims=True)
    offs = [jnp.zeros((1, 1), F32)]
    for g in range(1, N_EXPERT_GROUPS):
        offs.append(offs[-1] + cnt[g - 1:g])
    lpos = jnp.zeros((1, tm), F32)
    for g in range(N_EXPERT_GROUPS):
        lpos = lpos + onehot[g:g + 1] * (offs[g] + before[g:g + 1])
    tok = jnp.zeros((SUB, tm), F32)
    for e in range(PER_GRP):
        tok = jnp.where(row8 == e, w[e] / w_sum, tok)
    tok = jnp.where(row8 == LPOS_LANE, lpos, tok)
    tok_o[...] = jnp.concatenate([tok, jnp.zeros((LANES - SUB, tm), F32)], axis=0).T
    lrow_o[0] = lpos.astype(jnp.int32)
    rowc = lax.broadcasted_iota(jnp.int32, (SUB, LANES), 0)
    stats = jnp.zeros((SUB, LANES), F32)
    for g in range(N_EXPERT_GROUPS):
        stats = jnp.where(rowc == g, cnt[g:g + 1], stats)
        stats = jnp.where(rowc == N_EXPERT_GROUPS + g, offs[g], stats)
    cnt_o[0] = stats.astype(jnp.int32)


def moe_router(x2d, rwt, rb_col, upper):
    n, d = x2d.shape
    tm = upper.shape[0]
    nt = n // tm
    const = lambda a: pl.BlockSpec(a.shape, lambda i: (0, 0))
    return pl.pallas_call(
        _router_body, grid=(nt,),
        in_specs=[pl.BlockSpec((tm, d), lambda i: (i, 0)), const(rwt), const(rb_col), const(upper)],
        out_specs=[pl.BlockSpec((tm, LANES), lambda i: (i, 0)),
                   pl.BlockSpec((1, 1, tm), lambda i: (i, 0, 0)),
                   pl.BlockSpec((1, SUB, LANES), lambda i: (i, 0, 0))],
        out_shape=[jax.ShapeDtypeStruct((n, LANES), F32), jax.ShapeDtypeStruct((nt, 1, tm), jnp.int32),
                   jax.ShapeDtypeStruct((nt, SUB, LANES), jnp.int32)],
        compiler_params=_cparams("parallel"), name="moe_router",
    )(x2d, rwt, rb_col, upper)


def _split_bf16(x):
    hi = x.astype(BF16)
    return hi, (x - hi.astype(F32)).astype(BF16)


def _moe_body(alpha, offs_ref, cnts_ref, x_ref, tok_ref, lrow_ref, wg_ref, wu_ref, wd_ref, g_ref, b_ref, o_ref,
              xs_ref, gs_ref, acc_ref):
    i = pl.program_id(0)
    step = pl.program_id(1)
    tm = x_ref.shape[0]
    eps = wd_ref.shape[1]

    @pl.when(step == 0)
    def _():
        slot = lax.broadcasted_iota(jnp.int32, (tm, tm), 0)
        perm = jnp.where(slot == lrow_ref[0], 1.0, 0.0).astype(BF16)
        xs_ref[...] = _dot(perm, x_ref[...].astype(BF16)).astype(BF16)
        t_hi, t_lo = _split_bf16(tok_ref[...])
        gs_ref[...] = _dot(perm, t_hi) + _dot(perm, t_lo)
        acc_ref[...] = jnp.zeros_like(acc_ref)

    grp = (step * eps) // PER_GRP
    off = offs_ref[i * N_EXPERT_GROUPS + grp]
    end = off + cnts_ref[i * N_EXPERT_GROUPS + grp]
    lane = lax.broadcasted_iota(jnp.int32, (MOE_BLK, LANES), 1)
    for s in range(tm // MOE_BLK):
        lo = s * MOE_BLK

        @pl.when(jnp.logical_and(off < lo + MOE_BLK, end > lo))
        def _(lo=lo):
            rows = slice(lo, lo + MOE_BLK)
            xs = xs_ref[rows, :]
            ridx = lo + lax.broadcasted_iota(jnp.int32, (MOE_BLK, 1), 0)
            in_run = jnp.where(ridx >= off, jnp.where(ridx < end, 1.0, 0.0), 0.0)
            gs = gs_ref[rows, :]
            y = jnp.zeros((MOE_BLK, x_ref.shape[1]), F32)
            for k in range(eps):
                gt = _dot(xs, wg_ref[0, k])
                h = (gt * jax.nn.sigmoid(gt) * _dot(xs, wu_ref[0, k])).astype(BF16)
                in_grp = (step * eps + k) % PER_GRP
                gcol = jnp.sum(jnp.where(lane == in_grp, gs, 0.0), axis=-1, keepdims=True)
                y = y + (gcol * in_run) * _dot(h, wd_ref[0, k])
            acc_ref[rows, :] += y

    @pl.when(step == pl.num_programs(1) - 1)
    def _():
        lcol = tok_ref[:, LPOS_LANE:LPOS_LANE + 1].astype(jnp.int32)
        slot = lax.broadcasted_iota(jnp.int32, (tm, tm), 1)
        unperm = jnp.where(slot == lcol, 1.0, 0.0).astype(BF16)
        a_hi, a_lo = _split_bf16(acc_ref[...])
        y = _dot(unperm, a_hi) + _dot(unperm, a_lo)
        o_ref[...] = _ln(alpha * x_ref[...] + y, g_ref[...], b_ref[...])


def moe_ffn(x2d, tok, lrow, offs, cnts, layer, wg, wu, wd, g, b, alpha):
    n, d = x2d.shape
    _, ne, de, _ = wd.shape
    tm = lrow.shape[2]
    const = lambda a: pl.BlockSpec(a.shape, lambda i, e, o, c: (0,) * a.ndim)
    grid_spec = pltpu.PrefetchScalarGridSpec(
        num_scalar_prefetch=2, grid=(n // tm, ne // MOE_EPS),
        in_specs=[pl.BlockSpec((tm, d), lambda i, e, o, c: (i, 0), pipeline_mode=pl.Buffered(1)),
                  pl.BlockSpec((tm, LANES), lambda i, e, o, c: (i, 0)),
                  pl.BlockSpec((1, 1, tm), lambda i, e, o, c: (i, 0, 0)),
                  pl.BlockSpec((1, MOE_EPS, d, de), lambda i, e, o, c: (layer, e, 0, 0)),
                  pl.BlockSpec((1, MOE_EPS, d, de), lambda i, e, o, c: (layer, e, 0, 0)),
                  pl.BlockSpec((1, MOE_EPS, de, d), lambda i, e, o, c: (layer, e, 0, 0)),
                  const(g), const(b)],
        out_specs=pl.BlockSpec((tm, d), lambda i, e, o, c: (i, 0)),
        scratch_shapes=[pltpu.VMEM((tm, d), BF16), pltpu.VMEM((tm, LANES), F32), pltpu.VMEM((tm, d), F32)])
    return pl.pallas_call(
        functools.partial(_moe_body, alpha), grid_spec=grid_spec,
        out_shape=jax.ShapeDtypeStruct((n, d), F32),
        compiler_params=_cparams("parallel", "arbitrary"), name="moe_ffn",
    )(offs, cnts, x2d, tok, lrow, wg, wu, wd, g, b)


def _rope_tables(s):
    inv = ROPE_THETA ** (-jnp.arange(0, HEAD_DIM, 2, dtype=F32) / HEAD_DIM)
    ang = jnp.arange(s, dtype=F32)[:, None] * inv[None, :]
    cos, sin = jnp.cos(ang), jnp.sin(ang)
    cos_h = jnp.concatenate([cos, cos], axis=-1)
    sin_h = jnp.concatenate([-sin, sin], axis=-1)
    return jnp.tile(cos_h, (1, 2)), jnp.tile(sin_h, (1, 2))


def _overlap_matrix(s, n_cmp_pad):
    n_sel = s // SEL_LEN
    assert n_sel <= LANES
    cmp_start = jnp.arange(n_cmp_pad) * CMP_STRIDE
    sel_start = jnp.arange(LANES) * SEL_LEN
    ov = jnp.clip(jnp.minimum(cmp_start[None, :] + CMP_LEN, sel_start[:, None] + SEL_LEN)
                  - jnp.maximum(cmp_start[None, :], sel_start[:, None]), 0, None).astype(F32) / CMP_LEN
    n_cmp = s // CMP_STRIDE - (CMP_LEN // CMP_STRIDE - 1)
    real = (jnp.arange(n_cmp_pad)[None, :] < n_cmp) & (jnp.arange(LANES)[:, None] < n_sel)
    return jnp.where(real, ov, 0.0)


def _pad_rows(w, lo, total):
    out = jnp.zeros((total, w.shape[1]), w.dtype)
    return out.at[lo:lo + w.shape[0]].set(w)


def kernel(x, mem, ln_in_g, ln_in_b, w_in, w_out, conv_w, conv_b, conv_gn_g, conv_gn_b, rwkv_mu, rwkv_w0, rwkv_w_up, rwkv_a0, rwkv_a_up, rwkv_g_up, rwkv_k_k, rwkv_k_a, rwkv_r_k, rwkv_gn_g, rwkv_gn_b, rwkv_v_down, rwkv_v_mu, rwkv_v0, rwkv_v_up, gmlp_ln_g, gmlp_ln_b, gmlp_w_s, gmlp_b_s, nsa_pe_k, nsa_w1_k, nsa_w2_k, nsa_pe_v, nsa_w1_v, nsa_w2_v, ln1_g, ln1_b, xa_wq, xa_wk, xa_wv, xa_wo, ln2_g, ln2_b, router_w, router_bias, moe_w_gate, moe_w_up, moe_w_down, ln3_g, ln3_b):
    bsz, s, d = x.shape
    depth = w_in.shape[0]
    n = bsz * s
    alpha = (2 * depth) ** 0.25
    row = lambda a: a.reshape(1, -1)

    cos_t, sin_t = _rope_tables(s)
    n_blk = s // CMP_STRIDE
    overlap = _overlap_matrix(s, n_blk)
    avg64 = _group_avg_matrix(W_GRP, HEAD_DIM)
    ones64 = avg64 * HEAD_DIM
    moe_tm = min(MOE_TM, n)
    t_idx = jnp.arange(moe_tm)
    upper = (t_idx[:, None] < t_idx[None, :]).astype(BF16)
    mem2d = mem.reshape(bsz * mem.shape[1], d)

    wg_b, wu_b, wd_b = moe_w_gate.astype(BF16), moe_w_up.astype(BF16), moe_w_down.astype(BF16)

    xs = x.reshape(n, d)
    v_first = None
    for l in range(depth):
        main = IN_SPLITS[-1][1]
        w_misc = jnp.zeros((d, LANES), F32).at[:, :w_in.shape[2] - main].set(w_in[l, :, main:])
        if l > 0:
            w_misc = w_misc.at[:, MISC_VD_OFF:MISC_VD_OFF + rwkv_v_down.shape[2]].set(rwkv_v_down[l - 1])
        gmlp_p = (row(gmlp_ln_g[l]), row(gmlp_ln_b[l]), gmlp_w_s[l], gmlp_b_s[l].reshape(N_HEADS, GMLP_CHUNK, 1))
        outs = in_proj(xs, s, row(ln_in_g), row(ln_in_b), w_in, l, w_misc.astype(BF16), gmlp_p,
                       cos_t, sin_t, apply_ln=(l == 0))
        if l == 0:
            xs, outs = outs[0], outs[1:]
        conv_in, rwkv_in, misc, out_c, q_b, qr_b, kc, vc, ks, vs, kw, vw, gates = outs
        b3 = lambda a: a.reshape(bsz, s, a.shape[-1])

        out_a = conv_mixer(b3(conv_in), conv_w[l], row(conv_b[l]), row(conv_gn_g[l]), row(conv_gn_b[l]), avg64)

        rp = {"mu": row(rwkv_mu[l]), "w0": row(rwkv_w0[l]), "a0": row(rwkv_a0[l]),
              "wup": _pad_rows(rwkv_w_up[l], 0, W_GRP), "aup": _pad_rows(rwkv_a_up[l], 64, W_GRP),
              "gup": _pad_rows(rwkv_g_up[l], 128, W_GRP),
              "kk": row(rwkv_k_k[l]), "ka": row(rwkv_k_a[l]), "ones": ones64}
        if l > 0:
            rp["vmu"] = jnp.zeros((1, LANES), F32).at[0, MISC_VD_OFF:MISC_VD_OFF + rwkv_v_mu.shape[1]].set(rwkv_v_mu[l - 1])
            rp["v0"] = row(rwkv_v0[l - 1])
            rp["vup"] = _pad_rows(rwkv_v_up[l - 1], MISC_VD_OFF, LANES)
        r_, lw_, k_, v_, a_, b_, g_ = rwkv_prep(b3(rwkv_in), b3(misc), v_first, rp)
        if l == 0:
            v_first = v_
        out_b = wkv_scan(r_, lw_, k_, v_, a_, b_, g_, row(rwkv_r_k[l]), row(rwkv_gn_g[l]),
                         row(rwkv_gn_b[l]), avg64, ones64)

        pe = jnp.stack([nsa_pe_k[l].reshape(1, -1), nsa_pe_v[l].reshape(1, -1)])
        w1 = jnp.stack([nsa_w1_k[l], nsa_w1_v[l]]).astype(BF16)
        w2 = jnp.stack([nsa_w2_k[l], nsa_w2_v[l]]).astype(BF16)
        k_cmp, v_cmp = nsa_compress(kc.reshape(bsz, n_blk, CMP_STRIDE * HEAD_DIM),
                                    vc.reshape(bsz, n_blk, CMP_STRIDE * HEAD_DIM), pe, w1, w2)
        out_d = nsa_attention(q_b, qr_b, k_cmp, v_cmp, ks, vs, kw, vw, gates, overlap)

        flat = lambda a: a.reshape(n, W_GRP)
        xs = out_proj((flat(out_a), flat(out_b), out_c, flat(out_d)), xs, w_out[l].astype(BF16),
                      row(ln1_g[l]), row(ln1_b[l]), alpha)

        mk, mv = mem_kv(mem2d, xa_wk[l].astype(BF16), xa_wv[l].astype(BF16))
        m_len = mem.shape[1]
        xs = cross_attention(xs.reshape(bsz, s, d), mk.reshape(bsz, m_len, d), mv.reshape(bsz, m_len, d),
                             xa_wq[l].astype(BF16), xa_wo[l].astype(BF16),
                             row(ln2_g[l]), row(ln2_b[l]), alpha).reshape(n, d)

        tok, lrow, stats = moe_router(xs, router_w.T, router_bias.reshape(-1, 1), upper)
        cnts = stats[:, :N_EXPERT_GROUPS, 0].reshape(-1)
        offs = stats[:, N_EXPERT_GROUPS:2 * N_EXPERT_GROUPS, 0].reshape(-1)
        xs = moe_ffn(xs, tok, lrow, offs, cnts, l, wg_b, wu_b, wd_b,
                     row(ln3_g[l]), row(ln3_b[l]), alpha)
    return xs.reshape(bsz, s, d)
```

```python
import functools
import math

import jax
import jax.numpy as jnp
from jax import lax
from jax.experimental import pallas as pl
from jax.experimental.pallas import tpu as pltpu

F32 = jnp.float32
BF16 = jnp.bfloat16
HI = lax.Precision.HIGHEST

HEAD_DIM = 64
N_HEADS = 4
W_GRP = 256
CONV_WIDTH = 31
GMLP_CHUNK = 128
CMP_LEN = 32
CMP_STRIDE = 16
SEL_LEN = 64
SEL_TOPN = 16
WINDOW = 512
Q_BLOCK = 256
ROPE_THETA = 10000.0
N_MEM_HEADS = 4
N_EXPERTS = 16
N_EXPERT_GROUPS = 4
LN_EPS = 1e-5
RWKV_GN_EPS = 64e-5
NEG = -1e30
LOG2E = math.log2(math.e)
LANES = 128
WKV_CHUNK = 64
WKV_GROUP = 8

TM = 512
MOE_TM = 1024
MOE_BLK = 256
MOE_EPS = 4
SEL_SUB = 4
SEL_KT = 1024
VMEM_LIMIT = 56 * 1024 * 1024


def _cparams(*sem):
    return pltpu.CompilerParams(dimension_semantics=sem, vmem_limit_bytes=VMEM_LIMIT)


def _ln(x, g, b, eps=LN_EPS):
    mu = jnp.mean(x, axis=-1, keepdims=True)
    xc = x - mu
    var = jnp.mean(xc * xc, axis=-1, keepdims=True)
    return xc * lax.rsqrt(var + eps) * g + b


def _dot(a, b, precision=None):
    return jnp.dot(a, b, preferred_element_type=F32, precision=precision)


def _dot_nt(a, b, precision=None):
    return lax.dot_general(a, b, (((1,), (1,)), ((), ())),
                           preferred_element_type=F32, precision=precision)


def _dot_hilo(x, w_bf16):
    hi = x.astype(BF16)
    lo = (x - hi.astype(F32)).astype(BF16)
    return _dot(hi, w_bf16) + _dot(lo, w_bf16)


def _dot3(x, w):
    x_hi = x.astype(BF16)
    x_lo = (x - x_hi.astype(F32)).astype(BF16)
    w_hi = w.astype(BF16)
    w_lo = (w - w_hi.astype(F32)).astype(BF16)
    return _dot(x_hi, w_hi) + _dot(x_lo, w_hi) + _dot(x_hi, w_lo)


def _group_avg_matrix(width, group):
    r = jnp.arange(width)[:, None] // group
    c = jnp.arange(width)[None, :] // group
    return jnp.where(r == c, 1.0 / group, 0.0).astype(F32)


IN_SPLITS = (("conv", 0, 512), ("rwkv", 512, 1536), ("gmlp", 1536, 2048),
             ("q", 2048, 2304), ("kv", 2304, 2688), ("misc", 2688, 2816))
MISC_VD_OFF = 32


N_CONV_P, N_GMLP_P, N_RWKV_P, N_VMIX_P = 5, 4, 9, 3
N_NSA_OUT, N_RWKV_OUT = 9, 7


def _inproj_body(apply_ln, has_vfirst, tiles_per_seq, x_ref, g_ref, b_ref, w_ref, wm_ref, cos_ref, sin_ref, *rest):
    take = lambda k: (rest[:k], rest[k:])
    conv_p, rest = take(N_CONV_P)
    gmlp_p, rest = take(N_GMLP_P)
    rwkv_p, rest = take(N_RWKV_P + (N_VMIX_P if has_vfirst else 0))
    vf_ref = None
    if has_vfirst:
        (vf_ref,), rest = take(1)
    if apply_ln:
        (xln_o,), rest = take(1)
    (conv_o,), rest = take(1)
    rwkv_o, rest = take(N_RWKV_OUT)
    (gmlp_o,), rest = take(1)
    nsa_o, rest = take(N_NSA_OUT)
    wb_ref, hbuf, shifted, buf, mbuf = rest
    main = IN_SPLITS[-1][1]

    @pl.when(pl.program_id(0) == 0)
    def _():
        wb_ref[...] = w_ref[0, :, :main].astype(BF16)

    x = x_ref[...]
    if apply_ln:
        x = _ln(x, g_ref[...], b_ref[...])
        xln_o[...] = x
    xb = x.astype(BF16)
    cols = {name: (lo, hi) for name, lo, hi in IN_SPLITS}
    proj = lambda name: _dot(xb, wb_ref[:, cols[name][0]:cols[name][1]])
    tile_in_seq = pl.program_id(0) % tiles_per_seq
    first = tile_in_seq == 0
    misc = _dot(xb, wm_ref[...])
    _conv_math(proj("conv"), first, *conv_p, conv_o, hbuf, shifted)
    _rwkv_prep_math(proj("rwkv"), misc, first, vf_ref, rwkv_p, rwkv_o, buf, mbuf)
    _gmlp_math(proj("gmlp"), *gmlp_p, gmlp_o)
    _nsa_prep_math(proj("q"), proj("kv"), misc, tile_in_seq * x.shape[0], cos_ref, sin_ref, *nsa_o)


def in_proj(x2d, seq_len, g, b, w_in, layer, w_misc, cos_t, sin_t, conv_p, gmlp_p, rwkv_p, v_first, apply_ln):
    n, d = x2d.shape
    bsz = n // seq_len
    tps = seq_len // TM
    row = lambda w: pl.BlockSpec((TM, w), lambda i: (i, 0))
    const = lambda a: pl.BlockSpec(a.shape, lambda i: (0,) * a.ndim)
    seq = lambda w: pl.BlockSpec((1, TM, w), lambda i: (i // tps, i % tps, 0))
    sd = lambda w, dt: jax.ShapeDtypeStruct((bsz, seq_len, w), dt)
    flat = lambda w: jax.ShapeDtypeStruct((n, w), F32)
    out_shapes = ([flat(W_GRP)] + [sd(W_GRP, F32)] * N_RWKV_OUT + [flat(W_GRP)]
                  + [sd(W_GRP, BF16), sd(W_GRP, BF16), sd(HEAD_DIM, F32), sd(HEAD_DIM, F32),
                     sd(2 * LANES, BF16), jax.ShapeDtypeStruct((bsz, LANES, seq_len), BF16),
                     sd(HEAD_DIM, BF16), sd(LANES, BF16), sd(LANES, F32)])
    out_specs = ([row(W_GRP)] + [seq(W_GRP)] * N_RWKV_OUT + [row(W_GRP)]
                 + [seq(W_GRP), seq(W_GRP), seq(HEAD_DIM), seq(HEAD_DIM), seq(2 * LANES),
                    pl.BlockSpec((1, LANES, TM), lambda i: (i // tps, 0, i % tps)),
                    seq(HEAD_DIM), seq(LANES), seq(LANES)])
    if apply_ln:
        out_shapes = [flat(d)] + out_shapes
        out_specs = [row(d)] + out_specs
    has_vfirst = v_first is not None
    assert len(conv_p) == N_CONV_P and len(gmlp_p) == N_GMLP_P
    assert len(rwkv_p) == N_RWKV_P + (N_VMIX_P if has_vfirst else 0)
    tab = pl.BlockSpec((TM, LANES), lambda i: (i % tps, 0))
    params = list(conv_p) + list(gmlp_p) + list(rwkv_p)
    inputs = [x2d, g, b, w_in, w_misc, cos_t, sin_t] + params + ([v_first] if has_vfirst else [])
    in_specs = ([row(d), const(g), const(b),
                 pl.BlockSpec((1,) + w_in.shape[1:], lambda i: (layer, 0, 0), pipeline_mode=pl.Buffered(1)),
                 const(w_misc), tab, tab] + [const(a) for a in params]
                + ([seq(W_GRP)] if has_vfirst else []))
    return pl.pallas_call(
        functools.partial(_inproj_body, apply_ln, has_vfirst, tps),
        grid=(n // TM,),
        in_specs=in_specs, out_specs=out_specs, out_shape=out_shapes,
        scratch_shapes=[pltpu.VMEM((d, IN_SPLITS[-1][1]), BF16),
                        pltpu.VMEM((TM + CONV_HALO, W_GRP), F32), pltpu.VMEM((TM + CONV_HALO, W_GRP), F32),
                        pltpu.VMEM((TM + SHIFT_HALO, 4 * W_GRP), F32), pltpu.VMEM((TM + SHIFT_HALO, LANES), F32)],
        compiler_params=_cparams("arbitrary"), name="in_proj",
    )(*inputs)


CONV_HALO = 32


def _carry_rows(buf, halo, first):
    tail = buf.shape[0] - halo

    @pl.when(first)
    def _():
        buf[0:halo, :] = jnp.zeros((halo, buf.shape[1]), buf.dtype)

    @pl.when(jnp.logical_not(first))
    def _():
        buf[0:halo, :] = buf[tail:tail + halo, :]


def _conv_math(cur, first, w_ref, b_ref, gg_ref, gb_ref, avg_ref, o_ref, hbuf, shifted):
    ts = cur.shape[0]
    h = cur[:, :W_GRP] * jax.nn.sigmoid(cur[:, W_GRP:])
    _carry_rows(hbuf, CONV_HALO, first)
    hbuf[CONV_HALO:, :] = h
    acc = jnp.zeros((ts, W_GRP), F32)
    base = CONV_HALO - (CONV_WIDTH - 1)
    for phase in range(SUB):
        taps = [j for j in range(CONV_WIDTH) if (base + j) % SUB == phase]
        span = (base + taps[-1]) // SUB * SUB + ts
        shifted[0:span, :] = hbuf[phase:phase + span, :]
        for j in taps:
            lo = (base + j) // SUB * SUB
            acc = acc + w_ref[j:j + 1, :] * shifted[lo:lo + ts, :]
    acc = acc + b_ref[...]
    avg = avg_ref[...].astype(BF16)
    mu = _dot_hilo(acc, avg)
    xc = acc - mu
    var = _dot_hilo(xc * xc, avg)
    y = xc * lax.rsqrt(var + LN_EPS) * gg_ref[...] + gb_ref[...]
    o_ref[...] = y * jax.nn.sigmoid(y)


SHIFT_HALO = 8


def _shift_prev(buf, cur, first):
    ts = cur.shape[0]
    _carry_rows(buf, SHIFT_HALO, first)
    buf[SHIFT_HALO:, :] = cur
    return buf[SHIFT_HALO - 1:SHIFT_HALO - 1 + ts, :]


def _rwkv_prep_math(cur, misc, first, vf_ref, params, outs, buf, mbuf):
    mu_ref, w0_ref, a0_ref, wup_ref, aup_ref, gup_ref, kk_ref, ka_ref, ones_ref = params[:9]
    r_o, lw_o, k_o, v_o, a_o, b_o, g_o = outs
    prev = _shift_prev(buf, cur, first)
    y = cur + mu_ref[...] * (prev - cur)
    r = y[:, 0:256]
    k = y[:, 256:512]
    v = y[:, 512:768]
    lora = y[:, 768:1024]
    w = w0_ref[...] + _dot3(jnp.tanh(lora), wup_ref[...])
    a = jax.nn.sigmoid(a0_ref[...] + _dot3(lora, aup_ref[...]))
    g = _dot3(jax.nn.sigmoid(lora), gup_ref[...])
    z = -w
    sp = jnp.maximum(z, 0.0) + jnp.log(1.0 + jnp.exp(-jnp.abs(z)))
    lw = -jnp.exp(-sp - 0.5)
    if vf_ref is not None:
        vmu_ref, v0_ref, vup_ref = params[9:]
        mprev = _shift_prev(mbuf, misc, first)
        vd = misc + vmu_ref[...] * (mprev - misc)
        v_mix = jax.nn.sigmoid(v0_ref[...] + _dot3(vd, vup_ref[...]))
        v = v + (vf_ref[0] - v) * v_mix
    kk = k * kk_ref[...]
    n2 = _dot_hilo(kk * kk, ones_ref[...].astype(BF16))
    kk = kk / jnp.maximum(jnp.sqrt(n2), 1e-12)
    k2 = k * (1.0 + (a - 1.0) * ka_ref[...])
    r_o[0] = r
    lw_o[0] = lw
    k_o[0] = k2
    v_o[0] = v
    a_o[0] = -kk
    b_o[0] = kk * a
    g_o[0] = g


def _block_diag(x, headmask):
    return jnp.concatenate([x] * N_HEADS, axis=0) * headmask


def _wkv_body(ts, r_ref, lw_ref, k_ref, v_ref, a_ref, b_ref, g_ref, rk_ref, gg_ref, gb_ref,
              avg_ref, ones_ref, o_ref, st_ref):
    C = WKV_CHUNK
    n = N_HEADS * C

    @pl.when(pl.program_id(1) == 0)
    def _():
        st_ref[...] = jnp.zeros_like(st_ref)

    ri = lax.broadcasted_iota(jnp.int32, (n, n), 0)
    ci = lax.broadcasted_iota(jnp.int32, (n, n), 1)
    head_f = jnp.where((ri // C) == (ci // HEAD_DIM), 1.0, 0.0)
    head_b = head_f.astype(BF16)
    lag = jnp.where((ri // C) == (ci // C), (ri % C) - (ci % C), -1)
    strict = lag > 0
    incl = lag >= 0
    eye = ri == ci
    eye_f = jnp.where(eye, 1.0, 0.0)
    tr = lax.broadcasted_iota(jnp.int32, (C, C), 0)
    tc = lax.broadcasted_iota(jnp.int32, (C, C), 1)
    tri = jnp.where(tc <= tr, 1.0, 0.0).astype(F32)
    cast = lambda x: x.astype(BF16)
    G = ts // C
    split = lambda ref: ref[0].reshape(G, C, W_GRP)
    tile_heads = lambda x: jnp.concatenate([x] * N_HEADS, axis=1)
    bd16 = lambda x: tile_heads(cast(x)) * head_b[None]
    bmm = lambda x, y: lax.dot_general(x, y, (((2,), (1,)), ((0,), (0,))), preferred_element_type=F32)
    bmm_nt = lambda x, y: lax.dot_general(x, y, (((2,), (2,)), ((0,), (0,))), preferred_element_type=F32)

    lw = split(lw_ref)
    cum = lax.dot_general(jnp.broadcast_to(tri[None], (G, C, C)), lw, (((2,), (1,)), ((0,), (0,))),
                          preferred_element_type=F32, precision=HI)
    cum_c = cum[:, C - 1:C, :]
    e_in = jnp.exp(cum)
    e_neg = jnp.exp(-cum)
    e_tail = jnp.exp(cum_c - cum)
    b_c = split(b_ref)
    k_c = split(k_ref)
    a_t = bd16(split(a_ref) * jnp.exp(cum - lw))
    r_t = bd16(split(r_ref) * e_in)
    b_t = bd16(b_c * e_neg)
    k_t = bd16(k_c * e_neg)
    v_bd = bd16(split(v_ref))
    bh_t = cast(jnp.swapaxes(tile_heads(b_c * e_tail) * head_f[None], 1, 2))
    kh_t = cast(jnp.swapaxes(tile_heads(k_c * e_tail) * head_f[None], 1, 2))
    a_ab = jnp.where(strict[None], bmm_nt(a_t, b_t), 0.0)
    a_ak = cast(jnp.where(strict[None], bmm_nt(a_t, k_t), 0.0))
    a_rb = cast(jnp.where(incl[None], bmm_nt(r_t, b_t), 0.0))
    a_rk = cast(jnp.where(incl[None], bmm_nt(r_t, k_t), 0.0))
    t_inv = eye_f[None] + a_ab
    pw = cast(a_ab)
    for _ in range(int(math.log2(C)) - 1):
        pw = cast(bmm(pw, pw))
        t_inv = t_inv + bmm(cast(t_inv), pw)
    t16 = cast(t_inv)
    ta = cast(bmm(t16, a_t))
    u0 = bmm(cast(bmm(t16, a_ak)), v_bd)
    o0 = bmm(a_rk, v_bd)
    s0 = bmm(kh_t, v_bd)
    o_lhs = jnp.concatenate([r_t, a_rb], axis=2)
    w_col = jnp.sum(jnp.where(eye[None], jnp.exp(cum_c), 0.0), axis=2, keepdims=True)

    st = st_ref[...]
    outs = []
    for g in range(G):
        st16 = cast(st)
        u = cast(_dot(ta[g], st16) + u0[g])
        o_bd = _dot(o_lhs[g], jnp.concatenate([st16, u], axis=0)) + o0[g]
        st = w_col[g] * st + _dot(bh_t[g], u) + s0[g]
        outs.append(o_bd[0:C] + o_bd[C:2 * C] + o_bd[2 * C:3 * C] + o_bd[3 * C:4 * C])
    st_ref[...] = st

    o = jnp.concatenate(outs, axis=0)
    avg = avg_ref[...].astype(BF16)
    mu = _dot_hilo(o, avg)
    xc = o - mu
    var = _dot_hilo(xc * xc, avg)
    on = xc * lax.rsqrt(var + RWKV_GN_EPS) * gg_ref[...] + gb_ref[...]
    r = r_ref[0]
    k = k_ref[0]
    v = v_ref[0]
    bonus = _dot_hilo(r * k * rk_ref[...], ones_ref[...].astype(BF16)) * v
    o_ref[0] = (on + bonus) * g_ref[0]


def wkv_scan(r, lw, k, v, a, b, g, rk, gg, gb, avg, ones):
    bsz, s, _ = r.shape
    ts = WKV_GROUP * WKV_CHUNK
    tile = pl.BlockSpec((1, ts, W_GRP), lambda bi, i: (bi, i, 0))
    const = lambda arr: pl.BlockSpec(arr.shape, lambda bi, i: (0,) * arr.ndim)
    return pl.pallas_call(
        functools.partial(_wkv_body, ts),
        grid=(bsz, s // ts),
        in_specs=[tile] * 7 + [const(x) for x in (rk, gg, gb, avg, ones)],
        out_specs=tile, out_shape=jax.ShapeDtypeStruct((bsz, s, W_GRP), F32),
        scratch_shapes=[pltpu.VMEM((N_HEADS * WKV_CHUNK, W_GRP), F32)],
        compiler_params=_cparams("parallel", "arbitrary"), name="wkv_scan",
    )(r, lw, k, v, a, b, g, rk, gg, gb, avg, ones)


def _gmlp_math(x, g_ref, b_ref, ws_ref, bs_ref, o_ref):
    ts = x.shape[0]
    u = jax.nn.gelu(x[:, :W_GRP])
    v = _ln(jax.nn.gelu(x[:, W_GRP:]), g_ref[...], b_ref[...]).astype(BF16)
    tr = lax.broadcasted_iota(jnp.int32, (GMLP_CHUNK, GMLP_CHUNK), 0)
    tc = lax.broadcasted_iota(jnp.int32, (GMLP_CHUNK, GMLP_CHUNK), 1)
    ws = [jnp.where(tc <= tr, ws_ref[h], 0.0).astype(BF16) for h in range(N_HEADS)]
    for c in range(ts // GMLP_CHUNK):
        rows = slice(c * GMLP_CHUNK, (c + 1) * GMLP_CHUNK)
        mixed = [_dot(ws[h], v[rows, h * HEAD_DIM:(h + 1) * HEAD_DIM]) + bs_ref[h]
                 for h in range(N_HEADS)]
        o_ref[rows, :] = u[rows, :] * jnp.concatenate(mixed, axis=-1)


def _swap_halves(x, lane):
    w = x.shape[-1]
    half = HEAD_DIM // 2
    fwd = pltpu.roll(x, w - half, 1)
    bwd = pltpu.roll(x, half, 1)
    return jnp.where((lane % HEAD_DIM) < half, fwd, bwd)


def _nsa_prep_math(q, kv, misc, pos0, cos_ref, sin_ref,
                   q_o, qr_o, kc_o, vc_o, ks_o, vs_o, kw_o, vw_o, g_o):
    scale = HEAD_DIM ** -0.5
    cos = jnp.concatenate([cos_ref[...]] * 3, axis=1)
    sin = jnp.concatenate([sin_ref[...]] * 3, axis=1)
    lane_q = lax.broadcasted_iota(jnp.int32, q.shape, 1)
    q_rot = q * cos[:, :W_GRP] + _swap_halves(q, lane_q) * sin[:, :W_GRP]
    lane_kv = lax.broadcasted_iota(jnp.int32, kv.shape, 1)
    kv_rot = kv * cos + _swap_halves(kv, lane_kv) * sin
    q_o[0] = (q * scale).astype(BF16)
    qr_o[0] = (q_rot * (scale * LOG2E)).astype(BF16)
    kc_o[0] = kv[:, 0:64]
    vc_o[0] = kv[:, 64:128]
    ts = kv.shape[0]
    lane = lax.broadcasted_iota(jnp.int32, (ts, LANES), 1)
    key_blk = (pos0 + lax.broadcasted_iota(jnp.int32, (ts, 1), 0)) // SEL_LEN
    ks_o[0] = jnp.concatenate([jnp.where(lane < HEAD_DIM, kv_rot[:, 128:256], 0.0),
                               jnp.where(lane == key_blk, NEG, 0.0)], axis=1).astype(BF16)
    ones_col = jnp.where(lane == HEAD_DIM, 1.0, 0.0)
    vs_first = pltpu.roll(kv[:, 128:256], HEAD_DIM, 1)
    vs_o[0] = jnp.where(lane < HEAD_DIM, vs_first, ones_col).T.astype(BF16)
    kw_o[0] = kv_rot[:, 256:320].astype(BF16)
    vw_first = pltpu.roll(kv[:, 256:384], HEAD_DIM, 1)
    vw_o[0] = jnp.where(lane < HEAD_DIM, vw_first, ones_col).astype(BF16)
    g_o[0] = jax.nn.sigmoid(misc)


def _compress_body(kc_ref, vc_ref, pe_ref, w1_ref, w2_ref, ko_ref, vo_ref):
    half = CMP_STRIDE * HEAD_DIM
    for j, (c_ref, o_ref) in enumerate(((kc_ref, ko_ref), (vc_ref, vo_ref))):
        c = c_ref[0].astype(BF16)
        w1 = w1_ref[j]
        lo = _dot(c, w1[:half])
        hi = _dot(c, w1[half:])
        nb = hi.shape[0]
        hi_next = pltpu.roll(hi, nb - 1, 0)
        pe = jnp.broadcast_to(pe_ref[j], (8, 2 * half))
        pe_term = _dot(pe, w1.astype(F32), HI)[0:1]
        h = jax.nn.gelu(lo + hi_next + pe_term)
        o_ref[0] = _dot(h.astype(BF16), w2_ref[j]).astype(BF16)


def nsa_compress(kc_r, vc_r, pe, w1, w2):
    bsz, nb, w = kc_r.shape
    blk = pl.BlockSpec((1, nb, w), lambda bi: (bi, 0, 0))
    const = lambda a: pl.BlockSpec(a.shape, lambda bi: (0,) * a.ndim)
    out = jax.ShapeDtypeStruct((bsz, nb, HEAD_DIM), BF16)
    ospec = pl.BlockSpec((1, nb, HEAD_DIM), lambda bi: (bi, 0, 0))
    return pl.pallas_call(
        _compress_body, grid=(bsz,),
        in_specs=[blk, blk, const(pe), const(w1), const(w2)],
        out_specs=[ospec, ospec], out_shape=[out, out],
        compiler_params=_cparams("parallel"), name="nsa_compress",
    )(kc_r, vc_r, pe, w1, w2)


def _stack_heads(x):
    return jnp.concatenate([x[:, h * HEAD_DIM:(h + 1) * HEAD_DIM] for h in range(N_HEADS)], axis=0)


def _nsa_attn_body(n_top, q_ref, qr_ref, kc_ref, vc_ref, ks_ref, vst_ref, kw_ref, vw_ref, g_ref,
                   ov_ref, o_ref):
    T = Q_BLOCK
    i = pl.program_id(1)
    t0 = i * T
    qs = _stack_heads(q_ref[0])
    qrs = _stack_heads(qr_ref[0])
    pos = t0 + lax.broadcasted_iota(jnp.int32, (T, 1), 0)
    tile4 = lambda x: jnp.concatenate([x] * N_HEADS, axis=0)

    kc = kc_ref[0]
    n_cmp = kc.shape[0]
    cmp_end = lax.broadcasted_iota(jnp.int32, (1, n_cmp), 1) * CMP_STRIDE + (CMP_LEN - 1)
    valid_c = tile4(jnp.where(cmp_end <= pos, 1.0, 0.0))
    s_c = jnp.where(valid_c > 0.5, _dot_nt(qs, kc), NEG)
    p_c = jnp.exp(s_c - jnp.max(s_c, axis=-1, keepdims=True)) * valid_c
    p_c = p_c / jnp.maximum(jnp.sum(p_c, axis=-1, keepdims=True), 1e-20)
    o_c = _dot(p_c.astype(BF16), vc_ref[0])
    p_sum = p_c[0:T] + p_c[T:2 * T] + p_c[2 * T:3 * T] + p_c[3 * T:4 * T]
    ps_hi, ps_lo = _split_bf16(p_sum)
    ov = ov_ref[...].astype(BF16)
    imp_t = _dot_nt(ov, ps_hi) + _dot_nt(ov, ps_lo)

    WK = WINDOW + T
    start = pl.multiple_of(jnp.maximum(t0 - WINDOW, 0), T)
    kw = kw_ref[0, pl.ds(start, WK), :]
    vw = vw_ref[0, pl.ds(start, WK), :]
    kpos_w = start + lax.broadcasted_iota(jnp.int32, (1, WK), 1)
    bias_w = tile4(jnp.where(kpos_w <= pos, jnp.where(kpos_w > pos - WINDOW, 0.0, NEG), NEG))
    s_w = _dot_nt(qrs, kw) + bias_w
    p_w = jnp.exp2(s_w - jnp.max(s_w, axis=-1, keepdims=True))
    acc_w = _dot(p_w.astype(BF16), vw)
    o_w = acc_w[:, :HEAD_DIM] / acc_w[:, HEAD_DIM:HEAD_DIM + 1]

    n_sel = imp_t.shape[0]
    jj = lax.broadcasted_iota(jnp.int32, (n_sel, T), 0)
    blk = (t0 + lax.broadcasted_iota(jnp.int32, (1, T), 1)) // SEL_LEN
    val = jnp.where(jj == blk, 3e38, jnp.where(jj == 0, 3e38, jnp.where(jj <= blk, imp_t, -1.0)))
    sel_t = jnp.zeros((n_sel, T), F32)
    for _ in range(n_top):
        mx = jnp.max(val, axis=0, keepdims=True)
        idx = jnp.min(jnp.where(val == mx, jj, n_sel), axis=0, keepdims=True)
        hit = jj == idx
        sel_t = jnp.where(hit, 1.0, sel_t)
        val = jnp.where(hit, -2.0, val)
    sel = sel_t.T

    KT = SEL_KT
    unsel = tile4((1.0 - sel).astype(BF16))
    q_aug = jnp.concatenate([qrs, jnp.zeros((N_HEADS * T, HEAD_DIM), BF16), unsel], axis=1)
    krow = lax.broadcasted_iota(jnp.int32, (KT, 1), 0)
    pos_row = t0 + lax.broadcasted_iota(jnp.int32, (1, T), 1)

    KS = KT // SEL_SUB

    def sel_tile(jt, carry, diagonal):
        m, acc = carry
        k0 = pl.multiple_of(jt * KT, KT)

        def scores(j):
            ks = pl.multiple_of(k0 + j * KS, KS)
            s_t = _dot_nt(ks_ref[0, pl.ds(ks, KS), :], q_aug)
            if diagonal:
                causal = jnp.where(ks + krow[:KS] <= pos_row, 0.0, NEG)
                s_t = s_t + jnp.concatenate([causal] * N_HEADS, axis=1)
            return s_t

        s_next = scores(0)
        for j in range(SEL_SUB):
            s_t = s_next
            if j + 1 < SEL_SUB:
                s_next = scores(j + 1)
            m_new = jnp.maximum(m, jnp.max(s_t, axis=0, keepdims=True))
            p_t = jnp.exp2(s_t - m_new)
            ks = pl.multiple_of(k0 + j * KS, KS)
            acc = jnp.exp2(m - m_new) * acc + _dot(vst_ref[0, :, pl.ds(ks, KS)], p_t.astype(BF16))
            m = m_new
        return m, acc

    init = (jnp.full((1, N_HEADS * T), NEG, F32), jnp.zeros((LANES, N_HEADS * T), F32))
    n_full = t0 // KT
    carry = lax.fori_loop(0, n_full, functools.partial(sel_tile, diagonal=False), init)
    _, acc_t = sel_tile(n_full, carry, True)
    acc_s = acc_t.T
    o_s = acc_s[:, :HEAD_DIM] / acc_s[:, HEAD_DIM:HEAD_DIM + 1]

    g = g_ref[0]
    outs = []
    for h in range(N_HEADS):
        rows = slice(h * T, (h + 1) * T)
        outs.append(g[:, 3 * h:3 * h + 1] * o_c[rows] + g[:, 3 * h + 1:3 * h + 2] * o_s[rows]
                    + g[:, 3 * h + 2:3 * h + 3] * o_w[rows])
    o_ref[0] = jnp.concatenate(outs, axis=-1)


def nsa_attention(q, qr, k_cmp, v_cmp, ks, vs, kw, vw, gates, overlap):
    bsz, s, _ = q.shape
    n_top = min(SEL_TOPN, s // SEL_LEN)
    n_cmp = k_cmp.shape[1]
    qtile = lambda w: pl.BlockSpec((1, Q_BLOCK, w), lambda bi, i: (bi, i, 0))
    full = lambda rows, w: pl.BlockSpec((1, rows, w), lambda bi, i: (bi, 0, 0))
    return pl.pallas_call(
        functools.partial(_nsa_attn_body, n_top),
        grid=(bsz, s // Q_BLOCK),
        in_specs=[qtile(W_GRP), qtile(W_GRP), full(n_cmp, HEAD_DIM), full(n_cmp, HEAD_DIM),
                  full(s, 2 * LANES), full(LANES, s), full(s, HEAD_DIM), full(s, LANES),
                  qtile(LANES), pl.BlockSpec(overlap.shape, lambda bi, i: (0, 0))],
        out_specs=qtile(W_GRP), out_shape=jax.ShapeDtypeStruct((bsz, s, W_GRP), F32),
        compiler_params=_cparams("parallel", "parallel"), name="nsa_attention",
    )(q, qr, k_cmp, v_cmp, ks, vs, kw, vw, gates, overlap)


def _outproj_body(alpha, a_ref, b_ref, c_ref, d_ref, x_ref, w_ref, g_ref, bb_ref, o_ref):
    mix = jnp.zeros(x_ref.shape, F32)
    for j, part in enumerate((a_ref, b_ref, c_ref, d_ref)):
        mix = mix + _dot(part[...].astype(BF16), w_ref[j * W_GRP:(j + 1) * W_GRP, :])
    o_ref[...] = _ln(alpha * x_ref[...] + mix, g_ref[...], bb_ref[...])


def out_proj(parts, x2d, w, g, b, alpha):
    n, d = x2d.shape
    row = lambda w_: pl.BlockSpec((TM, w_), lambda i: (i, 0))
    const = lambda a: pl.BlockSpec(a.shape, lambda i: (0,) * a.ndim)
    return pl.pallas_call(
        functools.partial(_outproj_body, alpha),
        grid=(n // TM,),
        in_specs=[row(W_GRP)] * 4 + [row(d), const(w), const(g), const(b)],
        out_specs=row(d), out_shape=jax.ShapeDtypeStruct((n, d), F32),
        compiler_params=_cparams("parallel"), name="out_proj",
    )(*parts, x2d, w, g, b)


def _memkv_body(m_ref, wk_ref, wv_ref, k_o, v_o):
    mb = m_ref[...].astype(BF16)
    k_o[...] = _dot(mb, wk_ref[...]).astype(BF16)
    v_o[...] = _dot(mb, wv_ref[...]).astype(BF16)


def mem_kv(mem2d, wk, wv):
    n, d = mem2d.shape
    full = lambda a: pl.BlockSpec(a.shape, lambda i: (0, 0))
    out = jax.ShapeDtypeStruct((n, d), BF16)
    return pl.pallas_call(
        _memkv_body, grid=(1,),
        in_specs=[full(mem2d), full(wk), full(wv)],
        out_specs=[pl.BlockSpec((n, d), lambda i: (0, 0))] * 2, out_shape=[out, out],
        compiler_params=_cparams("arbitrary"), name="mem_kv",
    )(mem2d, wk, wv)


def _xattn_body(alpha, x_ref, k_ref, v_ref, wq_ref, wo_ref, g_ref, b_ref, o_ref):
    x = x_ref[0]
    d = x.shape[-1]
    hd = d // N_MEM_HEADS
    q = (_dot(x.astype(BF16), wq_ref[...]) * (hd ** -0.5)).astype(BF16)
    k = k_ref[0]
    v = v_ref[0]
    heads = []
    for h in range(N_MEM_HEADS):
        cs = slice(h * hd, (h + 1) * hd)
        s = _dot_nt(q[:, cs], k[:, cs])
        p = jnp.exp(s - jnp.max(s, axis=-1, keepdims=True))
        p = p / jnp.sum(p, axis=-1, keepdims=True)
        heads.append(_dot(p.astype(BF16), v[:, cs]))
    o = jnp.concatenate(heads, axis=-1).astype(BF16)
    o_ref[0] = _ln(alpha * x + _dot(o, wo_ref[...]), g_ref[...], b_ref[...])


def cross_attention(x, k, v, wq, wo, g, b, alpha):
    bsz, s, d = x.shape
    m = k.shape[1]
    const = lambda a: pl.BlockSpec(a.shape, lambda bi, i: (0,) * a.ndim)
    return pl.pallas_call(
        functools.partial(_xattn_body, alpha),
        grid=(bsz, s // TM),
        in_specs=[pl.BlockSpec((1, TM, d), lambda bi, i: (bi, i, 0)),
                  pl.BlockSpec((1, m, d), lambda bi, i: (bi, 0, 0)),
                  pl.BlockSpec((1, m, d), lambda bi, i: (bi, 0, 0)),
                  const(wq), const(wo), const(g), const(b)],
        out_specs=pl.BlockSpec((1, TM, d), lambda bi, i: (bi, i, 0)),
        out_shape=jax.ShapeDtypeStruct((bsz, s, d), F32),
        compiler_params=_cparams("parallel", "parallel"), name="cross_attention",
    )(x, k, v, wq, wo, g, b)


PER_GRP = N_EXPERTS // N_EXPERT_GROUPS
LPOS_LANE = PER_GRP
SUB = 8


def _router_body(x_ref, rwt_ref, rb_ref, upper_ref, tok_o, lrow_o, cnt_o):
    tm = x_ref.shape[0]
    logits = _dot_nt(rwt_ref[...], x_ref[...], HI)
    ex = jnp.exp(logits - jnp.max(logits, axis=0, keepdims=True))
    probs = ex / jnp.sum(ex, axis=0, keepdims=True)
    sel = probs + rb_ref[...]
    srow = [sel[e:e + 1] for e in range(N_EXPERTS)]
    prow = [probs[e:e + 1] for e in range(N_EXPERTS)]
    gscore = []
    for g in range(N_EXPERT_GROUPS):
        r = srow[g * PER_GRP:(g + 1) * PER_GRP]
        best = None
        for a in range(PER_GRP):
            for b in range(a + 1, PER_GRP):
                best = r[a] + r[b] if best is None else jnp.maximum(best, r[a] + r[b])
        gscore.append(best)
    g_idx = jnp.zeros((1, tm), jnp.int32)
    top = gscore[0]
    for g in range(1, N_EXPERT_GROUPS):
        better = gscore[g] > top
        g_idx = jnp.where(better, g, g_idx)
        top = jnp.where(better, gscore[g], top)

    def of_group(rows, e):
        out = rows[e]
        for g in range(1, N_EXPERT_GROUPS):
            out = jnp.where(g_idx == g, rows[g * PER_GRP + e], out)
        return out

    sg = [of_group(srow, e) for e in range(PER_GRP)]
    pg = [of_group(prow, e) for e in range(PER_GRP)]
    w = []
    for e in range(PER_GRP):
        rank = jnp.zeros((1, tm), F32)
        for o in range(PER_GRP):
            if o != e:
                ahead = (sg[o] >= sg[e]) if o < e else (sg[o] > sg[e])
                rank = rank + jnp.where(ahead, 1.0, 0.0)
        w.append(jnp.where(rank < 1.5, pg[e], 0.0))
    w_sum = w[0] + w[1] + w[2] + w[3]
    row8 = lax.broadcasted_iota(jnp.int32, (SUB, tm), 0)
    onehot = jnp.where(row8 == g_idx, 1.0, 0.0)
    before = _dot(onehot.astype(BF16), upper_ref[...])
    cnt = jnp.sum(onehot, axis=1, keepdims=True)
    offs = [jnp.zeros((1, 1), F32)]
    for g in range(1, N_EXPERT_GROUPS):
        offs.append(offs[-1] + cnt[g - 1:g])
    lpos = jnp.zeros((1, tm), F32)
    for g in range(N_EXPERT_GROUPS):
        lpos = lpos + onehot[g:g + 1] * (offs[g] + before[g:g + 1])
    tok = jnp.zeros((SUB, tm), F32)
    for e in range(PER_GRP):
        tok = jnp.where(row8 == e, w[e] / w_sum, tok)
    tok = jnp.where(row8 == LPOS_LANE, lpos, tok)
    tok_o[...] = jnp.concatenate([tok, jnp.zeros((LANES - SUB, tm), F32)], axis=0).T
    lrow_o[0] = lpos.astype(jnp.int32)
    rowc = lax.broadcasted_iota(jnp.int32, (SUB, LANES), 0)
    stats = jnp.zeros((SUB, LANES), F32)
    for g in range(N_EXPERT_GROUPS):
        stats = jnp.where(rowc == g, cnt[g:g + 1], stats)
        stats = jnp.where(rowc == N_EXPERT_GROUPS + g, offs[g], stats)
    cnt_o[0] = stats.astype(jnp.int32)


def moe_router(x2d, rwt, rb_col, upper):
    n, d = x2d.shape
    tm = upper.shape[0]
    nt = n // tm
    const = lambda a: pl.BlockSpec(a.shape, lambda i: (0, 0))
    return pl.pallas_call(
        _router_body, grid=(nt,),
        in_specs=[pl.BlockSpec((tm, d), lambda i: (i, 0)), const(rwt), const(rb_col), const(upper)],
        out_specs=[pl.BlockSpec((tm, LANES), lambda i: (i, 0)),
                   pl.BlockSpec((1, 1, tm), lambda i: (i, 0, 0)),
                   pl.BlockSpec((1, SUB, LANES), lambda i: (i, 0, 0))],
        out_shape=[jax.ShapeDtypeStruct((n, LANES), F32), jax.ShapeDtypeStruct((nt, 1, tm), jnp.int32),
                   jax.ShapeDtypeStruct((nt, SUB, LANES), jnp.int32)],
        compiler_params=_cparams("parallel"), name="moe_router",
    )(x2d, rwt, rb_col, upper)


def _split_bf16(x):
    hi = x.astype(BF16)
    return hi, (x - hi.astype(F32)).astype(BF16)


def _moe_body(alpha, offs_ref, cnts_ref, x_ref, tok_ref, lrow_ref, wg_ref, wu_ref, wd_ref, g_ref, b_ref, o_ref,
              xs_ref, gs_ref, acc_ref):
    i = pl.program_id(0)
    step = pl.program_id(1)
    tm = x_ref.shape[0]
    eps = wd_ref.shape[1]

    @pl.when(step == 0)
    def _():
        slot = lax.broadcasted_iota(jnp.int32, (tm, tm), 0)
        perm = jnp.where(slot == lrow_ref[0], 1.0, 0.0).astype(BF16)
        xs_ref[...] = _dot(perm, x_ref[...].astype(BF16)).astype(BF16)
        t_hi, t_lo = _split_bf16(tok_ref[...])
        gs_ref[...] = _dot(perm, t_hi) + _dot(perm, t_lo)
        acc_ref[...] = jnp.zeros_like(acc_ref)

    grp = (step * eps) // PER_GRP
    off = offs_ref[i * N_EXPERT_GROUPS + grp]
    end = off + cnts_ref[i * N_EXPERT_GROUPS + grp]
    lane = lax.broadcasted_iota(jnp.int32, (MOE_BLK, LANES), 1)
    for s in range(tm // MOE_BLK):
        lo = s * MOE_BLK

        @pl.when(jnp.logical_and(off < lo + MOE_BLK, end > lo))
        def _(lo=lo):
            rows = slice(lo, lo + MOE_BLK)
            xs = xs_ref[rows, :]
            ridx = lo + lax.broadcasted_iota(jnp.int32, (MOE_BLK, 1), 0)
            in_run = jnp.where(ridx >= off, jnp.where(ridx < end, 1.0, 0.0), 0.0)
            gs = gs_ref[rows, :]
            y = jnp.zeros((MOE_BLK, x_ref.shape[1]), F32)
            for k in range(eps):
                gt = _dot(xs, wg_ref[0, k])
                h = (gt * jax.nn.sigmoid(gt) * _dot(xs, wu_ref[0, k])).astype(BF16)
                in_grp = (step * eps + k) % PER_GRP
                gcol = jnp.sum(jnp.where(lane == in_grp, gs, 0.0), axis=-1, keepdims=True)
                y = y + (gcol * in_run) * _dot(h, wd_ref[0, k])
            acc_ref[rows, :] += y

    @pl.when(step == pl.num_programs(1) - 1)
    def _():
        lcol = tok_ref[:, LPOS_LANE:LPOS_LANE + 1].astype(jnp.int32)
        slot = lax.broadcasted_iota(jnp.int32, (tm, tm), 1)
        unperm = jnp.where(slot == lcol, 1.0, 0.0).astype(BF16)
        y = _dot(unperm, acc_ref[...].astype(BF16))
        o_ref[...] = _ln(alpha * x_ref[...] + y, g_ref[...], b_ref[...])


def moe_ffn(x2d, tok, lrow, offs, cnts, layer, wg, wu, wd, g, b, alpha):
    n, d = x2d.shape
    _, ne, de, _ = wd.shape
    tm = lrow.shape[2]
    const = lambda a: pl.BlockSpec(a.shape, lambda i, e, o, c: (0,) * a.ndim)
    grid_spec = pltpu.PrefetchScalarGridSpec(
        num_scalar_prefetch=2, grid=(n // tm, ne // MOE_EPS),
        in_specs=[pl.BlockSpec((tm, d), lambda i, e, o, c: (i, 0), pipeline_mode=pl.Buffered(1)),
                  pl.BlockSpec((tm, LANES), lambda i, e, o, c: (i, 0)),
                  pl.BlockSpec((1, 1, tm), lambda i, e, o, c: (i, 0, 0)),
                  pl.BlockSpec((1, MOE_EPS, d, de), lambda i, e, o, c: (layer, e, 0, 0)),
                  pl.BlockSpec((1, MOE_EPS, d, de), lambda i, e, o, c: (layer, e, 0, 0)),
                  pl.BlockSpec((1, MOE_EPS, de, d), lambda i, e, o, c: (layer, e, 0, 0)),
                  const(g), const(b)],
        out_specs=pl.BlockSpec((tm, d), lambda i, e, o, c: (i, 0)),
        scratch_shapes=[pltpu.VMEM((tm, d), BF16), pltpu.VMEM((tm, LANES), F32), pltpu.VMEM((tm, d), F32)])
    return pl.pallas_call(
        functools.partial(_moe_body, alpha), grid_spec=grid_spec,
        out_shape=jax.ShapeDtypeStruct((n, d), F32),
        compiler_params=_cparams("parallel", "arbitrary"), name="moe_ffn",
    )(offs, cnts, x2d, tok, lrow, wg, wu, wd, g, b)


def _rope_tables(s):
    inv = ROPE_THETA ** (-jnp.arange(0, HEAD_DIM, 2, dtype=F32) / HEAD_DIM)
    ang = jnp.arange(s, dtype=F32)[:, None] * inv[None, :]
    cos, sin = jnp.cos(ang), jnp.sin(ang)
    cos_h = jnp.concatenate([cos, cos], axis=-1)
    sin_h = jnp.concatenate([-sin, sin], axis=-1)
    return jnp.tile(cos_h, (1, 2)), jnp.tile(sin_h, (1, 2))


def _overlap_matrix(s, n_cmp_pad):
    n_sel = s // SEL_LEN
    assert n_sel <= LANES
    cmp_start = jnp.arange(n_cmp_pad) * CMP_STRIDE
    sel_start = jnp.arange(LANES) * SEL_LEN
    ov = jnp.clip(jnp.minimum(cmp_start[None, :] + CMP_LEN, sel_start[:, None] + SEL_LEN)
                  - jnp.maximum(cmp_start[None, :], sel_start[:, None]), 0, None).astype(F32) / CMP_LEN
    n_cmp = s // CMP_STRIDE - (CMP_LEN // CMP_STRIDE - 1)
    real = (jnp.arange(n_cmp_pad)[None, :] < n_cmp) & (jnp.arange(LANES)[:, None] < n_sel)
    return jnp.where(real, ov, 0.0)


def _pad_rows(w, lo, total):
    out = jnp.zeros((total, w.shape[1]), w.dtype)
    return out.at[lo:lo + w.shape[0]].set(w)


def kernel(x, mem, ln_in_g, ln_in_b, w_in, w_out, conv_w, conv_b, conv_gn_g, conv_gn_b, rwkv_mu, rwkv_w0, rwkv_w_up, rwkv_a0, rwkv_a_up, rwkv_g_up, rwkv_k_k, rwkv_k_a, rwkv_r_k, rwkv_gn_g, rwkv_gn_b, rwkv_v_down, rwkv_v_mu, rwkv_v0, rwkv_v_up, gmlp_ln_g, gmlp_ln_b, gmlp_w_s, gmlp_b_s, nsa_pe_k, nsa_w1_k, nsa_w2_k, nsa_pe_v, nsa_w1_v, nsa_w2_v, ln1_g, ln1_b, xa_wq, xa_wk, xa_wv, xa_wo, ln2_g, ln2_b, router_w, router_bias, moe_w_gate, moe_w_up, moe_w_down, ln3_g, ln3_b):
    bsz, s, d = x.shape
    depth = w_in.shape[0]
    n = bsz * s
    alpha = (2 * depth) ** 0.25
    row = lambda a: a.reshape(1, -1)

    cos_t, sin_t = _rope_tables(s)
    n_blk = s // CMP_STRIDE
    overlap = _overlap_matrix(s, n_blk)
    avg64 = _group_avg_matrix(W_GRP, HEAD_DIM)
    ones64 = avg64 * HEAD_DIM
    moe_tm = min(MOE_TM, n)
    t_idx = jnp.arange(moe_tm)
    upper = (t_idx[:, None] < t_idx[None, :]).astype(BF16)
    mem2d = mem.reshape(bsz * mem.shape[1], d)

    wg_b, wu_b, wd_b = moe_w_gate.astype(BF16), moe_w_up.astype(BF16), moe_w_down.astype(BF16)

    xs = x.reshape(n, d)
    v_first = None
    for l in range(depth):
        main = IN_SPLITS[-1][1]
        w_misc = jnp.zeros((d, LANES), F32).at[:, :w_in.shape[2] - main].set(w_in[l, :, main:])
        if l > 0:
            w_misc = w_misc.at[:, MISC_VD_OFF:MISC_VD_OFF + rwkv_v_down.shape[2]].set(rwkv_v_down[l - 1])
        gmlp_p = (row(gmlp_ln_g[l]), row(gmlp_ln_b[l]), gmlp_w_s[l], gmlp_b_s[l].reshape(N_HEADS, GMLP_CHUNK, 1))
        conv_p = (conv_w[l], row(conv_b[l]), row(conv_gn_g[l]), row(conv_gn_b[l]), avg64)
        rwkv_p = (row(rwkv_mu[l]), row(rwkv_w0[l]), row(rwkv_a0[l]), _pad_rows(rwkv_w_up[l], 0, W_GRP),
                  _pad_rows(rwkv_a_up[l], 64, W_GRP), _pad_rows(rwkv_g_up[l], 128, W_GRP),
                  row(rwkv_k_k[l]), row(rwkv_k_a[l]), ones64)
        if l > 0:
            vmu = jnp.zeros((1, LANES), F32).at[0, MISC_VD_OFF:MISC_VD_OFF + rwkv_v_mu.shape[1]].set(rwkv_v_mu[l - 1])
            rwkv_p += (vmu, row(rwkv_v0[l - 1]), _pad_rows(rwkv_v_up[l - 1], MISC_VD_OFF, LANES))
        outs = in_proj(xs, s, row(ln_in_g), row(ln_in_b), w_in, l, w_misc.astype(BF16), cos_t, sin_t,
                       conv_p, gmlp_p, rwkv_p, v_first, apply_ln=(l == 0))
        if l == 0:
            xs, outs = outs[0], outs[1:]
        out_a = outs[0]
        r_, lw_, k_, v_, a_, b_, g_ = outs[1:1 + N_RWKV_OUT]
        out_c = outs[1 + N_RWKV_OUT]
        q_b, qr_b, kc, vc, ks, vs, kw, vw, gates = outs[2 + N_RWKV_OUT:]
        if l == 0:
            v_first = v_
        out_b = wkv_scan(r_, lw_, k_, v_, a_, b_, g_, row(rwkv_r_k[l]), row(rwkv_gn_g[l]),
                         row(rwkv_gn_b[l]), avg64, ones64)

        pe = jnp.stack([nsa_pe_k[l].reshape(1, -1), nsa_pe_v[l].reshape(1, -1)])
        w1 = jnp.stack([nsa_w1_k[l], nsa_w1_v[l]]).astype(BF16)
        w2 = jnp.stack([nsa_w2_k[l], nsa_w2_v[l]]).astype(BF16)
        k_cmp, v_cmp = nsa_compress(kc.reshape(bsz, n_blk, CMP_STRIDE * HEAD_DIM),
                                    vc.reshape(bsz, n_blk, CMP_STRIDE * HEAD_DIM), pe, w1, w2)
        out_d = nsa_attention(q_b, qr_b, k_cmp, v_cmp, ks, vs, kw, vw, gates, overlap)

        flat = lambda a: a.reshape(n, W_GRP)
        xs = out_proj((flat(out_a), flat(out_b), out_c, flat(out_d)), xs, w_out[l].astype(BF16),
                      row(ln1_g[l]), row(ln1_b[l]), alpha)

        mk, mv = mem_kv(mem2d, xa_wk[l].astype(BF16), xa_wv[l].astype(BF16))
        m_len = mem.shape[1]
        xs = cross_attention(xs.reshape(bsz, s, d), mk.reshape(bsz, m_len, d), mv.reshape(bsz, m_len, d),
                             xa_wq[l].astype(BF16), xa_wo[l].astype(BF16),
                             row(ln2_g[l]), row(ln2_b[l]), alpha).reshape(n, d)

        tok, lrow, stats = moe_router(xs, router_w.T, router_bias.reshape(-1, 1), upper)
        cnts = stats[:, :N_EXPERT_GROUPS, 0].reshape(-1)
        offs = stats[:, N_EXPERT_GROUPS:2 * N_EXPERT_GROUPS, 0].reshape(-1)
        xs = moe_ffn(xs, tok, lrow, offs, cnts, l, wg_b, wu_b, wd_b,
                     row(ln3_g[l]), row(ln3_b[l]), alpha)
    return xs.reshape(bsz, s, d)
```

```python
import functools
import math

import jax
import jax.numpy as jnp
from jax import lax
from jax.experimental import pallas as pl
from jax.experimental.pallas import tpu as pltpu

F32 = jnp.float32
BF16 = jnp.bfloat16
HI = lax.Precision.HIGHEST

HEAD_DIM = 64
N_HEADS = 4
W_GRP = 256
CONV_WIDTH = 31
GMLP_CHUNK = 128
CMP_LEN = 32
CMP_STRIDE = 16
SEL_LEN = 64
SEL_TOPN = 16
WINDOW = 512
Q_BLOCK = 256
ROPE_THETA = 10000.0
N_MEM_HEADS = 4
N_EXPERTS = 16
N_EXPERT_GROUPS = 4
LN_EPS = 1e-5
RWKV_GN_EPS = 64e-5
NEG = -1e30
LOG2E = math.log2(math.e)
LANES = 128
WKV_CHUNK = 64
WKV_GROUP = 8

TM = 512
MOE_TM = 1024
MOE_BLK = 256
MOE_EPS = 4
SEL_SUB = 2
SEL_KT = 1024
VMEM_LIMIT = 56 * 1024 * 1024


def _cparams(*sem):
    return pltpu.CompilerParams(dimension_semantics=sem, vmem_limit_bytes=VMEM_LIMIT)


def _ln(x, g, b, eps=LN_EPS):
    mu = jnp.mean(x, axis=-1, keepdims=True)
    xc = x - mu
    var = jnp.mean(xc * xc, axis=-1, keepdims=True)
    return xc * lax.rsqrt(var + eps) * g + b


def _dot(a, b, precision=None):
    return jnp.dot(a, b, preferred_element_type=F32, precision=precision)


def _dot_nt(a, b, precision=None):
    return lax.dot_general(a, b, (((1,), (1,)), ((), ())),
                           preferred_element_type=F32, precision=precision)


def _dot_hilo(x, w_bf16):
    hi = x.astype(BF16)
    lo = (x - hi.astype(F32)).astype(BF16)
    return _dot(hi, w_bf16) + _dot(lo, w_bf16)


def _dot3(x, w):
    x_hi = x.astype(BF16)
    x_lo = (x - x_hi.astype(F32)).astype(BF16)
    w_hi = w.astype(BF16)
    w_lo = (w - w_hi.astype(F32)).astype(BF16)
    return _dot(x_hi, w_hi) + _dot(x_lo, w_hi) + _dot(x_hi, w_lo)


def _group_avg_matrix(width, group):
    r = jnp.arange(width)[:, None] // group
    c = jnp.arange(width)[None, :] // group
    return jnp.where(r == c, 1.0 / group, 0.0).astype(F32)


IN_SPLITS = (("conv", 0, 512), ("rwkv", 512, 1536), ("gmlp", 1536, 2048),
             ("q", 2048, 2304), ("kv", 2304, 2688), ("misc", 2688, 2816))
MISC_VD_OFF = 32


N_CONV_P, N_GMLP_P, N_RWKV_P, N_VMIX_P = 5, 4, 9, 3
N_NSA_OUT, N_RWKV_OUT = 9, 7


def _inproj_body(apply_ln, has_vfirst, tiles_per_seq, x_ref, g_ref, b_ref, w_ref, wm_ref, cos_ref, sin_ref, *rest):
    take = lambda k: (rest[:k], rest[k:])
    conv_p, rest = take(N_CONV_P)
    gmlp_p, rest = take(N_GMLP_P)
    rwkv_p, rest = take(N_RWKV_P + (N_VMIX_P if has_vfirst else 0))
    vf_ref = None
    if has_vfirst:
        (vf_ref,), rest = take(1)
    if apply_ln:
        (xln_o,), rest = take(1)
    (conv_o,), rest = take(1)
    rwkv_o, rest = take(N_RWKV_OUT)
    (gmlp_o,), rest = take(1)
    nsa_o, rest = take(N_NSA_OUT)
    wb_ref, hbuf, shifted, buf, mbuf = rest
    main = IN_SPLITS[-1][1]

    @pl.when(pl.program_id(0) == 0)
    def _():
        wb_ref[...] = w_ref[0, :, :main].astype(BF16)

    x = x_ref[...]
    if apply_ln:
        x = _ln(x, g_ref[...], b_ref[...])
        xln_o[...] = x
    xb = x.astype(BF16)
    cols = {name: (lo, hi) for name, lo, hi in IN_SPLITS}
    proj = lambda name: _dot(xb, wb_ref[:, cols[name][0]:cols[name][1]])
    tile_in_seq = pl.program_id(0) % tiles_per_seq
    first = tile_in_seq == 0
    misc = _dot(xb, wm_ref[...])
    _conv_math(proj("conv"), first, *conv_p, conv_o, hbuf, shifted)
    _rwkv_prep_math(proj("rwkv"), misc, first, vf_ref, rwkv_p, rwkv_o, buf, mbuf)
    _gmlp_math(proj("gmlp"), *gmlp_p, gmlp_o)
    _nsa_prep_math(proj("q"), proj("kv"), misc, tile_in_seq * x.shape[0], cos_ref, sin_ref, *nsa_o)


def in_proj(x2d, seq_len, g, b, w_in, layer, w_misc, cos_t, sin_t, conv_p, gmlp_p, rwkv_p, v_first, apply_ln):
    n, d = x2d.shape
    bsz = n // seq_len
    tps = seq_len // TM
    row = lambda w: pl.BlockSpec((TM, w), lambda i: (i, 0))
    const = lambda a: pl.BlockSpec(a.shape, lambda i: (0,) * a.ndim)
    seq = lambda w: pl.BlockSpec((1, TM, w), lambda i: (i // tps, i % tps, 0))
    sd = lambda w, dt: jax.ShapeDtypeStruct((bsz, seq_len, w), dt)
    flat = lambda w: jax.ShapeDtypeStruct((n, w), F32)
    out_shapes = ([flat(W_GRP)] + [sd(W_GRP, F32)] * N_RWKV_OUT + [flat(W_GRP)]
                  + [sd(W_GRP, BF16), sd(W_GRP, BF16), sd(HEAD_DIM, F32), sd(HEAD_DIM, F32),
                     sd(2 * LANES, BF16), jax.ShapeDtypeStruct((bsz, LANES, seq_len), BF16),
                     sd(HEAD_DIM, BF16), sd(LANES, BF16), sd(LANES, F32)])
    out_specs = ([row(W_GRP)] + [seq(W_GRP)] * N_RWKV_OUT + [row(W_GRP)]
                 + [seq(W_GRP), seq(W_GRP), seq(HEAD_DIM), seq(HEAD_DIM), seq(2 * LANES),
                    pl.BlockSpec((1, LANES, TM), lambda i: (i // tps, 0, i % tps)),
                    seq(HEAD_DIM), seq(LANES), seq(LANES)])
    if apply_ln:
        out_shapes = [flat(d)] + out_shapes
        out_specs = [row(d)] + out_specs
    has_vfirst = v_first is not None
    assert len(conv_p) == N_CONV_P and len(gmlp_p) == N_GMLP_P
    assert len(rwkv_p) == N_RWKV_P + (N_VMIX_P if has_vfirst else 0)
    tab = pl.BlockSpec((TM, LANES), lambda i: (i % tps, 0))
    params = list(conv_p) + list(gmlp_p) + list(rwkv_p)
    inputs = [x2d, g, b, w_in, w_misc, cos_t, sin_t] + params + ([v_first] if has_vfirst else [])
    in_specs = ([row(d), const(g), const(b),
                 pl.BlockSpec((1,) + w_in.shape[1:], lambda i: (layer, 0, 0), pipeline_mode=pl.Buffered(1)),
                 const(w_misc), tab, tab] + [const(a) for a in params]
                + ([seq(W_GRP)] if has_vfirst else []))
    return pl.pallas_call(
        functools.partial(_inproj_body, apply_ln, has_vfirst, tps),
        grid=(n // TM,),
        in_specs=in_specs, out_specs=out_specs, out_shape=out_shapes,
        scratch_shapes=[pltpu.VMEM((d, IN_SPLITS[-1][1]), BF16),
                        pltpu.VMEM((TM + CONV_HALO, W_GRP), F32), pltpu.VMEM((TM + CONV_HALO, W_GRP), F32),
                        pltpu.VMEM((TM + SHIFT_HALO, 4 * W_GRP), F32), pltpu.VMEM((TM + SHIFT_HALO, LANES), F32)],
        compiler_params=_cparams("arbitrary"), name="in_proj",
    )(*inputs)


CONV_HALO = 32


def _carry_rows(buf, halo, first):
    tail = buf.shape[0] - halo

    @pl.when(first)
    def _():
        buf[0:halo, :] = jnp.zeros((halo, buf.shape[1]), buf.dtype)

    @pl.when(jnp.logical_not(first))
    def _():
        buf[0:halo, :] = buf[tail:tail + halo, :]


def _conv_math(cur, first, w_ref, b_ref, gg_ref, gb_ref, avg_ref, o_ref, hbuf, shifted):
    ts = cur.shape[0]
    h = cur[:, :W_GRP] * jax.nn.sigmoid(cur[:, W_GRP:])
    _carry_rows(hbuf, CONV_HALO, first)
    hbuf[CONV_HALO:, :] = h
    acc = jnp.zeros((ts, W_GRP), F32)
    base = CONV_HALO - (CONV_WIDTH - 1)
    for phase in range(SUB):
        taps = [j for j in range(CONV_WIDTH) if (base + j) % SUB == phase]
        span = (base + taps[-1]) // SUB * SUB + ts
        shifted[0:span, :] = hbuf[phase:phase + span, :]
        for j in taps:
            lo = (base + j) // SUB * SUB
            acc = acc + w_ref[j:j + 1, :] * shifted[lo:lo + ts, :]
    acc = acc + b_ref[...]
    avg = avg_ref[...].astype(BF16)
    mu = _dot_hilo(acc, avg)
    xc = acc - mu
    var = _dot_hilo(xc * xc, avg)
    y = xc * lax.rsqrt(var + LN_EPS) * gg_ref[...] + gb_ref[...]
    o_ref[...] = y * jax.nn.sigmoid(y)


SHIFT_HALO = 8


def _shift_prev(buf, cur, first):
    ts = cur.shape[0]
    _carry_rows(buf, SHIFT_HALO, first)
    buf[SHIFT_HALO:, :] = cur
    return buf[SHIFT_HALO - 1:SHIFT_HALO - 1 + ts, :]


def _rwkv_prep_math(cur, misc, first, vf_ref, params, outs, buf, mbuf):
    mu_ref, w0_ref, a0_ref, wup_ref, aup_ref, gup_ref, kk_ref, ka_ref, ones_ref = params[:9]
    r_o, lw_o, k_o, v_o, a_o, b_o, g_o = outs
    prev = _shift_prev(buf, cur, first)
    y = cur + mu_ref[...] * (prev - cur)
    r = y[:, 0:256]
    k = y[:, 256:512]
    v = y[:, 512:768]
    lora = y[:, 768:1024]
    w = w0_ref[...] + _dot3(jnp.tanh(lora), wup_ref[...])
    a = jax.nn.sigmoid(a0_ref[...] + _dot3(lora, aup_ref[...]))
    g = _dot3(jax.nn.sigmoid(lora), gup_ref[...])
    z = -w
    sp = jnp.maximum(z, 0.0) + jnp.log(1.0 + jnp.exp(-jnp.abs(z)))
    lw = -jnp.exp(-sp - 0.5)
    if vf_ref is not None:
        vmu_ref, v0_ref, vup_ref = params[9:]
        mprev = _shift_prev(mbuf, misc, first)
        vd = misc + vmu_ref[...] * (mprev - misc)
        v_mix = jax.nn.sigmoid(v0_ref[...] + _dot3(vd, vup_ref[...]))
        v = v + (vf_ref[0] - v) * v_mix
    kk = k * kk_ref[...]
    n2 = _dot_hilo(kk * kk, ones_ref[...].astype(BF16))
    kk = kk / jnp.maximum(jnp.sqrt(n2), 1e-12)
    k2 = k * (1.0 + (a - 1.0) * ka_ref[...])
    r_o[0] = r
    lw_o[0] = lw
    k_o[0] = k2
    v_o[0] = v
    a_o[0] = -kk
    b_o[0] = kk * a
    g_o[0] = g


def _block_diag(x, headmask):
    return jnp.concatenate([x] * N_HEADS, axis=0) * headmask


def _wkv_body(ts, r_ref, lw_ref, k_ref, v_ref, a_ref, b_ref, g_ref, rk_ref, gg_ref, gb_ref,
              avg_ref, ones_ref, o_ref, st_ref):
    C = WKV_CHUNK
    n = N_HEADS * C

    @pl.when(pl.program_id(1) == 0)
    def _():
        st_ref[...] = jnp.zeros_like(st_ref)

    ri = lax.broadcasted_iota(jnp.int32, (n, n), 0)
    ci = lax.broadcasted_iota(jnp.int32, (n, n), 1)
    head_f = jnp.where((ri // C) == (ci // HEAD_DIM), 1.0, 0.0)
    head_b = head_f.astype(BF16)
    lag = jnp.where((ri // C) == (ci // C), (ri % C) - (ci % C), -1)
    strict = lag > 0
    incl = lag >= 0
    eye = ri == ci
    eye_f = jnp.where(eye, 1.0, 0.0)
    tr = lax.broadcasted_iota(jnp.int32, (C, C), 0)
    tc = lax.broadcasted_iota(jnp.int32, (C, C), 1)
    tri = jnp.where(tc <= tr, 1.0, 0.0).astype(F32)
    cast = lambda x: x.astype(BF16)
    G = ts // C
    split = lambda ref: ref[0].reshape(G, C, W_GRP)
    tile_heads = lambda x: jnp.concatenate([x] * N_HEADS, axis=1)
    bd16 = lambda x: tile_heads(cast(x)) * head_b[None]
    bmm = lambda x, y: lax.dot_general(x, y, (((2,), (1,)), ((0,), (0,))), preferred_element_type=F32)
    bmm_nt = lambda x, y: lax.dot_general(x, y, (((2,), (2,)), ((0,), (0,))), preferred_element_type=F32)

    lw = split(lw_ref)
    cum = lax.dot_general(jnp.broadcast_to(tri[None], (G, C, C)), lw, (((2,), (1,)), ((0,), (0,))),
                          preferred_element_type=F32, precision=HI)
    cum_c = cum[:, C - 1:C, :]
    e_in = jnp.exp(cum)
    e_neg = jnp.exp(-cum)
    e_tail = jnp.exp(cum_c - cum)
    b_c = split(b_ref)
    k_c = split(k_ref)
    a_t = bd16(split(a_ref) * jnp.exp(cum - lw))
    r_t = bd16(split(r_ref) * e_in)
    b_t = bd16(b_c * e_neg)
    k_t = bd16(k_c * e_neg)
    v_bd = bd16(split(v_ref))
    bh_t = cast(jnp.swapaxes(tile_heads(b_c * e_tail) * head_f[None], 1, 2))
    kh_t = cast(jnp.swapaxes(tile_heads(k_c * e_tail) * head_f[None], 1, 2))
    a_ab = jnp.where(strict[None], bmm_nt(a_t, b_t), 0.0)
    a_ak = cast(jnp.where(strict[None], bmm_nt(a_t, k_t), 0.0))
    a_rb = cast(jnp.where(incl[None], bmm_nt(r_t, b_t), 0.0))
    a_rk = cast(jnp.where(incl[None], bmm_nt(r_t, k_t), 0.0))
    t_inv = eye_f[None] + a_ab
    pw = cast(a_ab)
    for _ in range(int(math.log2(C)) - 1):
        pw = cast(bmm(pw, pw))
        t_inv = t_inv + bmm(cast(t_inv), pw)
    t16 = cast(t_inv)
    ta = cast(bmm(t16, a_t))
    u0 = bmm(cast(bmm(t16, a_ak)), v_bd)
    o0 = bmm(a_rk, v_bd)
    s0 = bmm(kh_t, v_bd)
    o_lhs = jnp.concatenate([r_t, a_rb], axis=2)
    w_col = jnp.sum(jnp.where(eye[None], jnp.exp(cum_c), 0.0), axis=2, keepdims=True)

    st = st_ref[...]
    outs = []
    for g in range(G):
        st16 = cast(st)
        u = cast(_dot(ta[g], st16) + u0[g])
        o_bd = _dot(o_lhs[g], jnp.concatenate([st16, u], axis=0)) + o0[g]
        st = w_col[g] * st + _dot(bh_t[g], u) + s0[g]
        outs.append(o_bd[0:C] + o_bd[C:2 * C] + o_bd[2 * C:3 * C] + o_bd[3 * C:4 * C])
    st_ref[...] = st

    o = jnp.concatenate(outs, axis=0)
    avg = avg_ref[...].astype(BF16)
    mu = _dot_hilo(o, avg)
    xc = o - mu
    var = _dot_hilo(xc * xc, avg)
    on = xc * lax.rsqrt(var + RWKV_GN_EPS) * gg_ref[...] + gb_ref[...]
    r = r_ref[0]
    k = k_ref[0]
    v = v_ref[0]
    bonus = _dot_hilo(r * k * rk_ref[...], ones_ref[...].astype(BF16)) * v
    o_ref[0] = (on + bonus) * g_ref[0]


def wkv_scan(r, lw, k, v, a, b, g, rk, gg, gb, avg, ones):
    bsz, s, _ = r.shape
    ts = WKV_GROUP * WKV_CHUNK
    tile = pl.BlockSpec((1, ts, W_GRP), lambda bi, i: (bi, i, 0))
    const = lambda arr: pl.BlockSpec(arr.shape, lambda bi, i: (0,) * arr.ndim)
    return pl.pallas_call(
        functools.partial(_wkv_body, ts),
        grid=(bsz, s // ts),
        in_specs=[tile] * 7 + [const(x) for x in (rk, gg, gb, avg, ones)],
        out_specs=tile, out_shape=jax.ShapeDtypeStruct((bsz, s, W_GRP), F32),
        scratch_shapes=[pltpu.VMEM((N_HEADS * WKV_CHUNK, W_GRP), F32)],
        compiler_params=_cparams("parallel", "arbitrary"), name="wkv_scan",
    )(r, lw, k, v, a, b, g, rk, gg, gb, avg, ones)


def _gmlp_math(x, g_ref, b_ref, ws_ref, bs_ref, o_ref):
    ts = x.shape[0]
    u = jax.nn.gelu(x[:, :W_GRP])
    v = _ln(jax.nn.gelu(x[:, W_GRP:]), g_ref[...], b_ref[...]).astype(BF16)
    tr = lax.broadcasted_iota(jnp.int32, (GMLP_CHUNK, GMLP_CHUNK), 0)
    tc = lax.broadcasted_iota(jnp.int32, (GMLP_CHUNK, GMLP_CHUNK), 1)
    ws = [jnp.where(tc <= tr, ws_ref[h], 0.0).astype(BF16) for h in range(N_HEADS)]
    for c in range(ts // GMLP_CHUNK):
        rows = slice(c * GMLP_CHUNK, (c + 1) * GMLP_CHUNK)
        mixed = [_dot(ws[h], v[rows, h * HEAD_DIM:(h + 1) * HEAD_DIM]) + bs_ref[h]
                 for h in range(N_HEADS)]
        o_ref[rows, :] = u[rows, :] * jnp.concatenate(mixed, axis=-1)


def _swap_halves(x, lane):
    w = x.shape[-1]
    half = HEAD_DIM // 2
    fwd = pltpu.roll(x, w - half, 1)
    bwd = pltpu.roll(x, half, 1)
    return jnp.where((lane % HEAD_DIM) < half, fwd, bwd)


def _nsa_prep_math(q, kv, misc, pos0, cos_ref, sin_ref,
                   q_o, qr_o, kc_o, vc_o, ks_o, vs_o, kw_o, vw_o, g_o):
    scale = HEAD_DIM ** -0.5
    cos = jnp.concatenate([cos_ref[...]] * 3, axis=1)
    sin = jnp.concatenate([sin_ref[...]] * 3, axis=1)
    lane_q = lax.broadcasted_iota(jnp.int32, q.shape, 1)
    q_rot = q * cos[:, :W_GRP] + _swap_halves(q, lane_q) * sin[:, :W_GRP]
    lane_kv = lax.broadcasted_iota(jnp.int32, kv.shape, 1)
    kv_rot = kv * cos + _swap_halves(kv, lane_kv) * sin
    q_o[0] = (q * scale).astype(BF16)
    qr_o[0] = (q_rot * (scale * LOG2E)).astype(BF16)
    kc_o[0] = kv[:, 0:64]
    vc_o[0] = kv[:, 64:128]
    ts = kv.shape[0]
    lane = lax.broadcasted_iota(jnp.int32, (ts, LANES), 1)
    key_blk = (pos0 + lax.broadcasted_iota(jnp.int32, (ts, 1), 0)) // SEL_LEN
    ks_o[0] = jnp.concatenate([jnp.where(lane < HEAD_DIM, kv_rot[:, 128:256], 0.0),
                               jnp.where(lane == key_blk, NEG, 0.0)], axis=1).astype(BF16)
    ones_col = jnp.where(lane == HEAD_DIM, 1.0, 0.0)
    vs_first = pltpu.roll(kv[:, 128:256], HEAD_DIM, 1)
    vs_o[0] = jnp.where(lane < HEAD_DIM, vs_first, ones_col).T.astype(BF16)
    kw_o[0] = kv_rot[:, 256:320].astype(BF16)
    vw_first = pltpu.roll(kv[:, 256:384], HEAD_DIM, 1)
    vw_o[0] = jnp.where(lane < HEAD_DIM, vw_first, ones_col).astype(BF16)
    g_o[0] = jax.nn.sigmoid(misc)


def _compress_body(kc_ref, vc_ref, pe_ref, w1_ref, w2_ref, ko_ref, vo_ref):
    half = CMP_STRIDE * HEAD_DIM
    for j, (c_ref, o_ref) in enumerate(((kc_ref, ko_ref), (vc_ref, vo_ref))):
        c = c_ref[0].astype(BF16)
        w1 = w1_ref[j]
        lo = _dot(c, w1[:half])
        hi = _dot(c, w1[half:])
        nb = hi.shape[0]
        hi_next = pltpu.roll(hi, nb - 1, 0)
        pe = jnp.broadcast_to(pe_ref[j], (8, 2 * half))
        pe_term = _dot(pe, w1.astype(F32), HI)[0:1]
        h = jax.nn.gelu(lo + hi_next + pe_term)
        o_ref[0] = _dot(h.astype(BF16), w2_ref[j]).astype(BF16)


def nsa_compress(kc_r, vc_r, pe, w1, w2):
    bsz, nb, w = kc_r.shape
    blk = pl.BlockSpec((1, nb, w), lambda bi: (bi, 0, 0))
    const = lambda a: pl.BlockSpec(a.shape, lambda bi: (0,) * a.ndim)
    out = jax.ShapeDtypeStruct((bsz, nb, HEAD_DIM), BF16)
    ospec = pl.BlockSpec((1, nb, HEAD_DIM), lambda bi: (bi, 0, 0))
    return pl.pallas_call(
        _compress_body, grid=(bsz,),
        in_specs=[blk, blk, const(pe), const(w1), const(w2)],
        out_specs=[ospec, ospec], out_shape=[out, out],
        compiler_params=_cparams("parallel"), name="nsa_compress",
    )(kc_r, vc_r, pe, w1, w2)


def _stack_heads(x):
    return jnp.concatenate([x[:, h * HEAD_DIM:(h + 1) * HEAD_DIM] for h in range(N_HEADS)], axis=0)


def _nsa_attn_body(n_top, q_ref, qr_ref, kc_ref, vc_ref, ks_ref, vst_ref, kw_ref, vw_ref, g_ref,
                   ov_ref, o_ref):
    T = Q_BLOCK
    i = pl.program_id(1)
    t0 = i * T
    qs = _stack_heads(q_ref[0])
    qrs = _stack_heads(qr_ref[0])
    pos = t0 + lax.broadcasted_iota(jnp.int32, (T, 1), 0)
    tile4 = lambda x: jnp.concatenate([x] * N_HEADS, axis=0)

    kc = kc_ref[0]
    n_cmp = kc.shape[0]
    cmp_end = lax.broadcasted_iota(jnp.int32, (1, n_cmp), 1) * CMP_STRIDE + (CMP_LEN - 1)
    valid_c = tile4(jnp.where(cmp_end <= pos, 1.0, 0.0))
    s_c = jnp.where(valid_c > 0.5, _dot_nt(qs, kc), NEG)
    p_c = jnp.exp(s_c - jnp.max(s_c, axis=-1, keepdims=True)) * valid_c
    p_c = p_c / jnp.maximum(jnp.sum(p_c, axis=-1, keepdims=True), 1e-20)
    o_c = _dot(p_c.astype(BF16), vc_ref[0])
    p_sum = p_c[0:T] + p_c[T:2 * T] + p_c[2 * T:3 * T] + p_c[3 * T:4 * T]
    ps_hi, ps_lo = _split_bf16(p_sum)
    ov = ov_ref[...].astype(BF16)
    imp_t = _dot_nt(ov, ps_hi) + _dot_nt(ov, ps_lo)

    WK = WINDOW + T
    start = pl.multiple_of(jnp.maximum(t0 - WINDOW, 0), T)
    kw = kw_ref[0, pl.ds(start, WK), :]
    vw = vw_ref[0, pl.ds(start, WK), :]
    kpos_w = start + lax.broadcasted_iota(jnp.int32, (1, WK), 1)
    bias_w = tile4(jnp.where(kpos_w <= pos, jnp.where(kpos_w > pos - WINDOW, 0.0, NEG), NEG))
    s_w = _dot_nt(qrs, kw) + bias_w
    p_w = jnp.exp2(s_w - jnp.max(s_w, axis=-1, keepdims=True))
    acc_w = _dot(p_w.astype(BF16), vw)
    o_w = acc_w[:, :HEAD_DIM] / acc_w[:, HEAD_DIM:HEAD_DIM + 1]

    n_sel = imp_t.shape[0]
    jj = lax.broadcasted_iota(jnp.int32, (n_sel, T), 0)
    blk = (t0 + lax.broadcasted_iota(jnp.int32, (1, T), 1)) // SEL_LEN
    val = jnp.where(jj == blk, 3e38, jnp.where(jj == 0, 3e38, jnp.where(jj <= blk, imp_t, -1.0)))
    sel_t = jnp.zeros((n_sel, T), F32)
    for _ in range(n_top):
        mx = jnp.max(val, axis=0, keepdims=True)
        idx = jnp.min(jnp.where(val == mx, jj, n_sel), axis=0, keepdims=True)
        hit = jj == idx
        sel_t = jnp.where(hit, 1.0, sel_t)
        val = jnp.where(hit, -2.0, val)
    sel = sel_t.T

    KT = SEL_KT
    unsel = tile4((1.0 - sel).astype(BF16))
    q_aug = jnp.concatenate([qrs, jnp.zeros((N_HEADS * T, HEAD_DIM), BF16), unsel], axis=1)
    krow = lax.broadcasted_iota(jnp.int32, (KT, 1), 0)
    pos_row = t0 + lax.broadcasted_iota(jnp.int32, (1, T), 1)

    KS = KT // SEL_SUB

    def sel_tile(jt, carry, diagonal):
        m, acc = carry
        k0 = pl.multiple_of(jt * KT, KT)

        def scores(j):
            ks = pl.multiple_of(k0 + j * KS, KS)
            s_t = _dot_nt(ks_ref[0, pl.ds(ks, KS), :], q_aug)
            if diagonal:
                causal = jnp.where(ks + krow[:KS] <= pos_row, 0.0, NEG)
                s_t = s_t + jnp.concatenate([causal] * N_HEADS, axis=1)
            return s_t

        s_next = scores(0)
        for j in range(SEL_SUB):
            s_t = s_next
            if j + 1 < SEL_SUB:
                s_next = scores(j + 1)
            m_new = jnp.maximum(m, jnp.max(s_t, axis=0, keepdims=True))
            p_t = jnp.exp2(s_t - m_new)
            ks = pl.multiple_of(k0 + j * KS, KS)
            acc = jnp.exp2(m - m_new) * acc + _dot(vst_ref[0, :, pl.ds(ks, KS)], p_t.astype(BF16))
            m = m_new
        return m, acc

    init = (jnp.full((1, N_HEADS * T), NEG, F32), jnp.zeros((LANES, N_HEADS * T), F32))
    n_full = t0 // KT
    carry = lax.fori_loop(0, n_full, functools.partial(sel_tile, diagonal=False), init)
    _, acc_t = sel_tile(n_full, carry, True)
    acc_s = acc_t.T
    o_s = acc_s[:, :HEAD_DIM] / acc_s[:, HEAD_DIM:HEAD_DIM + 1]

    g = g_ref[0]
    outs = []
    for h in range(N_HEADS):
        rows = slice(h * T, (h + 1) * T)
        outs.append(g[:, 3 * h:3 * h + 1] * o_c[rows] + g[:, 3 * h + 1:3 * h + 2] * o_s[rows]
                    + g[:, 3 * h + 2:3 * h + 3] * o_w[rows])
    o_ref[0] = jnp.concatenate(outs, axis=-1)


def nsa_attention(q, qr, k_cmp, v_cmp, ks, vs, kw, vw, gates, overlap):
    bsz, s, _ = q.shape
    n_top = min(SEL_TOPN, s // SEL_LEN)
    n_cmp = k_cmp.shape[1]
    qtile = lambda w: pl.BlockSpec((1, Q_BLOCK, w), lambda bi, i: (bi, i, 0))
    full = lambda rows, w: pl.BlockSpec((1, rows, w), lambda bi, i: (bi, 0, 0))
    return pl.pallas_call(
        functools.partial(_nsa_attn_body, n_top),
        grid=(bsz, s // Q_BLOCK),
        in_specs=[qtile(W_GRP), qtile(W_GRP), full(n_cmp, HEAD_DIM), full(n_cmp, HEAD_DIM),
                  full(s, 2 * LANES), full(LANES, s), full(s, HEAD_DIM), full(s, LANES),
                  qtile(LANES), pl.BlockSpec(overlap.shape, lambda bi, i: (0, 0))],
        out_specs=qtile(W_GRP), out_shape=jax.ShapeDtypeStruct((bsz, s, W_GRP), F32),
        compiler_params=_cparams("parallel", "parallel"), name="nsa_attention",
    )(q, qr, k_cmp, v_cmp, ks, vs, kw, vw, gates, overlap)


def _outproj_math(alpha, parts, x, w_ref, g_ref, b_ref):
    y = alpha * x
    for j, part in enumerate(parts):
        y = y + _dot(part[...].astype(BF16), w_ref[j * W_GRP:(j + 1) * W_GRP, :])
    return _ln(y, g_ref[...], b_ref[...])


def _memkv_body(m_ref, wk_ref, wv_ref, k_o, v_o):
    mb = m_ref[...].astype(BF16)
    k_o[...] = _dot(mb, wk_ref[...]).astype(BF16)
    v_o[...] = _dot(mb, wv_ref[...]).astype(BF16)


def mem_kv(mem2d, wk, wv):
    n, d = mem2d.shape
    full = lambda a: pl.BlockSpec(a.shape, lambda i: (0, 0))
    out = jax.ShapeDtypeStruct((n, d), BF16)
    return pl.pallas_call(
        _memkv_body, grid=(1,),
        in_specs=[full(mem2d), full(wk), full(wv)],
        out_specs=[pl.BlockSpec((n, d), lambda i: (0, 0))] * 2, out_shape=[out, out],
        compiler_params=_cparams("arbitrary"), name="mem_kv",
    )(mem2d, wk, wv)


def _xattn_body(alpha, a_ref, b_ref, c_ref, d_ref, x_ref, w1_ref, g1_ref, b1_ref,
                k_ref, v_ref, wq_ref, wo_ref, g_ref, b_ref_, o_ref):
    x = _outproj_math(alpha, (a_ref, b_ref, c_ref, d_ref), x_ref[...], w1_ref, g1_ref, b1_ref)
    d = x.shape[-1]
    hd = d // N_MEM_HEADS
    q = (_dot(x.astype(BF16), wq_ref[...]) * (hd ** -0.5 * LOG2E)).astype(BF16)
    k = k_ref[0]
    v = v_ref[0]
    cols = [slice(h * hd, (h + 1) * hd) for h in range(N_MEM_HEADS)]
    scores = [_dot_nt(q[:, cs], k[:, cs]) for cs in cols]
    probs = []
    for s in scores:
        p = jnp.exp2(s - jnp.max(s, axis=-1, keepdims=True))
        probs.append((p / jnp.sum(p, axis=-1, keepdims=True)).astype(BF16))
    y = alpha * x
    for cs, p in zip(cols, probs):
        y = y + _dot(_dot(p, v[:, cs]).astype(BF16), wo_ref[cs, :])
    o_ref[...] = _ln(y, g_ref[...], b_ref_[...])


def cross_attention(parts, x2d, seq_len, w_out, g1, b1, k, v, wq, wo, g, b, alpha):
    n, d = x2d.shape
    bsz, s = n // seq_len, seq_len
    tps = s // TM
    m = k.shape[1]
    const = lambda a: pl.BlockSpec(a.shape, lambda bi, i: (0,) * a.ndim)
    row = lambda w: pl.BlockSpec((TM, w), lambda bi, i: (bi * tps + i, 0))
    return pl.pallas_call(
        functools.partial(_xattn_body, alpha),
        grid=(bsz, s // TM),
        in_specs=[row(W_GRP)] * 4 + [row(d), const(w_out), const(g1), const(b1),
                  pl.BlockSpec((1, m, d), lambda bi, i: (bi, 0, 0)),
                  pl.BlockSpec((1, m, d), lambda bi, i: (bi, 0, 0)),
                  const(wq), const(wo), const(g), const(b)],
        out_specs=row(d), out_shape=jax.ShapeDtypeStruct((n, d), F32),
        compiler_params=_cparams("parallel", "parallel"), name="cross_attention",
    )(*parts, x2d, w_out, g1, b1, k, v, wq, wo, g, b)


PER_GRP = N_EXPERTS // N_EXPERT_GROUPS
LPOS_LANE = PER_GRP
SUB = 8


def _router_body(x_ref, rwt_ref, rb_ref, upper_ref, tok_o, lrow_o, cnt_o):
    tm = x_ref.shape[0]
    logits = _dot_nt(rwt_ref[...], x_ref[...], HI)
    ex = jnp.exp(logits - jnp.max(logits, axis=0, keepdims=True))
    probs = ex / jnp.sum(ex, axis=0, keepdims=True)
    sel = probs + rb_ref[...]
    srow = [sel[e:e + 1] for e in range(N_EXPERTS)]
    prow = [probs[e:e + 1] for e in range(N_EXPERTS)]
    gscore = []
    for g in range(N_EXPERT_GROUPS):
        r = srow[g * PER_GRP:(g + 1) * PER_GRP]
        best = None
        for a in range(PER_GRP):
            for b in range(a + 1, PER_GRP):
                best = r[a] + r[b] if best is None else jnp.maximum(best, r[a] + r[b])
        gscore.append(best)
    g_idx = jnp.zeros((1, tm), jnp.int32)
    top = gscore[0]
    for g in range(1, N_EXPERT_GROUPS):
        better = gscore[g] > top
        g_idx = jnp.where(better, g, g_idx)
        top = jnp.where(better, gscore[g], top)

    def of_group(rows, e):
        out = rows[e]
        for g in range(1, N_EXPERT_GROUPS):
            out = jnp.where(g_idx == g, rows[g * PER_GRP + e], out)
        return out

    sg = [of_group(srow, e) for e in range(PER_GRP)]
    pg = [of_group(prow, e) for e in range(PER_GRP)]
    w = []
    for e in range(PER_GRP):
        rank = jnp.zeros((1, tm), F32)
        for o in range(PER_GRP):
            if o != e:
                ahead = (sg[o] >= sg[e]) if o < e else (sg[o] > sg[e])
                rank = rank + jnp.where(ahead, 1.0, 0.0)
        w.append(jnp.where(rank < 1.5, pg[e], 0.0))
    w_sum = w[0] + w[1] + w[2] + w[3]
    row8 = lax.broadcasted_iota(jnp.int32, (SUB, tm), 0)
    onehot = jnp.where(row8 == g_idx, 1.0, 0.0)
    before = _dot(onehot.astype(BF16), upper_ref[...])
    cnt = jnp.sum(onehot, axis=1, keepdims=True)
    offs = [jnp.zeros((1, 1), F32)]
    for g in range(1, N_EXPERT_GROUPS):
        offs.append(offs[-1] + cnt[g - 1:g])
    lpos = jnp.zeros((1, tm), F32)
    for g in range(N_EXPERT_GROUPS):
        lpos = lpos + onehot[g:g + 1] * (offs[g] + before[g:g + 1])
    tok = jnp.zeros((SUB, tm), F32)
    for e in range(PER_GRP):
        tok = jnp.where(row8 == e, w[e] / w_sum, tok)
    tok = jnp.where(row8 == LPOS_LANE, lpos, tok)
    tok_o[...] = jnp.concatenate([tok, jnp.zeros((LANES - SUB, tm), F32)], axis=0).T
    lrow_o[0] = lpos.astype(jnp.int32)
    rowc = lax.broadcasted_iota(jnp.int32, (SUB, LANES), 0)
    stats = jnp.zeros((SUB, LANES), F32)
    for g in range(N_EXPERT_GROUPS):
        stats = jnp.where(rowc == g, cnt[g:g + 1], stats)
        stats = jnp.where(rowc == N_EXPERT_GROUPS + g, offs[g], stats)
    cnt_o[0] = stats.astype(jnp.int32)


def moe_router(x2d, rwt, rb_col, upper):
    n, d = x2d.shape
    tm = upper.shape[0]
    nt = n // tm
    const = lambda a: pl.BlockSpec(a.shape, lambda i: (0, 0))
    return pl.pallas_call(
        _router_body, grid=(nt,),
        in_specs=[pl.BlockSpec((tm, d), lambda i: (i, 0)), const(rwt), const(rb_col), const(upper)],
        out_specs=[pl.BlockSpec((tm, LANES), lambda i: (i, 0)),
                   pl.BlockSpec((1, 1, tm), lambda i: (i, 0, 0)),
                   pl.BlockSpec((1, SUB, LANES), lambda i: (i, 0, 0))],
        out_shape=[jax.ShapeDtypeStruct((n, LANES), F32), jax.ShapeDtypeStruct((nt, 1, tm), jnp.int32),
                   jax.ShapeDtypeStruct((nt, SUB, LANES), jnp.int32)],
        compiler_params=_cparams("parallel"), name="moe_router",
    )(x2d, rwt, rb_col, upper)


def _split_bf16(x):
    hi = x.astype(BF16)
    return hi, (x - hi.astype(F32)).astype(BF16)


def _moe_body(alpha, offs_ref, cnts_ref, x_ref, tok_ref, lrow_ref, wg_ref, wu_ref, wd_ref, g_ref, b_ref, o_ref,
              xs_ref, gs_ref, acc_ref):
    i = pl.program_id(0)
    step = pl.program_id(1)
    tm = x_ref.shape[0]
    eps = wd_ref.shape[1]

    @pl.when(step == 0)
    def _():
        slot = lax.broadcasted_iota(jnp.int32, (tm, tm), 0)
        perm = jnp.where(slot == lrow_ref[0], 1.0, 0.0).astype(BF16)
        xs_ref[...] = _dot(perm, x_ref[...].astype(BF16)).astype(BF16)
        t_hi, t_lo = _split_bf16(tok_ref[...])
        gs_ref[...] = _dot(perm, t_hi) + _dot(perm, t_lo)
        acc_ref[...] = jnp.zeros_like(acc_ref)

    grp = (step * eps) // PER_GRP
    off = offs_ref[i * N_EXPERT_GROUPS + grp]
    end = off + cnts_ref[i * N_EXPERT_GROUPS + grp]
    lane = lax.broadcasted_iota(jnp.int32, (MOE_BLK, LANES), 1)
    for s in range(tm // MOE_BLK):
        lo = s * MOE_BLK

        @pl.when(jnp.logical_and(off < lo + MOE_BLK, end > lo))
        def _(lo=lo):
            rows = slice(lo, lo + MOE_BLK)
            xs = xs_ref[rows, :]
            ridx = lo + lax.broadcasted_iota(jnp.int32, (MOE_BLK, 1), 0)
            in_run = jnp.where(ridx >= off, jnp.where(ridx < end, 1.0, 0.0), 0.0)
            gs = gs_ref[rows, :]
            y = jnp.zeros((MOE_BLK, x_ref.shape[1]), F32)
            for k in range(eps):
                gt = _dot(xs, wg_ref[0, k])
                h = (gt * jax.nn.sigmoid(gt) * _dot(xs, wu_ref[0, k])).astype(BF16)
                in_grp = (step * eps + k) % PER_GRP
                gcol = jnp.sum(jnp.where(lane == in_grp, gs, 0.0), axis=-1, keepdims=True)
                y = y + (gcol * in_run) * _dot(h, wd_ref[0, k])
            acc_ref[rows, :] += y

    @pl.when(step == pl.num_programs(1) - 1)
    def _():
        lcol = tok_ref[:, LPOS_LANE:LPOS_LANE + 1].astype(jnp.int32)
        slot = lax.broadcasted_iota(jnp.int32, (tm, tm), 1)
        unperm = jnp.where(slot == lcol, 1.0, 0.0).astype(BF16)
        y = _dot(unperm, acc_ref[...].astype(BF16))
        o_ref[...] = _ln(alpha * x_ref[...] + y, g_ref[...], b_ref[...])


def moe_ffn(x2d, tok, lrow, offs, cnts, layer, wg, wu, wd, g, b, alpha):
    n, d = x2d.shape
    _, ne, de, _ = wd.shape
    tm = lrow.shape[2]
    const = lambda a: pl.BlockSpec(a.shape, lambda i, e, o, c: (0,) * a.ndim)
    grid_spec = pltpu.PrefetchScalarGridSpec(
        num_scalar_prefetch=2, grid=(n // tm, ne // MOE_EPS),
        in_specs=[pl.BlockSpec((tm, d), lambda i, e, o, c: (i, 0), pipeline_mode=pl.Buffered(1)),
                  pl.BlockSpec((tm, LANES), lambda i, e, o, c: (i, 0)),
                  pl.BlockSpec((1, 1, tm), lambda i, e, o, c: (i, 0, 0)),
                  pl.BlockSpec((1, MOE_EPS, d, de), lambda i, e, o, c: (layer, e, 0, 0)),
                  pl.BlockSpec((1, MOE_EPS, d, de), lambda i, e, o, c: (layer, e, 0, 0)),
                  pl.BlockSpec((1, MOE_EPS, de, d), lambda i, e, o, c: (layer, e, 0, 0)),
                  const(g), const(b)],
        out_specs=pl.BlockSpec((tm, d), lambda i, e, o, c: (i, 0)),
        scratch_shapes=[pltpu.VMEM((tm, d), BF16), pltpu.VMEM((tm, LANES), F32), pltpu.VMEM((tm, d), F32)])
    return pl.pallas_call(
        functools.partial(_moe_body, alpha), grid_spec=grid_spec,
        out_shape=jax.ShapeDtypeStruct((n, d), F32),
        compiler_params=_cparams("parallel", "arbitrary"), name="moe_ffn",
    )(offs, cnts, x2d, tok, lrow, wg, wu, wd, g, b)


def _rope_tables(s):
    inv = ROPE_THETA ** (-jnp.arange(0, HEAD_DIM, 2, dtype=F32) / HEAD_DIM)
    ang = jnp.arange(s, dtype=F32)[:, None] * inv[None, :]
    cos, sin = jnp.cos(ang), jnp.sin(ang)
    cos_h = jnp.concatenate([cos, cos], axis=-1)
    sin_h = jnp.concatenate([-sin, sin], axis=-1)
    return jnp.tile(cos_h, (1, 2)), jnp.tile(sin_h, (1, 2))


def _overlap_matrix(s, n_cmp_pad):
    n_sel = s // SEL_LEN
    assert n_sel <= LANES
    cmp_start = jnp.arange(n_cmp_pad) * CMP_STRIDE
    sel_start = jnp.arange(LANES) * SEL_LEN
    ov = jnp.clip(jnp.minimum(cmp_start[None, :] + CMP_LEN, sel_start[:, None] + SEL_LEN)
                  - jnp.maximum(cmp_start[None, :], sel_start[:, None]), 0, None).astype(F32) / CMP_LEN
    n_cmp = s // CMP_STRIDE - (CMP_LEN // CMP_STRIDE - 1)
    real = (jnp.arange(n_cmp_pad)[None, :] < n_cmp) & (jnp.arange(LANES)[:, None] < n_sel)
    return jnp.where(real, ov, 0.0)


def _pad_rows(w, lo, total):
    out = jnp.zeros((total, w.shape[1]), w.dtype)
    return out.at[lo:lo + w.shape[0]].set(w)


def kernel(x, mem, ln_in_g, ln_in_b, w_in, w_out, conv_w, conv_b, conv_gn_g, conv_gn_b, rwkv_mu, rwkv_w0, rwkv_w_up, rwkv_a0, rwkv_a_up, rwkv_g_up, rwkv_k_k, rwkv_k_a, rwkv_r_k, rwkv_gn_g, rwkv_gn_b, rwkv_v_down, rwkv_v_mu, rwkv_v0, rwkv_v_up, gmlp_ln_g, gmlp_ln_b, gmlp_w_s, gmlp_b_s, nsa_pe_k, nsa_w1_k, nsa_w2_k, nsa_pe_v, nsa_w1_v, nsa_w2_v, ln1_g, ln1_b, xa_wq, xa_wk, xa_wv, xa_wo, ln2_g, ln2_b, router_w, router_bias, moe_w_gate, moe_w_up, moe_w_down, ln3_g, ln3_b):
    bsz, s, d = x.shape
    depth = w_in.shape[0]
    n = bsz * s
    alpha = (2 * depth) ** 0.25
    row = lambda a: a.reshape(1, -1)

    cos_t, sin_t = _rope_tables(s)
    n_blk = s // CMP_STRIDE
    overlap = _overlap_matrix(s, n_blk)
    avg64 = _group_avg_matrix(W_GRP, HEAD_DIM)
    ones64 = avg64 * HEAD_DIM
    moe_tm = min(MOE_TM, n)
    t_idx = jnp.arange(moe_tm)
    upper = (t_idx[:, None] < t_idx[None, :]).astype(BF16)
    mem2d = mem.reshape(bsz * mem.shape[1], d)

    wg_b, wu_b, wd_b = moe_w_gate.astype(BF16), moe_w_up.astype(BF16), moe_w_down.astype(BF16)

    xs = x.reshape(n, d)
    v_first = None
    for l in range(depth):
        main = IN_SPLITS[-1][1]
        w_misc = jnp.zeros((d, LANES), F32).at[:, :w_in.shape[2] - main].set(w_in[l, :, main:])
        if l > 0:
            w_misc = w_misc.at[:, MISC_VD_OFF:MISC_VD_OFF + rwkv_v_down.shape[2]].set(rwkv_v_down[l - 1])
        gmlp_p = (row(gmlp_ln_g[l]), row(gmlp_ln_b[l]), gmlp_w_s[l], gmlp_b_s[l].reshape(N_HEADS, GMLP_CHUNK, 1))
        conv_p = (conv_w[l], row(conv_b[l]), row(conv_gn_g[l]), row(conv_gn_b[l]), avg64)
        rwkv_p = (row(rwkv_mu[l]), row(rwkv_w0[l]), row(rwkv_a0[l]), _pad_rows(rwkv_w_up[l], 0, W_GRP),
                  _pad_rows(rwkv_a_up[l], 64, W_GRP), _pad_rows(rwkv_g_up[l], 128, W_GRP),
                  row(rwkv_k_k[l]), row(rwkv_k_a[l]), ones64)
        if l > 0:
            vmu = jnp.zeros((1, LANES), F32).at[0, MISC_VD_OFF:MISC_VD_OFF + rwkv_v_mu.shape[1]].set(rwkv_v_mu[l - 1])
            rwkv_p += (vmu, row(rwkv_v0[l - 1]), _pad_rows(rwkv_v_up[l - 1], MISC_VD_OFF, LANES))
        outs = in_proj(xs, s, row(ln_in_g), row(ln_in_b), w_in, l, w_misc.astype(BF16), cos_t, sin_t,
                       conv_p, gmlp_p, rwkv_p, v_first, apply_ln=(l == 0))
        if l == 0:
            xs, outs = outs[0], outs[1:]
        out_a = outs[0]
        r_, lw_, k_, v_, a_, b_, g_ = outs[1:1 + N_RWKV_OUT]
        out_c = outs[1 + N_RWKV_OUT]
        q_b, qr_b, kc, vc, ks, vs, kw, vw, gates = outs[2 + N_RWKV_OUT:]
        if l == 0:
            v_first = v_
        out_b = wkv_scan(r_, lw_, k_, v_, a_, b_, g_, row(rwkv_r_k[l]), row(rwkv_gn_g[l]),
                         row(rwkv_gn_b[l]), avg64, ones64)

        pe = jnp.stack([nsa_pe_k[l].reshape(1, -1), nsa_pe_v[l].reshape(1, -1)])
        w1 = jnp.stack([nsa_w1_k[l], nsa_w1_v[l]]).astype(BF16)
        w2 = jnp.stack([nsa_w2_k[l], nsa_w2_v[l]]).astype(BF16)
        k_cmp, v_cmp = nsa_compress(kc.reshape(bsz, n_blk, CMP_STRIDE * HEAD_DIM),
                                    vc.reshape(bsz, n_blk, CMP_STRIDE * HEAD_DIM), pe, w1, w2)
        out_d = nsa_attention(q_b, qr_b, k_cmp, v_cmp, ks, vs, kw, vw, gates, overlap)

        flat = lambda a: a.reshape(n, W_GRP)
        mk, mv = mem_kv(mem2d, xa_wk[l].astype(BF16), xa_wv[l].astype(BF16))
        m_len = mem.shape[1]
        xs = cross_attention((flat(out_a), flat(out_b), out_c, flat(out_d)), xs, s, w_out[l].astype(BF16),
                             row(ln1_g[l]), row(ln1_b[l]), mk.reshape(bsz, m_len, d), mv.reshape(bsz, m_len, d),
                             xa_wq[l].astype(BF16), xa_wo[l].astype(BF16), row(ln2_g[l]), row(ln2_b[l]), alpha)

        tok, lrow, stats = moe_router(xs, router_w.T, router_bias.reshape(-1, 1), upper)
        cnts = stats[:, :N_EXPERT_GROUPS, 0].reshape(-1)
        offs = stats[:, N_EXPERT_GROUPS:2 * N_EXPERT_GROUPS, 0].reshape(-1)
        xs = moe_ffn(xs, tok, lrow, offs, cnts, l, wg_b, wu_b, wd_b,
                     row(ln3_g[l]), row(ln3_b[l]), alpha)
    return xs.reshape(bsz, s, d)
```

```python
import functools
import math

import jax
import jax.numpy as jnp
from jax import lax
from jax.experimental import pallas as pl
from jax.experimental.pallas import tpu as pltpu

F32 = jnp.float32
BF16 = jnp.bfloat16
HI = lax.Precision.HIGHEST

HEAD_DIM = 64
N_HEADS = 4
W_GRP = 256
CONV_WIDTH = 31
GMLP_CHUNK = 128
CMP_LEN = 32
CMP_STRIDE = 16
SEL_LEN = 64
SEL_TOPN = 16
WINDOW = 512
Q_BLOCK = 256
ROPE_THETA = 10000.0
N_MEM_HEADS = 4
N_EXPERTS = 16
N_EXPERT_GROUPS = 4
LN_EPS = 1e-5
RWKV_GN_EPS = 64e-5
NEG = -1e30
LOG2E = math.log2(math.e)
LANES = 128
WKV_CHUNK = 64
WKV_GROUP = 8

TM = 512
MOE_TM = 1024
MOE_BLK = 256
MOE_EPS = 4
SEL_SUB = 2
SEL_KT = 1024
VMEM_LIMIT = 56 * 1024 * 1024


def _cparams(*sem):
    return pltpu.CompilerParams(dimension_semantics=sem, vmem_limit_bytes=VMEM_LIMIT)


def _ln(x, g, b, eps=LN_EPS):
    mu = jnp.mean(x, axis=-1, keepdims=True)
    xc = x - mu
    var = jnp.mean(xc * xc, axis=-1, keepdims=True)
    return xc * lax.rsqrt(var + eps) * g + b


def _dot(a, b, precision=None):
    return jnp.dot(a, b, preferred_element_type=F32, precision=precision)


def _dot_nt(a, b, precision=None):
    return lax.dot_general(a, b, (((1,), (1,)), ((), ())),
                           preferred_element_type=F32, precision=precision)


def _dot_hilo(x, w_bf16):
    hi = x.astype(BF16)
    lo = (x - hi.astype(F32)).astype(BF16)
    return _dot(hi, w_bf16) + _dot(lo, w_bf16)


def _dot3(x, w):
    x_hi = x.astype(BF16)
    x_lo = (x - x_hi.astype(F32)).astype(BF16)
    w_hi = w.astype(BF16)
    w_lo = (w - w_hi.astype(F32)).astype(BF16)
    return _dot(x_hi, w_hi) + _dot(x_lo, w_hi) + _dot(x_hi, w_lo)


def _group_avg_matrix(width, group):
    r = jnp.arange(width)[:, None] // group
    c = jnp.arange(width)[None, :] // group
    return jnp.where(r == c, 1.0 / group, 0.0).astype(F32)


IN_SPLITS = (("conv", 0, 512), ("rwkv", 512, 1536), ("gmlp", 1536, 2048),
             ("q", 2048, 2304), ("kv", 2304, 2688), ("misc", 2688, 2816))
MISC_VD_OFF = 32


N_CONV_P, N_GMLP_P, N_RWKV_P, N_VMIX_P = 5, 4, 9, 3
N_NSA_OUT, N_RWKV_OUT = 9, 7


def _inproj_body(apply_ln, has_vfirst, tiles_per_seq, x_ref, g_ref, b_ref, w_ref, wm_ref, cos_ref, sin_ref, *rest):
    take = lambda k: (rest[:k], rest[k:])
    conv_p, rest = take(N_CONV_P)
    gmlp_p, rest = take(N_GMLP_P)
    rwkv_p, rest = take(N_RWKV_P + (N_VMIX_P if has_vfirst else 0))
    vf_ref = None
    if has_vfirst:
        (vf_ref,), rest = take(1)
    if apply_ln:
        (xln_o,), rest = take(1)
    (conv_o,), rest = take(1)
    rwkv_o, rest = take(N_RWKV_OUT)
    (gmlp_o,), rest = take(1)
    nsa_o, rest = take(N_NSA_OUT)
    wb_ref, hbuf, shifted, buf, mbuf = rest
    main = IN_SPLITS[-1][1]

    @pl.when(pl.program_id(0) == 0)
    def _():
        wb_ref[...] = w_ref[0, :, :main].astype(BF16)

    x = x_ref[...]
    if apply_ln:
        x = _ln(x, g_ref[...], b_ref[...])
        xln_o[...] = x
    xb = x.astype(BF16)
    cols = {name: (lo, hi) for name, lo, hi in IN_SPLITS}
    proj = lambda name: _dot(xb, wb_ref[:, cols[name][0]:cols[name][1]])
    tile_in_seq = pl.program_id(0) % tiles_per_seq
    first = tile_in_seq == 0
    misc = _dot(xb, wm_ref[...])
    _conv_math(proj("conv"), first, *conv_p, conv_o, hbuf, shifted)
    _rwkv_prep_math(proj("rwkv"), misc, first, vf_ref, rwkv_p, rwkv_o, buf, mbuf)
    _gmlp_math(proj("gmlp"), *gmlp_p, gmlp_o)
    _nsa_prep_math(proj("q"), proj("kv"), misc, tile_in_seq * x.shape[0], cos_ref, sin_ref, *nsa_o)


def in_proj(x2d, seq_len, g, b, w_in, layer, w_misc, cos_t, sin_t, conv_p, gmlp_p, rwkv_p, v_first, apply_ln):
    n, d = x2d.shape
    bsz = n // seq_len
    tps = seq_len // TM
    row = lambda w: pl.BlockSpec((TM, w), lambda i: (i, 0))
    const = lambda a: pl.BlockSpec(a.shape, lambda i: (0,) * a.ndim)
    seq = lambda w: pl.BlockSpec((1, TM, w), lambda i: (i // tps, i % tps, 0))
    sd = lambda w, dt: jax.ShapeDtypeStruct((bsz, seq_len, w), dt)
    flat = lambda w: jax.ShapeDtypeStruct((n, w), F32)
    out_shapes = ([flat(W_GRP)] + [sd(W_GRP, F32)] * N_RWKV_OUT + [flat(W_GRP)]
                  + [sd(W_GRP, BF16), sd(W_GRP, BF16), sd(HEAD_DIM, F32), sd(HEAD_DIM, F32),
                     sd(2 * LANES, BF16), jax.ShapeDtypeStruct((bsz, LANES, seq_len), BF16),
                     sd(HEAD_DIM, BF16), sd(LANES, BF16), sd(LANES, F32)])
    out_specs = ([row(W_GRP)] + [seq(W_GRP)] * N_RWKV_OUT + [row(W_GRP)]
                 + [seq(W_GRP), seq(W_GRP), seq(HEAD_DIM), seq(HEAD_DIM), seq(2 * LANES),
                    pl.BlockSpec((1, LANES, TM), lambda i: (i // tps, 0, i % tps)),
                    seq(HEAD_DIM), seq(LANES), seq(LANES)])
    if apply_ln:
        out_shapes = [flat(d)] + out_shapes
        out_specs = [row(d)] + out_specs
    has_vfirst = v_first is not None
    assert len(conv_p) == N_CONV_P and len(gmlp_p) == N_GMLP_P
    assert len(rwkv_p) == N_RWKV_P + (N_VMIX_P if has_vfirst else 0)
    tab = pl.BlockSpec((TM, LANES), lambda i: (i % tps, 0))
    params = list(conv_p) + list(gmlp_p) + list(rwkv_p)
    inputs = [x2d, g, b, w_in, w_misc, cos_t, sin_t] + params + ([v_first] if has_vfirst else [])
    in_specs = ([row(d), const(g), const(b),
                 pl.BlockSpec((1,) + w_in.shape[1:], lambda i: (layer, 0, 0), pipeline_mode=pl.Buffered(1)),
                 const(w_misc), tab, tab] + [const(a) for a in params]
                + ([seq(W_GRP)] if has_vfirst else []))
    return pl.pallas_call(
        functools.partial(_inproj_body, apply_ln, has_vfirst, tps),
        grid=(n // TM,),
        in_specs=in_specs, out_specs=out_specs, out_shape=out_shapes,
        scratch_shapes=[pltpu.VMEM((d, IN_SPLITS[-1][1]), BF16),
                        pltpu.VMEM((TM + CONV_HALO, W_GRP), F32), pltpu.VMEM((TM + CONV_HALO, W_GRP), F32),
                        pltpu.VMEM((TM + SHIFT_HALO, 4 * W_GRP), F32), pltpu.VMEM((TM + SHIFT_HALO, LANES), F32)],
        compiler_params=_cparams("arbitrary"), name="in_proj",
    )(*inputs)


CONV_HALO = 32


def _carry_rows(buf, halo, first):
    tail = buf.shape[0] - halo

    @pl.when(first)
    def _():
        buf[0:halo, :] = jnp.zeros((halo, buf.shape[1]), buf.dtype)

    @pl.when(jnp.logical_not(first))
    def _():
        buf[0:halo, :] = buf[tail:tail + halo, :]


def _conv_math(cur, first, w_ref, b_ref, gg_ref, gb_ref, avg_ref, o_ref, hbuf, shifted):
    ts = cur.shape[0]
    h = cur[:, :W_GRP] * jax.nn.sigmoid(cur[:, W_GRP:])
    _carry_rows(hbuf, CONV_HALO, first)
    hbuf[CONV_HALO:, :] = h
    acc = jnp.zeros((ts, W_GRP), F32)
    base = CONV_HALO - (CONV_WIDTH - 1)
    for phase in range(SUB):
        taps = [j for j in range(CONV_WIDTH) if (base + j) % SUB == phase]
        span = (base + taps[-1]) // SUB * SUB + ts
        shifted[0:span, :] = hbuf[phase:phase + span, :]
        for j in taps:
            lo = (base + j) // SUB * SUB
            acc = acc + w_ref[j:j + 1, :] * shifted[lo:lo + ts, :]
    acc = acc + b_ref[...]
    avg = avg_ref[...].astype(BF16)
    mu = _dot_hilo(acc, avg)
    xc = acc - mu
    var = _dot_hilo(xc * xc, avg)
    y = xc * lax.rsqrt(var + LN_EPS) * gg_ref[...] + gb_ref[...]
    o_ref[...] = y * jax.nn.sigmoid(y)


SHIFT_HALO = 8


def _shift_prev(buf, cur, first):
    ts = cur.shape[0]
    _carry_rows(buf, SHIFT_HALO, first)
    buf[SHIFT_HALO:, :] = cur
    return buf[SHIFT_HALO - 1:SHIFT_HALO - 1 + ts, :]


def _rwkv_prep_math(cur, misc, first, vf_ref, params, outs, buf, mbuf):
    mu_ref, w0_ref, a0_ref, wup_ref, aup_ref, gup_ref, kk_ref, ka_ref, ones_ref = params[:9]
    r_o, lw_o, k_o, v_o, a_o, b_o, g_o = outs
    prev = _shift_prev(buf, cur, first)
    y = cur + mu_ref[...] * (prev - cur)
    r = y[:, 0:256]
    k = y[:, 256:512]
    v = y[:, 512:768]
    lora = y[:, 768:1024]
    w = w0_ref[...] + _dot3(jnp.tanh(lora), wup_ref[...])
    a = jax.nn.sigmoid(a0_ref[...] + _dot3(lora, aup_ref[...]))
    g = _dot3(jax.nn.sigmoid(lora), gup_ref[...])
    z = -w
    sp = jnp.maximum(z, 0.0) + jnp.log(1.0 + jnp.exp(-jnp.abs(z)))
    lw = -jnp.exp(-sp - 0.5)
    if vf_ref is not None:
        vmu_ref, v0_ref, vup_ref = params[9:]
        mprev = _shift_prev(mbuf, misc, first)
        vd = misc + vmu_ref[...] * (mprev - misc)
        v_mix = jax.nn.sigmoid(v0_ref[...] + _dot3(vd, vup_ref[...]))
        v = v + (vf_ref[0] - v) * v_mix
    kk = k * kk_ref[...]
    n2 = _dot_hilo(kk * kk, ones_ref[...].astype(BF16))
    kk = kk / jnp.maximum(jnp.sqrt(n2), 1e-12)
    k2 = k * (1.0 + (a - 1.0) * ka_ref[...])
    r_o[0] = r
    lw_o[0] = lw
    k_o[0] = k2
    v_o[0] = v
    a_o[0] = -kk
    b_o[0] = kk * a
    g_o[0] = g


def _block_diag(x, headmask):
    return jnp.concatenate([x] * N_HEADS, axis=0) * headmask


def _wkv_body(ts, r_ref, lw_ref, k_ref, v_ref, a_ref, b_ref, g_ref, rk_ref, gg_ref, gb_ref,
              avg_ref, ones_ref, o_ref, st_ref):
    C = WKV_CHUNK
    n = N_HEADS * C

    @pl.when(pl.program_id(1) == 0)
    def _():
        st_ref[...] = jnp.zeros_like(st_ref)

    ri = lax.broadcasted_iota(jnp.int32, (n, n), 0)
    ci = lax.broadcasted_iota(jnp.int32, (n, n), 1)
    head_f = jnp.where((ri // C) == (ci // HEAD_DIM), 1.0, 0.0)
    head_b = head_f.astype(BF16)
    lag = jnp.where((ri // C) == (ci // C), (ri % C) - (ci % C), -1)
    strict = lag > 0
    incl = lag >= 0
    eye = ri == ci
    eye_f = jnp.where(eye, 1.0, 0.0)
    tr = lax.broadcasted_iota(jnp.int32, (C, C), 0)
    tc = lax.broadcasted_iota(jnp.int32, (C, C), 1)
    tri = jnp.where(tc <= tr, 1.0, 0.0).astype(F32)
    cast = lambda x: x.astype(BF16)
    G = ts // C
    split = lambda ref: ref[0].reshape(G, C, W_GRP)
    tile_heads = lambda x: jnp.concatenate([x] * N_HEADS, axis=1)
    bd16 = lambda x: tile_heads(cast(x)) * head_b[None]
    bmm = lambda x, y: lax.dot_general(x, y, (((2,), (1,)), ((0,), (0,))), preferred_element_type=F32)
    bmm_nt = lambda x, y: lax.dot_general(x, y, (((2,), (2,)), ((0,), (0,))), preferred_element_type=F32)

    lw = split(lw_ref)
    cum = lax.dot_general(jnp.broadcast_to(tri[None], (G, C, C)), lw, (((2,), (1,)), ((0,), (0,))),
                          preferred_element_type=F32, precision=HI)
    cum_c = cum[:, C - 1:C, :]
    e_in = jnp.exp(cum)
    e_neg = jnp.exp(-cum)
    e_tail = jnp.exp(cum_c - cum)
    b_c = split(b_ref)
    k_c = split(k_ref)
    a_t = bd16(split(a_ref) * jnp.exp(cum - lw))
    r_t = bd16(split(r_ref) * e_in)
    b_t = bd16(b_c * e_neg)
    k_t = bd16(k_c * e_neg)
    v_bd = bd16(split(v_ref))
    bh_t = cast(jnp.swapaxes(tile_heads(b_c * e_tail) * head_f[None], 1, 2))
    kh_t = cast(jnp.swapaxes(tile_heads(k_c * e_tail) * head_f[None], 1, 2))
    a_ab = jnp.where(strict[None], bmm_nt(a_t, b_t), 0.0)
    a_ak = cast(jnp.where(strict[None], bmm_nt(a_t, k_t), 0.0))
    a_rb = cast(jnp.where(incl[None], bmm_nt(r_t, b_t), 0.0))
    a_rk = cast(jnp.where(incl[None], bmm_nt(r_t, k_t), 0.0))
    t_inv = eye_f[None] + a_ab
    pw = cast(a_ab)
    for _ in range(int(math.log2(C)) - 1):
        pw = cast(bmm(pw, pw))
        t_inv = t_inv + bmm(cast(t_inv), pw)
    t16 = cast(t_inv)
    ta = cast(bmm(t16, a_t))
    u0 = bmm(cast(bmm(t16, a_ak)), v_bd)
    o0 = bmm(a_rk, v_bd)
    s0 = bmm(kh_t, v_bd)
    o_lhs = jnp.concatenate([r_t, a_rb], axis=2)
    w_col = jnp.sum(jnp.where(eye[None], jnp.exp(cum_c), 0.0), axis=2, keepdims=True)

    st = st_ref[...]
    outs = []
    for g in range(G):
        st16 = cast(st)
        u = cast(_dot(ta[g], st16) + u0[g])
        o_bd = _dot(o_lhs[g], jnp.concatenate([st16, u], axis=0)) + o0[g]
        st = w_col[g] * st + _dot(bh_t[g], u) + s0[g]
        outs.append(o_bd[0:C] + o_bd[C:2 * C] + o_bd[2 * C:3 * C] + o_bd[3 * C:4 * C])
    st_ref[...] = st

    o = jnp.concatenate(outs, axis=0)
    avg = avg_ref[...].astype(BF16)
    mu = _dot_hilo(o, avg)
    xc = o - mu
    var = _dot_hilo(xc * xc, avg)
    on = xc * lax.rsqrt(var + RWKV_GN_EPS) * gg_ref[...] + gb_ref[...]
    r = r_ref[0]
    k = k_ref[0]
    v = v_ref[0]
    bonus = _dot_hilo(r * k * rk_ref[...], ones_ref[...].astype(BF16)) * v
    o_ref[0] = (on + bonus) * g_ref[0]


def wkv_scan(r, lw, k, v, a, b, g, rk, gg, gb, avg, ones):
    bsz, s, _ = r.shape
    ts = WKV_GROUP * WKV_CHUNK
    tile = pl.BlockSpec((1, ts, W_GRP), lambda bi, i: (bi, i, 0))
    const = lambda arr: pl.BlockSpec(arr.shape, lambda bi, i: (0,) * arr.ndim)
    return pl.pallas_call(
        functools.partial(_wkv_body, ts),
        grid=(bsz, s // ts),
        in_specs=[tile] * 7 + [const(x) for x in (rk, gg, gb, avg, ones)],
        out_specs=tile, out_shape=jax.ShapeDtypeStruct((bsz, s, W_GRP), F32),
        scratch_shapes=[pltpu.VMEM((N_HEADS * WKV_CHUNK, W_GRP), F32)],
        compiler_params=_cparams("parallel", "arbitrary"), name="wkv_scan",
    )(r, lw, k, v, a, b, g, rk, gg, gb, avg, ones)


def _gmlp_math(x, g_ref, b_ref, ws_ref, bs_ref, o_ref):
    ts = x.shape[0]
    u = jax.nn.gelu(x[:, :W_GRP])
    v = _ln(jax.nn.gelu(x[:, W_GRP:]), g_ref[...], b_ref[...]).astype(BF16)
    tr = lax.broadcasted_iota(jnp.int32, (GMLP_CHUNK, GMLP_CHUNK), 0)
    tc = lax.broadcasted_iota(jnp.int32, (GMLP_CHUNK, GMLP_CHUNK), 1)
    ws = [jnp.where(tc <= tr, ws_ref[h], 0.0).astype(BF16) for h in range(N_HEADS)]
    for c in range(ts // GMLP_CHUNK):
        rows = slice(c * GMLP_CHUNK, (c + 1) * GMLP_CHUNK)
        mixed = [_dot(ws[h], v[rows, h * HEAD_DIM:(h + 1) * HEAD_DIM]) + bs_ref[h]
                 for h in range(N_HEADS)]
        o_ref[rows, :] = u[rows, :] * jnp.concatenate(mixed, axis=-1)


def _swap_halves(x, lane):
    w = x.shape[-1]
    half = HEAD_DIM // 2
    fwd = pltpu.roll(x, w - half, 1)
    bwd = pltpu.roll(x, half, 1)
    return jnp.where((lane % HEAD_DIM) < half, fwd, bwd)


def _nsa_prep_math(q, kv, misc, pos0, cos_ref, sin_ref,
                   q_o, qr_o, kc_o, vc_o, ks_o, vs_o, kw_o, vw_o, g_o):
    scale = HEAD_DIM ** -0.5
    cos = jnp.concatenate([cos_ref[...]] * 3, axis=1)
    sin = jnp.concatenate([sin_ref[...]] * 3, axis=1)
    lane_q = lax.broadcasted_iota(jnp.int32, q.shape, 1)
    q_rot = q * cos[:, :W_GRP] + _swap_halves(q, lane_q) * sin[:, :W_GRP]
    lane_kv = lax.broadcasted_iota(jnp.int32, kv.shape, 1)
    kv_rot = kv * cos + _swap_halves(kv, lane_kv) * sin
    q_o[0] = (q * scale).astype(BF16)
    qr_o[0] = (q_rot * (scale * LOG2E)).astype(BF16)
    kc_o[0] = kv[:, 0:64]
    vc_o[0] = kv[:, 64:128]
    ts = kv.shape[0]
    lane = lax.broadcasted_iota(jnp.int32, (ts, LANES), 1)
    key_blk = (pos0 + lax.broadcasted_iota(jnp.int32, (ts, 1), 0)) // SEL_LEN
    ks_o[0] = jnp.concatenate([jnp.where(lane < HEAD_DIM, kv_rot[:, 128:256], 0.0),
                               jnp.where(lane == key_blk, NEG, 0.0)], axis=1).astype(BF16)
    ones_col = jnp.where(lane == HEAD_DIM, 1.0, 0.0)
    vs_first = pltpu.roll(kv[:, 128:256], HEAD_DIM, 1)
    vs_o[0] = jnp.where(lane < HEAD_DIM, vs_first, ones_col).T.astype(BF16)
    kw_o[0] = kv_rot[:, 256:320].astype(BF16)
    vw_first = pltpu.roll(kv[:, 256:384], HEAD_DIM, 1)
    vw_o[0] = jnp.where(lane < HEAD_DIM, vw_first, ones_col).astype(BF16)
    g_o[0] = jax.nn.sigmoid(misc)


def _compress_body(kc_ref, vc_ref, pe_ref, w1_ref, w2_ref, ko_ref, vo_ref):
    half = CMP_STRIDE * HEAD_DIM
    for j, (c_ref, o_ref) in enumerate(((kc_ref, ko_ref), (vc_ref, vo_ref))):
        c = c_ref[0].astype(BF16)
        w1 = w1_ref[j]
        lo = _dot(c, w1[:half])
        hi = _dot(c, w1[half:])
        nb = hi.shape[0]
        hi_next = pltpu.roll(hi, nb - 1, 0)
        pe = jnp.broadcast_to(pe_ref[j], (8, 2 * half))
        pe_term = _dot(pe, w1.astype(F32), HI)[0:1]
        h = jax.nn.gelu(lo + hi_next + pe_term)
        o_ref[0] = _dot(h.astype(BF16), w2_ref[j]).astype(BF16)


def nsa_compress(kc_r, vc_r, pe, w1, w2):
    bsz, nb, w = kc_r.shape
    blk = pl.BlockSpec((1, nb, w), lambda bi: (bi, 0, 0))
    const = lambda a: pl.BlockSpec(a.shape, lambda bi: (0,) * a.ndim)
    out = jax.ShapeDtypeStruct((bsz, nb, HEAD_DIM), BF16)
    ospec = pl.BlockSpec((1, nb, HEAD_DIM), lambda bi: (bi, 0, 0))
    return pl.pallas_call(
        _compress_body, grid=(bsz,),
        in_specs=[blk, blk, const(pe), const(w1), const(w2)],
        out_specs=[ospec, ospec], out_shape=[out, out],
        compiler_params=_cparams("parallel"), name="nsa_compress",
    )(kc_r, vc_r, pe, w1, w2)


def _stack_heads(x):
    return jnp.concatenate([x[:, h * HEAD_DIM:(h + 1) * HEAD_DIM] for h in range(N_HEADS)], axis=0)


def _nsa_attn_body(n_top, q_ref, qr_ref, kc_ref, vc_ref, ks_ref, vst_ref, kw_ref, vw_ref, g_ref,
                   ov_ref, o_ref):
    T = Q_BLOCK
    i = pl.program_id(1)
    t0 = i * T
    qs = _stack_heads(q_ref[0])
    qrs = _stack_heads(qr_ref[0])
    pos = t0 + lax.broadcasted_iota(jnp.int32, (T, 1), 0)
    tile4 = lambda x: jnp.concatenate([x] * N_HEADS, axis=0)

    kc = kc_ref[0]
    n_cmp = kc.shape[0]
    cmp_end = lax.broadcasted_iota(jnp.int32, (1, n_cmp), 1) * CMP_STRIDE + (CMP_LEN - 1)
    valid_c = tile4(jnp.where(cmp_end <= pos, 1.0, 0.0))
    s_c = jnp.where(valid_c > 0.5, _dot_nt(qs, kc), NEG)
    p_c = jnp.exp(s_c - jnp.max(s_c, axis=-1, keepdims=True)) * valid_c
    p_c = p_c / jnp.maximum(jnp.sum(p_c, axis=-1, keepdims=True), 1e-20)
    o_c = _dot(p_c.astype(BF16), vc_ref[0])
    p_sum = p_c[0:T] + p_c[T:2 * T] + p_c[2 * T:3 * T] + p_c[3 * T:4 * T]
    ps_hi, ps_lo = _split_bf16(p_sum)
    ov = ov_ref[...].astype(BF16)
    imp_t = _dot_nt(ov, ps_hi) + _dot_nt(ov, ps_lo)

    WK = WINDOW + T
    start = pl.multiple_of(jnp.maximum(t0 - WINDOW, 0), T)
    kw = kw_ref[0, pl.ds(start, WK), :]
    vw = vw_ref[0, pl.ds(start, WK), :]
    kpos_w = start + lax.broadcasted_iota(jnp.int32, (1, WK), 1)
    bias_w = tile4(jnp.where(kpos_w <= pos, jnp.where(kpos_w > pos - WINDOW, 0.0, NEG), NEG))
    s_w = _dot_nt(qrs, kw) + bias_w
    p_w = jnp.exp2(s_w - jnp.max(s_w, axis=-1, keepdims=True))
    acc_w = _dot(p_w.astype(BF16), vw)
    o_w = acc_w[:, :HEAD_DIM] / acc_w[:, HEAD_DIM:HEAD_DIM + 1]

    n_sel = imp_t.shape[0]
    jj = lax.broadcasted_iota(jnp.int32, (n_sel, T), 0)
    blk = (t0 + lax.broadcasted_iota(jnp.int32, (1, T), 1)) // SEL_LEN
    val = jnp.where(jj == blk, 3e38, jnp.where(jj == 0, 3e38, jnp.where(jj <= blk, imp_t, -1.0)))
    sel_t = jnp.zeros((n_sel, T), F32)
    for _ in range(n_top):
        mx = jnp.max(val, axis=0, keepdims=True)
        idx = jnp.min(jnp.where(val == mx, jj, n_sel), axis=0, keepdims=True)
        hit = jj == idx
        sel_t = jnp.where(hit, 1.0, sel_t)
        val = jnp.where(hit, -2.0, val)
    sel = sel_t.T

    KT = SEL_KT
    unsel = tile4((1.0 - sel).astype(BF16))
    q_aug = jnp.concatenate([qrs, jnp.zeros((N_HEADS * T, HEAD_DIM), BF16), unsel], axis=1)
    krow = lax.broadcasted_iota(jnp.int32, (KT, 1), 0)
    pos_row = t0 + lax.broadcasted_iota(jnp.int32, (1, T), 1)

    KS = KT // SEL_SUB

    def sel_tile(jt, carry, diagonal):
        m, acc = carry
        k0 = pl.multiple_of(jt * KT, KT)

        def scores(j):
            ks = pl.multiple_of(k0 + j * KS, KS)
            s_t = _dot_nt(ks_ref[0, pl.ds(ks, KS), :], q_aug)
            if diagonal:
                causal = jnp.where(ks + krow[:KS] <= pos_row, 0.0, NEG)
                s_t = s_t + jnp.concatenate([causal] * N_HEADS, axis=1)
            return s_t

        s_next = scores(0)
        for j in range(SEL_SUB):
            s_t = s_next
            if j + 1 < SEL_SUB:
                s_next = scores(j + 1)
            m_new = jnp.maximum(m, jnp.max(s_t, axis=0, keepdims=True))
            p_t = jnp.exp2(s_t - m_new)
            ks = pl.multiple_of(k0 + j * KS, KS)
            acc = jnp.exp2(m - m_new) * acc + _dot(vst_ref[0, :, pl.ds(ks, KS)], p_t.astype(BF16))
            m = m_new
        return m, acc

    init = (jnp.full((1, N_HEADS * T), NEG, F32), jnp.zeros((LANES, N_HEADS * T), F32))
    n_full = t0 // KT
    carry = lax.fori_loop(0, n_full, functools.partial(sel_tile, diagonal=False), init)
    _, acc_t = sel_tile(n_full, carry, True)
    acc_s = acc_t.T
    o_s = acc_s[:, :HEAD_DIM] / acc_s[:, HEAD_DIM:HEAD_DIM + 1]

    g = g_ref[0]
    outs = []
    for h in range(N_HEADS):
        rows = slice(h * T, (h + 1) * T)
        outs.append(g[:, 3 * h:3 * h + 1] * o_c[rows] + g[:, 3 * h + 1:3 * h + 2] * o_s[rows]
                    + g[:, 3 * h + 2:3 * h + 3] * o_w[rows])
    o_ref[0] = jnp.concatenate(outs, axis=-1)


def nsa_attention(q, qr, k_cmp, v_cmp, ks, vs, kw, vw, gates, overlap):
    bsz, s, _ = q.shape
    n_top = min(SEL_TOPN, s // SEL_LEN)
    n_cmp = k_cmp.shape[1]
    qtile = lambda w: pl.BlockSpec((1, Q_BLOCK, w), lambda bi, i: (bi, i, 0))
    full = lambda rows, w: pl.BlockSpec((1, rows, w), lambda bi, i: (bi, 0, 0))
    return pl.pallas_call(
        functools.partial(_nsa_attn_body, n_top),
        grid=(bsz, s // Q_BLOCK),
        in_specs=[qtile(W_GRP), qtile(W_GRP), full(n_cmp, HEAD_DIM), full(n_cmp, HEAD_DIM),
                  full(s, 2 * LANES), full(LANES, s), full(s, HEAD_DIM), full(s, LANES),
                  qtile(LANES), pl.BlockSpec(overlap.shape, lambda bi, i: (0, 0))],
        out_specs=qtile(W_GRP), out_shape=jax.ShapeDtypeStruct((bsz, s, W_GRP), F32),
        compiler_params=_cparams("parallel", "parallel"), name="nsa_attention",
    )(q, qr, k_cmp, v_cmp, ks, vs, kw, vw, gates, overlap)


def _outproj_math(alpha, parts, x, w_ref, g_ref, b_ref):
    y = alpha * x
    for j, part in enumerate(parts):
        y = y + _dot(part[...].astype(BF16), w_ref[j * W_GRP:(j + 1) * W_GRP, :])
    return _ln(y, g_ref[...], b_ref[...])


def _memkv_body(m_ref, wk_ref, wv_ref, k_o, v_o):
    mb = m_ref[...].astype(BF16)
    k_o[...] = _dot(mb, wk_ref[...]).astype(BF16)
    v_o[...] = _dot(mb, wv_ref[...]).astype(BF16)


def mem_kv(mem2d, wk, wv):
    n, d = mem2d.shape
    full = lambda a: pl.BlockSpec(a.shape, lambda i: (0, 0))
    out = jax.ShapeDtypeStruct((n, d), BF16)
    return pl.pallas_call(
        _memkv_body, grid=(1,),
        in_specs=[full(mem2d), full(wk), full(wv)],
        out_specs=[pl.BlockSpec((n, d), lambda i: (0, 0))] * 2, out_shape=[out, out],
        compiler_params=_cparams("arbitrary"), name="mem_kv",
    )(mem2d, wk, wv)


def _xattn_body(alpha, a_ref, b_ref, c_ref, d_ref, x_ref, w1_ref, g1_ref, b1_ref,
                k_ref, v_ref, wq_ref, wo_ref, g_ref, b_ref_, o_ref):
    x = _outproj_math(alpha, (a_ref, b_ref, c_ref, d_ref), x_ref[...], w1_ref, g1_ref, b1_ref)
    d = x.shape[-1]
    hd = d // N_MEM_HEADS
    q = (_dot(x.astype(BF16), wq_ref[...]) * (hd ** -0.5 * LOG2E)).astype(BF16)
    k = k_ref[0]
    v = v_ref[0]
    cols = [slice(h * hd, (h + 1) * hd) for h in range(N_MEM_HEADS)]
    scores = [_dot_nt(q[:, cs], k[:, cs]) for cs in cols]
    probs = []
    for s in scores:
        p = jnp.exp2(s - jnp.max(s, axis=-1, keepdims=True))
        probs.append((p / jnp.sum(p, axis=-1, keepdims=True)).astype(BF16))
    y = alpha * x
    for cs, p in zip(cols, probs):
        y = y + _dot(_dot(p, v[:, cs]).astype(BF16), wo_ref[cs, :])
    o_ref[...] = _ln(y, g_ref[...], b_ref_[...])


def cross_attention(parts, x2d, seq_len, w_out, g1, b1, k, v, wq, wo, g, b, alpha):
    n, d = x2d.shape
    bsz, s = n // seq_len, seq_len
    tps = s // TM
    m = k.shape[1]
    const = lambda a: pl.BlockSpec(a.shape, lambda bi, i: (0,) * a.ndim)
    row = lambda w: pl.BlockSpec((TM, w), lambda bi, i: (bi * tps + i, 0))
    return pl.pallas_call(
        functools.partial(_xattn_body, alpha),
        grid=(bsz, s // TM),
        in_specs=[row(W_GRP)] * 4 + [row(d), const(w_out), const(g1), const(b1),
                  pl.BlockSpec((1, m, d), lambda bi, i: (bi, 0, 0)),
                  pl.BlockSpec((1, m, d), lambda bi, i: (bi, 0, 0)),
                  const(wq), const(wo), const(g), const(b)],
        out_specs=row(d), out_shape=jax.ShapeDtypeStruct((n, d), F32),
        compiler_params=_cparams("parallel", "parallel"), name="cross_attention",
    )(*parts, x2d, w_out, g1, b1, k, v, wq, wo, g, b)


PER_GRP = N_EXPERTS // N_EXPERT_GROUPS
LPOS_LANE = PER_GRP
SUB = 8
BF16_ROWS = 16


def _router_body(x_ref, rwt_ref, rb_ref, upper_ref, tok_o, lrow_o, cnt_o):
    tm = x_ref.shape[0]
    logits = _dot3(x_ref[...], rwt_ref[...]).T[:N_EXPERTS]
    ex = jnp.exp(logits - jnp.max(logits, axis=0, keepdims=True))
    probs = ex / jnp.sum(ex, axis=0, keepdims=True)
    sel = probs + rb_ref[...]
    srow = [sel[e:e + 1] for e in range(N_EXPERTS)]
    prow = [probs[e:e + 1] for e in range(N_EXPERTS)]
    gscore = []
    for g in range(N_EXPERT_GROUPS):
        r = srow[g * PER_GRP:(g + 1) * PER_GRP]
        best = None
        for a in range(PER_GRP):
            for b in range(a + 1, PER_GRP):
                best = r[a] + r[b] if best is None else jnp.maximum(best, r[a] + r[b])
        gscore.append(best)
    g_idx = jnp.zeros((1, tm), jnp.int32)
    top = gscore[0]
    for g in range(1, N_EXPERT_GROUPS):
        better = gscore[g] > top
        g_idx = jnp.where(better, g, g_idx)
        top = jnp.where(better, gscore[g], top)

    def of_group(rows, e):
        out = rows[e]
        for g in range(1, N_EXPERT_GROUPS):
            out = jnp.where(g_idx == g, rows[g * PER_GRP + e], out)
        return out

    sg = [of_group(srow, e) for e in range(PER_GRP)]
    pg = [of_group(prow, e) for e in range(PER_GRP)]
    w = []
    for e in range(PER_GRP):
        rank = jnp.zeros((1, tm), F32)
        for o in range(PER_GRP):
            if o != e:
                ahead = (sg[o] >= sg[e]) if o < e else (sg[o] > sg[e])
                rank = rank + jnp.where(ahead, 1.0, 0.0)
        w.append(jnp.where(rank < 1.5, pg[e], 0.0))
    w_sum = w[0] + w[1] + w[2] + w[3]
    row8 = lax.broadcasted_iota(jnp.int32, (SUB, tm), 0)
    onehot = jnp.where(row8 == g_idx, 1.0, 0.0)
    before = _dot(onehot.astype(BF16), upper_ref[...])
    cnt = jnp.sum(onehot, axis=1, keepdims=True)
    offs = [jnp.zeros((1, 1), F32)]
    for g in range(1, N_EXPERT_GROUPS):
        offs.append(offs[-1] + cnt[g - 1:g])
    lpos = jnp.zeros((1, tm), F32)
    for g in range(N_EXPERT_GROUPS):
        lpos = lpos + onehot[g:g + 1] * (offs[g] + before[g:g + 1])
    tok = jnp.zeros((SUB, tm), F32)
    for e in range(PER_GRP):
        tok = jnp.where(row8 == e, w[e] / w_sum, tok)
    tok = jnp.where(row8 == LPOS_LANE, lpos, tok)
    tok_o[...] = jnp.concatenate([tok, jnp.zeros((LANES - SUB, tm), F32)], axis=0).T
    lrow_o[0] = lpos.astype(jnp.int32)
    rowc = lax.broadcasted_iota(jnp.int32, (SUB, LANES), 0)
    stats = jnp.zeros((SUB, LANES), F32)
    for g in range(N_EXPERT_GROUPS):
        stats = jnp.where(rowc == g, cnt[g:g + 1], stats)
        stats = jnp.where(rowc == N_EXPERT_GROUPS + g, offs[g], stats)
    cnt_o[0] = stats.astype(jnp.int32)


def moe_router(x2d, rwt, rb_col, upper):
    n, d = x2d.shape
    tm = upper.shape[0]
    nt = n // tm
    const = lambda a: pl.BlockSpec(a.shape, lambda i: (0, 0))
    return pl.pallas_call(
        _router_body, grid=(nt,),
        in_specs=[pl.BlockSpec((tm, d), lambda i: (i, 0)), const(rwt), const(rb_col), const(upper)],
        out_specs=[pl.BlockSpec((tm, LANES), lambda i: (i, 0)),
                   pl.BlockSpec((1, 1, tm), lambda i: (i, 0, 0)),
                   pl.BlockSpec((1, SUB, LANES), lambda i: (i, 0, 0))],
        out_shape=[jax.ShapeDtypeStruct((n, LANES), F32), jax.ShapeDtypeStruct((nt, 1, tm), jnp.int32),
                   jax.ShapeDtypeStruct((nt, SUB, LANES), jnp.int32)],
        compiler_params=_cparams("parallel"), name="moe_router",
    )(x2d, rwt, rb_col, upper)


def _split_bf16(x):
    hi = x.astype(BF16)
    return hi, (x - hi.astype(F32)).astype(BF16)


def _moe_body(alpha, offs_ref, cnts_ref, x_ref, tok_ref, lrow_ref, wg_ref, wu_ref, wd_ref, g_ref, b_ref, o_ref,
              xs_ref, gs_ref, acc_ref):
    i = pl.program_id(0)
    step = pl.program_id(1)
    tm = x_ref.shape[0]
    eps = wd_ref.shape[1]

    @pl.when(step == 0)
    def _():
        slot = lax.broadcasted_iota(jnp.int32, (tm, tm), 0)
        perm = jnp.where(slot == lrow_ref[0], 1.0, 0.0).astype(BF16)
        xs_ref[0:tm, :] = _dot(perm, x_ref[...].astype(BF16)).astype(BF16)
        t_hi, t_lo = _split_bf16(tok_ref[...])
        gs_ref[0:tm, :] = _dot(perm, t_hi) + _dot(perm, t_lo)
        xs_ref[tm:, :] = jnp.zeros((MOE_BLK, xs_ref.shape[1]), BF16)
        gs_ref[tm:, :] = jnp.zeros((MOE_BLK, LANES), F32)
        acc_ref[...] = jnp.zeros_like(acc_ref)

    grp = (step * eps) // PER_GRP
    off = offs_ref[i * N_EXPERT_GROUPS + grp]
    end = off + cnts_ref[i * N_EXPERT_GROUPS + grp]
    lane = lax.broadcasted_iota(jnp.int32, (MOE_BLK, LANES), 1)
    first = off // BF16_ROWS * BF16_ROWS
    for s in range(tm // MOE_BLK + 1):
        lo = pl.multiple_of(first + s * MOE_BLK, BF16_ROWS)

        @pl.when(lo < end)
        def _(lo=lo):
            rows = pl.ds(lo, MOE_BLK)
            xs = xs_ref[rows, :]
            ridx = lo + lax.broadcasted_iota(jnp.int32, (MOE_BLK, 1), 0)
            in_run = jnp.where(ridx >= off, jnp.where(ridx < end, 1.0, 0.0), 0.0)
            gs = gs_ref[rows, :]
            y = jnp.zeros((MOE_BLK, x_ref.shape[1]), F32)
            for k in range(eps):
                gt = _dot(xs, wg_ref[0, k])
                h = (gt * jax.nn.sigmoid(gt) * _dot(xs, wu_ref[0, k])).astype(BF16)
                in_grp = (step * eps + k) % PER_GRP
                gcol = jnp.sum(jnp.where(lane == in_grp, gs, 0.0), axis=-1, keepdims=True)
                y = y + (gcol * in_run) * _dot(h, wd_ref[0, k])
            acc_ref[rows, :] += y

    @pl.when(step == pl.num_programs(1) - 1)
    def _():
        lcol = tok_ref[:, LPOS_LANE:LPOS_LANE + 1].astype(jnp.int32)
        slot = lax.broadcasted_iota(jnp.int32, (tm, tm), 1)
        unperm = jnp.where(slot == lcol, 1.0, 0.0).astype(BF16)
        y = _dot(unperm, acc_ref[0:tm, :].astype(BF16))
        o_ref[...] = _ln(alpha * x_ref[...] + y, g_ref[...], b_ref[...])


def moe_ffn(x2d, tok, lrow, offs, cnts, layer, wg, wu, wd, g, b, alpha):
    n, d = x2d.shape
    _, ne, de, _ = wd.shape
    tm = lrow.shape[2]
    const = lambda a: pl.BlockSpec(a.shape, lambda i, e, o, c: (0,) * a.ndim)
    grid_spec = pltpu.PrefetchScalarGridSpec(
        num_scalar_prefetch=2, grid=(n // tm, ne // MOE_EPS),
        in_specs=[pl.BlockSpec((tm, d), lambda i, e, o, c: (i, 0), pipeline_mode=pl.Buffered(1)),
                  pl.BlockSpec((tm, LANES), lambda i, e, o, c: (i, 0)),
                  pl.BlockSpec((1, 1, tm), lambda i, e, o, c: (i, 0, 0)),
                  pl.BlockSpec((1, MOE_EPS, d, de), lambda i, e, o, c: (layer, e, 0, 0)),
                  pl.BlockSpec((1, MOE_EPS, d, de), lambda i, e, o, c: (layer, e, 0, 0)),
                  pl.BlockSpec((1, MOE_EPS, de, d), lambda i, e, o, c: (layer, e, 0, 0)),
                  const(g), const(b)],
        out_specs=pl.BlockSpec((tm, d), lambda i, e, o, c: (i, 0)),
        scratch_shapes=[pltpu.VMEM((tm + MOE_BLK, d), BF16), pltpu.VMEM((tm + MOE_BLK, LANES), F32),
                        pltpu.VMEM((tm + MOE_BLK, d), F32)])
    return pl.pallas_call(
        functools.partial(_moe_body, alpha), grid_spec=grid_spec,
        out_shape=jax.ShapeDtypeStruct((n, d), F32),
        compiler_params=_cparams("parallel", "arbitrary"), name="moe_ffn",
    )(offs, cnts, x2d, tok, lrow, wg, wu, wd, g, b)


def _rope_tables(s):
    inv = ROPE_THETA ** (-jnp.arange(0, HEAD_DIM, 2, dtype=F32) / HEAD_DIM)
    ang = jnp.arange(s, dtype=F32)[:, None] * inv[None, :]
    cos, sin = jnp.cos(ang), jnp.sin(ang)
    cos_h = jnp.concatenate([cos, cos], axis=-1)
    sin_h = jnp.concatenate([-sin, sin], axis=-1)
    return jnp.tile(cos_h, (1, 2)), jnp.tile(sin_h, (1, 2))


def _overlap_matrix(s, n_cmp_pad):
    n_sel = s // SEL_LEN
    assert n_sel <= LANES
    cmp_start = jnp.arange(n_cmp_pad) * CMP_STRIDE
    sel_start = jnp.arange(LANES) * SEL_LEN
    ov = jnp.clip(jnp.minimum(cmp_start[None, :] + CMP_LEN, sel_start[:, None] + SEL_LEN)
                  - jnp.maximum(cmp_start[None, :], sel_start[:, None]), 0, None).astype(F32) / CMP_LEN
    n_cmp = s // CMP_STRIDE - (CMP_LEN // CMP_STRIDE - 1)
    real = (jnp.arange(n_cmp_pad)[None, :] < n_cmp) & (jnp.arange(LANES)[:, None] < n_sel)
    return jnp.where(real, ov, 0.0)


def _pad_rows(w, lo, total):
    out = jnp.zeros((total, w.shape[1]), w.dtype)
    return out.at[lo:lo + w.shape[0]].set(w)


def kernel(x, mem, ln_in_g, ln_in_b, w_in, w_out, conv_w, conv_b, conv_gn_g, conv_gn_b, rwkv_mu, rwkv_w0, rwkv_w_up, rwkv_a0, rwkv_a_up, rwkv_g_up, rwkv_k_k, rwkv_k_a, rwkv_r_k, rwkv_gn_g, rwkv_gn_b, rwkv_v_down, rwkv_v_mu, rwkv_v0, rwkv_v_up, gmlp_ln_g, gmlp_ln_b, gmlp_w_s, gmlp_b_s, nsa_pe_k, nsa_w1_k, nsa_w2_k, nsa_pe_v, nsa_w1_v, nsa_w2_v, ln1_g, ln1_b, xa_wq, xa_wk, xa_wv, xa_wo, ln2_g, ln2_b, router_w, router_bias, moe_w_gate, moe_w_up, moe_w_down, ln3_g, ln3_b):
    bsz, s, d = x.shape
    depth = w_in.shape[0]
    n = bsz * s
    alpha = (2 * depth) ** 0.25
    row = lambda a: a.reshape(1, -1)

    cos_t, sin_t = _rope_tables(s)
    n_blk = s // CMP_STRIDE
    overlap = _overlap_matrix(s, n_blk)
    avg64 = _group_avg_matrix(W_GRP, HEAD_DIM)
    ones64 = avg64 * HEAD_DIM
    rw_pad = jnp.zeros((d, LANES), F32).at[:, :N_EXPERTS].set(router_w)
    moe_tm = min(MOE_TM, n)
    t_idx = jnp.arange(moe_tm)
    upper = (t_idx[:, None] < t_idx[None, :]).astype(BF16)
    mem2d = mem.reshape(bsz * mem.shape[1], d)

    wg_b, wu_b, wd_b = moe_w_gate.astype(BF16), moe_w_up.astype(BF16), moe_w_down.astype(BF16)

    xs = x.reshape(n, d)
    v_first = None
    for l in range(depth):
        main = IN_SPLITS[-1][1]
        w_misc = jnp.zeros((d, LANES), F32).at[:, :w_in.shape[2] - main].set(w_in[l, :, main:])
        if l > 0:
            w_misc = w_misc.at[:, MISC_VD_OFF:MISC_VD_OFF + rwkv_v_down.shape[2]].set(rwkv_v_down[l - 1])
        gmlp_p = (row(gmlp_ln_g[l]), row(gmlp_ln_b[l]), gmlp_w_s[l], gmlp_b_s[l].reshape(N_HEADS, GMLP_CHUNK, 1))
        conv_p = (conv_w[l], row(conv_b[l]), row(conv_gn_g[l]), row(conv_gn_b[l]), avg64)
        rwkv_p = (row(rwkv_mu[l]), row(rwkv_w0[l]), row(rwkv_a0[l]), _pad_rows(rwkv_w_up[l], 0, W_GRP),
                  _pad_rows(rwkv_a_up[l], 64, W_GRP), _pad_rows(rwkv_g_up[l], 128, W_GRP),
                  row(rwkv_k_k[l]), row(rwkv_k_a[l]), ones64)
        if l > 0:
            vmu = jnp.zeros((1, LANES), F32).at[0, MISC_VD_OFF:MISC_VD_OFF + rwkv_v_mu.shape[1]].set(rwkv_v_mu[l - 1])
            rwkv_p += (vmu, row(rwkv_v0[l - 1]), _pad_rows(rwkv_v_up[l - 1], MISC_VD_OFF, LANES))
        outs = in_proj(xs, s, row(ln_in_g), row(ln_in_b), w_in, l, w_misc.astype(BF16), cos_t, sin_t,
                       conv_p, gmlp_p, rwkv_p, v_first, apply_ln=(l == 0))
        if l == 0:
            xs, outs = outs[0], outs[1:]
        out_a = outs[0]
        r_, lw_, k_, v_, a_, b_, g_ = outs[1:1 + N_RWKV_OUT]
        out_c = outs[1 + N_RWKV_OUT]
        q_b, qr_b, kc, vc, ks, vs, kw, vw, gates = outs[2 + N_RWKV_OUT:]
        if l == 0:
            v_first = v_
        out_b = wkv_scan(r_, lw_, k_, v_, a_, b_, g_, row(rwkv_r_k[l]), row(rwkv_gn_g[l]),
                         row(rwkv_gn_b[l]), avg64, ones64)

        pe = jnp.stack([nsa_pe_k[l].reshape(1, -1), nsa_pe_v[l].reshape(1, -1)])
        w1 = jnp.stack([nsa_w1_k[l], nsa_w1_v[l]]).astype(BF16)
        w2 = jnp.stack([nsa_w2_k[l], nsa_w2_v[l]]).astype(BF16)
        k_cmp, v_cmp = nsa_compress(kc.reshape(bsz, n_blk, CMP_STRIDE * HEAD_DIM),
                                    vc.reshape(bsz, n_blk, CMP_STRIDE * HEAD_DIM), pe, w1, w2)
        out_d = nsa_attention(q_b, qr_b, k_cmp, v_cmp, ks, vs, kw, vw, gates, overlap)

        flat = lambda a: a.reshape(n, W_GRP)
        mk, mv = mem_kv(mem2d, xa_wk[l].astype(BF16), xa_wv[l].astype(BF16))
        m_len = mem.shape[1]
        xs = cross_attention((flat(out_a), flat(out_b), out_c, flat(out_d)), xs, s, w_out[l].astype(BF16),
                             row(ln1_g[l]), row(ln1_b[l]), mk.reshape(bsz, m_len, d), mv.reshape(bsz, m_len, d),
                             xa_wq[l].astype(BF16), xa_wo[l].astype(BF16), row(ln2_g[l]), row(ln2_b[l]), alpha)

        tok, lrow, stats = moe_router(xs, rw_pad, router_bias.reshape(-1, 1), upper)
        cnts = stats[:, :N_EXPERT_GROUPS, 0].reshape(-1)
        offs = stats[:, N_EXPERT_GROUPS:2 * N_EXPERT_GROUPS, 0].reshape(-1)
        xs = moe_ffn(xs, tok, lrow, offs, cnts, l, wg_b, wu_b, wd_b,
                     row(ln3_g[l]), row(ln3_b[l]), alpha)
    return xs.reshape(bsz, s, d)
```

```python
import functools
import math

import jax
import jax.numpy as jnp
from jax import lax
from jax.experimental import pallas as pl
from jax.experimental.pallas import tpu as pltpu

F32 = jnp.float32
BF16 = jnp.bfloat16
HI = lax.Precision.HIGHEST

HEAD_DIM = 64
N_HEADS = 4
W_GRP = 256
CONV_WIDTH = 31
GMLP_CHUNK = 128
CMP_LEN = 32
CMP_STRIDE = 16
SEL_LEN = 64
SEL_TOPN = 16
WINDOW = 512
Q_BLOCK = 256
ROPE_THETA = 10000.0
N_MEM_HEADS = 4
N_EXPERTS = 16
N_EXPERT_GROUPS = 4
LN_EPS = 1e-5
RWKV_GN_EPS = 64e-5
NEG = -1e30
LOG2E = math.log2(math.e)
LANES = 128
WKV_CHUNK = 64
WKV_GROUP = 8

TM = 512
MOE_TM = 1024
_RUN_SD = math.sqrt(MOE_TM * (N_EXPERT_GROUPS - 1)) / N_EXPERT_GROUPS
MOE_BLK = 16 * -(-(MOE_TM // N_EXPERT_GROUPS + int(2 * _RUN_SD) + 15) // 16)
MOE_EPS = 4
SEL_SUB = 2
SEL_KT = 1024
VMEM_LIMIT = 56 * 1024 * 1024


def _cparams(*sem):
    return pltpu.CompilerParams(dimension_semantics=sem, vmem_limit_bytes=VMEM_LIMIT)


def _ln(x, g, b, eps=LN_EPS):
    mu = jnp.mean(x, axis=-1, keepdims=True)
    xc = x - mu
    var = jnp.mean(xc * xc, axis=-1, keepdims=True)
    return xc * lax.rsqrt(var + eps) * g + b


def _dot(a, b, precision=None):
    return jnp.dot(a, b, preferred_element_type=F32, precision=precision)


def _dot_nt(a, b, precision=None):
    return lax.dot_general(a, b, (((1,), (1,)), ((), ())),
                           preferred_element_type=F32, precision=precision)


def _dot_hilo(x, w_bf16):
    hi = x.astype(BF16)
    lo = (x - hi.astype(F32)).astype(BF16)
    return _dot(hi, w_bf16) + _dot(lo, w_bf16)


def _dot3(x, w):
    x_hi = x.astype(BF16)
    x_lo = (x - x_hi.astype(F32)).astype(BF16)
    w_hi = w.astype(BF16)
    w_lo = (w - w_hi.astype(F32)).astype(BF16)
    return _dot(x_hi, w_hi) + _dot(x_lo, w_hi) + _dot(x_hi, w_lo)


def _group_avg_matrix(width, group):
    r = jnp.arange(width)[:, None] // group
    c = jnp.arange(width)[None, :] // group
    return jnp.where(r == c, 1.0 / group, 0.0).astype(F32)


IN_SPLITS = (("conv", 0, 512), ("rwkv", 512, 1536), ("gmlp", 1536, 2048),
             ("q", 2048, 2304), ("kv", 2304, 2688), ("misc", 2688, 2816))
MISC_VD_OFF = 32


N_CONV_P, N_GMLP_P, N_RWKV_P, N_VMIX_P = 5, 4, 9, 3
N_NSA_OUT, N_RWKV_OUT = 9, 7


def _inproj_body(apply_ln, has_vfirst, tiles_per_seq, x_ref, g_ref, b_ref, w_ref, wm_ref, cos_ref, sin_ref, *rest):
    take = lambda k: (rest[:k], rest[k:])
    conv_p, rest = take(N_CONV_P)
    gmlp_p, rest = take(N_GMLP_P)
    rwkv_p, rest = take(N_RWKV_P + (N_VMIX_P if has_vfirst else 0))
    vf_ref = None
    if has_vfirst:
        (vf_ref,), rest = take(1)
    if apply_ln:
        (xln_o,), rest = take(1)
    (conv_o,), rest = take(1)
    rwkv_o, rest = take(N_RWKV_OUT)
    (gmlp_o,), rest = take(1)
    nsa_o, rest = take(N_NSA_OUT)
    wb_ref, hbuf, shifted, buf, mbuf = rest
    main = IN_SPLITS[-1][1]

    @pl.when(pl.program_id(0) == 0)
    def _():
        wb_ref[...] = w_ref[0, :, :main].astype(BF16)

    x = x_ref[...]
    if apply_ln:
        x = _ln(x, g_ref[...], b_ref[...])
        xln_o[...] = x
    xb = x.astype(BF16)
    cols = {name: (lo, hi) for name, lo, hi in IN_SPLITS}
    proj = lambda name: _dot(xb, wb_ref[:, cols[name][0]:cols[name][1]])
    tile_in_seq = pl.program_id(0) % tiles_per_seq
    first = tile_in_seq == 0
    misc = _dot(xb, wm_ref[...])
    _conv_math(proj("conv"), first, *conv_p, conv_o, hbuf, shifted)
    _rwkv_prep_math(proj("rwkv"), misc, first, vf_ref, rwkv_p, rwkv_o, buf, mbuf)
    _gmlp_math(proj("gmlp"), *gmlp_p, gmlp_o)
    _nsa_prep_math(proj("q"), proj("kv"), misc, tile_in_seq * x.shape[0], cos_ref, sin_ref, *nsa_o)


def in_proj(x2d, seq_len, g, b, w_in, layer, w_misc, cos_t, sin_t, conv_p, gmlp_p, rwkv_p, v_first, apply_ln):
    n, d = x2d.shape
    bsz = n // seq_len
    tps = seq_len // TM
    row = lambda w: pl.BlockSpec((TM, w), lambda i: (i, 0))
    const = lambda a: pl.BlockSpec(a.shape, lambda i: (0,) * a.ndim)
    seq = lambda w: pl.BlockSpec((1, TM, w), lambda i: (i // tps, i % tps, 0))
    sd = lambda w, dt: jax.ShapeDtypeStruct((bsz, seq_len, w), dt)
    flat = lambda w: jax.ShapeDtypeStruct((n, w), F32)
    out_shapes = ([flat(W_GRP)] + [sd(W_GRP, F32)] * N_RWKV_OUT + [flat(W_GRP)]
                  + [sd(W_GRP, BF16), sd(W_GRP, BF16), sd(HEAD_DIM, F32), sd(HEAD_DIM, F32),
                     sd(2 * LANES, BF16), jax.ShapeDtypeStruct((bsz, LANES, seq_len), BF16),
                     sd(HEAD_DIM, BF16), sd(LANES, BF16), sd(LANES, F32)])
    out_specs = ([row(W_GRP)] + [seq(W_GRP)] * N_RWKV_OUT + [row(W_GRP)]
                 + [seq(W_GRP), seq(W_GRP), seq(HEAD_DIM), seq(HEAD_DIM), seq(2 * LANES),
                    pl.BlockSpec((1, LANES, TM), lambda i: (i // tps, 0, i % tps)),
                    seq(HEAD_DIM), seq(LANES), seq(LANES)])
    if apply_ln:
        out_shapes = [flat(d)] + out_shapes
        out_specs = [row(d)] + out_specs
    has_vfirst = v_first is not None
    assert len(conv_p) == N_CONV_P and len(gmlp_p) == N_GMLP_P
    assert len(rwkv_p) == N_RWKV_P + (N_VMIX_P if has_vfirst else 0)
    tab = pl.BlockSpec((TM, LANES), lambda i: (i % tps, 0))
    params = list(conv_p) + list(gmlp_p) + list(rwkv_p)
    inputs = [x2d, g, b, w_in, w_misc, cos_t, sin_t] + params + ([v_first] if has_vfirst else [])
    in_specs = ([row(d), const(g), const(b),
                 pl.BlockSpec((1,) + w_in.shape[1:], lambda i: (layer, 0, 0), pipeline_mode=pl.Buffered(1)),
                 const(w_misc), tab, tab] + [const(a) for a in params]
                + ([seq(W_GRP)] if has_vfirst else []))
    return pl.pallas_call(
        functools.partial(_inproj_body, apply_ln, has_vfirst, tps),
        grid=(n // TM,),
        in_specs=in_specs, out_specs=out_specs, out_shape=out_shapes,
        scratch_shapes=[pltpu.VMEM((d, IN_SPLITS[-1][1]), BF16),
                        pltpu.VMEM((TM + CONV_HALO, W_GRP), F32), pltpu.VMEM((TM + CONV_HALO, W_GRP), F32),
                        pltpu.VMEM((TM + SHIFT_HALO, 4 * W_GRP), F32), pltpu.VMEM((TM + SHIFT_HALO, LANES), F32)],
        compiler_params=_cparams("arbitrary"), name="in_proj",
    )(*inputs)


CONV_HALO = 32


def _carry_rows(buf, halo, first):
    tail = buf.shape[0] - halo

    @pl.when(first)
    def _():
        buf[0:halo, :] = jnp.zeros((halo, buf.shape[1]), buf.dtype)

    @pl.when(jnp.logical_not(first))
    def _():
        buf[0:halo, :] = buf[tail:tail + halo, :]


def _conv_math(cur, first, w_ref, b_ref, gg_ref, gb_ref, avg_ref, o_ref, hbuf, shifted):
    ts = cur.shape[0]
    h = cur[:, :W_GRP] * jax.nn.sigmoid(cur[:, W_GRP:])
    _carry_rows(hbuf, CONV_HALO, first)
    hbuf[CONV_HALO:, :] = h
    acc = jnp.zeros((ts, W_GRP), F32)
    base = CONV_HALO - (CONV_WIDTH - 1)
    for phase in range(SUB):
        taps = [j for j in range(CONV_WIDTH) if (base + j) % SUB == phase]
        span = (base + taps[-1]) // SUB * SUB + ts
        shifted[0:span, :] = hbuf[phase:phase + span, :]
        for j in taps:
            lo = (base + j) // SUB * SUB
            acc = acc + w_ref[j:j + 1, :] * shifted[lo:lo + ts, :]
    acc = acc + b_ref[...]
    avg = avg_ref[...].astype(BF16)
    mu = _dot_hilo(acc, avg)
    xc = acc - mu
    var = _dot_hilo(xc * xc, avg)
    y = xc * lax.rsqrt(var + LN_EPS) * gg_ref[...] + gb_ref[...]
    o_ref[...] = y * jax.nn.sigmoid(y)


SHIFT_HALO = 8


def _shift_prev(buf, cur, first):
    ts = cur.shape[0]
    _carry_rows(buf, SHIFT_HALO, first)
    buf[SHIFT_HALO:, :] = cur
    return buf[SHIFT_HALO - 1:SHIFT_HALO - 1 + ts, :]


def _rwkv_prep_math(cur, misc, first, vf_ref, params, outs, buf, mbuf):
    mu_ref, w0_ref, a0_ref, wup_ref, aup_ref, gup_ref, kk_ref, ka_ref, ones_ref = params[:9]
    r_o, lw_o, k_o, v_o, a_o, b_o, g_o = outs
    prev = _shift_prev(buf, cur, first)
    y = cur + mu_ref[...] * (prev - cur)
    r = y[:, 0:256]
    k = y[:, 256:512]
    v = y[:, 512:768]
    lora = y[:, 768:1024]
    w = w0_ref[...] + _dot3(jnp.tanh(lora), wup_ref[...])
    a = jax.nn.sigmoid(a0_ref[...] + _dot3(lora, aup_ref[...]))
    g = _dot3(jax.nn.sigmoid(lora), gup_ref[...])
    z = -w
    sp = jnp.maximum(z, 0.0) + jnp.log(1.0 + jnp.exp(-jnp.abs(z)))
    lw = -jnp.exp(-sp - 0.5)
    if vf_ref is not None:
        vmu_ref, v0_ref, vup_ref = params[9:]
        mprev = _shift_prev(mbuf, misc, first)
        vd = misc + vmu_ref[...] * (mprev - misc)
        v_mix = jax.nn.sigmoid(v0_ref[...] + _dot3(vd, vup_ref[...]))
        v = v + (vf_ref[0] - v) * v_mix
    kk = k * kk_ref[...]
    n2 = _dot_hilo(kk * kk, ones_ref[...].astype(BF16))
    kk = kk / jnp.maximum(jnp.sqrt(n2), 1e-12)
    k2 = k * (1.0 + (a - 1.0) * ka_ref[...])
    r_o[0] = r
    lw_o[0] = lw
    k_o[0] = k2
    v_o[0] = v
    a_o[0] = -kk
    b_o[0] = kk * a
    g_o[0] = g


def _block_diag(x, headmask):
    return jnp.concatenate([x] * N_HEADS, axis=0) * headmask


def _wkv_body(ts, r_ref, lw_ref, k_ref, v_ref, a_ref, b_ref, g_ref, rk_ref, gg_ref, gb_ref,
              avg_ref, ones_ref, o_ref, st_ref):
    C = WKV_CHUNK
    n = N_HEADS * C

    @pl.when(pl.program_id(1) == 0)
    def _():
        st_ref[...] = jnp.zeros_like(st_ref)

    ri = lax.broadcasted_iota(jnp.int32, (n, n), 0)
    ci = lax.broadcasted_iota(jnp.int32, (n, n), 1)
    head_f = jnp.where((ri // C) == (ci // HEAD_DIM), 1.0, 0.0)
    head_b = head_f.astype(BF16)
    lag = jnp.where((ri // C) == (ci // C), (ri % C) - (ci % C), -1)
    strict = lag > 0
    incl = lag >= 0
    eye = ri == ci
    eye_f = jnp.where(eye, 1.0, 0.0)
    tr = lax.broadcasted_iota(jnp.int32, (C, C), 0)
    tc = lax.broadcasted_iota(jnp.int32, (C, C), 1)
    tri = jnp.where(tc <= tr, 1.0, 0.0).astype(F32)
    cast = lambda x: x.astype(BF16)
    G = ts // C
    split = lambda ref: ref[0].reshape(G, C, W_GRP)
    tile_heads = lambda x: jnp.concatenate([x] * N_HEADS, axis=1)
    bd16 = lambda x: tile_heads(cast(x)) * head_b[None]
    bmm = lambda x, y: lax.dot_general(x, y, (((2,), (1,)), ((0,), (0,))), preferred_element_type=F32)
    bmm_nt = lambda x, y: lax.dot_general(x, y, (((2,), (2,)), ((0,), (0,))), preferred_element_type=F32)

    lw = split(lw_ref)
    cum = lax.dot_general(jnp.broadcast_to(tri[None], (G, C, C)), lw, (((2,), (1,)), ((0,), (0,))),
                          preferred_element_type=F32, precision=HI)
    cum_c = cum[:, C - 1:C, :]
    e_in = jnp.exp(cum)
    e_neg = jnp.exp(-cum)
    e_tail = jnp.exp(cum_c - cum)
    b_c = split(b_ref)
    k_c = split(k_ref)
    a_t = bd16(split(a_ref) * jnp.exp(cum - lw))
    r_t = bd16(split(r_ref) * e_in)
    b_t = bd16(b_c * e_neg)
    k_t = bd16(k_c * e_neg)
    v_bd = bd16(split(v_ref))
    bh_t = cast(jnp.swapaxes(tile_heads(b_c * e_tail) * head_f[None], 1, 2))
    kh_t = cast(jnp.swapaxes(tile_heads(k_c * e_tail) * head_f[None], 1, 2))
    a_ab = jnp.where(strict[None], bmm_nt(a_t, b_t), 0.0)
    a_ak = cast(jnp.where(strict[None], bmm_nt(a_t, k_t), 0.0))
    a_rb = cast(jnp.where(incl[None], bmm_nt(r_t, b_t), 0.0))
    a_rk = cast(jnp.where(incl[None], bmm_nt(r_t, k_t), 0.0))
    t_inv = eye_f[None] + a_ab
    pw = cast(a_ab)
    for _ in range(int(math.log2(C)) - 1):
        pw = cast(bmm(pw, pw))
        t_inv = t_inv + bmm(cast(t_inv), pw)
    t16 = cast(t_inv)
    ta = cast(bmm(t16, a_t))
    u0 = bmm(cast(bmm(t16, a_ak)), v_bd)
    o0 = bmm(a_rk, v_bd)
    s0 = bmm(kh_t, v_bd)
    o_lhs = jnp.concatenate([r_t, a_rb], axis=2)
    w_col = jnp.sum(jnp.where(eye[None], jnp.exp(cum_c), 0.0), axis=2, keepdims=True)

    st = st_ref[...]
    outs = []
    for g in range(G):
        st16 = cast(st)
        u = cast(_dot(ta[g], st16) + u0[g])
        o_bd = _dot(o_lhs[g], jnp.concatenate([st16, u], axis=0)) + o0[g]
        st = w_col[g] * st + _dot(bh_t[g], u) + s0[g]
        outs.append(o_bd[0:C] + o_bd[C:2 * C] + o_bd[2 * C:3 * C] + o_bd[3 * C:4 * C])
    st_ref[...] = st

    o = jnp.concatenate(outs, axis=0)
    avg = avg_ref[...].astype(BF16)
    mu = _dot_hilo(o, avg)
    xc = o - mu
    var = _dot_hilo(xc * xc, avg)
    on = xc * lax.rsqrt(var + RWKV_GN_EPS) * gg_ref[...] + gb_ref[...]
    r = r_ref[0]
    k = k_ref[0]
    v = v_ref[0]
    bonus = _dot_hilo(r * k * rk_ref[...], ones_ref[...].astype(BF16)) * v
    o_ref[0] = (on + bonus) * g_ref[0]


def wkv_scan(r, lw, k, v, a, b, g, rk, gg, gb, avg, ones):
    bsz, s, _ = r.shape
    ts = WKV_GROUP * WKV_CHUNK
    tile = pl.BlockSpec((1, ts, W_GRP), lambda bi, i: (bi, i, 0))
    const = lambda arr: pl.BlockSpec(arr.shape, lambda bi, i: (0,) * arr.ndim)
    return pl.pallas_call(
        functools.partial(_wkv_body, ts),
        grid=(bsz, s // ts),
        in_specs=[tile] * 7 + [const(x) for x in (rk, gg, gb, avg, ones)],
        out_specs=tile, out_shape=jax.ShapeDtypeStruct((bsz, s, W_GRP), F32),
        scratch_shapes=[pltpu.VMEM((N_HEADS * WKV_CHUNK, W_GRP), F32)],
        compiler_params=_cparams("parallel", "arbitrary"), name="wkv_scan",
    )(r, lw, k, v, a, b, g, rk, gg, gb, avg, ones)


def _gmlp_math(x, g_ref, b_ref, ws_ref, bs_ref, o_ref):
    ts = x.shape[0]
    u = jax.nn.gelu(x[:, :W_GRP])
    v = _ln(jax.nn.gelu(x[:, W_GRP:]), g_ref[...], b_ref[...]).astype(BF16)
    tr = lax.broadcasted_iota(jnp.int32, (GMLP_CHUNK, GMLP_CHUNK), 0)
    tc = lax.broadcasted_iota(jnp.int32, (GMLP_CHUNK, GMLP_CHUNK), 1)
    ws = [jnp.where(tc <= tr, ws_ref[h], 0.0).astype(BF16) for h in range(N_HEADS)]
    for c in range(ts // GMLP_CHUNK):
        rows = slice(c * GMLP_CHUNK, (c + 1) * GMLP_CHUNK)
        mixed = [_dot(ws[h], v[rows, h * HEAD_DIM:(h + 1) * HEAD_DIM]) + bs_ref[h]
                 for h in range(N_HEADS)]
        o_ref[rows, :] = u[rows, :] * jnp.concatenate(mixed, axis=-1)


def _swap_halves(x, lane):
    w = x.shape[-1]
    half = HEAD_DIM // 2
    fwd = pltpu.roll(x, w - half, 1)
    bwd = pltpu.roll(x, half, 1)
    return jnp.where((lane % HEAD_DIM) < half, fwd, bwd)


def _nsa_prep_math(q, kv, misc, pos0, cos_ref, sin_ref,
                   q_o, qr_o, kc_o, vc_o, ks_o, vs_o, kw_o, vw_o, g_o):
    scale = HEAD_DIM ** -0.5
    cos = jnp.concatenate([cos_ref[...]] * 3, axis=1)
    sin = jnp.concatenate([sin_ref[...]] * 3, axis=1)
    lane_q = lax.broadcasted_iota(jnp.int32, q.shape, 1)
    q_rot = q * cos[:, :W_GRP] + _swap_halves(q, lane_q) * sin[:, :W_GRP]
    lane_kv = lax.broadcasted_iota(jnp.int32, kv.shape, 1)
    kv_rot = kv * cos + _swap_halves(kv, lane_kv) * sin
    q_o[0] = (q * scale).astype(BF16)
    qr_o[0] = (q_rot * (scale * LOG2E)).astype(BF16)
    kc_o[0] = kv[:, 0:64]
    vc_o[0] = kv[:, 64:128]
    ts = kv.shape[0]
    lane = lax.broadcasted_iota(jnp.int32, (ts, LANES), 1)
    key_blk = (pos0 + lax.broadcasted_iota(jnp.int32, (ts, 1), 0)) // SEL_LEN
    ks_o[0] = jnp.concatenate([jnp.where(lane < HEAD_DIM, kv_rot[:, 128:256], 0.0),
                               jnp.where(lane == key_blk, NEG, 0.0)], axis=1).astype(BF16)
    ones_col = jnp.where(lane == HEAD_DIM, 1.0, 0.0)
    vs_first = pltpu.roll(kv[:, 128:256], HEAD_DIM, 1)
    vs_o[0] = jnp.where(lane < HEAD_DIM, vs_first, ones_col).T.astype(BF16)
    kw_o[0] = kv_rot[:, 256:320].astype(BF16)
    vw_first = pltpu.roll(kv[:, 256:384], HEAD_DIM, 1)
    vw_o[0] = jnp.where(lane < HEAD_DIM, vw_first, ones_col).astype(BF16)
    g_o[0] = jax.nn.sigmoid(misc)


def _compress_body(kc_ref, vc_ref, pe_ref, w1_ref, w2_ref, ko_ref, vo_ref):
    half = CMP_STRIDE * HEAD_DIM
    for j, (c_ref, o_ref) in enumerate(((kc_ref, ko_ref), (vc_ref, vo_ref))):
        c = c_ref[0].astype(BF16)
        w1 = w1_ref[j]
        lo = _dot(c, w1[:half])
        hi = _dot(c, w1[half:])
        nb = hi.shape[0]
        hi_next = pltpu.roll(hi, nb - 1, 0)
        pe = jnp.broadcast_to(pe_ref[j], (8, 2 * half))
        pe_term = _dot(pe, w1.astype(F32), HI)[0:1]
        h = jax.nn.gelu(lo + hi_next + pe_term)
        o_ref[0] = _dot(h.astype(BF16), w2_ref[j]).astype(BF16)


def nsa_compress(kc_r, vc_r, pe, w1, w2):
    bsz, nb, w = kc_r.shape
    blk = pl.BlockSpec((1, nb, w), lambda bi: (bi, 0, 0))
    const = lambda a: pl.BlockSpec(a.shape, lambda bi: (0,) * a.ndim)
    out = jax.ShapeDtypeStruct((bsz, nb, HEAD_DIM), BF16)
    ospec = pl.BlockSpec((1, nb, HEAD_DIM), lambda bi: (bi, 0, 0))
    return pl.pallas_call(
        _compress_body, grid=(bsz,),
        in_specs=[blk, blk, const(pe), const(w1), const(w2)],
        out_specs=[ospec, ospec], out_shape=[out, out],
        compiler_params=_cparams("parallel"), name="nsa_compress",
    )(kc_r, vc_r, pe, w1, w2)


def _stack_heads(x):
    return jnp.concatenate([x[:, h * HEAD_DIM:(h + 1) * HEAD_DIM] for h in range(N_HEADS)], axis=0)


def _nsa_attn_body(n_top, q_ref, qr_ref, kc_ref, vc_ref, ks_ref, vst_ref, kw_ref, vw_ref, g_ref,
                   ov_ref, o_ref):
    T = Q_BLOCK
    i = pl.program_id(1)
    t0 = i * T
    qs = _stack_heads(q_ref[0])
    qrs = _stack_heads(qr_ref[0])
    pos = t0 + lax.broadcasted_iota(jnp.int32, (T, 1), 0)
    tile4 = lambda x: jnp.concatenate([x] * N_HEADS, axis=0)

    kc = kc_ref[0]
    n_cmp = kc.shape[0]
    cmp_end = lax.broadcasted_iota(jnp.int32, (1, n_cmp), 1) * CMP_STRIDE + (CMP_LEN - 1)
    valid_c = tile4(jnp.where(cmp_end <= pos, 1.0, 0.0))
    s_c = jnp.where(valid_c > 0.5, _dot_nt(qs, kc), NEG)
    p_c = jnp.exp(s_c - jnp.max(s_c, axis=-1, keepdims=True)) * valid_c
    p_c = p_c / jnp.maximum(jnp.sum(p_c, axis=-1, keepdims=True), 1e-20)
    o_c = _dot(p_c.astype(BF16), vc_ref[0])
    p_sum = p_c[0:T] + p_c[T:2 * T] + p_c[2 * T:3 * T] + p_c[3 * T:4 * T]
    ps_hi, ps_lo = _split_bf16(p_sum)
    ov = ov_ref[...].astype(BF16)
    imp_t = _dot_nt(ov, ps_hi) + _dot_nt(ov, ps_lo)

    WK = WINDOW + T
    start = pl.multiple_of(jnp.maximum(t0 - WINDOW, 0), T)
    kw = kw_ref[0, pl.ds(start, WK), :]
    vw = vw_ref[0, pl.ds(start, WK), :]
    kpos_w = start + lax.broadcasted_iota(jnp.int32, (1, WK), 1)
    bias_w = tile4(jnp.where(kpos_w <= pos, jnp.where(kpos_w > pos - WINDOW, 0.0, NEG), NEG))
    s_w = _dot_nt(qrs, kw) + bias_w
    p_w = jnp.exp2(s_w - jnp.max(s_w, axis=-1, keepdims=True))
    acc_w = _dot(p_w.astype(BF16), vw)
    o_w = acc_w[:, :HEAD_DIM] / acc_w[:, HEAD_DIM:HEAD_DIM + 1]

    n_sel = imp_t.shape[0]
    jj = lax.broadcasted_iota(jnp.int32, (n_sel, T), 0)
    blk = (t0 + lax.broadcasted_iota(jnp.int32, (1, T), 1)) // SEL_LEN
    val = jnp.where(jj == blk, 3e38, jnp.where(jj == 0, 3e38, jnp.where(jj <= blk, imp_t, -1.0)))
    sel_t = jnp.zeros((n_sel, T), F32)
    for _ in range(n_top):
        mx = jnp.max(val, axis=0, keepdims=True)
        idx = jnp.min(jnp.where(val == mx, jj, n_sel), axis=0, keepdims=True)
        hit = jj == idx
        sel_t = jnp.where(hit, 1.0, sel_t)
        val = jnp.where(hit, -2.0, val)
    sel = sel_t.T

    KT = SEL_KT
    unsel = tile4((1.0 - sel).astype(BF16))
    q_aug = jnp.concatenate([qrs, jnp.zeros((N_HEADS * T, HEAD_DIM), BF16), unsel], axis=1)
    krow = lax.broadcasted_iota(jnp.int32, (KT, 1), 0)
    pos_row = t0 + lax.broadcasted_iota(jnp.int32, (1, T), 1)

    KS = KT // SEL_SUB

    def sel_tile(jt, carry, diagonal):
        m, acc = carry
        k0 = pl.multiple_of(jt * KT, KT)

        def scores(j):
            ks = pl.multiple_of(k0 + j * KS, KS)
            s_t = _dot_nt(ks_ref[0, pl.ds(ks, KS), :], q_aug)
            if diagonal:
                causal = jnp.where(ks + krow[:KS] <= pos_row, 0.0, NEG)
                s_t = s_t + jnp.concatenate([causal] * N_HEADS, axis=1)
            return s_t

        s_next = scores(0)
        for j in range(SEL_SUB):
            s_t = s_next
            if j + 1 < SEL_SUB:
                s_next = scores(j + 1)
            m_new = jnp.maximum(m, jnp.max(s_t, axis=0, keepdims=True))
            p_t = jnp.exp2(s_t - m_new)
            ks = pl.multiple_of(k0 + j * KS, KS)
            acc = jnp.exp2(m - m_new) * acc + _dot(vst_ref[0, :, pl.ds(ks, KS)], p_t.astype(BF16))
            m = m_new
        return m, acc

    init = (jnp.full((1, N_HEADS * T), NEG, F32), jnp.zeros((LANES, N_HEADS * T), F32))
    n_full = t0 // KT
    carry = lax.fori_loop(0, n_full, functools.partial(sel_tile, diagonal=False), init)
    _, acc_t = sel_tile(n_full, carry, True)
    acc_s = acc_t.T
    o_s = acc_s[:, :HEAD_DIM] / acc_s[:, HEAD_DIM:HEAD_DIM + 1]

    g = g_ref[0]
    outs = []
    for h in range(N_HEADS):
        rows = slice(h * T, (h + 1) * T)
        outs.append(g[:, 3 * h:3 * h + 1] * o_c[rows] + g[:, 3 * h + 1:3 * h + 2] * o_s[rows]
                    + g[:, 3 * h + 2:3 * h + 3] * o_w[rows])
    o_ref[0] = jnp.concatenate(outs, axis=-1)


def nsa_attention(q, qr, k_cmp, v_cmp, ks, vs, kw, vw, gates, overlap):
    bsz, s, _ = q.shape
    n_top = min(SEL_TOPN, s // SEL_LEN)
    n_cmp = k_cmp.shape[1]
    qtile = lambda w: pl.BlockSpec((1, Q_BLOCK, w), lambda bi, i: (bi, i, 0))
    full = lambda rows, w: pl.BlockSpec((1, rows, w), lambda bi, i: (bi, 0, 0))
    return pl.pallas_call(
        functools.partial(_nsa_attn_body, n_top),
        grid=(bsz, s // Q_BLOCK),
        in_specs=[qtile(W_GRP), qtile(W_GRP), full(n_cmp, HEAD_DIM), full(n_cmp, HEAD_DIM),
                  full(s, 2 * LANES), full(LANES, s), full(s, HEAD_DIM), full(s, LANES),
                  qtile(LANES), pl.BlockSpec(overlap.shape, lambda bi, i: (0, 0))],
        out_specs=qtile(W_GRP), out_shape=jax.ShapeDtypeStruct((bsz, s, W_GRP), F32),
        compiler_params=_cparams("parallel", "parallel"), name="nsa_attention",
    )(q, qr, k_cmp, v_cmp, ks, vs, kw, vw, gates, overlap)


def _outproj_math(alpha, parts, x, w_ref, g_ref, b_ref):
    y = alpha * x
    for j, part in enumerate(parts):
        y = y + _dot(part[...].astype(BF16), w_ref[j * W_GRP:(j + 1) * W_GRP, :])
    return _ln(y, g_ref[...], b_ref[...])


def _memkv_body(m_ref, wk_ref, wv_ref, k_o, v_o):
    mb = m_ref[...].astype(BF16)
    k_o[...] = _dot(mb, wk_ref[...]).astype(BF16)
    v_o[...] = _dot(mb, wv_ref[...]).astype(BF16)


def mem_kv(mem2d, wk, wv):
    n, d = mem2d.shape
    full = lambda a: pl.BlockSpec(a.shape, lambda i: (0, 0))
    out = jax.ShapeDtypeStruct((n, d), BF16)
    return pl.pallas_call(
        _memkv_body, grid=(1,),
        in_specs=[full(mem2d), full(wk), full(wv)],
        out_specs=[pl.BlockSpec((n, d), lambda i: (0, 0))] * 2, out_shape=[out, out],
        compiler_params=_cparams("arbitrary"), name="mem_kv",
    )(mem2d, wk, wv)


def _xattn_body(alpha, a_ref, b_ref, c_ref, d_ref, x_ref, w1_ref, g1_ref, b1_ref,
                k_ref, v_ref, wq_ref, wo_ref, g_ref, b_ref_, o_ref):
    x = _outproj_math(alpha, (a_ref, b_ref, c_ref, d_ref), x_ref[...], w1_ref, g1_ref, b1_ref)
    d = x.shape[-1]
    hd = d // N_MEM_HEADS
    q = (_dot(x.astype(BF16), wq_ref[...]) * (hd ** -0.5 * LOG2E)).astype(BF16)
    k = k_ref[0]
    v = v_ref[0]
    cols = [slice(h * hd, (h + 1) * hd) for h in range(N_MEM_HEADS)]
    scores = [_dot_nt(q[:, cs], k[:, cs]) for cs in cols]
    probs = []
    for s in scores:
        p = jnp.exp2(s - jnp.max(s, axis=-1, keepdims=True))
        probs.append((p / jnp.sum(p, axis=-1, keepdims=True)).astype(BF16))
    y = alpha * x
    for cs, p in zip(cols, probs):
        y = y + _dot(_dot(p, v[:, cs]).astype(BF16), wo_ref[cs, :])
    o_ref[...] = _ln(y, g_ref[...], b_ref_[...])


def cross_attention(parts, x2d, seq_len, w_out, g1, b1, k, v, wq, wo, g, b, alpha):
    n, d = x2d.shape
    bsz, s = n // seq_len, seq_len
    tps = s // TM
    m = k.shape[1]
    const = lambda a: pl.BlockSpec(a.shape, lambda bi, i: (0,) * a.ndim)
    row = lambda w: pl.BlockSpec((TM, w), lambda bi, i: (bi * tps + i, 0))
    return pl.pallas_call(
        functools.partial(_xattn_body, alpha),
        grid=(bsz, s // TM),
        in_specs=[row(W_GRP)] * 4 + [row(d), const(w_out), const(g1), const(b1),
                  pl.BlockSpec((1, m, d), lambda bi, i: (bi, 0, 0)),
                  pl.BlockSpec((1, m, d), lambda bi, i: (bi, 0, 0)),
                  const(wq), const(wo), const(g), const(b)],
        out_specs=row(d), out_shape=jax.ShapeDtypeStruct((n, d), F32),
        compiler_params=_cparams("parallel", "parallel"), name="cross_attention",
    )(*parts, x2d, w_out, g1, b1, k, v, wq, wo, g, b)


PER_GRP = N_EXPERTS // N_EXPERT_GROUPS
LPOS_LANE = PER_GRP
SUB = 8
BF16_ROWS = 16


def _router_body(x_ref, rwt_ref, rb_ref, upper_ref, tok_o, lrow_o, cnt_o):
    tm = x_ref.shape[0]
    logits = _dot3(x_ref[...], rwt_ref[...]).T[:N_EXPERTS]
    ex = jnp.exp(logits - jnp.max(logits, axis=0, keepdims=True))
    probs = ex / jnp.sum(ex, axis=0, keepdims=True)
    sel = probs + rb_ref[...]
    srow = [sel[e:e + 1] for e in range(N_EXPERTS)]
    prow = [probs[e:e + 1] for e in range(N_EXPERTS)]
    gscore = []
    for g in range(N_EXPERT_GROUPS):
        r = srow[g * PER_GRP:(g + 1) * PER_GRP]
        best = None
        for a in range(PER_GRP):
            for b in range(a + 1, PER_GRP):
                best = r[a] + r[b] if best is None else jnp.maximum(best, r[a] + r[b])
        gscore.append(best)
    g_idx = jnp.zeros((1, tm), jnp.int32)
    top = gscore[0]
    for g in range(1, N_EXPERT_GROUPS):
        better = gscore[g] > top
        g_idx = jnp.where(better, g, g_idx)
        top = jnp.where(better, gscore[g], top)

    def of_group(rows, e):
        out = rows[e]
        for g in range(1, N_EXPERT_GROUPS):
            out = jnp.where(g_idx == g, rows[g * PER_GRP + e], out)
        return out

    sg = [of_group(srow, e) for e in range(PER_GRP)]
    pg = [of_group(prow, e) for e in range(PER_GRP)]
    w = []
    for e in range(PER_GRP):
        rank = jnp.zeros((1, tm), F32)
        for o in range(PER_GRP):
            if o != e:
                ahead = (sg[o] >= sg[e]) if o < e else (sg[o] > sg[e])
                rank = rank + jnp.where(ahead, 1.0, 0.0)
        w.append(jnp.where(rank < 1.5, pg[e], 0.0))
    w_sum = w[0] + w[1] + w[2] + w[3]
    row8 = lax.broadcasted_iota(jnp.int32, (SUB, tm), 0)
    onehot = jnp.where(row8 == g_idx, 1.0, 0.0)
    before = _dot(onehot.astype(BF16), upper_ref[...])
    cnt = jnp.sum(onehot, axis=1, keepdims=True)
    offs = [jnp.zeros((1, 1), F32)]
    for g in range(1, N_EXPERT_GROUPS):
        offs.append(offs[-1] + cnt[g - 1:g])
    lpos = jnp.zeros((1, tm), F32)
    for g in range(N_EXPERT_GROUPS):
        lpos = lpos + onehot[g:g + 1] * (offs[g] + before[g:g + 1])
    tok = jnp.zeros((SUB, tm), F32)
    for e in range(PER_GRP):
        tok = jnp.where(row8 == e, w[e] / w_sum, tok)
    tok = jnp.where(row8 == LPOS_LANE, lpos, tok)
    tok_o[...] = jnp.concatenate([tok, jnp.zeros((LANES - SUB, tm), F32)], axis=0).T
    lrow_o[0] = lpos.astype(jnp.int32)
    rowc = lax.broadcasted_iota(jnp.int32, (SUB, LANES), 0)
    stats = jnp.zeros((SUB, LANES), F32)
    for g in range(N_EXPERT_GROUPS):
        stats = jnp.where(rowc == g, cnt[g:g + 1], stats)
        stats = jnp.where(rowc == N_EXPERT_GROUPS + g, offs[g], stats)
    cnt_o[0] = stats.astype(jnp.int32)


def moe_router(x2d, rwt, rb_col, upper):
    n, d = x2d.shape
    tm = upper.shape[0]
    nt = n // tm
    const = lambda a: pl.BlockSpec(a.shape, lambda i: (0, 0))
    return pl.pallas_call(
        _router_body, grid=(nt,),
        in_specs=[pl.BlockSpec((tm, d), lambda i: (i, 0)), const(rwt), const(rb_col), const(upper)],
        out_specs=[pl.BlockSpec((tm, LANES), lambda i: (i, 0)),
                   pl.BlockSpec((1, 1, tm), lambda i: (i, 0, 0)),
                   pl.BlockSpec((1, SUB, LANES), lambda i: (i, 0, 0))],
        out_shape=[jax.ShapeDtypeStruct((n, LANES), F32), jax.ShapeDtypeStruct((nt, 1, tm), jnp.int32),
                   jax.ShapeDtypeStruct((nt, SUB, LANES), jnp.int32)],
        compiler_params=_cparams("parallel"), name="moe_router",
    )(x2d, rwt, rb_col, upper)


def _split_bf16(x):
    hi = x.astype(BF16)
    return hi, (x - hi.astype(F32)).astype(BF16)


def _moe_body(alpha, offs_ref, cnts_ref, x_ref, tok_ref, lrow_ref, wg_ref, wu_ref, wd_ref, g_ref, b_ref, o_ref,
              xs_ref, gs_ref, acc_ref):
    i = pl.program_id(0)
    step = pl.program_id(1)
    tm = x_ref.shape[0]
    eps = wd_ref.shape[1]

    @pl.when(step == 0)
    def _():
        slot = lax.broadcasted_iota(jnp.int32, (tm, tm), 0)
        perm = jnp.where(slot == lrow_ref[0], 1.0, 0.0).astype(BF16)
        xs_ref[0:tm, :] = _dot(perm, x_ref[...].astype(BF16)).astype(BF16)
        t_hi, t_lo = _split_bf16(tok_ref[...])
        gs_ref[0:tm, :] = _dot(perm, t_hi) + _dot(perm, t_lo)
        xs_ref[tm:, :] = jnp.zeros((MOE_BLK, xs_ref.shape[1]), BF16)
        gs_ref[tm:, :] = jnp.zeros((MOE_BLK, LANES), F32)
        acc_ref[...] = jnp.zeros_like(acc_ref)

    grp = (step * eps) // PER_GRP
    off = offs_ref[i * N_EXPERT_GROUPS + grp]
    end = off + cnts_ref[i * N_EXPERT_GROUPS + grp]
    lane = lax.broadcasted_iota(jnp.int32, (MOE_BLK, LANES), 1)
    first = off // BF16_ROWS * BF16_ROWS
    for s in range(-(-(tm + BF16_ROWS - 1) // MOE_BLK)):
        lo = pl.multiple_of(first + s * MOE_BLK, BF16_ROWS)

        @pl.when(lo < end)
        def _(lo=lo):
            rows = pl.ds(lo, MOE_BLK)
            xs = xs_ref[rows, :]
            ridx = lo + lax.broadcasted_iota(jnp.int32, (MOE_BLK, 1), 0)
            in_run = jnp.where(ridx >= off, jnp.where(ridx < end, 1.0, 0.0), 0.0)
            gs = gs_ref[rows, :]
            y = jnp.zeros((MOE_BLK, x_ref.shape[1]), F32)
            for k in range(eps):
                gt = _dot(xs, wg_ref[0, k])
                h = (gt * jax.nn.sigmoid(gt) * _dot(xs, wu_ref[0, k])).astype(BF16)
                in_grp = (step * eps + k) % PER_GRP
                gcol = jnp.sum(jnp.where(lane == in_grp, gs, 0.0), axis=-1, keepdims=True)
                y = y + (gcol * in_run) * _dot(h, wd_ref[0, k])
            acc_ref[rows, :] += y

    @pl.when(step == pl.num_programs(1) - 1)
    def _():
        lcol = tok_ref[:, LPOS_LANE:LPOS_LANE + 1].astype(jnp.int32)
        slot = lax.broadcasted_iota(jnp.int32, (tm, tm), 1)
        unperm = jnp.where(slot == lcol, 1.0, 0.0).astype(BF16)
        y = _dot(unperm, acc_ref[0:tm, :].astype(BF16))
        o_ref[...] = _ln(alpha * x_ref[...] + y, g_ref[...], b_ref[...])


def moe_ffn(x2d, tok, lrow, offs, cnts, layer, wg, wu, wd, g, b, alpha):
    n, d = x2d.shape
    _, ne, de, _ = wd.shape
    tm = lrow.shape[2]
    const = lambda a: pl.BlockSpec(a.shape, lambda i, e, o, c: (0,) * a.ndim)
    grid_spec = pltpu.PrefetchScalarGridSpec(
        num_scalar_prefetch=2, grid=(n // tm, ne // MOE_EPS),
        in_specs=[pl.BlockSpec((tm, d), lambda i, e, o, c: (i, 0), pipeline_mode=pl.Buffered(1)),
                  pl.BlockSpec((tm, LANES), lambda i, e, o, c: (i, 0)),
                  pl.BlockSpec((1, 1, tm), lambda i, e, o, c: (i, 0, 0)),
                  pl.BlockSpec((1, MOE_EPS, d, de), lambda i, e, o, c: (layer, e, 0, 0)),
                  pl.BlockSpec((1, MOE_EPS, d, de), lambda i, e, o, c: (layer, e, 0, 0)),
                  pl.BlockSpec((1, MOE_EPS, de, d), lambda i, e, o, c: (layer, e, 0, 0)),
                  const(g), const(b)],
        out_specs=pl.BlockSpec((tm, d), lambda i, e, o, c: (i, 0)),
        scratch_shapes=[pltpu.VMEM((tm + MOE_BLK, d), BF16), pltpu.VMEM((tm + MOE_BLK, LANES), F32),
                        pltpu.VMEM((tm + MOE_BLK, d), F32)])
    return pl.pallas_call(
        functools.partial(_moe_body, alpha), grid_spec=grid_spec,
        out_shape=jax.ShapeDtypeStruct((n, d), F32),
        compiler_params=_cparams("parallel", "arbitrary"), name="moe_ffn",
    )(offs, cnts, x2d, tok, lrow, wg, wu, wd, g, b)


def _rope_tables(s):
    inv = ROPE_THETA ** (-jnp.arange(0, HEAD_DIM, 2, dtype=F32) / HEAD_DIM)
    ang = jnp.arange(s, dtype=F32)[:, None] * inv[None, :]
    cos, sin = jnp.cos(ang), jnp.sin(ang)
    cos_h = jnp.concatenate([cos, cos], axis=-1)
    sin_h = jnp.concatenate([-sin, sin], axis=-1)
    return jnp.tile(cos_h, (1, 2)), jnp.tile(sin_h, (1, 2))


def _overlap_matrix(s, n_cmp_pad):
    n_sel = s // SEL_LEN
    assert n_sel <= LANES
    cmp_start = jnp.arange(n_cmp_pad) * CMP_STRIDE
    sel_start = jnp.arange(LANES) * SEL_LEN
    ov = jnp.clip(jnp.minimum(cmp_start[None, :] + CMP_LEN, sel_start[:, None] + SEL_LEN)
                  - jnp.maximum(cmp_start[None, :], sel_start[:, None]), 0, None).astype(F32) / CMP_LEN
    n_cmp = s // CMP_STRIDE - (CMP_LEN // CMP_STRIDE - 1)
    real = (jnp.arange(n_cmp_pad)[None, :] < n_cmp) & (jnp.arange(LANES)[:, None] < n_sel)
    return jnp.where(real, ov, 0.0)


def _pad_rows(w, lo, total):
    out = jnp.zeros((total, w.shape[1]), w.dtype)
    return out.at[lo:lo + w.shape[0]].set(w)


def kernel(x, mem, ln_in_g, ln_in_b, w_in, w_out, conv_w, conv_b, conv_gn_g, conv_gn_b, rwkv_mu, rwkv_w0, rwkv_w_up, rwkv_a0, rwkv_a_up, rwkv_g_up, rwkv_k_k, rwkv_k_a, rwkv_r_k, rwkv_gn_g, rwkv_gn_b, rwkv_v_down, rwkv_v_mu, rwkv_v0, rwkv_v_up, gmlp_ln_g, gmlp_ln_b, gmlp_w_s, gmlp_b_s, nsa_pe_k, nsa_w1_k, nsa_w2_k, nsa_pe_v, nsa_w1_v, nsa_w2_v, ln1_g, ln1_b, xa_wq, xa_wk, xa_wv, xa_wo, ln2_g, ln2_b, router_w, router_bias, moe_w_gate, moe_w_up, moe_w_down, ln3_g, ln3_b):
    bsz, s, d = x.shape
    depth = w_in.shape[0]
    n = bsz * s
    alpha = (2 * depth) ** 0.25
    row = lambda a: a.reshape(1, -1)

    cos_t, sin_t = _rope_tables(s)
    n_blk = s // CMP_STRIDE
    overlap = _overlap_matrix(s, n_blk)
    avg64 = _group_avg_matrix(W_GRP, HEAD_DIM)
    ones64 = avg64 * HEAD_DIM
    rw_pad = jnp.zeros((d, LANES), F32).at[:, :N_EXPERTS].set(router_w)
    moe_tm = min(MOE_TM, n)
    t_idx = jnp.arange(moe_tm)
    upper = (t_idx[:, None] < t_idx[None, :]).astype(BF16)
    mem2d = mem.reshape(bsz * mem.shape[1], d)

    wg_b, wu_b, wd_b = moe_w_gate.astype(BF16), moe_w_up.astype(BF16), moe_w_down.astype(BF16)

    xs = x.reshape(n, d)
    v_first = None
    for l in range(depth):
        main = IN_SPLITS[-1][1]
        w_misc = jnp.zeros((d, LANES), F32).at[:, :w_in.shape[2] - main].set(w_in[l, :, main:])
        if l > 0:
            w_misc = w_misc.at[:, MISC_VD_OFF:MISC_VD_OFF + rwkv_v_down.shape[2]].set(rwkv_v_down[l - 1])
        gmlp_p = (row(gmlp_ln_g[l]), row(gmlp_ln_b[l]), gmlp_w_s[l], gmlp_b_s[l].reshape(N_HEADS, GMLP_CHUNK, 1))
        conv_p = (conv_w[l], row(conv_b[l]), row(conv_gn_g[l]), row(conv_gn_b[l]), avg64)
        rwkv_p = (row(rwkv_mu[l]), row(rwkv_w0[l]), row(rwkv_a0[l]), _pad_rows(rwkv_w_up[l], 0, W_GRP),
                  _pad_rows(rwkv_a_up[l], 64, W_GRP), _pad_rows(rwkv_g_up[l], 128, W_GRP),
                  row(rwkv_k_k[l]), row(rwkv_k_a[l]), ones64)
        if l > 0:
            vmu = jnp.zeros((1, LANES), F32).at[0, MISC_VD_OFF:MISC_VD_OFF + rwkv_v_mu.shape[1]].set(rwkv_v_mu[l - 1])
            rwkv_p += (vmu, row(rwkv_v0[l - 1]), _pad_rows(rwkv_v_up[l - 1], MISC_VD_OFF, LANES))
        outs = in_proj(xs, s, row(ln_in_g), row(ln_in_b), w_in, l, w_misc.astype(BF16), cos_t, sin_t,
                       conv_p, gmlp_p, rwkv_p, v_first, apply_ln=(l == 0))
        if l == 0:
            xs, outs = outs[0], outs[1:]
        out_a = outs[0]
        r_, lw_, k_, v_, a_, b_, g_ = outs[1:1 + N_RWKV_OUT]
        out_c = outs[1 + N_RWKV_OUT]
        q_b, qr_b, kc, vc, ks, vs, kw, vw, gates = outs[2 + N_RWKV_OUT:]
        if l == 0:
            v_first = v_
        out_b = wkv_scan(r_, lw_, k_, v_, a_, b_, g_, row(rwkv_r_k[l]), row(rwkv_gn_g[l]),
                         row(rwkv_gn_b[l]), avg64, ones64)

        pe = jnp.stack([nsa_pe_k[l].reshape(1, -1), nsa_pe_v[l].reshape(1, -1)])
        w1 = jnp.stack([nsa_w1_k[l], nsa_w1_v[l]]).astype(BF16)
        w2 = jnp.stack([nsa_w2_k[l], nsa_w2_v[l]]).astype(BF16)
        k_cmp, v_cmp = nsa_compress(kc.reshape(bsz, n_blk, CMP_STRIDE * HEAD_DIM),
                                    vc.reshape(bsz, n_blk, CMP_STRIDE * HEAD_DIM), pe, w1, w2)
        out_d = nsa_attention(q_b, qr_b, k_cmp, v_cmp, ks, vs, kw, vw, gates, overlap)

        flat = lambda a: a.reshape(n, W_GRP)
        mk, mv = mem_kv(mem2d, xa_wk[l].astype(BF16), xa_wv[l].astype(BF16))
        m_len = mem.shape[1]
        xs = cross_attention((flat(out_a), flat(out_b), out_c, flat(out_d)), xs, s, w_out[l].astype(BF16),
                             row(ln1_g[l]), row(ln1_b[l]), mk.reshape(bsz, m_len, d), mv.reshape(bsz, m_len, d),
                             xa_wq[l].astype(BF16), xa_wo[l].astype(BF16), row(ln2_g[l]), row(ln2_b[l]), alpha)

        tok, lrow, stats = moe_router(xs, rw_pad, router_bias.reshape(-1, 1), upper)
        cnts = stats[:, :N_EXPERT_GROUPS, 0].reshape(-1)
        offs = stats[:, N_EXPERT_GROUPS:2 * N_EXPERT_GROUPS, 0].reshape(-1)
        xs = moe_ffn(xs, tok, lrow, offs, cnts, l, wg_b, wu_b, wd_b,
                     row(ln3_g[l]), row(ln3_b[l]), alpha)
    return xs.reshape(bsz, s, d)
```

```python
import functools
import math

import jax
import jax.numpy as jnp
from jax import lax
from jax.experimental import pallas as pl
from jax.experimental.pallas import tpu as pltpu

F32 = jnp.float32
BF16 = jnp.bfloat16
HI = lax.Precision.HIGHEST

HEAD_DIM = 64
N_HEADS = 4
W_GRP = 256
CONV_WIDTH = 31
GMLP_CHUNK = 128
CMP_LEN = 32
CMP_STRIDE = 16
SEL_LEN = 64
SEL_TOPN = 16
WINDOW = 512
Q_BLOCK = 256
ROPE_THETA = 10000.0
N_MEM_HEADS = 4
N_EXPERTS = 16
N_EXPERT_GROUPS = 4
LN_EPS = 1e-5
RWKV_GN_EPS = 64e-5
NEG = -1e30
LOG2E = math.log2(math.e)
LANES = 128
WKV_CHUNK = 64
WKV_GROUP = 8

TM = 512
MOE_TM = 1024
_RUN_SD = math.sqrt(MOE_TM * (N_EXPERT_GROUPS - 1)) / N_EXPERT_GROUPS
MOE_BLK = 16 * -(-(MOE_TM // N_EXPERT_GROUPS + int(2 * _RUN_SD) + 15) // 16)
MOE_EPS = 4
SEL_SUB = 2
SEL_KT = 1024
VMEM_LIMIT = 56 * 1024 * 1024


def _cparams(*sem):
    return pltpu.CompilerParams(dimension_semantics=sem, vmem_limit_bytes=VMEM_LIMIT)


def _ln(x, g, b, eps=LN_EPS):
    mu = jnp.mean(x, axis=-1, keepdims=True)
    xc = x - mu
    var = jnp.mean(xc * xc, axis=-1, keepdims=True)
    return xc * lax.rsqrt(var + eps) * g + b


def _dot(a, b, precision=None):
    return jnp.dot(a, b, preferred_element_type=F32, precision=precision)


def _dot_nt(a, b, precision=None):
    return lax.dot_general(a, b, (((1,), (1,)), ((), ())),
                           preferred_element_type=F32, precision=precision)


def _dot_hilo(x, w_bf16):
    hi = x.astype(BF16)
    lo = (x - hi.astype(F32)).astype(BF16)
    return _dot(hi, w_bf16) + _dot(lo, w_bf16)


def _dot3(x, w):
    x_hi = x.astype(BF16)
    x_lo = (x - x_hi.astype(F32)).astype(BF16)
    w_hi = w.astype(BF16)
    w_lo = (w - w_hi.astype(F32)).astype(BF16)
    return _dot(x_hi, w_hi) + _dot(x_lo, w_hi) + _dot(x_hi, w_lo)


def _group_avg_matrix(width, group):
    r = jnp.arange(width)[:, None] // group
    c = jnp.arange(width)[None, :] // group
    return jnp.where(r == c, 1.0 / group, 0.0).astype(F32)


IN_SPLITS = (("conv", 0, 512), ("rwkv", 512, 1536), ("gmlp", 1536, 2048),
             ("q", 2048, 2304), ("kv", 2304, 2688), ("misc", 2688, 2816))
MISC_VD_OFF = 32


N_CONV_P, N_GMLP_P, N_RWKV_P, N_VMIX_P = 5, 4, 9, 3
N_NSA_OUT, N_RWKV_OUT = 9, 7


def _inproj_body(apply_ln, has_vfirst, tiles_per_seq, x_ref, g_ref, b_ref, w_ref, wm_ref, cos_ref, sin_ref, *rest):
    take = lambda k: (rest[:k], rest[k:])
    conv_p, rest = take(N_CONV_P)
    gmlp_p, rest = take(N_GMLP_P)
    rwkv_p, rest = take(N_RWKV_P + (N_VMIX_P if has_vfirst else 0))
    vf_ref = None
    if has_vfirst:
        (vf_ref,), rest = take(1)
    if apply_ln:
        (xln_o,), rest = take(1)
    (conv_o,), rest = take(1)
    rwkv_o, rest = take(N_RWKV_OUT)
    (gmlp_o,), rest = take(1)
    nsa_o, rest = take(N_NSA_OUT)
    wb_ref, hbuf, shifted, buf, mbuf = rest
    main = IN_SPLITS[-1][1]

    @pl.when(pl.program_id(0) == 0)
    def _():
        wb_ref[...] = w_ref[0, :, :main].astype(BF16)

    x = x_ref[...]
    if apply_ln:
        x = _ln(x, g_ref[...], b_ref[...])
        xln_o[...] = x
    xb = x.astype(BF16)
    cols = {name: (lo, hi) for name, lo, hi in IN_SPLITS}
    proj = lambda name: _dot(xb, wb_ref[:, cols[name][0]:cols[name][1]])
    tile_in_seq = pl.program_id(0) % tiles_per_seq
    first = tile_in_seq == 0
    misc = _dot(xb, wm_ref[...])
    _conv_math(proj("conv"), first, *conv_p, conv_o, hbuf, shifted)
    _rwkv_prep_math(proj("rwkv"), misc, first, vf_ref, rwkv_p, rwkv_o, buf, mbuf)
    _gmlp_math(proj("gmlp"), *gmlp_p, gmlp_o)
    _nsa_prep_math(proj("q"), proj("kv"), misc, tile_in_seq * x.shape[0], cos_ref, sin_ref, *nsa_o)


def in_proj(x2d, seq_len, g, b, w_in, layer, w_misc, cos_t, sin_t, conv_p, gmlp_p, rwkv_p, v_first, apply_ln):
    n, d = x2d.shape
    bsz = n // seq_len
    tps = seq_len // TM
    row = lambda w: pl.BlockSpec((TM, w), lambda i: (i, 0))
    const = lambda a: pl.BlockSpec(a.shape, lambda i: (0,) * a.ndim)
    seq = lambda w: pl.BlockSpec((1, TM, w), lambda i: (i // tps, i % tps, 0))
    sd = lambda w, dt: jax.ShapeDtypeStruct((bsz, seq_len, w), dt)
    flat = lambda w: jax.ShapeDtypeStruct((n, w), F32)
    out_shapes = ([flat(W_GRP)] + [sd(W_GRP, F32)] * N_RWKV_OUT + [flat(W_GRP)]
                  + [sd(W_GRP, BF16), sd(W_GRP, BF16), sd(HEAD_DIM, F32), sd(HEAD_DIM, F32),
                     sd(2 * LANES, BF16), jax.ShapeDtypeStruct((bsz, LANES, seq_len), BF16),
                     sd(HEAD_DIM, BF16), jax.ShapeDtypeStruct((bsz, LANES, seq_len), BF16), sd(LANES, F32)])
    out_specs = ([row(W_GRP)] + [seq(W_GRP)] * N_RWKV_OUT + [row(W_GRP)]
                 + [seq(W_GRP), seq(W_GRP), seq(HEAD_DIM), seq(HEAD_DIM), seq(2 * LANES),
                    pl.BlockSpec((1, LANES, TM), lambda i: (i // tps, 0, i % tps)),
                    seq(HEAD_DIM), pl.BlockSpec((1, LANES, TM), lambda i: (i // tps, 0, i % tps)), seq(LANES)])
    if apply_ln:
        out_shapes = [flat(d)] + out_shapes
        out_specs = [row(d)] + out_specs
    has_vfirst = v_first is not None
    assert len(conv_p) == N_CONV_P and len(gmlp_p) == N_GMLP_P
    assert len(rwkv_p) == N_RWKV_P + (N_VMIX_P if has_vfirst else 0)
    tab = pl.BlockSpec((TM, LANES), lambda i: (i % tps, 0))
    params = list(conv_p) + list(gmlp_p) + list(rwkv_p)
    inputs = [x2d, g, b, w_in, w_misc, cos_t, sin_t] + params + ([v_first] if has_vfirst else [])
    in_specs = ([row(d), const(g), const(b),
                 pl.BlockSpec((1,) + w_in.shape[1:], lambda i: (layer, 0, 0), pipeline_mode=pl.Buffered(1)),
                 const(w_misc), tab, tab] + [const(a) for a in params]
                + ([seq(W_GRP)] if has_vfirst else []))
    return pl.pallas_call(
        functools.partial(_inproj_body, apply_ln, has_vfirst, tps),
        grid=(n // TM,),
        in_specs=in_specs, out_specs=out_specs, out_shape=out_shapes,
        scratch_shapes=[pltpu.VMEM((d, IN_SPLITS[-1][1]), BF16),
                        pltpu.VMEM((TM + CONV_HALO, W_GRP), F32), pltpu.VMEM((TM + CONV_HALO, W_GRP), F32),
                        pltpu.VMEM((TM + SHIFT_HALO, 4 * W_GRP), F32), pltpu.VMEM((TM + SHIFT_HALO, LANES), F32)],
        compiler_params=_cparams("arbitrary"), name="in_proj",
    )(*inputs)


CONV_HALO = 32


def _carry_rows(buf, halo, first):
    tail = buf.shape[0] - halo

    @pl.when(first)
    def _():
        buf[0:halo, :] = jnp.zeros((halo, buf.shape[1]), buf.dtype)

    @pl.when(jnp.logical_not(first))
    def _():
        buf[0:halo, :] = buf[tail:tail + halo, :]


def _conv_math(cur, first, w_ref, b_ref, gg_ref, gb_ref, avg_ref, o_ref, hbuf, shifted):
    ts = cur.shape[0]
    h = cur[:, :W_GRP] * jax.nn.sigmoid(cur[:, W_GRP:])
    _carry_rows(hbuf, CONV_HALO, first)
    hbuf[CONV_HALO:, :] = h
    acc = jnp.zeros((ts, W_GRP), F32)
    base = CONV_HALO - (CONV_WIDTH - 1)
    for phase in range(SUB):
        taps = [j for j in range(CONV_WIDTH) if (base + j) % SUB == phase]
        span = (base + taps[-1]) // SUB * SUB + ts
        shifted[0:span, :] = hbuf[phase:phase + span, :]
        for j in taps:
            lo = (base + j) // SUB * SUB
            acc = acc + w_ref[j:j + 1, :] * shifted[lo:lo + ts, :]
    acc = acc + b_ref[...]
    avg = avg_ref[...].astype(BF16)
    mu = _dot_hilo(acc, avg)
    xc = acc - mu
    var = _dot_hilo(xc * xc, avg)
    y = xc * lax.rsqrt(var + LN_EPS) * gg_ref[...] + gb_ref[...]
    o_ref[...] = y * jax.nn.sigmoid(y)


SHIFT_HALO = 8


def _shift_prev(buf, cur, first):
    ts = cur.shape[0]
    _carry_rows(buf, SHIFT_HALO, first)
    buf[SHIFT_HALO:, :] = cur
    return buf[SHIFT_HALO - 1:SHIFT_HALO - 1 + ts, :]


def _rwkv_prep_math(cur, misc, first, vf_ref, params, outs, buf, mbuf):
    mu_ref, w0_ref, a0_ref, wup_ref, aup_ref, gup_ref, kk_ref, ka_ref, ones_ref = params[:9]
    r_o, lw_o, k_o, v_o, a_o, b_o, g_o = outs
    prev = _shift_prev(buf, cur, first)
    y = cur + mu_ref[...] * (prev - cur)
    r = y[:, 0:256]
    k = y[:, 256:512]
    v = y[:, 512:768]
    lora = y[:, 768:1024]
    w = w0_ref[...] + _dot3(jnp.tanh(lora), wup_ref[...])
    a = jax.nn.sigmoid(a0_ref[...] + _dot3(lora, aup_ref[...]))
    g = _dot3(jax.nn.sigmoid(lora), gup_ref[...])
    z = -w
    sp = jnp.maximum(z, 0.0) + jnp.log(1.0 + jnp.exp(-jnp.abs(z)))
    lw = -jnp.exp(-sp - 0.5)
    if vf_ref is not None:
        vmu_ref, v0_ref, vup_ref = params[9:]
        mprev = _shift_prev(mbuf, misc, first)
        vd = misc + vmu_ref[...] * (mprev - misc)
        v_mix = jax.nn.sigmoid(v0_ref[...] + _dot3(vd, vup_ref[...]))
        v = v + (vf_ref[0] - v) * v_mix
    kk = k * kk_ref[...]
    n2 = _dot_hilo(kk * kk, ones_ref[...].astype(BF16))
    kk = kk / jnp.maximum(jnp.sqrt(n2), 1e-12)
    k2 = k * (1.0 + (a - 1.0) * ka_ref[...])
    r_o[0] = r
    lw_o[0] = lw
    k_o[0] = k2
    v_o[0] = v
    a_o[0] = -kk
    b_o[0] = kk * a
    g_o[0] = g


def _block_diag(x, headmask):
    return jnp.concatenate([x] * N_HEADS, axis=0) * headmask


def _wkv_body(ts, r_ref, lw_ref, k_ref, v_ref, a_ref, b_ref, g_ref, rk_ref, gg_ref, gb_ref,
              avg_ref, ones_ref, o_ref, st_ref):
    C = WKV_CHUNK
    n = N_HEADS * C

    @pl.when(pl.program_id(1) == 0)
    def _():
        st_ref[...] = jnp.zeros_like(st_ref)

    ri = lax.broadcasted_iota(jnp.int32, (n, n), 0)
    ci = lax.broadcasted_iota(jnp.int32, (n, n), 1)
    head_f = jnp.where((ri // C) == (ci // HEAD_DIM), 1.0, 0.0)
    head_b = head_f.astype(BF16)
    lag = jnp.where((ri // C) == (ci // C), (ri % C) - (ci % C), -1)
    strict = lag > 0
    incl = lag >= 0
    eye = ri == ci
    eye_f = jnp.where(eye, 1.0, 0.0)
    tr = lax.broadcasted_iota(jnp.int32, (C, C), 0)
    tc = lax.broadcasted_iota(jnp.int32, (C, C), 1)
    tri = jnp.where(tc <= tr, 1.0, 0.0).astype(F32)
    cast = lambda x: x.astype(BF16)
    G = ts // C
    split = lambda ref: ref[0].reshape(G, C, W_GRP)
    tile_heads = lambda x: jnp.concatenate([x] * N_HEADS, axis=1)
    bd16 = lambda x: tile_heads(cast(x)) * head_b[None]
    bmm = lambda x, y: lax.dot_general(x, y, (((2,), (1,)), ((0,), (0,))), preferred_element_type=F32)
    bmm_nt = lambda x, y: lax.dot_general(x, y, (((2,), (2,)), ((0,), (0,))), preferred_element_type=F32)

    lw = split(lw_ref)
    cum = lax.dot_general(jnp.broadcast_to(tri[None], (G, C, C)), lw, (((2,), (1,)), ((0,), (0,))),
                          preferred_element_type=F32, precision=HI)
    cum_c = cum[:, C - 1:C, :]
    e_in = jnp.exp(cum)
    e_neg = jnp.exp(-cum)
    e_tail = jnp.exp(cum_c - cum)
    b_c = split(b_ref)
    k_c = split(k_ref)
    a_t = bd16(split(a_ref) * jnp.exp(cum - lw))
    r_t = bd16(split(r_ref) * e_in)
    b_t = bd16(b_c * e_neg)
    k_t = bd16(k_c * e_neg)
    v_bd = bd16(split(v_ref))
    bh_t = cast(jnp.swapaxes(tile_heads(b_c * e_tail) * head_f[None], 1, 2))
    kh_t = cast(jnp.swapaxes(tile_heads(k_c * e_tail) * head_f[None], 1, 2))
    a_ab = jnp.where(strict[None], bmm_nt(a_t, b_t), 0.0)
    a_ak = cast(jnp.where(strict[None], bmm_nt(a_t, k_t), 0.0))
    a_rb = cast(jnp.where(incl[None], bmm_nt(r_t, b_t), 0.0))
    a_rk = cast(jnp.where(incl[None], bmm_nt(r_t, k_t), 0.0))
    t_inv = eye_f[None] + a_ab
    pw = cast(a_ab)
    for _ in range(int(math.log2(C)) - 1):
        pw = cast(bmm(pw, pw))
        t_inv = t_inv + bmm(cast(t_inv), pw)
    t16 = cast(t_inv)
    ta = cast(bmm(t16, a_t))
    u0 = bmm(cast(bmm(t16, a_ak)), v_bd)
    o0 = bmm(a_rk, v_bd)
    s0 = bmm(kh_t, v_bd)
    o_lhs = jnp.concatenate([r_t, a_rb], axis=2)
    w_col = jnp.sum(jnp.where(eye[None], jnp.exp(cum_c), 0.0), axis=2, keepdims=True)

    st = st_ref[...]
    outs = []
    for g in range(G):
        st16 = cast(st)
        u = cast(_dot(ta[g], st16) + u0[g])
        o_bd = _dot(o_lhs[g], jnp.concatenate([st16, u], axis=0)) + o0[g]
        st = w_col[g] * st + _dot(bh_t[g], u) + s0[g]
        outs.append(o_bd[0:C] + o_bd[C:2 * C] + o_bd[2 * C:3 * C] + o_bd[3 * C:4 * C])
    st_ref[...] = st

    o = jnp.concatenate(outs, axis=0)
    avg = avg_ref[...].astype(BF16)
    mu = _dot_hilo(o, avg)
    xc = o - mu
    var = _dot_hilo(xc * xc, avg)
    on = xc * lax.rsqrt(var + RWKV_GN_EPS) * gg_ref[...] + gb_ref[...]
    r = r_ref[0]
    k = k_ref[0]
    v = v_ref[0]
    bonus = _dot_hilo(r * k * rk_ref[...], ones_ref[...].astype(BF16)) * v
    o_ref[0] = (on + bonus) * g_ref[0]


def wkv_scan(r, lw, k, v, a, b, g, rk, gg, gb, avg, ones):
    bsz, s, _ = r.shape
    ts = WKV_GROUP * WKV_CHUNK
    tile = pl.BlockSpec((1, ts, W_GRP), lambda bi, i: (bi, i, 0))
    const = lambda arr: pl.BlockSpec(arr.shape, lambda bi, i: (0,) * arr.ndim)
    return pl.pallas_call(
        functools.partial(_wkv_body, ts),
        grid=(bsz, s // ts),
        in_specs=[tile] * 7 + [const(x) for x in (rk, gg, gb, avg, ones)],
        out_specs=tile, out_shape=jax.ShapeDtypeStruct((bsz, s, W_GRP), F32),
        scratch_shapes=[pltpu.VMEM((N_HEADS * WKV_CHUNK, W_GRP), F32)],
        compiler_params=_cparams("parallel", "arbitrary"), name="wkv_scan",
    )(r, lw, k, v, a, b, g, rk, gg, gb, avg, ones)


def _gmlp_math(x, g_ref, b_ref, ws_ref, bs_ref, o_ref):
    ts = x.shape[0]
    u = jax.nn.gelu(x[:, :W_GRP])
    v = _ln(jax.nn.gelu(x[:, W_GRP:]), g_ref[...], b_ref[...]).astype(BF16)
    tr = lax.broadcasted_iota(jnp.int32, (GMLP_CHUNK, GMLP_CHUNK), 0)
    tc = lax.broadcasted_iota(jnp.int32, (GMLP_CHUNK, GMLP_CHUNK), 1)
    ws = [jnp.where(tc <= tr, ws_ref[h], 0.0).astype(BF16) for h in range(N_HEADS)]
    for c in range(ts // GMLP_CHUNK):
        rows = slice(c * GMLP_CHUNK, (c + 1) * GMLP_CHUNK)
        mixed = [_dot(ws[h], v[rows, h * HEAD_DIM:(h + 1) * HEAD_DIM]) + bs_ref[h]
                 for h in range(N_HEADS)]
        o_ref[rows, :] = u[rows, :] * jnp.concatenate(mixed, axis=-1)


def _swap_halves(x, lane):
    w = x.shape[-1]
    half = HEAD_DIM // 2
    fwd = pltpu.roll(x, w - half, 1)
    bwd = pltpu.roll(x, half, 1)
    return jnp.where((lane % HEAD_DIM) < half, fwd, bwd)


def _nsa_prep_math(q, kv, misc, pos0, cos_ref, sin_ref,
                   q_o, qr_o, kc_o, vc_o, ks_o, vs_o, kw_o, vw_o, g_o):
    scale = HEAD_DIM ** -0.5
    cos = jnp.concatenate([cos_ref[...]] * 3, axis=1)
    sin = jnp.concatenate([sin_ref[...]] * 3, axis=1)
    lane_q = lax.broadcasted_iota(jnp.int32, q.shape, 1)
    q_rot = q * cos[:, :W_GRP] + _swap_halves(q, lane_q) * sin[:, :W_GRP]
    lane_kv = lax.broadcasted_iota(jnp.int32, kv.shape, 1)
    kv_rot = kv * cos + _swap_halves(kv, lane_kv) * sin
    q_o[0] = (q * scale).astype(BF16)
    qr_o[0] = (q_rot * (scale * LOG2E)).astype(BF16)
    kc_o[0] = kv[:, 0:64]
    vc_o[0] = kv[:, 64:128]
    ts = kv.shape[0]
    lane = lax.broadcasted_iota(jnp.int32, (ts, LANES), 1)
    key_blk = (pos0 + lax.broadcasted_iota(jnp.int32, (ts, 1), 0)) // SEL_LEN
    ks_o[0] = jnp.concatenate([jnp.where(lane < HEAD_DIM, kv_rot[:, 128:256], 0.0),
                               jnp.where(lane == key_blk, NEG, 0.0)], axis=1).astype(BF16)
    ones_col = jnp.where(lane == HEAD_DIM, 1.0, 0.0)
    vs_first = pltpu.roll(kv[:, 128:256], HEAD_DIM, 1)
    vs_o[0] = jnp.where(lane < HEAD_DIM, vs_first, ones_col).T.astype(BF16)
    kw_o[0] = kv_rot[:, 256:320].astype(BF16)
    vw_first = pltpu.roll(kv[:, 256:384], HEAD_DIM, 1)
    vw_o[0] = jnp.where(lane < HEAD_DIM, vw_first, ones_col).T.astype(BF16)
    g_o[0] = jax.nn.sigmoid(misc)


def _compress_body(kc_ref, vc_ref, pe_ref, w1_ref, w2k_ref, w2vt_ref, ko_ref, vo_ref):
    half = CMP_STRIDE * HEAD_DIM
    for j, c_ref in enumerate((kc_ref, vc_ref)):
        c = c_ref[0].astype(BF16)
        w1 = w1_ref[j]
        lo = _dot(c, w1[:half])
        hi = _dot(c, w1[half:])
        nb = hi.shape[0]
        hi_next = pltpu.roll(hi, nb - 1, 0)
        pe = jnp.broadcast_to(pe_ref[j], (8, 2 * half))
        pe_term = _dot(pe, w1.astype(F32), HI)[0:1]
        h = jax.nn.gelu(lo + hi_next + pe_term)
        if j == 0:
            ko_ref[0] = _dot(h.astype(BF16), w2k_ref[...]).astype(BF16)
        else:
            vo_ref[0] = _dot_nt(w2vt_ref[...], h.astype(BF16)).astype(BF16)


def nsa_compress(kc_r, vc_r, pe, w1, w2k, w2vt):
    bsz, nb, w = kc_r.shape
    blk = pl.BlockSpec((1, nb, w), lambda bi: (bi, 0, 0))
    const = lambda a: pl.BlockSpec(a.shape, lambda bi: (0,) * a.ndim)
    return pl.pallas_call(
        _compress_body, grid=(bsz,),
        in_specs=[blk, blk, const(pe), const(w1), const(w2k), const(w2vt)],
        out_specs=[pl.BlockSpec((1, nb, HEAD_DIM), lambda bi: (bi, 0, 0)),
                   pl.BlockSpec((1, LANES, nb), lambda bi: (bi, 0, 0))],
        out_shape=[jax.ShapeDtypeStruct((bsz, nb, HEAD_DIM), BF16), jax.ShapeDtypeStruct((bsz, LANES, nb), BF16)],
        compiler_params=_cparams("parallel"), name="nsa_compress",
    )(kc_r, vc_r, pe, w1, w2k, w2vt)


def _stack_heads(x):
    return jnp.concatenate([x[:, h * HEAD_DIM:(h + 1) * HEAD_DIM] for h in range(N_HEADS)], axis=0)


def _nsa_attn_body(n_top, q_ref, qr_ref, kc_ref, vct_ref, ks_ref, vst_ref, kw_ref, vwt_ref, g_ref,
                   ov_ref, o_ref):
    T = Q_BLOCK
    i = pl.program_id(1)
    t0 = i * T
    qs = _stack_heads(q_ref[0])
    qrs = _stack_heads(qr_ref[0])
    tile4 = lambda x: jnp.concatenate([x] * N_HEADS, axis=0)

    pos_row = t0 + lax.broadcasted_iota(jnp.int32, (1, T), 1)
    lanes4 = lambda x: jnp.concatenate([x] * N_HEADS, axis=1)
    kc = kc_ref[0]
    n_cmp = kc.shape[0]
    cmp_end = lax.broadcasted_iota(jnp.int32, (n_cmp, 1), 0) * CMP_STRIDE + (CMP_LEN - 1)
    valid_c = lanes4(jnp.where(cmp_end <= pos_row, 1.0, 0.0))
    s_c = jnp.where(valid_c > 0.5, _dot_nt(kc, qs), NEG)
    p_c = jnp.exp(s_c - jnp.max(s_c, axis=0, keepdims=True)) * valid_c
    p_c = p_c / jnp.maximum(jnp.sum(p_c, axis=0, keepdims=True), 1e-20)
    o_c_t = _dot(vct_ref[0], p_c.astype(BF16))
    p_sum = p_c[:, 0:T] + p_c[:, T:2 * T] + p_c[:, 2 * T:3 * T] + p_c[:, 3 * T:4 * T]
    ps_hi, ps_lo = _split_bf16(p_sum)
    ov = ov_ref[...].astype(BF16)
    imp_t = _dot(ov, ps_hi) + _dot(ov, ps_lo)

    WK = WINDOW + T
    start = pl.multiple_of(jnp.maximum(t0 - WINDOW, 0), T)
    kpos_w = start + lax.broadcasted_iota(jnp.int32, (WK, 1), 0)
    bias_w = jnp.where(kpos_w <= pos_row, jnp.where(kpos_w > pos_row - WINDOW, 0.0, NEG), NEG)
    s_w = _dot_nt(kw_ref[0, pl.ds(start, WK), :], qrs) + lanes4(bias_w)
    p_w = jnp.exp2(s_w - jnp.max(s_w, axis=0, keepdims=True))
    acc_w = _dot(vwt_ref[0, :, pl.ds(start, WK)], p_w.astype(BF16))

    n_sel = imp_t.shape[0]
    jj = lax.broadcasted_iota(jnp.int32, (n_sel, T), 0)
    blk = (t0 + lax.broadcasted_iota(jnp.int32, (1, T), 1)) // SEL_LEN
    val = jnp.where(jj == blk, 3e38, jnp.where(jj == 0, 3e38, jnp.where(jj <= blk, imp_t, -1.0)))
    sel_t = jnp.zeros((n_sel, T), F32)
    for _ in range(n_top):
        mx = jnp.max(val, axis=0, keepdims=True)
        idx = jnp.min(jnp.where(val == mx, jj, n_sel), axis=0, keepdims=True)
        hit = jj == idx
        sel_t = jnp.where(hit, 1.0, sel_t)
        val = jnp.where(hit, -2.0, val)
    sel = sel_t.T

    KT = SEL_KT
    unsel = tile4((1.0 - sel).astype(BF16))
    q_aug = jnp.concatenate([qrs, jnp.zeros((N_HEADS * T, HEAD_DIM), BF16), unsel], axis=1)
    krow = lax.broadcasted_iota(jnp.int32, (KT, 1), 0)

    KS = KT // SEL_SUB

    def sel_tile(jt, carry, diagonal):
        m, acc = carry
        k0 = pl.multiple_of(jt * KT, KT)

        def scores(j):
            ks = pl.multiple_of(k0 + j * KS, KS)
            s_t = _dot_nt(ks_ref[0, pl.ds(ks, KS), :], q_aug)
            if diagonal:
                causal = jnp.where(ks + krow[:KS] <= pos_row, 0.0, NEG)
                s_t = s_t + jnp.concatenate([causal] * N_HEADS, axis=1)
            return s_t

        s_next = scores(0)
        for j in range(SEL_SUB):
            s_t = s_next
            if j + 1 < SEL_SUB:
                s_next = scores(j + 1)
            m_new = jnp.maximum(m, jnp.max(s_t, axis=0, keepdims=True))
            p_t = jnp.exp2(s_t - m_new)
            ks = pl.multiple_of(k0 + j * KS, KS)
            acc = jnp.exp2(m - m_new) * acc + _dot(vst_ref[0, :, pl.ds(ks, KS)], p_t.astype(BF16))
            m = m_new
        return m, acc

    init = (jnp.full((1, N_HEADS * T), NEG, F32), jnp.zeros((LANES, N_HEADS * T), F32))
    n_full = t0 // KT
    carry = lax.fori_loop(0, n_full, functools.partial(sel_tile, diagonal=False), init)
    _, acc_t = sel_tile(n_full, carry, True)
    acc = jnp.concatenate([acc_t, acc_w, o_c_t], axis=0).T
    o_c = acc[:, 2 * LANES:2 * LANES + HEAD_DIM]
    o_s = acc[:, :HEAD_DIM] / acc[:, HEAD_DIM:HEAD_DIM + 1]
    o_w = acc[:, LANES:LANES + HEAD_DIM] / acc[:, LANES + HEAD_DIM:LANES + HEAD_DIM + 1]

    g = g_ref[0]
    outs = []
    for h in range(N_HEADS):
        rows = slice(h * T, (h + 1) * T)
        outs.append(g[:, 3 * h:3 * h + 1] * o_c[rows] + g[:, 3 * h + 1:3 * h + 2] * o_s[rows]
                    + g[:, 3 * h + 2:3 * h + 3] * o_w[rows])
    o_ref[0] = jnp.concatenate(outs, axis=-1)


def nsa_attention(q, qr, k_cmp, v_cmp, ks, vs, kw, vw, gates, overlap):
    bsz, s, _ = q.shape
    n_top = min(SEL_TOPN, s // SEL_LEN)
    n_cmp = k_cmp.shape[1]
    qtile = lambda w: pl.BlockSpec((1, Q_BLOCK, w), lambda bi, i: (bi, i, 0))
    full = lambda rows, w: pl.BlockSpec((1, rows, w), lambda bi, i: (bi, 0, 0))
    return pl.pallas_call(
        functools.partial(_nsa_attn_body, n_top),
        grid=(bsz, s // Q_BLOCK),
        in_specs=[qtile(W_GRP), qtile(W_GRP), full(n_cmp, HEAD_DIM), full(LANES, n_cmp),
                  full(s, 2 * LANES), full(LANES, s), full(s, HEAD_DIM), full(LANES, s),
                  qtile(LANES), pl.BlockSpec(overlap.shape, lambda bi, i: (0, 0))],
        out_specs=qtile(W_GRP), out_shape=jax.ShapeDtypeStruct((bsz, s, W_GRP), F32),
        compiler_params=_cparams("parallel", "parallel"), name="nsa_attention",
    )(q, qr, k_cmp, v_cmp, ks, vs, kw, vw, gates, overlap)


def _outproj_math(alpha, parts, x, w_ref, g_ref, b_ref):
    y = alpha * x
    for j, part in enumerate(parts):
        y = y + _dot(part[...].astype(BF16), w_ref[j * W_GRP:(j + 1) * W_GRP, :])
    return _ln(y, g_ref[...], b_ref[...])


def _memkv_body(m_ref, wk_ref, wv_ref, k_o, v_o):
    mb = m_ref[...].astype(BF16)
    k_o[...] = _dot(mb, wk_ref[...]).astype(BF16)
    v_o[...] = _dot(mb, wv_ref[...]).astype(BF16)


def mem_kv(mem2d, wk, wv):
    n, d = mem2d.shape
    full = lambda a: pl.BlockSpec(a.shape, lambda i: (0, 0))
    out = jax.ShapeDtypeStruct((n, d), BF16)
    return pl.pallas_call(
        _memkv_body, grid=(1,),
        in_specs=[full(mem2d), full(wk), full(wv)],
        out_specs=[pl.BlockSpec((n, d), lambda i: (0, 0))] * 2, out_shape=[out, out],
        compiler_params=_cparams("arbitrary"), name="mem_kv",
    )(mem2d, wk, wv)


def _xattn_body(alpha, a_ref, b_ref, c_ref, d_ref, x_ref, w1_ref, g1_ref, b1_ref,
                k_ref, v_ref, wq_ref, wo_ref, g_ref, b_ref_, o_ref):
    x = _outproj_math(alpha, (a_ref, b_ref, c_ref, d_ref), x_ref[...], w1_ref, g1_ref, b1_ref)
    d = x.shape[-1]
    hd = d // N_MEM_HEADS
    q = (_dot(x.astype(BF16), wq_ref[...]) * (hd ** -0.5 * LOG2E)).astype(BF16)
    k = k_ref[0]
    v = v_ref[0]
    cols = [slice(h * hd, (h + 1) * hd) for h in range(N_MEM_HEADS)]
    scores = [_dot_nt(q[:, cs], k[:, cs]) for cs in cols]
    probs = []
    for s in scores:
        p = jnp.exp2(s - jnp.max(s, axis=-1, keepdims=True))
        probs.append((p / jnp.sum(p, axis=-1, keepdims=True)).astype(BF16))
    y = alpha * x
    for cs, p in zip(cols, probs):
        y = y + _dot(_dot(p, v[:, cs]).astype(BF16), wo_ref[cs, :])
    o_ref[...] = _ln(y, g_ref[...], b_ref_[...])


def cross_attention(parts, x2d, seq_len, w_out, g1, b1, k, v, wq, wo, g, b, alpha):
    n, d = x2d.shape
    bsz, s = n // seq_len, seq_len
    tps = s // TM
    m = k.shape[1]
    const = lambda a: pl.BlockSpec(a.shape, lambda bi, i: (0,) * a.ndim)
    row = lambda w: pl.BlockSpec((TM, w), lambda bi, i: (bi * tps + i, 0))
    return pl.pallas_call(
        functools.partial(_xattn_body, alpha),
        grid=(bsz, s // TM),
        in_specs=[row(W_GRP)] * 4 + [row(d), const(w_out), const(g1), const(b1),
                  pl.BlockSpec((1, m, d), lambda bi, i: (bi, 0, 0)),
                  pl.BlockSpec((1, m, d), lambda bi, i: (bi, 0, 0)),
                  const(wq), const(wo), const(g), const(b)],
        out_specs=row(d), out_shape=jax.ShapeDtypeStruct((n, d), F32),
        compiler_params=_cparams("parallel", "parallel"), name="cross_attention",
    )(*parts, x2d, w_out, g1, b1, k, v, wq, wo, g, b)


PER_GRP = N_EXPERTS // N_EXPERT_GROUPS
LPOS_LANE = PER_GRP
SUB = 8
BF16_ROWS = 16


def _router_body(x_ref, rwt_ref, rb_ref, upper_ref, tok_o, lrow_o, cnt_o):
    tm = x_ref.shape[0]
    logits = _dot3(x_ref[...], rwt_ref[...]).T[:N_EXPERTS]
    ex = jnp.exp(logits - jnp.max(logits, axis=0, keepdims=True))
    probs = ex / jnp.sum(ex, axis=0, keepdims=True)
    sel = probs + rb_ref[...]
    srow = [sel[e:e + 1] for e in range(N_EXPERTS)]
    prow = [probs[e:e + 1] for e in range(N_EXPERTS)]
    gscore = []
    for g in range(N_EXPERT_GROUPS):
        r = srow[g * PER_GRP:(g + 1) * PER_GRP]
        best = None
        for a in range(PER_GRP):
            for b in range(a + 1, PER_GRP):
                best = r[a] + r[b] if best is None else jnp.maximum(best, r[a] + r[b])
        gscore.append(best)
    g_idx = jnp.zeros((1, tm), jnp.int32)
    top = gscore[0]
    for g in range(1, N_EXPERT_GROUPS):
        better = gscore[g] > top
        g_idx = jnp.where(better, g, g_idx)
        top = jnp.where(better, gscore[g], top)

    def of_group(rows, e):
        out = rows[e]
        for g in range(1, N_EXPERT_GROUPS):
            out = jnp.where(g_idx == g, rows[g * PER_GRP + e], out)
        return out

    sg = [of_group(srow, e) for e in range(PER_GRP)]
    pg = [of_group(prow, e) for e in range(PER_GRP)]
    w = []
    for e in range(PER_GRP):
        rank = jnp.zeros((1, tm), F32)
        for o in range(PER_GRP):
            if o != e:
                ahead = (sg[o] >= sg[e]) if o < e else (sg[o] > sg[e])
                rank = rank + jnp.where(ahead, 1.0, 0.0)
        w.append(jnp.where(rank < 1.5, pg[e], 0.0))
    w_sum = w[0] + w[1] + w[2] + w[3]
    row8 = lax.broadcasted_iota(jnp.int32, (SUB, tm), 0)
    onehot = jnp.where(row8 == g_idx, 1.0, 0.0)
    before = _dot(onehot.astype(BF16), upper_ref[...])
    cnt = jnp.sum(onehot, axis=1, keepdims=True)
    offs = [jnp.zeros((1, 1), F32)]
    for g in range(1, N_EXPERT_GROUPS):
        offs.append(offs[-1] + cnt[g - 1:g])
    lpos = jnp.zeros((1, tm), F32)
    for g in range(N_EXPERT_GROUPS):
        lpos = lpos + onehot[g:g + 1] * (offs[g] + before[g:g + 1])
    tok = jnp.zeros((SUB, tm), F32)
    for e in range(PER_GRP):
        tok = jnp.where(row8 == e, w[e] / w_sum, tok)
    tok = jnp.where(row8 == LPOS_LANE, lpos, tok)
    tok_o[...] = jnp.concatenate([tok, jnp.zeros((LANES - SUB, tm), F32)], axis=0).T
    lrow_o[0] = lpos.astype(jnp.int32)
    rowc = lax.broadcasted_iota(jnp.int32, (SUB, LANES), 0)
    stats = jnp.zeros((SUB, LANES), F32)
    for g in range(N_EXPERT_GROUPS):
        stats = jnp.where(rowc == g, cnt[g:g + 1], stats)
        stats = jnp.where(rowc == N_EXPERT_GROUPS + g, offs[g], stats)
    cnt_o[0] = stats.astype(jnp.int32)


def moe_router(x2d, rwt, rb_col, upper):
    n, d = x2d.shape
    tm = upper.shape[0]
    nt = n // tm
    const = lambda a: pl.BlockSpec(a.shape, lambda i: (0, 0))
    return pl.pallas_call(
        _router_body, grid=(nt,),
        in_specs=[pl.BlockSpec((tm, d), lambda i: (i, 0)), const(rwt), const(rb_col), const(upper)],
        out_specs=[pl.BlockSpec((tm, LANES), lambda i: (i, 0)),
                   pl.BlockSpec((1, 1, tm), lambda i: (i, 0, 0)),
                   pl.BlockSpec((1, SUB, LANES), lambda i: (i, 0, 0))],
        out_shape=[jax.ShapeDtypeStruct((n, LANES), F32), jax.ShapeDtypeStruct((nt, 1, tm), jnp.int32),
                   jax.ShapeDtypeStruct((nt, SUB, LANES), jnp.int32)],
        compiler_params=_cparams("parallel"), name="moe_router",
    )(x2d, rwt, rb_col, upper)


def _split_bf16(x):
    hi = x.astype(BF16)
    return hi, (x - hi.astype(F32)).astype(BF16)


def _moe_body(alpha, offs_ref, cnts_ref, x_ref, tok_ref, lrow_ref, wg_ref, wu_ref, wd_ref, g_ref, b_ref, o_ref,
              xs_ref, gs_ref, acc_ref):
    i = pl.program_id(0)
    step = pl.program_id(1)
    tm = x_ref.shape[0]
    eps = wd_ref.shape[1]

    @pl.when(step == 0)
    def _():
        slot = lax.broadcasted_iota(jnp.int32, (tm, tm), 0)
        perm = jnp.where(slot == lrow_ref[0], 1.0, 0.0).astype(BF16)
        xs_ref[0:tm, :] = _dot(perm, x_ref[...].astype(BF16)).astype(BF16)
        t_hi, t_lo = _split_bf16(tok_ref[...])
        gs_ref[0:tm, :] = _dot(perm, t_hi) + _dot(perm, t_lo)
        xs_ref[tm:, :] = jnp.zeros((MOE_BLK, xs_ref.shape[1]), BF16)
        gs_ref[tm:, :] = jnp.zeros((MOE_BLK, LANES), F32)
        acc_ref[...] = jnp.zeros_like(acc_ref)

    grp = (step * eps) // PER_GRP
    off = offs_ref[i * N_EXPERT_GROUPS + grp]
    end = off + cnts_ref[i * N_EXPERT_GROUPS + grp]
    lane = lax.broadcasted_iota(jnp.int32, (MOE_BLK, LANES), 1)
    first = off // BF16_ROWS * BF16_ROWS
    for s in range(-(-(tm + BF16_ROWS - 1) // MOE_BLK)):
        lo = pl.multiple_of(first + s * MOE_BLK, BF16_ROWS)

        @pl.when(lo < end)
        def _(lo=lo):
            rows = pl.ds(lo, MOE_BLK)
            xs = xs_ref[rows, :]
            ridx = lo + lax.broadcasted_iota(jnp.int32, (MOE_BLK, 1), 0)
            in_run = jnp.where(ridx >= off, jnp.where(ridx < end, 1.0, 0.0), 0.0)
            gs = gs_ref[rows, :]
            y = jnp.zeros((MOE_BLK, x_ref.shape[1]), F32)
            for k in range(eps):
                gt = _dot(xs, wg_ref[0, k])
                h = (gt * jax.nn.sigmoid(gt) * _dot(xs, wu_ref[0, k])).astype(BF16)
                in_grp = (step * eps + k) % PER_GRP
                gcol = jnp.sum(jnp.where(lane == in_grp, gs, 0.0), axis=-1, keepdims=True)
                y = y + (gcol * in_run) * _dot(h, wd_ref[0, k])
            acc_ref[rows, :] += y

    @pl.when(step == pl.num_programs(1) - 1)
    def _():
        lcol = tok_ref[:, LPOS_LANE:LPOS_LANE + 1].astype(jnp.int32)
        slot = lax.broadcasted_iota(jnp.int32, (tm, tm), 1)
        unperm = jnp.where(slot == lcol, 1.0, 0.0).astype(BF16)
        y = _dot(unperm, acc_ref[0:tm, :].astype(BF16))
        o_ref[...] = _ln(alpha * x_ref[...] + y, g_ref[...], b_ref[...])


def moe_ffn(x2d, tok, lrow, offs, cnts, layer, wg, wu, wd, g, b, alpha):
    n, d = x2d.shape
    _, ne, de, _ = wd.shape
    tm = lrow.shape[2]
    const = lambda a: pl.BlockSpec(a.shape, lambda i, e, o, c: (0,) * a.ndim)
    grid_spec = pltpu.PrefetchScalarGridSpec(
        num_scalar_prefetch=2, grid=(n // tm, ne // MOE_EPS),
        in_specs=[pl.BlockSpec((tm, d), lambda i, e, o, c: (i, 0), pipeline_mode=pl.Buffered(1)),
                  pl.BlockSpec((tm, LANES), lambda i, e, o, c: (i, 0)),
                  pl.BlockSpec((1, 1, tm), lambda i, e, o, c: (i, 0, 0)),
                  pl.BlockSpec((1, MOE_EPS, d, de), lambda i, e, o, c: (layer, e, 0, 0)),
                  pl.BlockSpec((1, MOE_EPS, d, de), lambda i, e, o, c: (layer, e, 0, 0)),
                  pl.BlockSpec((1, MOE_EPS, de, d), lambda i, e, o, c: (layer, e, 0, 0)),
                  const(g), const(b)],
        out_specs=pl.BlockSpec((tm, d), lambda i, e, o, c: (i, 0)),
        scratch_shapes=[pltpu.VMEM((tm + MOE_BLK, d), BF16), pltpu.VMEM((tm + MOE_BLK, LANES), F32),
                        pltpu.VMEM((tm + MOE_BLK, d), F32)])
    return pl.pallas_call(
        functools.partial(_moe_body, alpha), grid_spec=grid_spec,
        out_shape=jax.ShapeDtypeStruct((n, d), F32),
        compiler_params=_cparams("parallel", "arbitrary"), name="moe_ffn",
    )(offs, cnts, x2d, tok, lrow, wg, wu, wd, g, b)


def _rope_tables(s):
    inv = ROPE_THETA ** (-jnp.arange(0, HEAD_DIM, 2, dtype=F32) / HEAD_DIM)
    ang = jnp.arange(s, dtype=F32)[:, None] * inv[None, :]
    cos, sin = jnp.cos(ang), jnp.sin(ang)
    cos_h = jnp.concatenate([cos, cos], axis=-1)
    sin_h = jnp.concatenate([-sin, sin], axis=-1)
    return jnp.tile(cos_h, (1, 2)), jnp.tile(sin_h, (1, 2))


def _overlap_matrix(s, n_cmp_pad):
    n_sel = s // SEL_LEN
    assert n_sel <= LANES
    cmp_start = jnp.arange(n_cmp_pad) * CMP_STRIDE
    sel_start = jnp.arange(LANES) * SEL_LEN
    ov = jnp.clip(jnp.minimum(cmp_start[None, :] + CMP_LEN, sel_start[:, None] + SEL_LEN)
                  - jnp.maximum(cmp_start[None, :], sel_start[:, None]), 0, None).astype(F32) / CMP_LEN
    n_cmp = s // CMP_STRIDE - (CMP_LEN // CMP_STRIDE - 1)
    real = (jnp.arange(n_cmp_pad)[None, :] < n_cmp) & (jnp.arange(LANES)[:, None] < n_sel)
    return jnp.where(real, ov, 0.0)


def _pad_rows(w, lo, total):
    out = jnp.zeros((total, w.shape[1]), w.dtype)
    return out.at[lo:lo + w.shape[0]].set(w)


def kernel(x, mem, ln_in_g, ln_in_b, w_in, w_out, conv_w, conv_b, conv_gn_g, conv_gn_b, rwkv_mu, rwkv_w0, rwkv_w_up, rwkv_a0, rwkv_a_up, rwkv_g_up, rwkv_k_k, rwkv_k_a, rwkv_r_k, rwkv_gn_g, rwkv_gn_b, rwkv_v_down, rwkv_v_mu, rwkv_v0, rwkv_v_up, gmlp_ln_g, gmlp_ln_b, gmlp_w_s, gmlp_b_s, nsa_pe_k, nsa_w1_k, nsa_w2_k, nsa_pe_v, nsa_w1_v, nsa_w2_v, ln1_g, ln1_b, xa_wq, xa_wk, xa_wv, xa_wo, ln2_g, ln2_b, router_w, router_bias, moe_w_gate, moe_w_up, moe_w_down, ln3_g, ln3_b):
    bsz, s, d = x.shape
    depth = w_in.shape[0]
    n = bsz * s
    alpha = (2 * depth) ** 0.25
    row = lambda a: a.reshape(1, -1)

    cos_t, sin_t = _rope_tables(s)
    n_blk = s // CMP_STRIDE
    overlap = _overlap_matrix(s, n_blk)
    avg64 = _group_avg_matrix(W_GRP, HEAD_DIM)
    ones64 = avg64 * HEAD_DIM
    rw_pad = jnp.zeros((d, LANES), F32).at[:, :N_EXPERTS].set(router_w)
    moe_tm = min(MOE_TM, n)
    t_idx = jnp.arange(moe_tm)
    upper = (t_idx[:, None] < t_idx[None, :]).astype(BF16)
    mem2d = mem.reshape(bsz * mem.shape[1], d)

    wg_b, wu_b, wd_b = moe_w_gate.astype(BF16), moe_w_up.astype(BF16), moe_w_down.astype(BF16)

    xs = x.reshape(n, d)
    v_first = None
    for l in range(depth):
        main = IN_SPLITS[-1][1]
        w_misc = jnp.zeros((d, LANES), F32).at[:, :w_in.shape[2] - main].set(w_in[l, :, main:])
        if l > 0:
            w_misc = w_misc.at[:, MISC_VD_OFF:MISC_VD_OFF + rwkv_v_down.shape[2]].set(rwkv_v_down[l - 1])
        gmlp_p = (row(gmlp_ln_g[l]), row(gmlp_ln_b[l]), gmlp_w_s[l], gmlp_b_s[l].reshape(N_HEADS, GMLP_CHUNK, 1))
        conv_p = (conv_w[l], row(conv_b[l]), row(conv_gn_g[l]), row(conv_gn_b[l]), avg64)
        rwkv_p = (row(rwkv_mu[l]), row(rwkv_w0[l]), row(rwkv_a0[l]), _pad_rows(rwkv_w_up[l], 0, W_GRP),
                  _pad_rows(rwkv_a_up[l], 64, W_GRP), _pad_rows(rwkv_g_up[l], 128, W_GRP),
                  row(rwkv_k_k[l]), row(rwkv_k_a[l]), ones64)
        if l > 0:
            vmu = jnp.zeros((1, LANES), F32).at[0, MISC_VD_OFF:MISC_VD_OFF + rwkv_v_mu.shape[1]].set(rwkv_v_mu[l - 1])
            rwkv_p += (vmu, row(rwkv_v0[l - 1]), _pad_rows(rwkv_v_up[l - 1], MISC_VD_OFF, LANES))
        outs = in_proj(xs, s, row(ln_in_g), row(ln_in_b), w_in, l, w_misc.astype(BF16), cos_t, sin_t,
                       conv_p, gmlp_p, rwkv_p, v_first, apply_ln=(l == 0))
        if l == 0:
            xs, outs = outs[0], outs[1:]
        out_a = outs[0]
        r_, lw_, k_, v_, a_, b_, g_ = outs[1:1 + N_RWKV_OUT]
        out_c = outs[1 + N_RWKV_OUT]
        q_b, qr_b, kc, vc, ks, vs, kw, vw, gates = outs[2 + N_RWKV_OUT:]
        if l == 0:
            v_first = v_
        out_b = wkv_scan(r_, lw_, k_, v_, a_, b_, g_, row(rwkv_r_k[l]), row(rwkv_gn_g[l]),
                         row(rwkv_gn_b[l]), avg64, ones64)

        pe = jnp.stack([nsa_pe_k[l].reshape(1, -1), nsa_pe_v[l].reshape(1, -1)])
        w1 = jnp.stack([nsa_w1_k[l], nsa_w1_v[l]]).astype(BF16)
        w2vt = jnp.zeros((LANES, nsa_w2_v.shape[1]), F32).at[:HEAD_DIM].set(nsa_w2_v[l].T).astype(BF16)
        k_cmp, v_cmp = nsa_compress(kc.reshape(bsz, n_blk, CMP_STRIDE * HEAD_DIM),
                                    vc.reshape(bsz, n_blk, CMP_STRIDE * HEAD_DIM), pe, w1,
                                    nsa_w2_k[l].astype(BF16), w2vt)
        out_d = nsa_attention(q_b, qr_b, k_cmp, v_cmp, ks, vs, kw, vw, gates, overlap)

        flat = lambda a: a.reshape(n, W_GRP)
        mk, mv = mem_kv(mem2d, xa_wk[l].astype(BF16), xa_wv[l].astype(BF16))
        m_len = mem.shape[1]
        xs = cross_attention((flat(out_a), flat(out_b), out_c, flat(out_d)), xs, s, w_out[l].astype(BF16),
                             row(ln1_g[l]), row(ln1_b[l]), mk.reshape(bsz, m_len, d), mv.reshape(bsz, m_len, d),
                             xa_wq[l].astype(BF16), xa_wo[l].astype(BF16), row(ln2_g[l]), row(ln2_b[l]), alpha)

        tok, lrow, stats = moe_router(xs, rw_pad, router_bias.reshape(-1, 1), upper)
        cnts = stats[:, :N_EXPERT_GROUPS, 0].reshape(-1)
        offs = stats[:, N_EXPERT_GROUPS:2 * N_EXPERT_GROUPS, 0].reshape(-1)
        xs = moe_ffn(xs, tok, lrow, offs, cnts, l, wg_b, wu_b, wd_b,
                     row(ln3_g[l]), row(ln3_b[l]), alpha)
    return xs.reshape(bsz, s, d)
```

```python
import functools
import math

import jax
import jax.numpy as jnp
from jax import lax
from jax.experimental import pallas as pl
from jax.experimental.pallas import tpu as pltpu

F32 = jnp.float32
BF16 = jnp.bfloat16
HI = lax.Precision.HIGHEST

HEAD_DIM = 64
N_HEADS = 4
W_GRP = 256
CONV_WIDTH = 31
GMLP_CHUNK = 128
CMP_LEN = 32
CMP_STRIDE = 16
SEL_LEN = 64
SEL_TOPN = 16
WINDOW = 512
Q_BLOCK = 256
ROPE_THETA = 10000.0
N_MEM_HEADS = 4
N_EXPERTS = 16
N_EXPERT_GROUPS = 4
LN_EPS = 1e-5
RWKV_GN_EPS = 64e-5
NEG = -1e30
LOG2E = math.log2(math.e)
LANES = 128
WKV_CHUNK = 64
WKV_GROUP = 8

TM = 512
MOE_TM = 1024
_RUN_SD = math.sqrt(MOE_TM * (N_EXPERT_GROUPS - 1)) / N_EXPERT_GROUPS
MOE_BLK = 16 * -(-(MOE_TM // N_EXPERT_GROUPS + int(2 * _RUN_SD) + 15) // 16)
MOE_EPS = 4
SEL_SUB = 2
SEL_KT = 1024
VMEM_LIMIT = 56 * 1024 * 1024


def _cparams(*sem):
    return pltpu.CompilerParams(dimension_semantics=sem, vmem_limit_bytes=VMEM_LIMIT)


def _ln(x, g, b, eps=LN_EPS):
    mu = jnp.mean(x, axis=-1, keepdims=True)
    xc = x - mu
    var = jnp.mean(xc * xc, axis=-1, keepdims=True)
    return xc * lax.rsqrt(var + eps) * g + b


def _dot(a, b, precision=None):
    return jnp.dot(a, b, preferred_element_type=F32, precision=precision)


def _dot_nt(a, b, precision=None):
    return lax.dot_general(a, b, (((1,), (1,)), ((), ())),
                           preferred_element_type=F32, precision=precision)


def _dot_hilo(x, w_bf16):
    hi = x.astype(BF16)
    lo = (x - hi.astype(F32)).astype(BF16)
    return _dot(hi, w_bf16) + _dot(lo, w_bf16)


def _dot3(x, w):
    x_hi = x.astype(BF16)
    x_lo = (x - x_hi.astype(F32)).astype(BF16)
    w_hi = w.astype(BF16)
    w_lo = (w - w_hi.astype(F32)).astype(BF16)
    return _dot(x_hi, w_hi) + _dot(x_lo, w_hi) + _dot(x_hi, w_lo)


def _group_avg_matrix(width, group):
    r = jnp.arange(width)[:, None] // group
    c = jnp.arange(width)[None, :] // group
    return jnp.where(r == c, 1.0 / group, 0.0).astype(F32)


IN_SPLITS = (("conv", 0, 512), ("rwkv", 512, 1536), ("gmlp", 1536, 2048),
             ("q", 2048, 2304), ("kv", 2304, 2688), ("misc", 2688, 2816))
MISC_VD_OFF = 32


N_CONV_P, N_GMLP_P, N_RWKV_P, N_VMIX_P = 5, 4, 9, 3
N_NSA_OUT, N_RWKV_OUT = 9, 7


def _inproj_body(apply_ln, has_vfirst, tiles_per_seq, x_ref, g_ref, b_ref, w_ref, wm_ref, cos_ref, sin_ref, *rest):
    take = lambda k: (rest[:k], rest[k:])
    conv_p, rest = take(N_CONV_P)
    gmlp_p, rest = take(N_GMLP_P)
    rwkv_p, rest = take(N_RWKV_P + (N_VMIX_P if has_vfirst else 0))
    vf_ref = None
    if has_vfirst:
        (vf_ref,), rest = take(1)
    if apply_ln:
        (xln_o,), rest = take(1)
    (conv_o,), rest = take(1)
    rwkv_o, rest = take(N_RWKV_OUT)
    (gmlp_o,), rest = take(1)
    nsa_o, rest = take(N_NSA_OUT)
    wb_ref, hbuf, shifted, buf, mbuf = rest
    main = IN_SPLITS[-1][1]

    @pl.when(pl.program_id(0) == 0)
    def _():
        wb_ref[...] = w_ref[0, :, :main].astype(BF16)

    x = x_ref[...]
    if apply_ln:
        x = _ln(x, g_ref[...], b_ref[...])
        xln_o[...] = x
    xb = x.astype(BF16)
    cols = {name: (lo, hi) for name, lo, hi in IN_SPLITS}
    proj = lambda name: _dot(xb, wb_ref[:, cols[name][0]:cols[name][1]])
    tile_in_seq = pl.program_id(0) % tiles_per_seq
    first = tile_in_seq == 0
    misc = _dot(xb, wm_ref[...])
    _conv_math(proj("conv"), first, *conv_p, conv_o, hbuf, shifted)
    _rwkv_prep_math(proj("rwkv"), misc, first, vf_ref, rwkv_p, rwkv_o, buf, mbuf)
    _gmlp_math(proj("gmlp"), *gmlp_p, gmlp_o)
    _nsa_prep_math(proj("q"), proj("kv"), misc, tile_in_seq * x.shape[0], cos_ref, sin_ref, *nsa_o)


def in_proj(x2d, seq_len, g, b, w_in, layer, w_misc, cos_t, sin_t, conv_p, gmlp_p, rwkv_p, v_first, apply_ln):
    n, d = x2d.shape
    bsz = n // seq_len
    tps = seq_len // TM
    row = lambda w: pl.BlockSpec((TM, w), lambda i: (i, 0))
    const = lambda a: pl.BlockSpec(a.shape, lambda i: (0,) * a.ndim)
    seq = lambda w: pl.BlockSpec((1, TM, w), lambda i: (i // tps, i % tps, 0))
    sd = lambda w, dt: jax.ShapeDtypeStruct((bsz, seq_len, w), dt)
    flat = lambda w: jax.ShapeDtypeStruct((n, w), F32)
    out_shapes = ([flat(W_GRP)] + [sd(W_GRP, F32)] * N_RWKV_OUT + [flat(W_GRP)]
                  + [sd(W_GRP, BF16), sd(W_GRP, BF16), sd(HEAD_DIM, F32), sd(HEAD_DIM, F32),
                     sd(2 * LANES, BF16), jax.ShapeDtypeStruct((bsz, LANES, seq_len), BF16),
                     sd(HEAD_DIM, BF16), jax.ShapeDtypeStruct((bsz, LANES, seq_len), BF16), sd(LANES, F32)])
    out_specs = ([row(W_GRP)] + [seq(W_GRP)] * N_RWKV_OUT + [row(W_GRP)]
                 + [seq(W_GRP), seq(W_GRP), seq(HEAD_DIM), seq(HEAD_DIM), seq(2 * LANES),
                    pl.BlockSpec((1, LANES, TM), lambda i: (i // tps, 0, i % tps)),
                    seq(HEAD_DIM), pl.BlockSpec((1, LANES, TM), lambda i: (i // tps, 0, i % tps)), seq(LANES)])
    if apply_ln:
        out_shapes = [flat(d)] + out_shapes
        out_specs = [row(d)] + out_specs
    has_vfirst = v_first is not None
    assert len(conv_p) == N_CONV_P and len(gmlp_p) == N_GMLP_P
    assert len(rwkv_p) == N_RWKV_P + (N_VMIX_P if has_vfirst else 0)
    tab = pl.BlockSpec((TM, LANES), lambda i: (i % tps, 0))
    params = list(conv_p) + list(gmlp_p) + list(rwkv_p)
    inputs = [x2d, g, b, w_in, w_misc, cos_t, sin_t] + params + ([v_first] if has_vfirst else [])
    in_specs = ([row(d), const(g), const(b),
                 pl.BlockSpec((1,) + w_in.shape[1:], lambda i: (layer, 0, 0), pipeline_mode=pl.Buffered(1)),
                 const(w_misc), tab, tab] + [const(a) for a in params]
                + ([seq(W_GRP)] if has_vfirst else []))
    return pl.pallas_call(
        functools.partial(_inproj_body, apply_ln, has_vfirst, tps),
        grid=(n // TM,),
        in_specs=in_specs, out_specs=out_specs, out_shape=out_shapes,
        scratch_shapes=[pltpu.VMEM((d, IN_SPLITS[-1][1]), BF16),
                        pltpu.VMEM((TM + CONV_HALO, W_GRP), F32), pltpu.VMEM((TM + CONV_HALO, W_GRP), F32),
                        pltpu.VMEM((TM + SHIFT_HALO, 4 * W_GRP), F32), pltpu.VMEM((TM + SHIFT_HALO, LANES), F32)],
        compiler_params=_cparams("arbitrary"), name="in_proj",
    )(*inputs)


CONV_HALO = 32


def _carry_rows(buf, halo, first):
    tail = buf.shape[0] - halo

    @pl.when(first)
    def _():
        buf[0:halo, :] = jnp.zeros((halo, buf.shape[1]), buf.dtype)

    @pl.when(jnp.logical_not(first))
    def _():
        buf[0:halo, :] = buf[tail:tail + halo, :]


def _conv_math(cur, first, w_ref, b_ref, gg_ref, gb_ref, avg_ref, o_ref, hbuf, shifted):
    ts = cur.shape[0]
    h = cur[:, :W_GRP] * jax.nn.sigmoid(cur[:, W_GRP:])
    _carry_rows(hbuf, CONV_HALO, first)
    hbuf[CONV_HALO:, :] = h
    acc = jnp.zeros((ts, W_GRP), F32)
    base = CONV_HALO - (CONV_WIDTH - 1)
    for phase in range(SUB):
        taps = [j for j in range(CONV_WIDTH) if (base + j) % SUB == phase]
        span = (base + taps[-1]) // SUB * SUB + ts
        shifted[0:span, :] = hbuf[phase:phase + span, :]
        for j in taps:
            lo = (base + j) // SUB * SUB
            acc = acc + w_ref[j:j + 1, :] * shifted[lo:lo + ts, :]
    acc = acc + b_ref[...]
    avg = avg_ref[...].astype(BF16)
    mu = _dot_hilo(acc, avg)
    xc = acc - mu
    var = _dot_hilo(xc * xc, avg)
    y = xc * lax.rsqrt(var + LN_EPS) * gg_ref[...] + gb_ref[...]
    o_ref[...] = y * jax.nn.sigmoid(y)


SHIFT_HALO = 8


def _shift_prev(buf, cur, first):
    ts = cur.shape[0]
    _carry_rows(buf, SHIFT_HALO, first)
    buf[SHIFT_HALO:, :] = cur
    return buf[SHIFT_HALO - 1:SHIFT_HALO - 1 + ts, :]


def _rwkv_prep_math(cur, misc, first, vf_ref, params, outs, buf, mbuf):
    mu_ref, w0_ref, a0_ref, wup_ref, aup_ref, gup_ref, kk_ref, ka_ref, ones_ref = params[:9]
    r_o, lw_o, k_o, v_o, a_o, b_o, g_o = outs
    prev = _shift_prev(buf, cur, first)
    y = cur + mu_ref[...] * (prev - cur)
    r = y[:, 0:256]
    k = y[:, 256:512]
    v = y[:, 512:768]
    lora = y[:, 768:1024]
    w = w0_ref[...] + _dot3(jnp.tanh(lora), wup_ref[...])
    a = jax.nn.sigmoid(a0_ref[...] + _dot3(lora, aup_ref[...]))
    g = _dot3(jax.nn.sigmoid(lora), gup_ref[...])
    z = -w
    sp = jnp.maximum(z, 0.0) + jnp.log(1.0 + jnp.exp(-jnp.abs(z)))
    lw = -jnp.exp(-sp - 0.5)
    if vf_ref is not None:
        vmu_ref, v0_ref, vup_ref = params[9:]
        mprev = _shift_prev(mbuf, misc, first)
        vd = misc + vmu_ref[...] * (mprev - misc)
        v_mix = jax.nn.sigmoid(v0_ref[...] + _dot3(vd, vup_ref[...]))
        v = v + (vf_ref[0] - v) * v_mix
    kk = k * kk_ref[...]
    n2 = _dot_hilo(kk * kk, ones_ref[...].astype(BF16))
    kk = kk / jnp.maximum(jnp.sqrt(n2), 1e-12)
    k2 = k * (1.0 + (a - 1.0) * ka_ref[...])
    r_o[0] = r
    lw_o[0] = lw
    k_o[0] = k2
    v_o[0] = v
    a_o[0] = -kk
    b_o[0] = kk * a
    g_o[0] = g


def _block_diag(x, headmask):
    return jnp.concatenate([x] * N_HEADS, axis=0) * headmask


def _wkv_body(ts, r_ref, lw_ref, k_ref, v_ref, a_ref, b_ref, g_ref, rk_ref, gg_ref, gb_ref,
              avg_ref, ones_ref, o_ref, st_ref):
    C = WKV_CHUNK
    n = N_HEADS * C

    @pl.when(pl.program_id(1) == 0)
    def _():
        st_ref[...] = jnp.zeros_like(st_ref)

    ri = lax.broadcasted_iota(jnp.int32, (n, n), 0)
    ci = lax.broadcasted_iota(jnp.int32, (n, n), 1)
    head_f = jnp.where((ri // C) == (ci // HEAD_DIM), 1.0, 0.0)
    head_b = head_f.astype(BF16)
    lag = jnp.where((ri // C) == (ci // C), (ri % C) - (ci % C), -1)
    strict = lag > 0
    incl = lag >= 0
    eye = ri == ci
    eye_f = jnp.where(eye, 1.0, 0.0)
    tr = lax.broadcasted_iota(jnp.int32, (C, C), 0)
    tc = lax.broadcasted_iota(jnp.int32, (C, C), 1)
    tri = jnp.where(tc <= tr, 1.0, 0.0).astype(F32)
    cast = lambda x: x.astype(BF16)
    G = ts // C
    split = lambda ref: ref[0].reshape(G, C, W_GRP)
    tile_heads = lambda x: jnp.concatenate([x] * N_HEADS, axis=1)
    bd16 = lambda x: tile_heads(cast(x)) * head_b[None]
    bmm = lambda x, y: lax.dot_general(x, y, (((2,), (1,)), ((0,), (0,))), preferred_element_type=F32)
    bmm_nt = lambda x, y: lax.dot_general(x, y, (((2,), (2,)), ((0,), (0,))), preferred_element_type=F32)

    lw = split(lw_ref)
    cum = lax.dot_general(jnp.broadcast_to(tri[None], (G, C, C)), lw, (((2,), (1,)), ((0,), (0,))),
                          preferred_element_type=F32, precision=HI)
    cum_c = cum[:, C - 1:C, :]
    e_in = jnp.exp(cum)
    e_neg = jnp.exp(-cum)
    e_tail = jnp.exp(cum_c - cum)
    b_c = split(b_ref)
    k_c = split(k_ref)
    a_t = bd16(split(a_ref) * jnp.exp(cum - lw))
    r_t = bd16(split(r_ref) * e_in)
    b_t = bd16(b_c * e_neg)
    k_t = bd16(k_c * e_neg)
    v_bd = bd16(split(v_ref))
    bh_t = cast(jnp.swapaxes(tile_heads(b_c * e_tail) * head_f[None], 1, 2))
    kh_t = cast(jnp.swapaxes(tile_heads(k_c * e_tail) * head_f[None], 1, 2))
    a_ab = jnp.where(strict[None], bmm_nt(a_t, b_t), 0.0)
    a_ak = cast(jnp.where(strict[None], bmm_nt(a_t, k_t), 0.0))
    a_rb = cast(jnp.where(incl[None], bmm_nt(r_t, b_t), 0.0))
    a_rk = cast(jnp.where(incl[None], bmm_nt(r_t, k_t), 0.0))
    t_inv = eye_f[None] + a_ab
    pw = cast(a_ab)
    for _ in range(int(math.log2(C)) - 1):
        pw = cast(bmm(pw, pw))
        t_inv = t_inv + bmm(cast(t_inv), pw)
    t16 = cast(t_inv)
    ta = cast(bmm(t16, a_t))
    u0 = bmm(cast(bmm(t16, a_ak)), v_bd)
    o0 = bmm(a_rk, v_bd)
    s0 = bmm(kh_t, v_bd)
    o_lhs = jnp.concatenate([r_t, a_rb], axis=2)
    w_col = jnp.sum(jnp.where(eye[None], jnp.exp(cum_c), 0.0), axis=2, keepdims=True)

    st = st_ref[...]
    outs = []
    for g in range(G):
        st16 = cast(st)
        u = cast(_dot(ta[g], st16) + u0[g])
        o_bd = _dot(o_lhs[g], jnp.concatenate([st16, u], axis=0)) + o0[g]
        st = w_col[g] * st + _dot(bh_t[g], u) + s0[g]
        outs.append(o_bd[0:C] + o_bd[C:2 * C] + o_bd[2 * C:3 * C] + o_bd[3 * C:4 * C])
    st_ref[...] = st

    o = jnp.concatenate(outs, axis=0)
    avg = avg_ref[...].astype(BF16)
    mu = _dot_hilo(o, avg)
    xc = o - mu
    var = _dot_hilo(xc * xc, avg)
    on = xc * lax.rsqrt(var + RWKV_GN_EPS) * gg_ref[...] + gb_ref[...]
    r = r_ref[0]
    k = k_ref[0]
    v = v_ref[0]
    bonus = _dot_hilo(r * k * rk_ref[...], ones_ref[...].astype(BF16)) * v
    o_ref[0] = (on + bonus) * g_ref[0]


def wkv_scan(r, lw, k, v, a, b, g, rk, gg, gb, avg, ones):
    bsz, s, _ = r.shape
    ts = WKV_GROUP * WKV_CHUNK
    tile = pl.BlockSpec((1, ts, W_GRP), lambda bi, i: (bi, i, 0))
    const = lambda arr: pl.BlockSpec(arr.shape, lambda bi, i: (0,) * arr.ndim)
    return pl.pallas_call(
        functools.partial(_wkv_body, ts),
        grid=(bsz, s // ts),
        in_specs=[tile] * 7 + [const(x) for x in (rk, gg, gb, avg, ones)],
        out_specs=tile, out_shape=jax.ShapeDtypeStruct((bsz, s, W_GRP), F32),
        scratch_shapes=[pltpu.VMEM((N_HEADS * WKV_CHUNK, W_GRP), F32)],
        compiler_params=_cparams("parallel", "arbitrary"), name="wkv_scan",
    )(r, lw, k, v, a, b, g, rk, gg, gb, avg, ones)


def _gmlp_math(x, g_ref, b_ref, ws_ref, bs_ref, o_ref):
    ts = x.shape[0]
    u = jax.nn.gelu(x[:, :W_GRP])
    v = _ln(jax.nn.gelu(x[:, W_GRP:]), g_ref[...], b_ref[...]).astype(BF16)
    tr = lax.broadcasted_iota(jnp.int32, (GMLP_CHUNK, GMLP_CHUNK), 0)
    tc = lax.broadcasted_iota(jnp.int32, (GMLP_CHUNK, GMLP_CHUNK), 1)
    ws = [jnp.where(tc <= tr, ws_ref[h], 0.0).astype(BF16) for h in range(N_HEADS)]
    for c in range(ts // GMLP_CHUNK):
        rows = slice(c * GMLP_CHUNK, (c + 1) * GMLP_CHUNK)
        mixed = [_dot(ws[h], v[rows, h * HEAD_DIM:(h + 1) * HEAD_DIM]) + bs_ref[h]
                 for h in range(N_HEADS)]
        o_ref[rows, :] = u[rows, :] * jnp.concatenate(mixed, axis=-1)


def _swap_halves(x, lane):
    w = x.shape[-1]
    half = HEAD_DIM // 2
    fwd = pltpu.roll(x, w - half, 1)
    bwd = pltpu.roll(x, half, 1)
    return jnp.where((lane % HEAD_DIM) < half, fwd, bwd)


def _nsa_prep_math(q, kv, misc, pos0, cos_ref, sin_ref,
                   q_o, qr_o, kc_o, vc_o, ks_o, vs_o, kw_o, vw_o, g_o):
    scale = HEAD_DIM ** -0.5
    cos = jnp.concatenate([cos_ref[...]] * 3, axis=1)
    sin = jnp.concatenate([sin_ref[...]] * 3, axis=1)
    lane_q = lax.broadcasted_iota(jnp.int32, q.shape, 1)
    q_rot = q * cos[:, :W_GRP] + _swap_halves(q, lane_q) * sin[:, :W_GRP]
    lane_kv = lax.broadcasted_iota(jnp.int32, kv.shape, 1)
    kv_rot = kv * cos + _swap_halves(kv, lane_kv) * sin
    q_o[0] = (q * scale).astype(BF16)
    qr_o[0] = (q_rot * (scale * LOG2E)).astype(BF16)
    kc_o[0] = kv[:, 0:64]
    vc_o[0] = kv[:, 64:128]
    ts = kv.shape[0]
    lane = lax.broadcasted_iota(jnp.int32, (ts, LANES), 1)
    key_blk = (pos0 + lax.broadcasted_iota(jnp.int32, (ts, 1), 0)) // SEL_LEN
    ks_o[0] = jnp.concatenate([jnp.where(lane < HEAD_DIM, kv_rot[:, 128:256], 0.0),
                               jnp.where(lane == key_blk, NEG, 0.0)], axis=1).astype(BF16)
    ones_col = jnp.where(lane == HEAD_DIM, 1.0, 0.0)
    vs_first = pltpu.roll(kv[:, 128:256], HEAD_DIM, 1)
    vs_o[0] = jnp.where(lane < HEAD_DIM, vs_first, ones_col).T.astype(BF16)
    kw_o[0] = kv_rot[:, 256:320].astype(BF16)
    vw_first = pltpu.roll(kv[:, 256:384], HEAD_DIM, 1)
    vw_o[0] = jnp.where(lane < HEAD_DIM, vw_first, ones_col).T.astype(BF16)
    g_o[0] = jax.nn.sigmoid(misc)


def _compress_body(kc_ref, vc_ref, pe_ref, w1_ref, w2k_ref, w2vt_ref, ko_ref, vo_ref):
    half = CMP_STRIDE * HEAD_DIM
    for j, c_ref in enumerate((kc_ref, vc_ref)):
        c = c_ref[0].astype(BF16)
        w1 = w1_ref[j]
        lo = _dot(c, w1[:half])
        hi = _dot(c, w1[half:])
        nb = hi.shape[0]
        hi_next = pltpu.roll(hi, nb - 1, 0)
        pe = jnp.broadcast_to(pe_ref[j], (8, 2 * half))
        pe_term = _dot(pe, w1.astype(F32), HI)[0:1]
        h = jax.nn.gelu(lo + hi_next + pe_term)
        if j == 0:
            ko_ref[0] = _dot(h.astype(BF16), w2k_ref[...]).astype(BF16)
        else:
            vo_ref[0] = _dot_nt(w2vt_ref[...], h.astype(BF16)).astype(BF16)


def nsa_compress(kc_r, vc_r, pe, w1, w2k, w2vt):
    bsz, nb, w = kc_r.shape
    blk = pl.BlockSpec((1, nb, w), lambda bi: (bi, 0, 0))
    const = lambda a: pl.BlockSpec(a.shape, lambda bi: (0,) * a.ndim)
    return pl.pallas_call(
        _compress_body, grid=(bsz,),
        in_specs=[blk, blk, const(pe), const(w1), const(w2k), const(w2vt)],
        out_specs=[pl.BlockSpec((1, nb, HEAD_DIM), lambda bi: (bi, 0, 0)),
                   pl.BlockSpec((1, LANES, nb), lambda bi: (bi, 0, 0))],
        out_shape=[jax.ShapeDtypeStruct((bsz, nb, HEAD_DIM), BF16), jax.ShapeDtypeStruct((bsz, LANES, nb), BF16)],
        compiler_params=_cparams("parallel"), name="nsa_compress",
    )(kc_r, vc_r, pe, w1, w2k, w2vt)


def _stack_heads(x):
    return jnp.concatenate([x[:, h * HEAD_DIM:(h + 1) * HEAD_DIM] for h in range(N_HEADS)], axis=0)


def _nsa_attn_body(n_top, q_ref, qr_ref, kc_ref, vct_ref, ks_ref, vst_ref, kw_ref, vwt_ref, g_ref,
                   ov_ref, o_ref):
    T = Q_BLOCK
    i = pl.program_id(1)
    t0 = i * T
    qs = _stack_heads(q_ref[0])
    qrs = _stack_heads(qr_ref[0])
    tile4 = lambda x: jnp.concatenate([x] * N_HEADS, axis=0)

    pos_row = t0 + lax.broadcasted_iota(jnp.int32, (1, T), 1)
    lanes4 = lambda x: jnp.concatenate([x] * N_HEADS, axis=1)
    kc = kc_ref[0]
    n_cmp = kc.shape[0]
    cmp_end = lax.broadcasted_iota(jnp.int32, (n_cmp, 1), 0) * CMP_STRIDE + (CMP_LEN - 1)
    valid_c = lanes4(jnp.where(cmp_end <= pos_row, 1.0, 0.0))
    s_c = jnp.where(valid_c > 0.5, _dot_nt(kc, qs), NEG)
    p_c = jnp.exp(s_c - jnp.max(s_c, axis=0, keepdims=True)) * valid_c
    p_c = p_c / jnp.maximum(jnp.sum(p_c, axis=0, keepdims=True), 1e-20)
    o_c_t = _dot(vct_ref[0], p_c.astype(BF16))
    p_sum = p_c[:, 0:T] + p_c[:, T:2 * T] + p_c[:, 2 * T:3 * T] + p_c[:, 3 * T:4 * T]
    ps_hi, ps_lo = _split_bf16(p_sum)
    ov = ov_ref[...].astype(BF16)
    imp_t = _dot(ov, ps_hi) + _dot(ov, ps_lo)

    WK = WINDOW + T
    start = pl.multiple_of(jnp.maximum(t0 - WINDOW, 0), T)
    kpos_w = start + lax.broadcasted_iota(jnp.int32, (WK, 1), 0)
    bias_w = jnp.where(kpos_w <= pos_row, jnp.where(kpos_w > pos_row - WINDOW, 0.0, NEG), NEG)
    s_w = _dot_nt(kw_ref[0, pl.ds(start, WK), :], qrs) + lanes4(bias_w)
    p_w = jnp.exp2(s_w - jnp.max(s_w, axis=0, keepdims=True))
    acc_w = _dot(vwt_ref[0, :, pl.ds(start, WK)], p_w.astype(BF16))

    n_sel = imp_t.shape[0]
    jj = lax.broadcasted_iota(jnp.int32, (n_sel, T), 0)
    blk = (t0 + lax.broadcasted_iota(jnp.int32, (1, T), 1)) // SEL_LEN
    val = jnp.where(jj == blk, 3e38, jnp.where(jj == 0, 3e38, jnp.where(jj <= blk, imp_t, -1.0)))
    sel_t = jnp.zeros((n_sel, T), F32)
    for _ in range(n_top):
        mx = jnp.max(val, axis=0, keepdims=True)
        idx = jnp.min(jnp.where(val == mx, jj, n_sel), axis=0, keepdims=True)
        hit = jj == idx
        sel_t = jnp.where(hit, 1.0, sel_t)
        val = jnp.where(hit, -2.0, val)
    sel = sel_t.T

    KT = SEL_KT
    unsel = tile4((1.0 - sel).astype(BF16))
    q_aug = jnp.concatenate([qrs, jnp.zeros((N_HEADS * T, HEAD_DIM), BF16), unsel], axis=1)
    krow = lax.broadcasted_iota(jnp.int32, (KT, 1), 0)

    KS = KT // SEL_SUB

    def sel_tile(jt, carry, diagonal):
        m, acc = carry
        k0 = pl.multiple_of(jt * KT, KT)

        def scores(j):
            ks = pl.multiple_of(k0 + j * KS, KS)
            s_t = _dot_nt(ks_ref[0, pl.ds(ks, KS), :], q_aug)
            if diagonal:
                causal = jnp.where(ks + krow[:KS] <= pos_row, 0.0, NEG)
                s_t = s_t + jnp.concatenate([causal] * N_HEADS, axis=1)
            return s_t

        s_next = scores(0)
        for j in range(SEL_SUB):
            s_t = s_next
            if j + 1 < SEL_SUB:
                s_next = scores(j + 1)
            m_new = jnp.maximum(m, jnp.max(s_t, axis=0, keepdims=True))
            p_t = jnp.exp2(s_t - m_new)
            ks = pl.multiple_of(k0 + j * KS, KS)
            acc = jnp.exp2(m - m_new) * acc + _dot(vst_ref[0, :, pl.ds(ks, KS)], p_t.astype(BF16))
            m = m_new
        return m, acc

    init = (jnp.full((1, N_HEADS * T), NEG, F32), jnp.zeros((LANES, N_HEADS * T), F32))
    n_full = t0 // KT
    carry = lax.fori_loop(0, n_full, functools.partial(sel_tile, diagonal=False), init)
    _, acc_t = sel_tile(n_full, carry, True)
    acc = jnp.concatenate([acc_t, acc_w, o_c_t], axis=0).T
    o_c = acc[:, 2 * LANES:2 * LANES + HEAD_DIM]
    o_s = acc[:, :HEAD_DIM] / acc[:, HEAD_DIM:HEAD_DIM + 1]
    o_w = acc[:, LANES:LANES + HEAD_DIM] / acc[:, LANES + HEAD_DIM:LANES + HEAD_DIM + 1]

    g = g_ref[0]
    outs = []
    for h in range(N_HEADS):
        rows = slice(h * T, (h + 1) * T)
        outs.append(g[:, 3 * h:3 * h + 1] * o_c[rows] + g[:, 3 * h + 1:3 * h + 2] * o_s[rows]
                    + g[:, 3 * h + 2:3 * h + 3] * o_w[rows])
    o_ref[0] = jnp.concatenate(outs, axis=-1)


def nsa_attention(q, qr, k_cmp, v_cmp, ks, vs, kw, vw, gates, overlap):
    bsz, s, _ = q.shape
    n_top = min(SEL_TOPN, s // SEL_LEN)
    n_cmp = k_cmp.shape[1]
    qtile = lambda w: pl.BlockSpec((1, Q_BLOCK, w), lambda bi, i: (bi, i, 0))
    full = lambda rows, w: pl.BlockSpec((1, rows, w), lambda bi, i: (bi, 0, 0))
    return pl.pallas_call(
        functools.partial(_nsa_attn_body, n_top),
        grid=(bsz, s // Q_BLOCK),
        in_specs=[qtile(W_GRP), qtile(W_GRP), full(n_cmp, HEAD_DIM), full(LANES, n_cmp),
                  full(s, 2 * LANES), full(LANES, s), full(s, HEAD_DIM), full(LANES, s),
                  qtile(LANES), pl.BlockSpec(overlap.shape, lambda bi, i: (0, 0))],
        out_specs=qtile(W_GRP), out_shape=jax.ShapeDtypeStruct((bsz, s, W_GRP), F32),
        compiler_params=_cparams("parallel", "parallel"), name="nsa_attention",
    )(q, qr, k_cmp, v_cmp, ks, vs, kw, vw, gates, overlap)


def _outproj_math(alpha, parts, x, w_ref, g_ref, b_ref):
    y = alpha * x
    for j, part in enumerate(parts):
        y = y + _dot(part[...].astype(BF16), w_ref[j * W_GRP:(j + 1) * W_GRP, :])
    return _ln(y, g_ref[...], b_ref[...])


def _memkv_body(m_ref, wk_ref, wv_ref, k_o, v_o):
    mb = m_ref[...].astype(BF16)
    k_o[...] = _dot(mb, wk_ref[...]).astype(BF16)
    v_o[...] = _dot(mb, wv_ref[...]).astype(BF16)


def mem_kv(mem2d, wk, wv):
    n, d = mem2d.shape
    full = lambda a: pl.BlockSpec(a.shape, lambda i: (0, 0))
    out = jax.ShapeDtypeStruct((n, d), BF16)
    return pl.pallas_call(
        _memkv_body, grid=(1,),
        in_specs=[full(mem2d), full(wk), full(wv)],
        out_specs=[pl.BlockSpec((n, d), lambda i: (0, 0))] * 2, out_shape=[out, out],
        compiler_params=_cparams("arbitrary"), name="mem_kv",
    )(mem2d, wk, wv)


def _xattn_body(alpha, a_ref, b_ref, c_ref, d_ref, x_ref, w1_ref, g1_ref, b1_ref,
                k_ref, v_ref, wq_ref, wo_ref, g_ref, b_ref_, o_ref):
    x = _outproj_math(alpha, (a_ref, b_ref, c_ref, d_ref), x_ref[...], w1_ref, g1_ref, b1_ref)
    d = x.shape[-1]
    hd = d // N_MEM_HEADS
    q = (_dot(x.astype(BF16), wq_ref[...]) * (hd ** -0.5 * LOG2E)).astype(BF16)
    k = k_ref[0]
    v = v_ref[0]
    cols = [slice(h * hd, (h + 1) * hd) for h in range(N_MEM_HEADS)]
    scores = [_dot_nt(q[:, cs], k[:, cs]) for cs in cols]
    probs = []
    for s in scores:
        p = jnp.exp2(s - jnp.max(s, axis=-1, keepdims=True))
        probs.append((p / jnp.sum(p, axis=-1, keepdims=True)).astype(BF16))
    y = alpha * x
    for cs, p in zip(cols, probs):
        y = y + _dot(_dot(p, v[:, cs]).astype(BF16), wo_ref[cs, :])
    o_ref[...] = _ln(y, g_ref[...], b_ref_[...])


def cross_attention(parts, x2d, seq_len, w_out, g1, b1, k, v, wq, wo, g, b, alpha):
    n, d = x2d.shape
    bsz, s = n // seq_len, seq_len
    tps = s // TM
    m = k.shape[1]
    const = lambda a: pl.BlockSpec(a.shape, lambda bi, i: (0,) * a.ndim)
    row = lambda w: pl.BlockSpec((TM, w), lambda bi, i: (bi * tps + i, 0))
    return pl.pallas_call(
        functools.partial(_xattn_body, alpha),
        grid=(bsz, s // TM),
        in_specs=[row(W_GRP)] * 4 + [row(d), const(w_out), const(g1), const(b1),
                  pl.BlockSpec((1, m, d), lambda bi, i: (bi, 0, 0)),
                  pl.BlockSpec((1, m, d), lambda bi, i: (bi, 0, 0)),
                  const(wq), const(wo), const(g), const(b)],
        out_specs=row(d), out_shape=jax.ShapeDtypeStruct((n, d), F32),
        compiler_params=_cparams("parallel", "parallel"), name="cross_attention",
    )(*parts, x2d, w_out, g1, b1, k, v, wq, wo, g, b)


PER_GRP = N_EXPERTS // N_EXPERT_GROUPS
LPOS_LANE = PER_GRP
SUB = 8
BF16_ROWS = 16


def _router_body(x_ref, rwt_ref, rb_ref, upper_ref, tok_o, lrow_o, cnt_o):
    tm = x_ref.shape[0]
    logits = _dot3(x_ref[...], rwt_ref[...]).T[:N_EXPERTS]
    ex = jnp.exp(logits - jnp.max(logits, axis=0, keepdims=True))
    probs = ex / jnp.sum(ex, axis=0, keepdims=True)
    sel = probs + rb_ref[...]
    srow = [sel[e:e + 1] for e in range(N_EXPERTS)]
    prow = [probs[e:e + 1] for e in range(N_EXPERTS)]
    gscore = []
    for g in range(N_EXPERT_GROUPS):
        r = srow[g * PER_GRP:(g + 1) * PER_GRP]
        best = None
        for a in range(PER_GRP):
            for b in range(a + 1, PER_GRP):
                best = r[a] + r[b] if best is None else jnp.maximum(best, r[a] + r[b])
        gscore.append(best)
    g_idx = jnp.zeros((1, tm), jnp.int32)
    top = gscore[0]
    for g in range(1, N_EXPERT_GROUPS):
        better = gscore[g] > top
        g_idx = jnp.where(better, g, g_idx)
        top = jnp.where(better, gscore[g], top)

    def of_group(rows, e):
        out = rows[e]
        for g in range(1, N_EXPERT_GROUPS):
            out = jnp.where(g_idx == g, rows[g * PER_GRP + e], out)
        return out

    sg = [of_group(srow, e) for e in range(PER_GRP)]
    pg = [of_group(prow, e) for e in range(PER_GRP)]
    w = []
    for e in range(PER_GRP):
        rank = jnp.zeros((1, tm), F32)
        for o in range(PER_GRP):
            if o != e:
                ahead = (sg[o] >= sg[e]) if o < e else (sg[o] > sg[e])
                rank = rank + jnp.where(ahead, 1.0, 0.0)
        w.append(jnp.where(rank < 1.5, pg[e], 0.0))
    w_sum = w[0] + w[1] + w[2] + w[3]
    row8 = lax.broadcasted_iota(jnp.int32, (SUB, tm), 0)
    onehot = jnp.where(row8 == g_idx, 1.0, 0.0)
    before = _dot(onehot.astype(BF16), upper_ref[...])
    cnt = jnp.sum(onehot, axis=1, keepdims=True)
    offs = [jnp.zeros((1, 1), F32)]
    for g in range(1, N_EXPERT_GROUPS):
        offs.append(offs[-1] + cnt[g - 1:g])
    lpos = jnp.zeros((1, tm), F32)
    for g in range(N_EXPERT_GROUPS):
        lpos = lpos + onehot[g:g + 1] * (offs[g] + before[g:g + 1])
    tok = jnp.zeros((SUB, tm), F32)
    for e in range(PER_GRP):
        tok = jnp.where(row8 == e, w[e] / w_sum, tok)
    tok = jnp.where(row8 == LPOS_LANE, lpos, tok)
    tok_o[...] = jnp.concatenate([tok, jnp.zeros((LANES - SUB, tm), F32)], axis=0).T
    lrow_o[0] = lpos.astype(jnp.int32)
    rowc = lax.broadcasted_iota(jnp.int32, (SUB, LANES), 0)
    stats = jnp.zeros((SUB, LANES), F32)
    for g in range(N_EXPERT_GROUPS):
        stats = jnp.where(rowc == g, cnt[g:g + 1], stats)
        stats = jnp.where(rowc == N_EXPERT_GROUPS + g, offs[g], stats)
    cnt_o[0] = stats.astype(jnp.int32)


def moe_router(x2d, rwt, rb_col, upper):
    n, d = x2d.shape
    tm = upper.shape[0]
    nt = n // tm
    const = lambda a: pl.BlockSpec(a.shape, lambda i: (0, 0))
    return pl.pallas_call(
        _router_body, grid=(nt,),
        in_specs=[pl.BlockSpec((tm, d), lambda i: (i, 0)), const(rwt), const(rb_col), const(upper)],
        out_specs=[pl.BlockSpec((tm, LANES), lambda i: (i, 0)),
                   pl.BlockSpec((1, 1, tm), lambda i: (i, 0, 0)),
                   pl.BlockSpec((1, SUB, LANES), lambda i: (i, 0, 0))],
        out_shape=[jax.ShapeDtypeStruct((n, LANES), F32), jax.ShapeDtypeStruct((nt, 1, tm), jnp.int32),
                   jax.ShapeDtypeStruct((nt, SUB, LANES), jnp.int32)],
        compiler_params=_cparams("parallel"), name="moe_router",
    )(x2d, rwt, rb_col, upper)


def _serpentine(tile, step, n_steps):
    return jnp.where(tile % 2 == 0, step, n_steps - 1 - step)


def _split_bf16(x):
    hi = x.astype(BF16)
    return hi, (x - hi.astype(F32)).astype(BF16)


def _moe_body(alpha, offs_ref, cnts_ref, x_ref, tok_ref, lrow_ref, wg_ref, wu_ref, wd_ref, g_ref, b_ref, o_ref,
              xs_ref, gs_ref, acc_ref):
    i = pl.program_id(0)
    step = pl.program_id(1)
    tm = x_ref.shape[0]
    eps = wd_ref.shape[1]

    @pl.when(step == 0)
    def _():
        slot = lax.broadcasted_iota(jnp.int32, (tm, tm), 0)
        perm = jnp.where(slot == lrow_ref[0], 1.0, 0.0).astype(BF16)
        xs_ref[0:tm, :] = _dot(perm, x_ref[...].astype(BF16)).astype(BF16)
        t_hi, t_lo = _split_bf16(tok_ref[...])
        gs_ref[0:tm, :] = _dot(perm, t_hi) + _dot(perm, t_lo)
        xs_ref[tm:, :] = jnp.zeros((MOE_BLK, xs_ref.shape[1]), BF16)
        gs_ref[tm:, :] = jnp.zeros((MOE_BLK, LANES), F32)
        acc_ref[...] = jnp.zeros_like(acc_ref)

    wstep = _serpentine(i, step, pl.num_programs(1))
    grp = (wstep * eps) // PER_GRP
    off = offs_ref[i * N_EXPERT_GROUPS + grp]
    end = off + cnts_ref[i * N_EXPERT_GROUPS + grp]
    lane = lax.broadcasted_iota(jnp.int32, (MOE_BLK, LANES), 1)
    first = off // BF16_ROWS * BF16_ROWS
    for s in range(-(-(tm + BF16_ROWS - 1) // MOE_BLK)):
        lo = pl.multiple_of(first + s * MOE_BLK, BF16_ROWS)

        @pl.when(lo < end)
        def _(lo=lo):
            rows = pl.ds(lo, MOE_BLK)
            xs = xs_ref[rows, :]
            ridx = lo + lax.broadcasted_iota(jnp.int32, (MOE_BLK, 1), 0)
            in_run = jnp.where(ridx >= off, jnp.where(ridx < end, 1.0, 0.0), 0.0)
            gs = gs_ref[rows, :]
            y = jnp.zeros((MOE_BLK, x_ref.shape[1]), F32)
            for k in range(eps):
                gt = _dot(xs, wg_ref[0, k])
                h = (gt * jax.nn.sigmoid(gt) * _dot(xs, wu_ref[0, k])).astype(BF16)
                in_grp = (wstep * eps + k) % PER_GRP
                gcol = jnp.sum(jnp.where(lane == in_grp, gs, 0.0), axis=-1, keepdims=True)
                y = y + (gcol * in_run) * _dot(h, wd_ref[0, k])
            acc_ref[rows, :] += y

    @pl.when(step == pl.num_programs(1) - 1)
    def _():
        lcol = tok_ref[:, LPOS_LANE:LPOS_LANE + 1].astype(jnp.int32)
        slot = lax.broadcasted_iota(jnp.int32, (tm, tm), 1)
        unperm = jnp.where(slot == lcol, 1.0, 0.0).astype(BF16)
        y = _dot(unperm, acc_ref[0:tm, :].astype(BF16))
        o_ref[...] = _ln(alpha * x_ref[...] + y, g_ref[...], b_ref[...])


def moe_ffn(x2d, tok, lrow, offs, cnts, layer, wg, wu, wd, g, b, alpha):
    n, d = x2d.shape
    _, ne, de, _ = wd.shape
    tm = lrow.shape[2]
    const = lambda a: pl.BlockSpec(a.shape, lambda i, e, o, c: (0,) * a.ndim)
    wmap = lambda i, e, o, c: (layer, _serpentine(i, e, ne // MOE_EPS), 0, 0)
    grid_spec = pltpu.PrefetchScalarGridSpec(
        num_scalar_prefetch=2, grid=(n // tm, ne // MOE_EPS),
        in_specs=[pl.BlockSpec((tm, d), lambda i, e, o, c: (i, 0)),
                  pl.BlockSpec((tm, LANES), lambda i, e, o, c: (i, 0)),
                  pl.BlockSpec((1, 1, tm), lambda i, e, o, c: (i, 0, 0)),
                  pl.BlockSpec((1, MOE_EPS, d, de), wmap), pl.BlockSpec((1, MOE_EPS, d, de), wmap),
                  pl.BlockSpec((1, MOE_EPS, de, d), wmap),
                  const(g), const(b)],
        out_specs=pl.BlockSpec((tm, d), lambda i, e, o, c: (i, 0)),
        scratch_shapes=[pltpu.VMEM((tm + MOE_BLK, d), BF16), pltpu.VMEM((tm + MOE_BLK, LANES), F32),
                        pltpu.VMEM((tm + MOE_BLK, d), F32)])
    return pl.pallas_call(
        functools.partial(_moe_body, alpha), grid_spec=grid_spec,
        out_shape=jax.ShapeDtypeStruct((n, d), F32),
        compiler_params=_cparams("parallel", "arbitrary"), name="moe_ffn",
    )(offs, cnts, x2d, tok, lrow, wg, wu, wd, g, b)


def _rope_tables(s):
    inv = ROPE_THETA ** (-jnp.arange(0, HEAD_DIM, 2, dtype=F32) / HEAD_DIM)
    ang = jnp.arange(s, dtype=F32)[:, None] * inv[None, :]
    cos, sin = jnp.cos(ang), jnp.sin(ang)
    cos_h = jnp.concatenate([cos, cos], axis=-1)
    sin_h = jnp.concatenate([-sin, sin], axis=-1)
    return jnp.tile(cos_h, (1, 2)), jnp.tile(sin_h, (1, 2))


def _overlap_matrix(s, n_cmp_pad):
    n_sel = s // SEL_LEN
    assert n_sel <= LANES
    cmp_start = jnp.arange(n_cmp_pad) * CMP_STRIDE
    sel_start = jnp.arange(LANES) * SEL_LEN
    ov = jnp.clip(jnp.minimum(cmp_start[None, :] + CMP_LEN, sel_start[:, None] + SEL_LEN)
                  - jnp.maximum(cmp_start[None, :], sel_start[:, None]), 0, None).astype(F32) / CMP_LEN
    n_cmp = s // CMP_STRIDE - (CMP_LEN // CMP_STRIDE - 1)
    real = (jnp.arange(n_cmp_pad)[None, :] < n_cmp) & (jnp.arange(LANES)[:, None] < n_sel)
    return jnp.where(real, ov, 0.0)


def _pad_rows(w, lo, total):
    out = jnp.zeros((total, w.shape[1]), w.dtype)
    return out.at[lo:lo + w.shape[0]].set(w)


def kernel(x, mem, ln_in_g, ln_in_b, w_in, w_out, conv_w, conv_b, conv_gn_g, conv_gn_b, rwkv_mu, rwkv_w0, rwkv_w_up, rwkv_a0, rwkv_a_up, rwkv_g_up, rwkv_k_k, rwkv_k_a, rwkv_r_k, rwkv_gn_g, rwkv_gn_b, rwkv_v_down, rwkv_v_mu, rwkv_v0, rwkv_v_up, gmlp_ln_g, gmlp_ln_b, gmlp_w_s, gmlp_b_s, nsa_pe_k, nsa_w1_k, nsa_w2_k, nsa_pe_v, nsa_w1_v, nsa_w2_v, ln1_g, ln1_b, xa_wq, xa_wk, xa_wv, xa_wo, ln2_g, ln2_b, router_w, router_bias, moe_w_gate, moe_w_up, moe_w_down, ln3_g, ln3_b):
    bsz, s, d = x.shape
    depth = w_in.shape[0]
    n = bsz * s
    alpha = (2 * depth) ** 0.25
    row = lambda a: a.reshape(1, -1)

    cos_t, sin_t = _rope_tables(s)
    n_blk = s // CMP_STRIDE
    overlap = _overlap_matrix(s, n_blk)
    avg64 = _group_avg_matrix(W_GRP, HEAD_DIM)
    ones64 = avg64 * HEAD_DIM
    rw_pad = jnp.zeros((d, LANES), F32).at[:, :N_EXPERTS].set(router_w)
    moe_tm = min(MOE_TM, n)
    t_idx = jnp.arange(moe_tm)
    upper = (t_idx[:, None] < t_idx[None, :]).astype(BF16)
    mem2d = mem.reshape(bsz * mem.shape[1], d)

    wg_b, wu_b, wd_b = moe_w_gate.astype(BF16), moe_w_up.astype(BF16), moe_w_down.astype(BF16)

    xs = x.reshape(n, d)
    v_first = None
    for l in range(depth):
        main = IN_SPLITS[-1][1]
        w_misc = jnp.zeros((d, LANES), F32).at[:, :w_in.shape[2] - main].set(w_in[l, :, main:])
        if l > 0:
            w_misc = w_misc.at[:, MISC_VD_OFF:MISC_VD_OFF + rwkv_v_down.shape[2]].set(rwkv_v_down[l - 1])
        gmlp_p = (row(gmlp_ln_g[l]), row(gmlp_ln_b[l]), gmlp_w_s[l], gmlp_b_s[l].reshape(N_HEADS, GMLP_CHUNK, 1))
        conv_p = (conv_w[l], row(conv_b[l]), row(conv_gn_g[l]), row(conv_gn_b[l]), avg64)
        rwkv_p = (row(rwkv_mu[l]), row(rwkv_w0[l]), row(rwkv_a0[l]), _pad_rows(rwkv_w_up[l], 0, W_GRP),
                  _pad_rows(rwkv_a_up[l], 64, W_GRP), _pad_rows(rwkv_g_up[l], 128, W_GRP),
                  row(rwkv_k_k[l]), row(rwkv_k_a[l]), ones64)
        if l > 0:
            vmu = jnp.zeros((1, LANES), F32).at[0, MISC_VD_OFF:MISC_VD_OFF + rwkv_v_mu.shape[1]].set(rwkv_v_mu[l - 1])
            rwkv_p += (vmu, row(rwkv_v0[l - 1]), _pad_rows(rwkv_v_up[l - 1], MISC_VD_OFF, LANES))
        outs = in_proj(xs, s, row(ln_in_g), row(ln_in_b), w_in, l, w_misc.astype(BF16), cos_t, sin_t,
                       conv_p, gmlp_p, rwkv_p, v_first, apply_ln=(l == 0))
        if l == 0:
            xs, outs = outs[0], outs[1:]
        out_a = outs[0]
        r_, lw_, k_, v_, a_, b_, g_ = outs[1:1 + N_RWKV_OUT]
        out_c = outs[1 + N_RWKV_OUT]
        q_b, qr_b, kc, vc, ks, vs, kw, vw, gates = outs[2 + N_RWKV_OUT:]
        if l == 0:
            v_first = v_
        out_b = wkv_scan(r_, lw_, k_, v_, a_, b_, g_, row(rwkv_r_k[l]), row(rwkv_gn_g[l]),
                         row(rwkv_gn_b[l]), avg64, ones64)

        pe = jnp.stack([nsa_pe_k[l].reshape(1, -1), nsa_pe_v[l].reshape(1, -1)])
        w1 = jnp.stack([nsa_w1_k[l], nsa_w1_v[l]]).astype(BF16)
        w2vt = jnp.zeros((LANES, nsa_w2_v.shape[1]), F32).at[:HEAD_DIM].set(nsa_w2_v[l].T).astype(BF16)
        k_cmp, v_cmp = nsa_compress(kc.reshape(bsz, n_blk, CMP_STRIDE * HEAD_DIM),
                                    vc.reshape(bsz, n_blk, CMP_STRIDE * HEAD_DIM), pe, w1,
                                    nsa_w2_k[l].astype(BF16), w2vt)
        out_d = nsa_attention(q_b, qr_b, k_cmp, v_cmp, ks, vs, kw, vw, gates, overlap)

        flat = lambda a: a.reshape(n, W_GRP)
        mk, mv = mem_kv(mem2d, xa_wk[l].astype(BF16), xa_wv[l].astype(BF16))
        m_len = mem.shape[1]
        xs = cross_attention((flat(out_a), flat(out_b), out_c, flat(out_d)), xs, s, w_out[l].astype(BF16),
                             row(ln1_g[l]), row(ln1_b[l]), mk.reshape(bsz, m_len, d), mv.reshape(bsz, m_len, d),
                             xa_wq[l].astype(BF16), xa_wo[l].astype(BF16), row(ln2_g[l]), row(ln2_b[l]), alpha)

        tok, lrow, stats = moe_router(xs, rw_pad, router_bias.reshape(-1, 1), upper)
        cnts = stats[:, :N_EXPERT_GROUPS, 0].reshape(-1)
        offs = stats[:, N_EXPERT_GROUPS:2 * N_EXPERT_GROUPS, 0].reshape(-1)
        xs = moe_ffn(xs, tok, lrow, offs, cnts, l, wg_b, wu_b, wd_b,
                     row(ln3_g[l]), row(ln3_b[l]), alpha)
    return xs.reshape(bsz, s, d)
```

```python
import functools
import math

import jax
import jax.numpy as jnp
from jax import lax
from jax.experimental import pallas as pl
from jax.experimental.pallas import tpu as pltpu

F32 = jnp.float32
BF16 = jnp.bfloat16
HI = lax.Precision.HIGHEST

HEAD_DIM = 64
N_HEADS = 4
W_GRP = 256
CONV_WIDTH = 31
GMLP_CHUNK = 128
CMP_LEN = 32
CMP_STRIDE = 16
SEL_LEN = 64
SEL_TOPN = 16
WINDOW = 512
Q_BLOCK = 256
ROPE_THETA = 10000.0
N_MEM_HEADS = 4
N_EXPERTS = 16
N_EXPERT_GROUPS = 4
LN_EPS = 1e-5
RWKV_GN_EPS = 64e-5
NEG = -1e30
LOG2E = math.log2(math.e)
LANES = 128
WKV_CHUNK = 64
WKV_GROUP = 8

TM = 512
XA_TM = 1024
MOE_TM = 1024
_RUN_SD = math.sqrt(MOE_TM * (N_EXPERT_GROUPS - 1)) / N_EXPERT_GROUPS
MOE_BLK = 16 * -(-(MOE_TM // N_EXPERT_GROUPS + int(2 * _RUN_SD) + 15) // 16)
MOE_EPS = 4
SEL_SUB = 2
SEL_KT = 1024
VMEM_LIMIT = 56 * 1024 * 1024


def _cparams(*sem):
    return pltpu.CompilerParams(dimension_semantics=sem, vmem_limit_bytes=VMEM_LIMIT)


def _ln(x, g, b, eps=LN_EPS):
    mu = jnp.mean(x, axis=-1, keepdims=True)
    xc = x - mu
    var = jnp.mean(xc * xc, axis=-1, keepdims=True)
    return xc * lax.rsqrt(var + eps) * g + b


def _dot(a, b, precision=None):
    return jnp.dot(a, b, preferred_element_type=F32, precision=precision)


def _dot_nt(a, b, precision=None):
    return lax.dot_general(a, b, (((1,), (1,)), ((), ())),
                           preferred_element_type=F32, precision=precision)


def _dot_hilo(x, w_bf16):
    hi = x.astype(BF16)
    lo = (x - hi.astype(F32)).astype(BF16)
    return _dot(hi, w_bf16) + _dot(lo, w_bf16)


def _dot3(x, w):
    x_hi = x.astype(BF16)
    x_lo = (x - x_hi.astype(F32)).astype(BF16)
    w_hi = w.astype(BF16)
    w_lo = (w - w_hi.astype(F32)).astype(BF16)
    return _dot(x_hi, w_hi) + _dot(x_lo, w_hi) + _dot(x_hi, w_lo)


def _group_avg_matrix(width, group):
    r = jnp.arange(width)[:, None] // group
    c = jnp.arange(width)[None, :] // group
    return jnp.where(r == c, 1.0 / group, 0.0).astype(F32)


IN_SPLITS = (("conv", 0, 512), ("rwkv", 512, 1536), ("gmlp", 1536, 2048),
             ("q", 2048, 2304), ("kv", 2304, 2688), ("misc", 2688, 2816))
MISC_VD_OFF = 32


N_CONV_P, N_GMLP_P, N_RWKV_P, N_VMIX_P = 5, 4, 9, 3
N_NSA_OUT, N_RWKV_OUT = 9, 7


def _inproj_body(apply_ln, has_vfirst, tiles_per_seq, x_ref, g_ref, b_ref, w_ref, wm_ref, cos_ref, sin_ref, *rest):
    take = lambda k: (rest[:k], rest[k:])
    conv_p, rest = take(N_CONV_P)
    gmlp_p, rest = take(N_GMLP_P)
    rwkv_p, rest = take(N_RWKV_P + (N_VMIX_P if has_vfirst else 0))
    vf_ref = None
    if has_vfirst:
        (vf_ref,), rest = take(1)
    if apply_ln:
        (xln_o,), rest = take(1)
    (conv_o,), rest = take(1)
    rwkv_o, rest = take(N_RWKV_OUT)
    (gmlp_o,), rest = take(1)
    nsa_o, rest = take(N_NSA_OUT)
    wb_ref, hbuf, shifted, buf, mbuf = rest
    main = IN_SPLITS[-1][1]

    @pl.when(pl.program_id(0) == 0)
    def _():
        wb_ref[...] = w_ref[0, :, :main].astype(BF16)

    x = x_ref[...]
    if apply_ln:
        x = _ln(x, g_ref[...], b_ref[...])
        xln_o[...] = x
    xb = x.astype(BF16)
    cols = {name: (lo, hi) for name, lo, hi in IN_SPLITS}
    proj = lambda name: _dot(xb, wb_ref[:, cols[name][0]:cols[name][1]])
    tile_in_seq = pl.program_id(0) % tiles_per_seq
    first = tile_in_seq == 0
    misc = _dot(xb, wm_ref[...])
    _conv_math(proj("conv"), first, *conv_p, conv_o, hbuf, shifted)
    _rwkv_prep_math(proj("rwkv"), misc, first, vf_ref, rwkv_p, rwkv_o, buf, mbuf)
    _gmlp_math(proj("gmlp"), *gmlp_p, gmlp_o)
    _nsa_prep_math(proj("q"), proj("kv"), misc, tile_in_seq * x.shape[0], cos_ref, sin_ref, *nsa_o)


def in_proj(x2d, seq_len, g, b, w_in, layer, w_misc, cos_t, sin_t, conv_p, gmlp_p, rwkv_p, v_first, apply_ln):
    n, d = x2d.shape
    bsz = n // seq_len
    tps = seq_len // TM
    row = lambda w: pl.BlockSpec((TM, w), lambda i: (i, 0))
    const = lambda a: pl.BlockSpec(a.shape, lambda i: (0,) * a.ndim)
    seq = lambda w: pl.BlockSpec((1, TM, w), lambda i: (i // tps, i % tps, 0))
    sd = lambda w, dt: jax.ShapeDtypeStruct((bsz, seq_len, w), dt)
    flat = lambda w: jax.ShapeDtypeStruct((n, w), F32)
    out_shapes = ([flat(W_GRP)] + [sd(W_GRP, F32)] * N_RWKV_OUT + [flat(W_GRP)]
                  + [sd(W_GRP, BF16), sd(W_GRP, BF16), sd(HEAD_DIM, F32), sd(HEAD_DIM, F32),
                     sd(2 * LANES, BF16), jax.ShapeDtypeStruct((bsz, LANES, seq_len), BF16),
                     sd(HEAD_DIM, BF16), jax.ShapeDtypeStruct((bsz, LANES, seq_len), BF16), sd(LANES, F32)])
    out_specs = ([row(W_GRP)] + [seq(W_GRP)] * N_RWKV_OUT + [row(W_GRP)]
                 + [seq(W_GRP), seq(W_GRP), seq(HEAD_DIM), seq(HEAD_DIM), seq(2 * LANES),
                    pl.BlockSpec((1, LANES, TM), lambda i: (i // tps, 0, i % tps)),
                    seq(HEAD_DIM), pl.BlockSpec((1, LANES, TM), lambda i: (i // tps, 0, i % tps)), seq(LANES)])
    if apply_ln:
        out_shapes = [flat(d)] + out_shapes
        out_specs = [row(d)] + out_specs
    has_vfirst = v_first is not None
    assert len(conv_p) == N_CONV_P and len(gmlp_p) == N_GMLP_P
    assert len(rwkv_p) == N_RWKV_P + (N_VMIX_P if has_vfirst else 0)
    tab = pl.BlockSpec((TM, LANES), lambda i: (i % tps, 0))
    params = list(conv_p) + list(gmlp_p) + list(rwkv_p)
    inputs = [x2d, g, b, w_in, w_misc, cos_t, sin_t] + params + ([v_first] if has_vfirst else [])
    in_specs = ([row(d), const(g), const(b),
                 pl.BlockSpec((1,) + w_in.shape[1:], lambda i: (layer, 0, 0), pipeline_mode=pl.Buffered(1)),
                 const(w_misc), tab, tab] + [const(a) for a in params]
                + ([seq(W_GRP)] if has_vfirst else []))
    return pl.pallas_call(
        functools.partial(_inproj_body, apply_ln, has_vfirst, tps),
        grid=(n // TM,),
        in_specs=in_specs, out_specs=out_specs, out_shape=out_shapes,
        scratch_shapes=[pltpu.VMEM((d, IN_SPLITS[-1][1]), BF16),
                        pltpu.VMEM((TM + CONV_HALO, W_GRP), F32), pltpu.VMEM((TM + CONV_HALO, W_GRP), F32),
                        pltpu.VMEM((TM + SHIFT_HALO, 4 * W_GRP), F32), pltpu.VMEM((TM + SHIFT_HALO, LANES), F32)],
        compiler_params=_cparams("arbitrary"), name="in_proj",
    )(*inputs)


CONV_HALO = 32


def _carry_rows(buf, halo, first):
    tail = buf.shape[0] - halo

    @pl.when(first)
    def _():
        buf[0:halo, :] = jnp.zeros((halo, buf.shape[1]), buf.dtype)

    @pl.when(jnp.logical_not(first))
    def _():
        buf[0:halo, :] = buf[tail:tail + halo, :]


def _conv_math(cur, first, w_ref, b_ref, gg_ref, gb_ref, avg_ref, o_ref, hbuf, shifted):
    ts = cur.shape[0]
    h = cur[:, :W_GRP] * jax.nn.sigmoid(cur[:, W_GRP:])
    _carry_rows(hbuf, CONV_HALO, first)
    hbuf[CONV_HALO:, :] = h
    acc = jnp.zeros((ts, W_GRP), F32)
    base = CONV_HALO - (CONV_WIDTH - 1)
    for phase in range(SUB):
        taps = [j for j in range(CONV_WIDTH) if (base + j) % SUB == phase]
        span = (base + taps[-1]) // SUB * SUB + ts
        shifted[0:span, :] = hbuf[phase:phase + span, :]
        for j in taps:
            lo = (base + j) // SUB * SUB
            acc = acc + w_ref[j:j + 1, :] * shifted[lo:lo + ts, :]
    acc = acc + b_ref[...]
    avg = avg_ref[...].astype(BF16)
    mu = _dot_hilo(acc, avg)
    xc = acc - mu
    var = _dot_hilo(xc * xc, avg)
    y = xc * lax.rsqrt(var + LN_EPS) * gg_ref[...] + gb_ref[...]
    o_ref[...] = y * jax.nn.sigmoid(y)


SHIFT_HALO = 8


def _shift_prev(buf, cur, first):
    ts = cur.shape[0]
    _carry_rows(buf, SHIFT_HALO, first)
    buf[SHIFT_HALO:, :] = cur
    return buf[SHIFT_HALO - 1:SHIFT_HALO - 1 + ts, :]


def _rwkv_prep_math(cur, misc, first, vf_ref, params, outs, buf, mbuf):
    mu_ref, w0_ref, a0_ref, wup_ref, aup_ref, gup_ref, kk_ref, ka_ref, ones_ref = params[:9]
    r_o, lw_o, k_o, v_o, a_o, b_o, g_o = outs
    prev = _shift_prev(buf, cur, first)
    y = cur + mu_ref[...] * (prev - cur)
    r = y[:, 0:256]
    k = y[:, 256:512]
    v = y[:, 512:768]
    lora = y[:, 768:1024]
    w = w0_ref[...] + _dot3(jnp.tanh(lora), wup_ref[...])
    a = jax.nn.sigmoid(a0_ref[...] + _dot3(lora, aup_ref[...]))
    g = _dot3(jax.nn.sigmoid(lora), gup_ref[...])
    z = -w
    sp = jnp.maximum(z, 0.0) + jnp.log(1.0 + jnp.exp(-jnp.abs(z)))
    lw = -jnp.exp(-sp - 0.5)
    if vf_ref is not None:
        vmu_ref, v0_ref, vup_ref = params[9:]
        mprev = _shift_prev(mbuf, misc, first)
        vd = misc + vmu_ref[...] * (mprev - misc)
        v_mix = jax.nn.sigmoid(v0_ref[...] + _dot3(vd, vup_ref[...]))
        v = v + (vf_ref[0] - v) * v_mix
    kk = k * kk_ref[...]
    n2 = _dot_hilo(kk * kk, ones_ref[...].astype(BF16))
    kk = kk / jnp.maximum(jnp.sqrt(n2), 1e-12)
    k2 = k * (1.0 + (a - 1.0) * ka_ref[...])
    r_o[0] = r
    lw_o[0] = lw
    k_o[0] = k2
    v_o[0] = v
    a_o[0] = -kk
    b_o[0] = kk * a
    g_o[0] = g


def _block_diag(x, headmask):
    return jnp.concatenate([x] * N_HEADS, axis=0) * headmask


def _wkv_body(ts, r_ref, lw_ref, k_ref, v_ref, a_ref, b_ref, g_ref, rk_ref, gg_ref, gb_ref,
              avg_ref, ones_ref, o_ref, st_ref):
    C = WKV_CHUNK
    n = N_HEADS * C

    @pl.when(pl.program_id(1) == 0)
    def _():
        st_ref[...] = jnp.zeros_like(st_ref)

    ri = lax.broadcasted_iota(jnp.int32, (n, n), 0)
    ci = lax.broadcasted_iota(jnp.int32, (n, n), 1)
    head_f = jnp.where((ri // C) == (ci // HEAD_DIM), 1.0, 0.0)
    head_b = head_f.astype(BF16)
    lag = jnp.where((ri // C) == (ci // C), (ri % C) - (ci % C), -1)
    strict = lag > 0
    incl = lag >= 0
    eye = ri == ci
    eye_f = jnp.where(eye, 1.0, 0.0)
    tr = lax.broadcasted_iota(jnp.int32, (C, C), 0)
    tc = lax.broadcasted_iota(jnp.int32, (C, C), 1)
    tri = jnp.where(tc <= tr, 1.0, 0.0).astype(F32)
    cast = lambda x: x.astype(BF16)
    G = ts // C
    split = lambda ref: ref[0].reshape(G, C, W_GRP)
    tile_heads = lambda x: jnp.concatenate([x] * N_HEADS, axis=1)
    bd16 = lambda x: tile_heads(cast(x)) * head_b[None]
    bmm = lambda x, y: lax.dot_general(x, y, (((2,), (1,)), ((0,), (0,))), preferred_element_type=F32)
    bmm_nt = lambda x, y: lax.dot_general(x, y, (((2,), (2,)), ((0,), (0,))), preferred_element_type=F32)

    lw = split(lw_ref)
    cum = lax.dot_general(jnp.broadcast_to(tri[None], (G, C, C)), lw, (((2,), (1,)), ((0,), (0,))),
                          preferred_element_type=F32, precision=HI)
    cum_c = cum[:, C - 1:C, :]
    e_in = jnp.exp(cum)
    e_neg = jnp.exp(-cum)
    e_tail = jnp.exp(cum_c - cum)
    b_c = split(b_ref)
    k_c = split(k_ref)
    a_t = bd16(split(a_ref) * jnp.exp(cum - lw))
    r_t = bd16(split(r_ref) * e_in)
    b_t = bd16(b_c * e_neg)
    k_t = bd16(k_c * e_neg)
    v_bd = bd16(split(v_ref))
    bh_t = cast(jnp.swapaxes(tile_heads(b_c * e_tail) * head_f[None], 1, 2))
    kh_t = cast(jnp.swapaxes(tile_heads(k_c * e_tail) * head_f[None], 1, 2))
    a_ab = jnp.where(strict[None], bmm_nt(a_t, b_t), 0.0)
    a_ak = cast(jnp.where(strict[None], bmm_nt(a_t, k_t), 0.0))
    a_rb = cast(jnp.where(incl[None], bmm_nt(r_t, b_t), 0.0))
    a_rk = cast(jnp.where(incl[None], bmm_nt(r_t, k_t), 0.0))
    t_inv = eye_f[None] + a_ab
    pw = cast(a_ab)
    for _ in range(int(math.log2(C)) - 1):
        pw = cast(bmm(pw, pw))
        t_inv = t_inv + bmm(cast(t_inv), pw)
    t16 = cast(t_inv)
    ta = cast(bmm(t16, a_t))
    u0 = bmm(cast(bmm(t16, a_ak)), v_bd)
    o0 = bmm(a_rk, v_bd)
    s0 = bmm(kh_t, v_bd)
    o_lhs = jnp.concatenate([r_t, a_rb], axis=2)
    w_col = jnp.sum(jnp.where(eye[None], jnp.exp(cum_c), 0.0), axis=2, keepdims=True)

    st = st_ref[...]
    outs = []
    for g in range(G):
        st16 = cast(st)
        u = cast(_dot(ta[g], st16) + u0[g])
        o_bd = _dot(o_lhs[g], jnp.concatenate([st16, u], axis=0)) + o0[g]
        st = w_col[g] * st + _dot(bh_t[g], u) + s0[g]
        outs.append(o_bd[0:C] + o_bd[C:2 * C] + o_bd[2 * C:3 * C] + o_bd[3 * C:4 * C])
    st_ref[...] = st

    o = jnp.concatenate(outs, axis=0)
    avg = avg_ref[...].astype(BF16)
    mu = _dot_hilo(o, avg)
    xc = o - mu
    var = _dot_hilo(xc * xc, avg)
    on = xc * lax.rsqrt(var + RWKV_GN_EPS) * gg_ref[...] + gb_ref[...]
    r = r_ref[0]
    k = k_ref[0]
    v = v_ref[0]
    bonus = _dot_hilo(r * k * rk_ref[...], ones_ref[...].astype(BF16)) * v
    o_ref[0] = (on + bonus) * g_ref[0]


def wkv_scan(r, lw, k, v, a, b, g, rk, gg, gb, avg, ones):
    bsz, s, _ = r.shape
    ts = WKV_GROUP * WKV_CHUNK
    tile = pl.BlockSpec((1, ts, W_GRP), lambda bi, i: (bi, i, 0))
    const = lambda arr: pl.BlockSpec(arr.shape, lambda bi, i: (0,) * arr.ndim)
    return pl.pallas_call(
        functools.partial(_wkv_body, ts),
        grid=(bsz, s // ts),
        in_specs=[tile] * 7 + [const(x) for x in (rk, gg, gb, avg, ones)],
        out_specs=tile, out_shape=jax.ShapeDtypeStruct((bsz, s, W_GRP), F32),
        scratch_shapes=[pltpu.VMEM((N_HEADS * WKV_CHUNK, W_GRP), F32)],
        compiler_params=_cparams("parallel", "arbitrary"), name="wkv_scan",
    )(r, lw, k, v, a, b, g, rk, gg, gb, avg, ones)


def _gmlp_math(x, g_ref, b_ref, ws_ref, bs_ref, o_ref):
    ts = x.shape[0]
    u = jax.nn.gelu(x[:, :W_GRP])
    v = _ln(jax.nn.gelu(x[:, W_GRP:]), g_ref[...], b_ref[...]).astype(BF16)
    tr = lax.broadcasted_iota(jnp.int32, (GMLP_CHUNK, GMLP_CHUNK), 0)
    tc = lax.broadcasted_iota(jnp.int32, (GMLP_CHUNK, GMLP_CHUNK), 1)
    ws = [jnp.where(tc <= tr, ws_ref[h], 0.0).astype(BF16) for h in range(N_HEADS)]
    for c in range(ts // GMLP_CHUNK):
        rows = slice(c * GMLP_CHUNK, (c + 1) * GMLP_CHUNK)
        mixed = [_dot(ws[h], v[rows, h * HEAD_DIM:(h + 1) * HEAD_DIM]) + bs_ref[h]
                 for h in range(N_HEADS)]
        o_ref[rows, :] = u[rows, :] * jnp.concatenate(mixed, axis=-1)


def _swap_halves(x, lane):
    w = x.shape[-1]
    half = HEAD_DIM // 2
    fwd = pltpu.roll(x, w - half, 1)
    bwd = pltpu.roll(x, half, 1)
    return jnp.where((lane % HEAD_DIM) < half, fwd, bwd)


def _nsa_prep_math(q, kv, misc, pos0, cos_ref, sin_ref,
                   q_o, qr_o, kc_o, vc_o, ks_o, vs_o, kw_o, vw_o, g_o):
    scale = HEAD_DIM ** -0.5
    cos = jnp.concatenate([cos_ref[...]] * 3, axis=1)
    sin = jnp.concatenate([sin_ref[...]] * 3, axis=1)
    lane_q = lax.broadcasted_iota(jnp.int32, q.shape, 1)
    q_rot = q * cos[:, :W_GRP] + _swap_halves(q, lane_q) * sin[:, :W_GRP]
    lane_kv = lax.broadcasted_iota(jnp.int32, kv.shape, 1)
    kv_rot = kv * cos + _swap_halves(kv, lane_kv) * sin
    q_o[0] = (q * scale).astype(BF16)
    qr_o[0] = (q_rot * (scale * LOG2E)).astype(BF16)
    kc_o[0] = kv[:, 0:64]
    vc_o[0] = kv[:, 64:128]
    ts = kv.shape[0]
    lane = lax.broadcasted_iota(jnp.int32, (ts, LANES), 1)
    key_blk = (pos0 + lax.broadcasted_iota(jnp.int32, (ts, 1), 0)) // SEL_LEN
    ks_o[0] = jnp.concatenate([jnp.where(lane < HEAD_DIM, kv_rot[:, 128:256], 0.0),
                               jnp.where(lane == key_blk, NEG, 0.0)], axis=1).astype(BF16)
    ones_col = jnp.where(lane == HEAD_DIM, 1.0, 0.0)
    vs_first = pltpu.roll(kv[:, 128:256], HEAD_DIM, 1)
    vs_o[0] = jnp.where(lane < HEAD_DIM, vs_first, ones_col).T.astype(BF16)
    kw_o[0] = kv_rot[:, 256:320].astype(BF16)
    vw_first = pltpu.roll(kv[:, 256:384], HEAD_DIM, 1)
    vw_o[0] = jnp.where(lane < HEAD_DIM, vw_first, ones_col).T.astype(BF16)
    g_o[0] = jax.nn.sigmoid(misc)


def _compress_body(kc_ref, vc_ref, pe_ref, w1_ref, w2k_ref, w2vt_ref, ko_ref, vo_ref):
    half = CMP_STRIDE * HEAD_DIM
    for j, c_ref in enumerate((kc_ref, vc_ref)):
        c = c_ref[0].astype(BF16)
        w1 = w1_ref[j]
        lo = _dot(c, w1[:half])
        hi = _dot(c, w1[half:])
        nb = hi.shape[0]
        hi_next = pltpu.roll(hi, nb - 1, 0)
        pe = jnp.broadcast_to(pe_ref[j], (8, 2 * half))
        pe_term = _dot(pe, w1.astype(F32), HI)[0:1]
        h = jax.nn.gelu(lo + hi_next + pe_term)
        if j == 0:
            ko_ref[0] = _dot(h.astype(BF16), w2k_ref[...]).astype(BF16)
        else:
            vo_ref[0] = _dot_nt(w2vt_ref[...], h.astype(BF16)).astype(BF16)


def nsa_compress(kc_r, vc_r, pe, w1, w2k, w2vt):
    bsz, nb, w = kc_r.shape
    blk = pl.BlockSpec((1, nb, w), lambda bi: (bi, 0, 0))
    const = lambda a: pl.BlockSpec(a.shape, lambda bi: (0,) * a.ndim)
    return pl.pallas_call(
        _compress_body, grid=(bsz,),
        in_specs=[blk, blk, const(pe), const(w1), const(w2k), const(w2vt)],
        out_specs=[pl.BlockSpec((1, nb, HEAD_DIM), lambda bi: (bi, 0, 0)),
                   pl.BlockSpec((1, LANES, nb), lambda bi: (bi, 0, 0))],
        out_shape=[jax.ShapeDtypeStruct((bsz, nb, HEAD_DIM), BF16), jax.ShapeDtypeStruct((bsz, LANES, nb), BF16)],
        compiler_params=_cparams("parallel"), name="nsa_compress",
    )(kc_r, vc_r, pe, w1, w2k, w2vt)


def _stack_heads(x):
    return jnp.concatenate([x[:, h * HEAD_DIM:(h + 1) * HEAD_DIM] for h in range(N_HEADS)], axis=0)


def _nsa_attn_body(n_top, q_ref, qr_ref, kc_ref, vct_ref, ks_ref, vst_ref, kw_ref, vwt_ref, g_ref,
                   ov_ref, o_ref):
    T = Q_BLOCK
    i = pl.program_id(1)
    t0 = i * T
    qs = _stack_heads(q_ref[0])
    qrs = _stack_heads(qr_ref[0])
    tile4 = lambda x: jnp.concatenate([x] * N_HEADS, axis=0)

    pos_row = t0 + lax.broadcasted_iota(jnp.int32, (1, T), 1)
    lanes4 = lambda x: jnp.concatenate([x] * N_HEADS, axis=1)
    kc = kc_ref[0]
    n_cmp = kc.shape[0]
    cmp_end = lax.broadcasted_iota(jnp.int32, (n_cmp, 1), 0) * CMP_STRIDE + (CMP_LEN - 1)
    valid_c = lanes4(jnp.where(cmp_end <= pos_row, 1.0, 0.0))
    s_c = jnp.where(valid_c > 0.5, _dot_nt(kc, qs), NEG)
    p_c = jnp.exp(s_c - jnp.max(s_c, axis=0, keepdims=True)) * valid_c
    p_c = p_c / jnp.maximum(jnp.sum(p_c, axis=0, keepdims=True), 1e-20)
    o_c_t = _dot(vct_ref[0], p_c.astype(BF16))
    p_sum = p_c[:, 0:T] + p_c[:, T:2 * T] + p_c[:, 2 * T:3 * T] + p_c[:, 3 * T:4 * T]
    ps_hi, ps_lo = _split_bf16(p_sum)
    ov = ov_ref[...].astype(BF16)
    imp_t = _dot(ov, ps_hi) + _dot(ov, ps_lo)

    WK = WINDOW + T
    start = pl.multiple_of(jnp.maximum(t0 - WINDOW, 0), T)
    kpos_w = start + lax.broadcasted_iota(jnp.int32, (WK, 1), 0)
    bias_w = jnp.where(kpos_w <= pos_row, jnp.where(kpos_w > pos_row - WINDOW, 0.0, NEG), NEG)
    s_w = _dot_nt(kw_ref[0, pl.ds(start, WK), :], qrs) + lanes4(bias_w)
    p_w = jnp.exp2(s_w - jnp.max(s_w, axis=0, keepdims=True))
    acc_w = _dot(vwt_ref[0, :, pl.ds(start, WK)], p_w.astype(BF16))

    n_sel = imp_t.shape[0]
    jj = lax.broadcasted_iota(jnp.int32, (n_sel, T), 0)
    blk = (t0 + lax.broadcasted_iota(jnp.int32, (1, T), 1)) // SEL_LEN
    val = jnp.where(jj == blk, 3e38, jnp.where(jj == 0, 3e38, jnp.where(jj <= blk, imp_t, -1.0)))
    sel_t = jnp.zeros((n_sel, T), F32)
    for _ in range(n_top):
        mx = jnp.max(val, axis=0, keepdims=True)
        idx = jnp.min(jnp.where(val == mx, jj, n_sel), axis=0, keepdims=True)
        hit = jj == idx
        sel_t = jnp.where(hit, 1.0, sel_t)
        val = jnp.where(hit, -2.0, val)
    sel = sel_t.T

    KT = SEL_KT
    unsel = tile4((1.0 - sel).astype(BF16))
    q_aug = jnp.concatenate([qrs, jnp.zeros((N_HEADS * T, HEAD_DIM), BF16), unsel], axis=1)
    krow = lax.broadcasted_iota(jnp.int32, (KT, 1), 0)

    KS = KT // SEL_SUB

    def sel_tile(jt, carry, diagonal):
        m, acc = carry
        k0 = pl.multiple_of(jt * KT, KT)

        def scores(j):
            ks = pl.multiple_of(k0 + j * KS, KS)
            s_t = _dot_nt(ks_ref[0, pl.ds(ks, KS), :], q_aug)
            if diagonal:
                causal = jnp.where(ks + krow[:KS] <= pos_row, 0.0, NEG)
                s_t = s_t + jnp.concatenate([causal] * N_HEADS, axis=1)
            return s_t

        s_next = scores(0)
        for j in range(SEL_SUB):
            s_t = s_next
            if j + 1 < SEL_SUB:
                s_next = scores(j + 1)
            m_new = jnp.maximum(m, jnp.max(s_t, axis=0, keepdims=True))
            p_t = jnp.exp2(s_t - m_new)
            ks = pl.multiple_of(k0 + j * KS, KS)
            acc = jnp.exp2(m - m_new) * acc + _dot(vst_ref[0, :, pl.ds(ks, KS)], p_t.astype(BF16))
            m = m_new
        return m, acc

    init = (jnp.full((1, N_HEADS * T), NEG, F32), jnp.zeros((LANES, N_HEADS * T), F32))
    n_full = t0 // KT
    carry = lax.fori_loop(0, n_full, functools.partial(sel_tile, diagonal=False), init)
    _, acc_t = sel_tile(n_full, carry, True)
    acc = jnp.concatenate([acc_t, acc_w, o_c_t], axis=0).T
    o_c = acc[:, 2 * LANES:2 * LANES + HEAD_DIM]
    o_s = acc[:, :HEAD_DIM] / acc[:, HEAD_DIM:HEAD_DIM + 1]
    o_w = acc[:, LANES:LANES + HEAD_DIM] / acc[:, LANES + HEAD_DIM:LANES + HEAD_DIM + 1]

    g = g_ref[0]
    outs = []
    for h in range(N_HEADS):
        rows = slice(h * T, (h + 1) * T)
        outs.append(g[:, 3 * h:3 * h + 1] * o_c[rows] + g[:, 3 * h + 1:3 * h + 2] * o_s[rows]
                    + g[:, 3 * h + 2:3 * h + 3] * o_w[rows])
    o_ref[0] = jnp.concatenate(outs, axis=-1)


def nsa_attention(q, qr, k_cmp, v_cmp, ks, vs, kw, vw, gates, overlap):
    bsz, s, _ = q.shape
    n_top = min(SEL_TOPN, s // SEL_LEN)
    n_cmp = k_cmp.shape[1]
    qtile = lambda w: pl.BlockSpec((1, Q_BLOCK, w), lambda bi, i: (bi, i, 0))
    full = lambda rows, w: pl.BlockSpec((1, rows, w), lambda bi, i: (bi, 0, 0))
    return pl.pallas_call(
        functools.partial(_nsa_attn_body, n_top),
        grid=(bsz, s // Q_BLOCK),
        in_specs=[qtile(W_GRP), qtile(W_GRP), full(n_cmp, HEAD_DIM), full(LANES, n_cmp),
                  full(s, 2 * LANES), full(LANES, s), full(s, HEAD_DIM), full(LANES, s),
                  qtile(LANES), pl.BlockSpec(overlap.shape, lambda bi, i: (0, 0))],
        out_specs=qtile(W_GRP), out_shape=jax.ShapeDtypeStruct((bsz, s, W_GRP), F32),
        compiler_params=_cparams("parallel", "parallel"), name="nsa_attention",
    )(q, qr, k_cmp, v_cmp, ks, vs, kw, vw, gates, overlap)


def _outproj_math(alpha, parts, x, w_ref, g_ref, b_ref):
    y = alpha * x
    for j, part in enumerate(parts):
        y = y + _dot(part[...].astype(BF16), w_ref[j * W_GRP:(j + 1) * W_GRP, :])
    return _ln(y, g_ref[...], b_ref[...])


def _memkv_body(m_ref, wk_ref, wv_ref, k_o, v_o):
    mb = m_ref[...].astype(BF16)
    k_o[...] = _dot(mb, wk_ref[...]).astype(BF16)
    v_o[...] = _dot(mb, wv_ref[...]).astype(BF16)


def mem_kv(mem2d, wk, wv):
    n, d = mem2d.shape
    full = lambda a: pl.BlockSpec(a.shape, lambda i: (0, 0))
    out = jax.ShapeDtypeStruct((n, d), BF16)
    return pl.pallas_call(
        _memkv_body, grid=(1,),
        in_specs=[full(mem2d), full(wk), full(wv)],
        out_specs=[pl.BlockSpec((n, d), lambda i: (0, 0))] * 2, out_shape=[out, out],
        compiler_params=_cparams("arbitrary"), name="mem_kv",
    )(mem2d, wk, wv)


def _xattn_body(alpha, a_ref, b_ref, c_ref, d_ref, x_ref, w1_ref, g1_ref, b1_ref,
                k_ref, v_ref, wq_ref, wo_ref, g_ref, b_ref_, o_ref):
    x = _outproj_math(alpha, (a_ref, b_ref, c_ref, d_ref), x_ref[...], w1_ref, g1_ref, b1_ref)
    d = x.shape[-1]
    hd = d // N_MEM_HEADS
    q = (_dot(x.astype(BF16), wq_ref[...]) * (hd ** -0.5 * LOG2E)).astype(BF16)
    k = k_ref[0]
    v = v_ref[0]
    cols = [slice(h * hd, (h + 1) * hd) for h in range(N_MEM_HEADS)]
    scores = [_dot_nt(q[:, cs], k[:, cs]) for cs in cols]
    probs = []
    for s in scores:
        p = jnp.exp2(s - jnp.max(s, axis=-1, keepdims=True))
        probs.append((p / jnp.sum(p, axis=-1, keepdims=True)).astype(BF16))
    y = alpha * x
    for cs, p in zip(cols, probs):
        y = y + _dot(_dot(p, v[:, cs]).astype(BF16), wo_ref[cs, :])
    o_ref[...] = _ln(y, g_ref[...], b_ref_[...])


def cross_attention(parts, x2d, seq_len, w_out, g1, b1, k, v, wq, wo, g, b, alpha):
    n, d = x2d.shape
    bsz, s = n // seq_len, seq_len
    tps = s // XA_TM
    m = k.shape[1]
    const = lambda a: pl.BlockSpec(a.shape, lambda bi, i: (0,) * a.ndim)
    row = lambda w: pl.BlockSpec((XA_TM, w), lambda bi, i: (bi * tps + i, 0))
    return pl.pallas_call(
        functools.partial(_xattn_body, alpha),
        grid=(bsz, s // XA_TM),
        in_specs=[row(W_GRP)] * 4 + [row(d), const(w_out), const(g1), const(b1),
                  pl.BlockSpec((1, m, d), lambda bi, i: (bi, 0, 0)),
                  pl.BlockSpec((1, m, d), lambda bi, i: (bi, 0, 0)),
                  const(wq), const(wo), const(g), const(b)],
        out_specs=row(d), out_shape=jax.ShapeDtypeStruct((n, d), F32),
        compiler_params=_cparams("parallel", "parallel"), name="cross_attention",
    )(*parts, x2d, w_out, g1, b1, k, v, wq, wo, g, b)


PER_GRP = N_EXPERTS // N_EXPERT_GROUPS
LPOS_LANE = PER_GRP
SUB = 8
BF16_ROWS = 16


def _router_body(x_ref, rwt_ref, rb_ref, upper_ref, tok_o, lrow_o, cnt_o):
    tm = x_ref.shape[0]
    logits = _dot3(x_ref[...], rwt_ref[...]).T[:N_EXPERTS]
    ex = jnp.exp(logits - jnp.max(logits, axis=0, keepdims=True))
    probs = ex / jnp.sum(ex, axis=0, keepdims=True)
    sel = probs + rb_ref[...]
    srow = [sel[e:e + 1] for e in range(N_EXPERTS)]
    prow = [probs[e:e + 1] for e in range(N_EXPERTS)]
    gscore = []
    for g in range(N_EXPERT_GROUPS):
        r = srow[g * PER_GRP:(g + 1) * PER_GRP]
        best = None
        for a in range(PER_GRP):
            for b in range(a + 1, PER_GRP):
                best = r[a] + r[b] if best is None else jnp.maximum(best, r[a] + r[b])
        gscore.append(best)
    g_idx = jnp.zeros((1, tm), jnp.int32)
    top = gscore[0]
    for g in range(1, N_EXPERT_GROUPS):
        better = gscore[g] > top
        g_idx = jnp.where(better, g, g_idx)
        top = jnp.where(better, gscore[g], top)

    def of_group(rows, e):
        out = rows[e]
        for g in range(1, N_EXPERT_GROUPS):
            out = jnp.where(g_idx == g, rows[g * PER_GRP + e], out)
        return out

    sg = [of_group(srow, e) for e in range(PER_GRP)]
    pg = [of_group(prow, e) for e in range(PER_GRP)]
    w = []
    for e in range(PER_GRP):
        rank = jnp.zeros((1, tm), F32)
        for o in range(PER_GRP):
            if o != e:
                ahead = (sg[o] >= sg[e]) if o < e else (sg[o] > sg[e])
                rank = rank + jnp.where(ahead, 1.0, 0.0)
        w.append(jnp.where(rank < 1.5, pg[e], 0.0))
    w_sum = w[0] + w[1] + w[2] + w[3]
    row8 = lax.broadcasted_iota(jnp.int32, (SUB, tm), 0)
    onehot = jnp.where(row8 == g_idx, 1.0, 0.0)
    before = _dot(onehot.astype(BF16), upper_ref[...])
    cnt = jnp.sum(onehot, axis=1, keepdims=True)
    offs = [jnp.zeros((1, 1), F32)]
    for g in range(1, N_EXPERT_GROUPS):
        offs.append(offs[-1] + cnt[g - 1:g])
    lpos = jnp.zeros((1, tm), F32)
    for g in range(N_EXPERT_GROUPS):
        lpos = lpos + onehot[g:g + 1] * (offs[g] + before[g:g + 1])
    tok = jnp.zeros((SUB, tm), F32)
    for e in range(PER_GRP):
        tok = jnp.where(row8 == e, w[e] / w_sum, tok)
    tok = jnp.where(row8 == LPOS_LANE, lpos, tok)
    tok_o[...] = jnp.concatenate([tok, jnp.zeros((LANES - SUB, tm), F32)], axis=0).T
    lrow_o[0] = lpos.astype(jnp.int32)
    rowc = lax.broadcasted_iota(jnp.int32, (SUB, LANES), 0)
    stats = jnp.zeros((SUB, LANES), F32)
    for g in range(N_EXPERT_GROUPS):
        stats = jnp.where(rowc == g, cnt[g:g + 1], stats)
        stats = jnp.where(rowc == N_EXPERT_GROUPS + g, offs[g], stats)
    cnt_o[0] = stats.astype(jnp.int32)


def moe_router(x2d, rwt, rb_col, upper):
    n, d = x2d.shape
    tm = upper.shape[0]
    nt = n // tm
    const = lambda a: pl.BlockSpec(a.shape, lambda i: (0, 0))
    return pl.pallas_call(
        _router_body, grid=(nt,),
        in_specs=[pl.BlockSpec((tm, d), lambda i: (i, 0)), const(rwt), const(rb_col), const(upper)],
        out_specs=[pl.BlockSpec((tm, LANES), lambda i: (i, 0)),
                   pl.BlockSpec((1, 1, tm), lambda i: (i, 0, 0)),
                   pl.BlockSpec((1, SUB, LANES), lambda i: (i, 0, 0))],
        out_shape=[jax.ShapeDtypeStruct((n, LANES), F32), jax.ShapeDtypeStruct((nt, 1, tm), jnp.int32),
                   jax.ShapeDtypeStruct((nt, SUB, LANES), jnp.int32)],
        compiler_params=_cparams("parallel"), name="moe_router",
    )(x2d, rwt, rb_col, upper)


def _serpentine(tile, step, n_steps):
    return jnp.where(tile % 2 == 0, step, n_steps - 1 - step)


def _split_bf16(x):
    hi = x.astype(BF16)
    return hi, (x - hi.astype(F32)).astype(BF16)


def _moe_body(alpha, offs_ref, cnts_ref, x_ref, tok_ref, lrow_ref, wg_ref, wu_ref, wd_ref, g_ref, b_ref, o_ref,
              xs_ref, gs_ref, acc_ref):
    i = pl.program_id(0)
    step = pl.program_id(1)
    tm = x_ref.shape[0]
    eps = wd_ref.shape[1]

    @pl.when(step == 0)
    def _():
        slot = lax.broadcasted_iota(jnp.int32, (tm, tm), 0)
        perm = jnp.where(slot == lrow_ref[0], 1.0, 0.0).astype(BF16)
        xs_ref[0:tm, :] = _dot(perm, x_ref[...].astype(BF16)).astype(BF16)
        t_hi, t_lo = _split_bf16(tok_ref[...])
        gs_ref[0:tm, :] = _dot(perm, t_hi) + _dot(perm, t_lo)
        xs_ref[tm:, :] = jnp.zeros((MOE_BLK, xs_ref.shape[1]), BF16)
        gs_ref[tm:, :] = jnp.zeros((MOE_BLK, LANES), F32)
        acc_ref[...] = jnp.zeros_like(acc_ref)

    wstep = _serpentine(i, step, pl.num_programs(1))
    grp = (wstep * eps) // PER_GRP
    off = offs_ref[i * N_EXPERT_GROUPS + grp]
    end = off + cnts_ref[i * N_EXPERT_GROUPS + grp]
    lane = lax.broadcasted_iota(jnp.int32, (MOE_BLK, LANES), 1)
    first = off // BF16_ROWS * BF16_ROWS
    for s in range(-(-(tm + BF16_ROWS - 1) // MOE_BLK)):
        lo = pl.multiple_of(first + s * MOE_BLK, BF16_ROWS)

        @pl.when(lo < end)
        def _(lo=lo):
            rows = pl.ds(lo, MOE_BLK)
            xs = xs_ref[rows, :]
            ridx = lo + lax.broadcasted_iota(jnp.int32, (MOE_BLK, 1), 0)
            in_run = jnp.where(ridx >= off, jnp.where(ridx < end, 1.0, 0.0), 0.0)
            gs = gs_ref[rows, :]
            y = jnp.zeros((MOE_BLK, x_ref.shape[1]), F32)
            for k in range(eps):
                gt = _dot(xs, wg_ref[0, k])
                h = (gt * jax.nn.sigmoid(gt) * _dot(xs, wu_ref[0, k])).astype(BF16)
                in_grp = (wstep * eps + k) % PER_GRP
                gcol = jnp.sum(jnp.where(lane == in_grp, gs, 0.0), axis=-1, keepdims=True)
                y = y + (gcol * in_run) * _dot(h, wd_ref[0, k])
            acc_ref[rows, :] += y

    @pl.when(step == pl.num_programs(1) - 1)
    def _():
        lcol = tok_ref[:, LPOS_LANE:LPOS_LANE + 1].astype(jnp.int32)
        slot = lax.broadcasted_iota(jnp.int32, (tm, tm), 1)
        unperm = jnp.where(slot == lcol, 1.0, 0.0).astype(BF16)
        y = _dot(unperm, acc_ref[0:tm, :].astype(BF16))
        o_ref[...] = _ln(alpha * x_ref[...] + y, g_ref[...], b_ref[...])


def moe_ffn(x2d, tok, lrow, offs, cnts, layer, wg, wu, wd, g, b, alpha):
    n, d = x2d.shape
    _, ne, de, _ = wd.shape
    tm = lrow.shape[2]
    const = lambda a: pl.BlockSpec(a.shape, lambda i, e, o, c: (0,) * a.ndim)
    wmap = lambda i, e, o, c: (layer, _serpentine(i, e, ne // MOE_EPS), 0, 0)
    grid_spec = pltpu.PrefetchScalarGridSpec(
        num_scalar_prefetch=2, grid=(n // tm, ne // MOE_EPS),
        in_specs=[pl.BlockSpec((tm, d), lambda i, e, o, c: (i, 0)),
                  pl.BlockSpec((tm, LANES), lambda i, e, o, c: (i, 0)),
                  pl.BlockSpec((1, 1, tm), lambda i, e, o, c: (i, 0, 0)),
                  pl.BlockSpec((1, MOE_EPS, d, de), wmap), pl.BlockSpec((1, MOE_EPS, d, de), wmap),
                  pl.BlockSpec((1, MOE_EPS, de, d), wmap),
                  const(g), const(b)],
        out_specs=pl.BlockSpec((tm, d), lambda i, e, o, c: (i, 0)),
        scratch_shapes=[pltpu.VMEM((tm + MOE_BLK, d), BF16), pltpu.VMEM((tm + MOE_BLK, LANES), F32),
                        pltpu.VMEM((tm + MOE_BLK, d), F32)])
    return pl.pallas_call(
        functools.partial(_moe_body, alpha), grid_spec=grid_spec,
        out_shape=jax.ShapeDtypeStruct((n, d), F32),
        compiler_params=_cparams("parallel", "arbitrary"), name="moe_ffn",
    )(offs, cnts, x2d, tok, lrow, wg, wu, wd, g, b)


def _rope_tables(s):
    inv = ROPE_THETA ** (-jnp.arange(0, HEAD_DIM, 2, dtype=F32) / HEAD_DIM)
    ang = jnp.arange(s, dtype=F32)[:, None] * inv[None, :]
    cos, sin = jnp.cos(ang), jnp.sin(ang)
    cos_h = jnp.concatenate([cos, cos], axis=-1)
    sin_h = jnp.concatenate([-sin, sin], axis=-1)
    return jnp.tile(cos_h, (1, 2)), jnp.tile(sin_h, (1, 2))


def _overlap_matrix(s, n_cmp_pad):
    n_sel = s // SEL_LEN
    assert n_sel <= LANES
    cmp_start = jnp.arange(n_cmp_pad) * CMP_STRIDE
    sel_start = jnp.arange(LANES) * SEL_LEN
    ov = jnp.clip(jnp.minimum(cmp_start[None, :] + CMP_LEN, sel_start[:, None] + SEL_LEN)
                  - jnp.maximum(cmp_start[None, :], sel_start[:, None]), 0, None).astype(F32) / CMP_LEN
    n_cmp = s // CMP_STRIDE - (CMP_LEN // CMP_STRIDE - 1)
    real = (jnp.arange(n_cmp_pad)[None, :] < n_cmp) & (jnp.arange(LANES)[:, None] < n_sel)
    return jnp.where(real, ov, 0.0)


def _pad_rows(w, lo, total):
    out = jnp.zeros((total, w.shape[1]), w.dtype)
    return out.at[lo:lo + w.shape[0]].set(w)


def kernel(x, mem, ln_in_g, ln_in_b, w_in, w_out, conv_w, conv_b, conv_gn_g, conv_gn_b, rwkv_mu, rwkv_w0, rwkv_w_up, rwkv_a0, rwkv_a_up, rwkv_g_up, rwkv_k_k, rwkv_k_a, rwkv_r_k, rwkv_gn_g, rwkv_gn_b, rwkv_v_down, rwkv_v_mu, rwkv_v0, rwkv_v_up, gmlp_ln_g, gmlp_ln_b, gmlp_w_s, gmlp_b_s, nsa_pe_k, nsa_w1_k, nsa_w2_k, nsa_pe_v, nsa_w1_v, nsa_w2_v, ln1_g, ln1_b, xa_wq, xa_wk, xa_wv, xa_wo, ln2_g, ln2_b, router_w, router_bias, moe_w_gate, moe_w_up, moe_w_down, ln3_g, ln3_b):
    bsz, s, d = x.shape
    depth = w_in.shape[0]
    n = bsz * s
    alpha = (2 * depth) ** 0.25
    row = lambda a: a.reshape(1, -1)

    cos_t, sin_t = _rope_tables(s)
    n_blk = s // CMP_STRIDE
    overlap = _overlap_matrix(s, n_blk)
    avg64 = _group_avg_matrix(W_GRP, HEAD_DIM)
    ones64 = avg64 * HEAD_DIM
    rw_pad = jnp.zeros((d, LANES), F32).at[:, :N_EXPERTS].set(router_w)
    moe_tm = min(MOE_TM, n)
    t_idx = jnp.arange(moe_tm)
    upper = (t_idx[:, None] < t_idx[None, :]).astype(BF16)
    mem2d = mem.reshape(bsz * mem.shape[1], d)

    wg_b, wu_b, wd_b = moe_w_gate.astype(BF16), moe_w_up.astype(BF16), moe_w_down.astype(BF16)

    xs = x.reshape(n, d)
    v_first = None
    for l in range(depth):
        main = IN_SPLITS[-1][1]
        w_misc = jnp.zeros((d, LANES), F32).at[:, :w_in.shape[2] - main].set(w_in[l, :, main:])
        if l > 0:
            w_misc = w_misc.at[:, MISC_VD_OFF:MISC_VD_OFF + rwkv_v_down.shape[2]].set(rwkv_v_down[l - 1])
        gmlp_p = (row(gmlp_ln_g[l]), row(gmlp_ln_b[l]), gmlp_w_s[l], gmlp_b_s[l].reshape(N_HEADS, GMLP_CHUNK, 1))
        conv_p = (conv_w[l], row(conv_b[l]), row(conv_gn_g[l]), row(conv_gn_b[l]), avg64)
        rwkv_p = (row(rwkv_mu[l]), row(rwkv_w0[l]), row(rwkv_a0[l]), _pad_rows(rwkv_w_up[l], 0, W_GRP),
                  _pad_rows(rwkv_a_up[l], 64, W_GRP), _pad_rows(rwkv_g_up[l], 128, W_GRP),
                  row(rwkv_k_k[l]), row(rwkv_k_a[l]), ones64)
        if l > 0:
            vmu = jnp.zeros((1, LANES), F32).at[0, MISC_VD_OFF:MISC_VD_OFF + rwkv_v_mu.shape[1]].set(rwkv_v_mu[l - 1])
            rwkv_p += (vmu, row(rwkv_v0[l - 1]), _pad_rows(rwkv_v_up[l - 1], MISC_VD_OFF, LANES))
        outs = in_proj(xs, s, row(ln_in_g), row(ln_in_b), w_in, l, w_misc.astype(BF16), cos_t, sin_t,
                       conv_p, gmlp_p, rwkv_p, v_first, apply_ln=(l == 0))
        if l == 0:
            xs, outs = outs[0], outs[1:]
        out_a = outs[0]
        r_, lw_, k_, v_, a_, b_, g_ = outs[1:1 + N_RWKV_OUT]
        out_c = outs[1 + N_RWKV_OUT]
        q_b, qr_b, kc, vc, ks, vs, kw, vw, gates = outs[2 + N_RWKV_OUT:]
        if l == 0:
            v_first = v_
        out_b = wkv_scan(r_, lw_, k_, v_, a_, b_, g_, row(rwkv_r_k[l]), row(rwkv_gn_g[l]),
                         row(rwkv_gn_b[l]), avg64, ones64)

        pe = jnp.stack([nsa_pe_k[l].reshape(1, -1), nsa_pe_v[l].reshape(1, -1)])
        w1 = jnp.stack([nsa_w1_k[l], nsa_w1_v[l]]).astype(BF16)
        w2vt = jnp.zeros((LANES, nsa_w2_v.shape[1]), F32).at[:HEAD_DIM].set(nsa_w2_v[l].T).astype(BF16)
        k_cmp, v_cmp = nsa_compress(kc.reshape(bsz, n_blk, CMP_STRIDE * HEAD_DIM),
                                    vc.reshape(bsz, n_blk, CMP_STRIDE * HEAD_DIM), pe, w1,
                                    nsa_w2_k[l].astype(BF16), w2vt)
        out_d = nsa_attention(q_b, qr_b, k_cmp, v_cmp, ks, vs, kw, vw, gates, overlap)

        flat = lambda a: a.reshape(n, W_GRP)
        mk, mv = mem_kv(mem2d, xa_wk[l].astype(BF16), xa_wv[l].astype(BF16))
        m_len = mem.shape[1]
        xs = cross_attention((flat(out_a), flat(out_b), out_c, flat(out_d)), xs, s, w_out[l].astype(BF16),
                             row(ln1_g[l]), row(ln1_b[l]), mk.reshape(bsz, m_len, d), mv.reshape(bsz, m_len, d),
                             xa_wq[l].astype(BF16), xa_wo[l].astype(BF16), row(ln2_g[l]), row(ln2_b[l]), alpha)

        tok, lrow, stats = moe_router(xs, rw_pad, router_bias.reshape(-1, 1), upper)
        cnts = stats[:, :N_EXPERT_GROUPS, 0].reshape(-1)
        offs = stats[:, N_EXPERT_GROUPS:2 * N_EXPERT_GROUPS, 0].reshape(-1)
        xs = moe_ffn(xs, tok, lrow, offs, cnts, l, wg_b, wu_b, wd_b,
                     row(ln3_g[l]), row(ln3_b[l]), alpha)
    return xs.reshape(bsz, s, d)
```
